```python
import jax, jax.numpy as jnp
from jax import lax
import numpy as np

D_MODEL = 1024
BATCH = 1
SEQ = 16384
DEPTH = 1
DEC_BATCH = 32
DEC_SEQ = 32
PAST_LEN = 2048

CHUNK = 64
N_HEADS = 4
D_MLSTM = D_MODEL
HEAD_DIM = D_MLSTM // N_HEADS
N_POOL_GROUPS = 4
D_POOL = D_MODEL // 2
POOL_GROUP = D_POOL // N_POOL_GROUPS
POOL_WINDOWS = (2, 4, 8, 16)
POOL_HIST = max(POOL_WINDOWS) - 1
D_FF = 4 * D_MODEL
EPS = 1e-6
IN_SIZES = (D_MLSTM, D_MLSTM, D_MLSTM, N_HEADS, N_HEADS, D_MLSTM, D_POOL, D_MODEL, D_MODEL)
N_IN = sum(IN_SIZES)
IN_SPLITS = [int(s) for s in np.cumsum(IN_SIZES)[:-1]]

kernel_name = "mlstm_pool_gated_hybrid_step"


def rmsnorm(x, g):
    xf = x.astype(jnp.float32)
    y = xf * lax.rsqrt(jnp.mean(xf * xf, axis=-1, keepdims=True) + EPS)
    return (y * g.astype(jnp.float32)).astype(x.dtype)


def mlstm_chunk(state, inp):
    C, n, m = state
    q, k, v, ig, lf = inp
    L = q.shape[2]
    b = jnp.cumsum(lf, axis=-1)
    causal = jnp.tril(jnp.ones((L, L), dtype=bool))
    dlog = jnp.where(causal, b[..., :, None] - b[..., None, :] + ig[..., None, :], -jnp.inf)
    m_inter = b + m[..., None]
    m_t = jnp.maximum(m_inter, jnp.max(dlog, axis=-1))
    s = jnp.einsum('bhtd,bhsd->bhts', q, k) * jnp.exp(dlog - m_t[..., None])
    w_inter = jnp.exp(m_inter - m_t)
    num = w_inter[..., None] * jnp.einsum('bhtd,bhde->bhte', q, C) + jnp.einsum('bhts,bhse->bhte', s, v)
    den = w_inter * jnp.einsum('bhtd,bhd->bht', q, n) + jnp.sum(s, axis=-1)
    h = num / jnp.maximum(jnp.abs(den), jnp.exp(-m_t))[..., None]
    b_last = b[..., -1]
    logw = b_last[..., None] - b + ig
    m_new = jnp.maximum(b_last + m, jnp.max(logw, axis=-1))
    decay = jnp.exp(b_last + m - m_new)
    w = jnp.exp(logw - m_new[..., None])
    C_new = decay[..., None, None] * C + jnp.einsum('bhs,bhsd,bhse->bhde', w, k, v)
    n_new = decay[..., None] * n + jnp.einsum('bhs,bhsd->bhd', w, k)
    return (C_new, n_new, m_new), h


def mlstm_mix(q, k, v, ig, lf, C, n, m):
    B, T = q.shape[0], q.shape[1]
    L = min(T, CHUNK)
    nc = T // L
    def heads(a):
        return a.reshape(B, nc, L, N_HEADS, HEAD_DIM).transpose(1, 0, 3, 2, 4)
    def gates(a):
        return a.reshape(B, nc, L, N_HEADS).transpose(1, 0, 3, 2)
    init = (C.astype(jnp.float32), n.astype(jnp.float32), m.astype(jnp.float32))
    (C_new, n_new, m_new), h = lax.scan(mlstm_chunk, init, (heads(q), heads(k), heads(v), gates(ig), gates(lf)))
    h = h.transpose(1, 0, 3, 2, 4).reshape(B, T, N_HEADS, HEAD_DIM)
    return h, C_new, n_new, m_new


def pool_mix(p, hist, pos0, w_grp, scale):
    T = p.shape[1]
    a = jnp.concatenate([hist.astype(jnp.float32), p.astype(jnp.float32)], axis=1)
    cs = jnp.concatenate([jnp.zeros_like(a[:, :1]), jnp.cumsum(a, axis=1)], axis=1)
    pos = pos0 + jnp.arange(T)
    hi = POOL_HIST + 1
    outs = []
    for g, win in enumerate(POOL_WINDOWS):
        lo_c, hi_c = g * POOL_GROUP, (g + 1) * POOL_GROUP
        tot = cs[:, hi:hi + T, lo_c:hi_c] - cs[:, hi - win:hi - win + T, lo_c:hi_c]
        cnt = jnp.minimum(pos + 1, win).astype(jnp.float32)[None, :, None]
        d = tot / cnt - a[:, POOL_HIST:, lo_c:hi_c]
        outs.append(jnp.einsum('btc,cd->btd', d, w_grp[g].astype(jnp.float32)))
    out = jnp.concatenate(outs, axis=-1) * scale.astype(jnp.float32)
    new_hist = a[:, -POOL_HIST:].astype(hist.dtype)
    return out, new_hist


def layer(x, C, n, m, hist, pos0, w_in, b_i, b_f, g_mix, g_head, w_grp, pool_scale,
          w_ba, w_bb, w_out, g_ffn, w_up, w_down):
    B, T, _ = x.shape
    u = rmsnorm(x, g_mix)
    z = u @ w_in
    q, k, v, ig, fg, og, p, ga, gb = jnp.split(z, IN_SPLITS, axis=-1)
    f32 = jnp.float32
    q = q.astype(f32).reshape(B, T, N_HEADS, HEAD_DIM)
    k = k.astype(f32).reshape(B, T, N_HEADS, HEAD_DIM) * (HEAD_DIM ** -0.5)
    v = v.astype(f32).reshape(B, T, N_HEADS, HEAD_DIM)
    ig = ig.astype(f32) + b_i.astype(f32)
    lf = jax.nn.log_sigmoid(fg.astype(f32) + b_f.astype(f32))
    h, C_new, n_new, m_new = mlstm_mix(q, k, v, ig, lf, C, n, m)
    h = h * lax.rsqrt(jnp.mean(h * h, axis=-1, keepdims=True) + EPS) * g_head.astype(f32).reshape(N_HEADS, HEAD_DIM)
    h = h.reshape(B, T, D_MLSTM) * jax.nn.sigmoid(og.astype(f32))
    branch_a = h.astype(x.dtype) @ w_ba
    pooled, new_hist = pool_mix(p, hist, pos0, w_grp, pool_scale)
    branch_b = pooled.astype(x.dtype) @ w_bb
    mixed = jax.nn.sigmoid(ga) * branch_a + jax.nn.sigmoid(gb) * branch_b
    x = (x + mixed @ w_out).astype(x.dtype)
    hid = rmsnorm(x, g_ffn) @ w_up
    x = (x + jnp.square(jax.nn.relu(hid)) @ w_down).astype(x.dtype)
    return x, C_new, n_new, m_new, new_hist


def setup_inputs(seed: int = 0) -> dict:
    key = jax.random.key(seed)
    ks = jax.random.split(key, 24)
    f32 = jnp.float32
    nrm = lambda k, s, sc: jax.random.normal(k, s, f32) * sc
    return {
        "x_prompt": nrm(ks[0], (BATCH, SEQ, D_MODEL), 1.0),
        "x_sample": nrm(ks[1], (DEC_BATCH, DEC_SEQ, D_MODEL), 1.0),
        "state_C": nrm(ks[2], (DEPTH, DEC_BATCH, N_HEADS, HEAD_DIM, HEAD_DIM), 0.05),
        "state_n": nrm(ks[3], (DEPTH, DEC_BATCH, N_HEADS, HEAD_DIM), 0.3),
        "state_m": nrm(ks[4], (DEPTH, DEC_BATCH, N_HEADS), 1.0),
        "state_pool": nrm(ks[5], (DEPTH, DEC_BATCH, POOL_HIST, D_POOL), 1.0),
        "w_in": nrm(ks[6], (DEPTH, D_MODEL, N_IN), D_MODEL ** -0.5),
        "b_igate": nrm(ks[7], (DEPTH, N_HEADS), 0.1),
        "b_fgate": jnp.broadcast_to(jnp.linspace(3.0, 6.0, N_HEADS, dtype=f32), (DEPTH, N_HEADS)) + nrm(ks[8], (DEPTH, N_HEADS), 0.1),
        "g_norm_mix": 1.0 + nrm(ks[9], (DEPTH, D_MODEL), 0.1),
        "g_head": 1.0 + nrm(ks[10], (DEPTH, D_MLSTM), 0.1),
        "w_pool_grp": nrm(ks[11], (DEPTH, N_POOL_GROUPS, POOL_GROUP, POOL_GROUP), POOL_GROUP ** -0.5),
        "pool_scale": 1.0 + nrm(ks[12], (DEPTH, D_POOL), 0.1),
        "w_branch_mlstm": nrm(ks[13], (DEPTH, D_MLSTM, D_MODEL), D_MLSTM ** -0.5),
        "w_branch_pool": nrm(ks[14], (DEPTH, D_POOL, D_MODEL), D_POOL ** -0.5),
        "w_out": nrm(ks[15], (DEPTH, D_MODEL, D_MODEL), D_MODEL ** -0.5),
        "g_norm_ffn": 1.0 + nrm(ks[16], (DEPTH, D_MODEL), 0.1),
        "w_up": nrm(ks[17], (DEPTH, D_MODEL, D_FF), D_MODEL ** -0.5),
        "w_down": nrm(ks[18], (DEPTH, D_FF, D_MODEL), D_FF ** -0.5),
        "g_final": 1.0 + nrm(ks[19], (D_MODEL,), 0.1),
    }


def reference(x_prompt, x_sample, state_C, state_n, state_m, state_pool, w_in, b_igate, b_fgate,
              g_norm_mix, g_head, w_pool_grp, pool_scale, w_branch_mlstm, w_branch_pool, w_out,
              g_norm_ffn, w_up, w_down, g_final):
    bp = x_prompt.shape[0]
    hp, hs = x_prompt, x_sample
    cp_l, np_l, mp_l, pp_l = [], [], [], []
    cs_l, ns_l, ms_l, ps_l = [], [], [], []
    for l in range(DEPTH):
        params = (w_in[l], b_igate[l], b_fgate[l], g_norm_mix[l], g_head[l], w_pool_grp[l], pool_scale[l],
                  w_branch_mlstm[l], w_branch_pool[l], w_out[l], g_norm_ffn[l], w_up[l], w_down[l])
        c0 = jnp.zeros((bp, N_HEADS, HEAD_DIM, HEAD_DIM), jnp.float32)
        n0 = jnp.zeros((bp, N_HEADS, HEAD_DIM), jnp.float32)
        m0 = jnp.zeros((bp, N_HEADS), jnp.float32)
        hist0 = jnp.zeros((bp, POOL_HIST, D_POOL), x_prompt.dtype)
        hp, cp, np_, mp, pp = layer(hp, c0, n0, m0, hist0, 0, *params)
        hs, cs, ns, ms, ps = layer(hs, state_C[l], state_n[l], state_m[l], state_pool[l], PAST_LEN, *params)
        cp_l.append(cp); np_l.append(np_); mp_l.append(mp); pp_l.append(pp)
        cs_l.append(cs); ns_l.append(ns); ms_l.append(ms); ps_l.append(ps)
    y_prompt = rmsnorm(hp, g_final)
    y_sample = rmsnorm(hs, g_final)
    return (y_prompt, y_sample,
            jnp.stack(cp_l), jnp.stack(np_l), jnp.stack(mp_l), jnp.stack(pp_l),
            jnp.stack(cs_l), jnp.stack(ns_l), jnp.stack(ms_l), jnp.stack(ps_l))
```

```python
import functools

import jax
import jax.numpy as jnp
from jax import lax
from jax.experimental import pallas as pl
from jax.experimental.pallas import tpu as pltpu

F32 = jnp.float32
BF16 = jnp.bfloat16

D_MODEL = 1024
N_HEADS = 4
HEAD_DIM = 256
D_POOL = 512
POOL_GROUP = 128
POOL_WINDOWS = (2, 4, 8, 16)
HIST_ROWS = 16
D_FF = 4096
EPS = 1e-6
PAST_LEN = 2048
LANES = 128
GATE_ROWS = 16
V7X_VMEM_LIMIT_BYTES = 56 * 1024 * 1024

_Q0, _V0, _OG0, _P0, _GA0, _GB0, _W1_COLS = 0, 1024, 2048, 3072, 3584, 4608, 5632


def _dot(a, b):
    return jnp.dot(a, b, preferred_element_type=F32)


def _dot_nt(a, b):
    return lax.dot_general(a, b, (((1,), (1,)), ((), ())), preferred_element_type=F32)


def _rmsnorm(x, g):
    return x * lax.rsqrt(jnp.mean(x * x, axis=-1, keepdims=True) + EPS) * g


def _sigmoid(x):
    return 1.0 / (1.0 + jnp.exp(-x))


def _log_sigmoid(x):
    return jnp.minimum(x, 0.0) - jnp.log1p(jnp.exp(-jnp.abs(x)))


def _mixer_kernel(x_ref, gmix_ref, w1_ref, wkg_ref, bias_ref, ghead_ref, wgrp_ref, pscale_ref,
                  wba_ref, wbb_ref, wout_ref, c_in, n_in, m_in, hist_in,
                  x1_ref, c_out, n_out, m_out, hist_out,
                  q_s, v_s, kt_s, og_s, ga_s, gb_s, gate_s, abuf, hcat_s,
                  *, tm, seg, carry, pos0):
    i = pl.program_id(0)
    nseg = tm // seg

    if carry:
        @pl.when(i == 0)
        def _():
            c_out[...] = c_in[...]
            n_out[...] = n_in[...]
            m_out[...] = m_in[...]
            abuf[0, 0:HIST_ROWS, :] = hist_in[0]
        c_rd, n_rd, m_rd = c_out, n_out, m_out
    else:
        c_rd, n_rd, m_rd = c_in, n_in, m_in
        for j in range(nseg):
            abuf[j, 0:HIST_ROWS, :] = hist_in[j]

    x = x_ref[...]
    u = _rmsnorm(x, gmix_ref[...]).astype(BF16)
    kg = _dot_nt(wkg_ref[...], u)
    kt_s[...] = kg[0:D_MODEL] * (HEAD_DIM ** -0.5)
    gates = kg[D_MODEL:D_MODEL + GATE_ROWS] + bias_ref[...]
    q_s[...] = _dot(u, w1_ref[:, _Q0:_V0]).astype(BF16)
    v_s[...] = _dot(u, w1_ref[:, _V0:_OG0]).astype(BF16)
    og_s[...] = _dot(u, w1_ref[:, _OG0:_P0])
    p = _dot(u, w1_ref[:, _P0:_GA0])
    ga_s[...] = _dot(u, w1_ref[:, _GA0:_GB0])
    gb_s[...] = _dot(u, w1_ref[:, _GB0:_W1_COLS])

    row = lax.broadcasted_iota(jnp.int32, (tm, tm), 0)
    col = lax.broadcasted_iota(jnp.int32, (tm, tm), 1)
    if nseg == 1:
        causal = col <= row
        upper = row <= col
    else:
        shift = seg.bit_length() - 1
        same = (row >> shift) == (col >> shift)
        causal = (col <= row) & same
        upper = (row <= col) & same

    lf = _log_sigmoid(gates[8:16])
    hi = lf.astype(BF16).astype(F32)
    r1 = lf - hi
    mid = r1.astype(BF16).astype(F32)
    lo = r1 - mid
    pieces = jnp.concatenate([hi, mid, lo, jnp.zeros_like(lo)], axis=0).astype(BF16)
    cs = _dot(pieces, jnp.where(upper, 1.0, 0.0).astype(BF16))
    b = cs[0:8] + cs[8:16] + cs[16:24]
    gate_s[0:8, :] = gates[0:8] - b
    gate_s[8:16, :] = lf

    for h in range(N_HEADS):
        sl = slice(h * HEAD_DIM, (h + 1) * HEAD_DIM)
        a_m = jnp.where(causal, jnp.broadcast_to(gate_s[h:h + 1, :], (tm, tm)), -jnp.inf)
        mprev = jnp.concatenate(
            [jnp.broadcast_to(m_rd[j:j + 1, h * LANES:h * LANES + 1], (seg, 1)) for j in range(nseg)], axis=0)
        g = jnp.maximum(jnp.max(a_m, axis=-1, keepdims=True), mprev)
        dm = jnp.exp(a_m - g)
        lf_b = jnp.broadcast_to(gate_s[8 + h:9 + h, :], (tm, tm))
        bcol = jnp.sum(jnp.where(causal, lf_b, 0.0), axis=-1, keepdims=True)
        mcol = bcol + g
        wi = jnp.exp(mprev - g)

        qh = q_s[:, sl]
        vh = v_s[:, sl]
        kth = kt_s[sl, :]
        kth_bf = kth.astype(BF16)
        s = _dot(qh, kth_bf) * dm
        rowsum = jnp.sum(s, axis=-1, keepdims=True)
        sv = _dot(s.astype(BF16), vh)

        qf = qh.astype(F32)
        qc_parts, qn_parts = [], []
        for j in range(nseg):
            rs = slice(j * seg, (j + 1) * seg)
            qc_parts.append(_dot(qh[rs], c_rd[j, h].astype(BF16)))
            qn_parts.append(jnp.sum(qf[rs] * n_rd[j:j + 1, sl], axis=-1, keepdims=True))
        qc = jnp.concatenate(qc_parts, axis=0)
        qn = jnp.concatenate(qn_parts, axis=0)

        num = wi * qc + sv
        den = wi * qn + rowsum
        hh = num / jnp.maximum(jnp.abs(den), jnp.exp(-mcol))
        hh = hh * lax.rsqrt(jnp.mean(hh * hh, axis=-1, keepdims=True) + EPS) * ghead_ref[:, sl]
        hh = hh * _sigmoid(og_s[:, sl])
        hcat_s[:, sl] = hh.astype(BF16)

        wrows = []
        for j in range(nseg):
            r = (j + 1) * seg - 1
            w_row = dm[r:r + 1, :]
            wrows.append(w_row)
            decay = wi[r:r + 1, :]
            kw = (kth * w_row).astype(BF16)
            c_out[j, h] = decay * c_rd[j, h] + _dot(kw, vh)
            m_out[j:j + 1, h * LANES:(h + 1) * LANES] = jnp.broadcast_to(mcol[r:r + 1, :], (1, LANES))
        ridx = lax.broadcasted_iota(jnp.int32, (GATE_ROWS, tm), 0)
        wmat = jnp.zeros((GATE_ROWS, tm), F32)
        for j in range(nseg):
            wmat = jnp.where(ridx == j, jnp.broadcast_to(wrows[j], (GATE_ROWS, tm)), wmat)
        nupd = _dot_nt(wmat.astype(BF16), kth_bf)
        for j in range(nseg):
            r = (j + 1) * seg - 1
            n_out[j:j + 1, sl] = wi[r:r + 1, :] * n_rd[j:j + 1, sl] + nupd[j:j + 1, :]

    for j in range(nseg):
        abuf[j, HIST_ROWS:HIST_ROWS + seg, :] = p[j * seg:(j + 1) * seg]
    tpos = lax.broadcasted_iota(jnp.int32, (seg, POOL_GROUP), 0) + pos0
    if carry:
        tpos = tpos + i * tm
    grp_out = []
    for gi, win in enumerate(POOL_WINDOWS):
        ls = slice(gi * POOL_GROUP, (gi + 1) * POOL_GROUP)
        cnt = jnp.minimum(tpos + 1, win).astype(F32)
        dparts = []
        for j in range(nseg):
            tok = abuf[j, HIST_ROWS:HIST_ROWS + seg, ls]
            tot = tok
            for k in range(1, win):
                tot = tot + abuf[j, HIST_ROWS - k:HIST_ROWS - k + seg, ls]
            dparts.append(tot / cnt - tok)
        d = jnp.concatenate(dparts, axis=0).astype(BF16)
        grp_out.append(_dot(d, wgrp_ref[gi]))
    pooled = jnp.concatenate(grp_out, axis=-1) * pscale_ref[...]
    for j in range(nseg):
        new_hist = abuf[j, seg:seg + HIST_ROWS, :]
        hist_out[j] = new_hist
        if carry:
            abuf[j, 0:HIST_ROWS, :] = new_hist

    branch_a = _dot(hcat_s[...], wba_ref[...])
    branch_b = _dot(pooled.astype(BF16), wbb_ref[...])
    mixed = _sigmoid(ga_s[...]) * branch_a + _sigmoid(gb_s[...]) * branch_b
    x1_ref[...] = x + _dot(mixed.astype(BF16), wout_ref[...])


def _ffn_kernel(x_ref, gffn_ref, wup_ref, wdown_ref, gfin_ref, y_ref):
    x = x_ref[...]
    u = _rmsnorm(x, gffn_ref[...]).astype(BF16)
    hid = _dot(u, wup_ref[...])
    act = jnp.square(jnp.maximum(hid, 0.0)).astype(BF16)
    x2 = x + _dot(act, wdown_ref[...])
    y_ref[...] = _rmsnorm(x2, gfin_ref[...])


def _resident(shape):
    zeros = (0,) * len(shape)
    return pl.BlockSpec(shape, lambda i: zeros, pipeline_mode=pl.Buffered(1))


def _mixer(x, weights, state, *, tm, seg, carry, pos0):
    t = x.shape[0]
    ntiles = t // tm
    nseg = tm // seg
    c0, n0, m0, hist0 = state
    n_streams = c0.shape[0]
    sblk = 1 if carry else nseg
    smap = (lambda i: 0) if carry else (lambda i: i)
    n0 = n0.reshape(n_streams // sblk, sblk, D_MODEL)
    m0 = jnp.repeat(m0, LANES, axis=-1).reshape(n_streams // sblk, sblk, N_HEADS * LANES)
    hist0 = jnp.pad(hist0, ((0, 0), (1, 0), (0, 0)))

    in_specs = [
        pl.BlockSpec((tm, D_MODEL), lambda i: (i, 0)),
        _resident((1, D_MODEL)),
        _resident((D_MODEL, _W1_COLS)),
        _resident((D_MODEL + GATE_ROWS, D_MODEL)),
        _resident((GATE_ROWS, tm)),
        _resident((1, D_MODEL)),
        _resident((len(POOL_WINDOWS), POOL_GROUP, POOL_GROUP)),
        _resident((1, D_POOL)),
        _resident((D_MODEL, D_MODEL)),
        _resident((D_POOL, D_MODEL)),
        _resident((D_MODEL, D_MODEL)),
        pl.BlockSpec((sblk, N_HEADS, HEAD_DIM, HEAD_DIM), lambda i: (smap(i), 0, 0, 0)),
        pl.BlockSpec((None, sblk, D_MODEL), lambda i: (smap(i), 0, 0)),
        pl.BlockSpec((None, sblk, N_HEADS * LANES), lambda i: (smap(i), 0, 0)),
        pl.BlockSpec((sblk, HIST_ROWS, D_POOL), lambda i: (smap(i), 0, 0)),
    ]
    out_specs = [
        pl.BlockSpec((tm, D_MODEL), lambda i: (i, 0)),
        pl.BlockSpec((sblk, N_HEADS, HEAD_DIM, HEAD_DIM), lambda i: (smap(i), 0, 0, 0)),
        pl.BlockSpec((None, sblk, D_MODEL), lambda i: (smap(i), 0, 0)),
        pl.BlockSpec((None, sblk, N_HEADS * LANES), lambda i: (smap(i), 0, 0)),
        pl.BlockSpec((sblk, HIST_ROWS, D_POOL), lambda i: (smap(i), 0, 0)),
    ]
    out_shape = [
        jax.ShapeDtypeStruct((t, D_MODEL), F32),
        jax.ShapeDtypeStruct(c0.shape, F32),
        jax.ShapeDtypeStruct(n0.shape, F32),
        jax.ShapeDtypeStruct(m0.shape, F32),
        jax.ShapeDtypeStruct(hist0.shape, F32),
    ]
    scratch = [
        pltpu.VMEM((tm, D_MODEL), BF16),
        pltpu.VMEM((tm, D_MODEL), BF16),
        pltpu.VMEM((D_MODEL, tm), F32),
        pltpu.VMEM((tm, D_MODEL), F32),
        pltpu.VMEM((tm, D_MODEL), F32),
        pltpu.VMEM((tm, D_MODEL), F32),
        pltpu.VMEM((GATE_ROWS, tm), F32),
        pltpu.VMEM((nseg, HIST_ROWS + seg, D_POOL), F32),
        pltpu.VMEM((tm, D_MODEL), BF16),
    ]
    bias = jnp.broadcast_to(weights["gate_bias"][:, None], (GATE_ROWS, tm))
    x1, c1, n1, m1, hist1 = pl.pallas_call(
        functools.partial(_mixer_kernel, tm=tm, seg=seg, carry=carry, pos0=pos0),
        grid=(ntiles,),
        in_specs=in_specs,
        out_specs=out_specs,
        out_shape=out_shape,
        scratch_shapes=scratch,
        compiler_params=pltpu.CompilerParams(
            dimension_semantics=("arbitrary",), vmem_limit_bytes=V7X_VMEM_LIMIT_BYTES),
        name="mixer_carry" if carry else "mixer_streams",
    )(x, weights["g_mix"], weights["w1"], weights["wkg"], bias, weights["g_head"], weights["w_grp"],
      weights["pool_scale"], weights["w_ba"], weights["w_bb"], weights["w_out"], c0, n0, m0, hist0)
    n1 = n1.reshape(n_streams, N_HEADS, HEAD_DIM)
    m1 = m1.reshape(n_streams, N_HEADS, LANES)[:, :, 0]
    return x1, c1, n1, m1, hist1[:, 1:, :]


def _ffn(x, weights, *, tm):
    t = x.shape[0]
    return pl.pallas_call(
        _ffn_kernel,
        grid=(t // tm,),
        in_specs=[
            pl.BlockSpec((tm, D_MODEL), lambda i: (i, 0)),
            _resident((1, D_MODEL)),
            _resident((D_MODEL, D_FF)),
            _resident((D_FF, D_MODEL)),
            _resident((1, D_MODEL)),
        ],
        out_specs=pl.BlockSpec((tm, D_MODEL), lambda i: (i, 0)),
        out_shape=jax.ShapeDtypeStruct((t, D_MODEL), F32),
        compiler_params=pltpu.CompilerParams(
            dimension_semantics=("arbitrary",), vmem_limit_bytes=V7X_VMEM_LIMIT_BYTES),
        name="ffn",
    )(x, weights["g_ffn"], weights["w_up"], weights["w_down"], weights["g_final"])


def _pack_weights(w_in, b_igate, b_fgate, g_norm_mix, g_head, w_pool_grp, pool_scale, w_branch_mlstm,
                  w_branch_pool, w_out, g_norm_ffn, w_up, w_down, g_final):
    d = D_MODEL
    wq, wk, wv = w_in[:, 0:d], w_in[:, d:2 * d], w_in[:, 2 * d:3 * d]
    wi = w_in[:, 3 * d:3 * d + N_HEADS]
    wf = w_in[:, 3 * d + N_HEADS:3 * d + 2 * N_HEADS]
    o0 = 3 * d + 2 * N_HEADS
    wog = w_in[:, o0:o0 + d]
    wp = w_in[:, o0 + d:o0 + d + D_POOL]
    wga = w_in[:, o0 + d + D_POOL:o0 + 2 * d + D_POOL]
    wgb = w_in[:, o0 + 2 * d + D_POOL:o0 + 3 * d + D_POOL]
    return {
        "w1": jnp.concatenate([wq, wv, wog, wp, wga, wgb], axis=1).astype(BF16),
        "wkg": jnp.concatenate([wk, wi, wf, wf, wi], axis=1).T.astype(BF16),
        "gate_bias": jnp.concatenate([b_igate, b_fgate, b_fgate, b_igate]).astype(F32),
        "g_mix": g_norm_mix.reshape(1, d),
        "g_head": g_head.reshape(1, d),
        "w_grp": w_pool_grp.astype(BF16),
        "pool_scale": pool_scale.reshape(1, D_POOL),
        "w_ba": w_branch_mlstm.astype(BF16),
        "w_bb": w_branch_pool.astype(BF16),
        "w_out": w_out.astype(BF16),
        "g_ffn": g_norm_ffn.reshape(1, d),
        "w_up": w_up.astype(BF16),
        "w_down": w_down.astype(BF16),
        "g_final": g_final.reshape(1, d),
    }


def kernel(x_prompt, x_sample, state_C, state_n, state_m, state_pool, w_in, b_igate, b_fgate, g_norm_mix, g_head,
           w_pool_grp, pool_scale, w_branch_mlstm, w_branch_pool, w_out, g_norm_ffn, w_up, w_down, g_final):
    depth = w_in.shape[0]
    bp, sp, d = x_prompt.shape
    bs, ss, _ = x_sample.shape
    assert depth == 1 and bp == 1 and d == D_MODEL
    hp = x_prompt.reshape(bp * sp, d)
    hs = x_sample.reshape(bs * ss, d)
    weights = _pack_weights(w_in[0], b_igate[0], b_fgate[0], g_norm_mix[0], g_head[0], w_pool_grp[0], pool_scale[0],
                            w_branch_mlstm[0], w_branch_pool[0], w_out[0], g_norm_ffn[0], w_up[0], w_down[0],
                            g_final)
    zero_state = (jnp.zeros((bp, N_HEADS, HEAD_DIM, HEAD_DIM), F32), jnp.zeros((bp, D_MODEL), F32),
                  jnp.zeros((bp, N_HEADS), F32), jnp.zeros((bp, HIST_ROWS - 1, D_POOL), F32))
    hp, cp, np_, mp, pp = _mixer(hp, weights, zero_state, tm=256, seg=256, carry=True, pos0=0)
    sample_state = (state_C[0], state_n[0].reshape(bs, D_MODEL), state_m[0], state_pool[0])
    hs, cs, ns, ms, ps = _mixer(hs, weights, sample_state, tm=4 * ss, seg=ss, carry=False, pos0=PAST_LEN)
    y_prompt = _ffn(hp, weights, tm=256).reshape(bp, sp, d)
    y_sample = _ffn(hs, weights, tm=256).reshape(bs, ss, d)
    return (y_prompt, y_sample, cp[None], np_[None], mp[None], pp[None], cs[None], ns[None], ms[None], ps[None])
```

```python
import functools

import jax
import jax.numpy as jnp
from jax import lax
from jax.experimental import pallas as pl
from jax.experimental.pallas import tpu as pltpu

F32 = jnp.float32
BF16 = jnp.bfloat16

D_MODEL = 1024
N_HEADS = 4
HEAD_DIM = 256
D_POOL = 512
POOL_GROUP = 128
POOL_WINDOWS = (2, 4, 8, 16)
HIST_ROWS = 16
D_FF = 4096
EPS = 1e-6
PAST_LEN = 2048
LANES = 128
GATE_ROWS = 16
V7X_VMEM_LIMIT_BYTES = 56 * 1024 * 1024

_K0, _V0, _QKV_ROWS = 1024, 2048, 3072
_REST0 = _QKV_ROWS + 2 * N_HEADS
_P0, _GA0, _GB0, _REST_ROWS = 1024, 1536, 2560, 3584


def _dot(a, b):
    return jnp.dot(a, b, preferred_element_type=F32)


def _dot_nt(a, b):
    return lax.dot_general(a, b, (((1,), (1,)), ((), ())), preferred_element_type=F32)


def _rmsnorm(x, g):
    return x * lax.rsqrt(jnp.mean(x * x, axis=-1, keepdims=True) + EPS) * g


def _sigmoid(x):
    return 1.0 / (1.0 + jnp.exp(-x))


def _log_sigmoid(x):
    return jnp.minimum(x, 0.0) - jnp.log1p(jnp.exp(-jnp.abs(x)))


def _mixer_kernel(x_ref, gmix_ref, wqkv_ref, wrest_ref, wgt_ref, bias_ref, ghead_ref, wgrp_ref, pscale_ref,
                  wba_ref, wbb_ref, wout_ref, c_in, n_in, m_in, hist_in,
                  x1_ref, c_out, n_out, m_out, hist_out,
                  wkg_s, q_s, v_s, kt_s, og_s, ga_s, gb_s, gate_s, abuf, hcat_s,
                  *, tm, seg, carry, pos0):
    i = pl.program_id(0)
    nseg = tm // seg

    @pl.when(i == 0)
    def _():
        wkg_s[0:D_MODEL, :] = wqkv_ref[_K0:_V0, :]
        wkg_s[D_MODEL:D_MODEL + GATE_ROWS, :] = wgt_ref[...].astype(BF16)

    if carry:
        @pl.when(i == 0)
        def _():
            c_out[...] = c_in[...]
            n_out[...] = n_in[...]
            m_out[...] = m_in[...]
            abuf[0, 0:HIST_ROWS, :] = hist_in[0]
        c_rd, n_rd, m_rd = c_out, n_out, m_out
    else:
        c_rd, n_rd, m_rd = c_in, n_in, m_in
        for j in range(nseg):
            abuf[j, 0:HIST_ROWS, :] = hist_in[j]

    x = x_ref[...]
    u = _rmsnorm(x, gmix_ref[...]).astype(BF16)
    kg = _dot_nt(wkg_s[...], u)
    kt_s[...] = kg[0:D_MODEL] * (HEAD_DIM ** -0.5)
    gates = kg[D_MODEL:D_MODEL + GATE_ROWS] + bias_ref[...]
    q_s[...] = _dot_nt(u, wqkv_ref[0:_K0, :]).astype(BF16)
    v_s[...] = _dot_nt(u, wqkv_ref[_V0:_QKV_ROWS, :]).astype(BF16)
    og_s[...] = _dot_nt(u, wrest_ref[0:_P0, :])
    p = _dot_nt(u, wrest_ref[_P0:_GA0, :])
    ga_s[...] = _dot_nt(u, wrest_ref[_GA0:_GB0, :])
    gb_s[...] = _dot_nt(u, wrest_ref[_GB0:_REST_ROWS, :])

    row = lax.broadcasted_iota(jnp.int32, (tm, tm), 0)
    col = lax.broadcasted_iota(jnp.int32, (tm, tm), 1)
    if nseg == 1:
        causal = col <= row
        upper = row <= col
    else:
        shift = seg.bit_length() - 1
        same = (row >> shift) == (col >> shift)
        causal = (col <= row) & same
        upper = (row <= col) & same

    lf = _log_sigmoid(gates[8:16])
    hi = lf.astype(BF16).astype(F32)
    r1 = lf - hi
    mid = r1.astype(BF16).astype(F32)
    lo = r1 - mid
    pieces = jnp.concatenate([hi, mid, lo, jnp.zeros_like(lo)], axis=0).astype(BF16)
    cs = _dot(pieces, jnp.where(upper, 1.0, 0.0).astype(BF16))
    b = cs[0:8] + cs[8:16] + cs[16:24]
    gate_s[0:8, :] = gates[0:8] - b
    gate_s[8:16, :] = lf

    for h in range(N_HEADS):
        sl = slice(h * HEAD_DIM, (h + 1) * HEAD_DIM)
        a_m = jnp.where(causal, jnp.broadcast_to(gate_s[h:h + 1, :], (tm, tm)), -jnp.inf)
        mprev = jnp.concatenate(
            [jnp.broadcast_to(m_rd[j:j + 1, h * LANES:h * LANES + 1], (seg, 1)) for j in range(nseg)], axis=0)
        g = jnp.maximum(jnp.max(a_m, axis=-1, keepdims=True), mprev)
        dm = jnp.exp(a_m - g)
        lf_b = jnp.broadcast_to(gate_s[8 + h:9 + h, :], (tm, tm))
        bcol = jnp.sum(jnp.where(causal, lf_b, 0.0), axis=-1, keepdims=True)
        mcol = bcol + g
        wi = jnp.exp(mprev - g)

        qh = q_s[:, sl]
        vh = v_s[:, sl]
        kth = kt_s[sl, :]
        kth_bf = kth.astype(BF16)
        s = _dot(qh, kth_bf) * dm
        rowsum = jnp.sum(s, axis=-1, keepdims=True)
        sv = _dot(s.astype(BF16), vh)

        qf = qh.astype(F32)
        qc_parts, qn_parts = [], []
        for j in range(nseg):
            rs = slice(j * seg, (j + 1) * seg)
            qc_parts.append(_dot(qh[rs], c_rd[j, h].astype(BF16)))
            qn_parts.append(jnp.sum(qf[rs] * n_rd[j:j + 1, sl], axis=-1, keepdims=True))
        qc = jnp.concatenate(qc_parts, axis=0)
        qn = jnp.concatenate(qn_parts, axis=0)

        num = wi * qc + sv
        den = wi * qn + rowsum
        hh = num / jnp.maximum(jnp.abs(den), jnp.exp(-mcol))
        hh = hh * lax.rsqrt(jnp.mean(hh * hh, axis=-1, keepdims=True) + EPS) * ghead_ref[:, sl]
        hh = hh * _sigmoid(og_s[:, sl])
        hcat_s[:, sl] = hh.astype(BF16)

        wrows = []
        for j in range(nseg):
            r = (j + 1) * seg - 1
            w_row = dm[r:r + 1, :]
            wrows.append(w_row)
            decay = wi[r:r + 1, :]
            kw = (kth * w_row).astype(BF16)
            c_out[j, h] = decay * c_rd[j, h] + _dot(kw, vh)
            m_out[j:j + 1, h * LANES:(h + 1) * LANES] = jnp.broadcast_to(mcol[r:r + 1, :], (1, LANES))
        ridx = lax.broadcasted_iota(jnp.int32, (GATE_ROWS, tm), 0)
        wmat = jnp.zeros((GATE_ROWS, tm), F32)
        for j in range(nseg):
            wmat = jnp.where(ridx == j, jnp.broadcast_to(wrows[j], (GATE_ROWS, tm)), wmat)
        nupd = _dot_nt(wmat.astype(BF16), kth_bf)
        for j in range(nseg):
            r = (j + 1) * seg - 1
            n_out[j:j + 1, sl] = wi[r:r + 1, :] * n_rd[j:j + 1, sl] + nupd[j:j + 1, :]

    for j in range(nseg):
        abuf[j, HIST_ROWS:HIST_ROWS + seg, :] = p[j * seg:(j + 1) * seg]
    tpos = lax.broadcasted_iota(jnp.int32, (seg, POOL_GROUP), 0) + pos0
    if carry:
        tpos = tpos + i * tm
    grp_out = []
    for gi, win in enumerate(POOL_WINDOWS):
        ls = slice(gi * POOL_GROUP, (gi + 1) * POOL_GROUP)
        cnt = jnp.minimum(tpos + 1, win).astype(F32)
        dparts = []
        for j in range(nseg):
            tok = abuf[j, HIST_ROWS:HIST_ROWS + seg, ls]
            tot = tok
            for k in range(1, win):
                tot = tot + abuf[j, HIST_ROWS - k:HIST_ROWS - k + seg, ls]
            dparts.append(tot / cnt - tok)
        d = jnp.concatenate(dparts, axis=0).astype(BF16)
        grp_out.append(_dot(d, wgrp_ref[gi]))
    pooled = jnp.concatenate(grp_out, axis=-1) * pscale_ref[...]
    for j in range(nseg):
        new_hist = abuf[j, seg:seg + HIST_ROWS, :]
        hist_out[j] = new_hist
        if carry:
            abuf[j, 0:HIST_ROWS, :] = new_hist

    branch_a = _dot(hcat_s[...], wba_ref[...])
    branch_b = _dot(pooled.astype(BF16), wbb_ref[...])
    mixed = _sigmoid(ga_s[...]) * branch_a + _sigmoid(gb_s[...]) * branch_b
    x1_ref[...] = x + _dot(mixed.astype(BF16), wout_ref[...])


def _ffn_kernel(x_ref, gffn_ref, wup_ref, wdown_ref, gfin_ref, y_ref, *, sub):
    for r in range(x_ref.shape[0] // sub):
        rows = slice(r * sub, (r + 1) * sub)
        x = x_ref[rows, :]
        u = _rmsnorm(x, gffn_ref[...]).astype(BF16)
        hid = _dot(u, wup_ref[...])
        act = jnp.square(jnp.maximum(hid, 0.0)).astype(BF16)
        x2 = x + _dot(act, wdown_ref[...])
        y_ref[rows, :] = _rmsnorm(x2, gfin_ref[...])


def _resident(shape):
    zeros = (0,) * len(shape)
    return pl.BlockSpec(shape, lambda i: zeros, pipeline_mode=pl.Buffered(1))


def _mixer(x, weights, state, *, tm, seg, carry, pos0):
    t = x.shape[0]
    ntiles = t // tm
    nseg = tm // seg
    c0, n0, m0, hist0 = state
    n_streams = c0.shape[0]
    sblk = 1 if carry else nseg
    smap = (lambda i: 0) if carry else (lambda i: i)
    n0 = n0.reshape(n_streams // sblk, sblk, D_MODEL)
    m0 = jnp.repeat(m0, LANES, axis=-1).reshape(n_streams // sblk, sblk, N_HEADS * LANES)
    hist0 = jnp.pad(hist0, ((0, 0), (1, 0), (0, 0)))

    in_specs = [
        pl.BlockSpec((tm, D_MODEL), lambda i: (i, 0)),
        _resident((1, D_MODEL)),
        _resident((_QKV_ROWS, D_MODEL)),
        _resident((_REST_ROWS, D_MODEL)),
        _resident((GATE_ROWS, D_MODEL)),
        _resident((GATE_ROWS, tm)),
        _resident((1, D_MODEL)),
        _resident((len(POOL_WINDOWS), POOL_GROUP, POOL_GROUP)),
        _resident((1, D_POOL)),
        _resident((D_MODEL, D_MODEL)),
        _resident((D_POOL, D_MODEL)),
        _resident((D_MODEL, D_MODEL)),
        pl.BlockSpec((sblk, N_HEADS, HEAD_DIM, HEAD_DIM), lambda i: (smap(i), 0, 0, 0)),
        pl.BlockSpec((None, sblk, D_MODEL), lambda i: (smap(i), 0, 0)),
        pl.BlockSpec((None, sblk, N_HEADS * LANES), lambda i: (smap(i), 0, 0)),
        pl.BlockSpec((sblk, HIST_ROWS, D_POOL), lambda i: (smap(i), 0, 0)),
    ]
    out_specs = [
        pl.BlockSpec((tm, D_MODEL), lambda i: (i, 0)),
        pl.BlockSpec((sblk, N_HEADS, HEAD_DIM, HEAD_DIM), lambda i: (smap(i), 0, 0, 0)),
        pl.BlockSpec((None, sblk, D_MODEL), lambda i: (smap(i), 0, 0)),
        pl.BlockSpec((None, sblk, N_HEADS * LANES), lambda i: (smap(i), 0, 0)),
        pl.BlockSpec((sblk, HIST_ROWS, D_POOL), lambda i: (smap(i), 0, 0)),
    ]
    out_shape = [
        jax.ShapeDtypeStruct((t, D_MODEL), F32),
        jax.ShapeDtypeStruct(c0.shape, F32),
        jax.ShapeDtypeStruct(n0.shape, F32),
        jax.ShapeDtypeStruct(m0.shape, F32),
        jax.ShapeDtypeStruct(hist0.shape, F32),
    ]
    scratch = [
        pltpu.VMEM((D_MODEL + GATE_ROWS, D_MODEL), BF16),
        pltpu.VMEM((tm, D_MODEL), BF16),
        pltpu.VMEM((tm, D_MODEL), BF16),
        pltpu.VMEM((D_MODEL, tm), F32),
        pltpu.VMEM((tm, D_MODEL), F32),
        pltpu.VMEM((tm, D_MODEL), F32),
        pltpu.VMEM((tm, D_MODEL), F32),
        pltpu.VMEM((GATE_ROWS, tm), F32),
        pltpu.VMEM((nseg, HIST_ROWS + seg, D_POOL), F32),
        pltpu.VMEM((tm, D_MODEL), BF16),
    ]
    bias = jnp.broadcast_to(weights["gate_bias"][:, None], (GATE_ROWS, tm))
    x1, c1, n1, m1, hist1 = pl.pallas_call(
        functools.partial(_mixer_kernel, tm=tm, seg=seg, carry=carry, pos0=pos0),
        grid=(ntiles,),
        in_specs=in_specs,
        out_specs=out_specs,
        out_shape=out_shape,
        scratch_shapes=scratch,
        compiler_params=pltpu.CompilerParams(
            dimension_semantics=("arbitrary",), vmem_limit_bytes=V7X_VMEM_LIMIT_BYTES),
        name="mixer_carry" if carry else "mixer_streams",
    )(x, weights["g_mix"], weights["w_qkv"], weights["w_rest"], weights["wg_t"], bias, weights["g_head"], weights["w_grp"],
      weights["pool_scale"], weights["w_ba"], weights["w_bb"], weights["w_out"], c0, n0, m0, hist0)
    n1 = n1.reshape(n_streams, N_HEADS, HEAD_DIM)
    m1 = m1.reshape(n_streams, N_HEADS, LANES)[:, :, 0]
    return x1, c1, n1, m1, hist1[:, 1:, :]


def _ffn(x, weights, *, tm):
    t = x.shape[0]
    return pl.pallas_call(
        functools.partial(_ffn_kernel, sub=256),
        grid=(t // tm,),
        in_specs=[
            pl.BlockSpec((tm, D_MODEL), lambda i: (i, 0)),
            _resident((1, D_MODEL)),
            _resident((D_MODEL, D_FF)),
            _resident((D_FF, D_MODEL)),
            _resident((1, D_MODEL)),
        ],
        out_specs=pl.BlockSpec((tm, D_MODEL), lambda i: (i, 0)),
        out_shape=jax.ShapeDtypeStruct((t, D_MODEL), F32),
        compiler_params=pltpu.CompilerParams(
            dimension_semantics=("arbitrary",), vmem_limit_bytes=V7X_VMEM_LIMIT_BYTES),
        name="ffn",
    )(x, weights["g_ffn"], weights["w_up"], weights["w_down"], weights["g_final"])


def _pack_weights(w_in, b_igate, b_fgate, g_norm_mix, g_head, w_pool_grp, pool_scale, w_branch_mlstm,
                  w_branch_pool, w_out, g_norm_ffn, w_up, w_down, g_final):
    d = D_MODEL
    w_in_t = w_in.T
    wi_t = w_in_t[_QKV_ROWS:_QKV_ROWS + N_HEADS]
    wf_t = w_in_t[_QKV_ROWS + N_HEADS:_REST0]
    return {
        "w_qkv": w_in_t[0:_QKV_ROWS].astype(BF16),
        "w_rest": w_in_t[_REST0:_REST0 + _REST_ROWS].astype(BF16),
        "wg_t": jnp.concatenate([wi_t, wf_t, wf_t, wi_t], axis=0),
        "gate_bias": jnp.concatenate([b_igate, b_fgate, b_fgate, b_igate]).astype(F32),
        "g_mix": g_norm_mix.reshape(1, d),
        "g_head": g_head.reshape(1, d),
        "w_grp": w_pool_grp.astype(BF16),
        "pool_scale": pool_scale.reshape(1, D_POOL),
        "w_ba": w_branch_mlstm.astype(BF16),
        "w_bb": w_branch_pool.astype(BF16),
        "w_out": w_out.astype(BF16),
        "g_ffn": g_norm_ffn.reshape(1, d),
        "w_up": w_up.astype(BF16),
        "w_down": w_down.astype(BF16),
        "g_final": g_final.reshape(1, d),
    }


def kernel(x_prompt, x_sample, state_C, state_n, state_m, state_pool, w_in, b_igate, b_fgate, g_norm_mix, g_head,
           w_pool_grp, pool_scale, w_branch_mlstm, w_branch_pool, w_out, g_norm_ffn, w_up, w_down, g_final):
    depth = w_in.shape[0]
    bp, sp, d = x_prompt.shape
    bs, ss, _ = x_sample.shape
    assert depth == 1 and bp == 1 and d == D_MODEL
    hp = x_prompt.reshape(bp * sp, d)
    hs = x_sample.reshape(bs * ss, d)
    weights = _pack_weights(w_in[0], b_igate[0], b_fgate[0], g_norm_mix[0], g_head[0], w_pool_grp[0], pool_scale[0],
                            w_branch_mlstm[0], w_branch_pool[0], w_out[0], g_norm_ffn[0], w_up[0], w_down[0],
                            g_final)
    zero_state = (jnp.zeros((bp, N_HEADS, HEAD_DIM, HEAD_DIM), F32), jnp.zeros((bp, D_MODEL), F32),
                  jnp.zeros((bp, N_HEADS), F32), jnp.zeros((bp, HIST_ROWS - 1, D_POOL), F32))
    hp, cp, np_, mp, pp = _mixer(hp, weights, zero_state, tm=256, seg=256, carry=True, pos0=0)
    sample_state = (state_C[0], state_n[0].reshape(bs, D_MODEL), state_m[0], state_pool[0])
    hs, cs, ns, ms, ps = _mixer(hs, weights, sample_state, tm=4 * ss, seg=ss, carry=False, pos0=PAST_LEN)
    y_prompt = _ffn(hp, weights, tm=512).reshape(bp, sp, d)
    y_sample = _ffn(hs, weights, tm=512).reshape(bs, ss, d)
    return (y_prompt, y_sample, cp[None], np_[None], mp[None], pp[None], cs[None], ns[None], ms[None], ps[None])
```

```python
import collections
import functools

import jax
import jax.numpy as jnp
from jax import lax
from jax.experimental import pallas as pl
from jax.experimental.pallas import tpu as pltpu

F32 = jnp.float32
BF16 = jnp.bfloat16

D_MODEL = 1024
N_HEADS = 4
HEAD_DIM = 256
D_POOL = 512
POOL_GROUP = 128
POOL_WINDOWS = (2, 4, 8, 16)
HIST_ROWS = 16
D_FF = 4096
EPS = 1e-6
PAST_LEN = 2048
LANES = 128
GATE_ROWS = 16
V7X_VMEM_LIMIT_BYTES = 56 * 1024 * 1024

_K0, _V0, _QKV_ROWS = 1024, 2048, 3072
_REST0 = _QKV_ROWS + 2 * N_HEADS
_P0, _GA0, _GB0, _REST_ROWS = 1024, 1536, 2560, 3584

_Slot = collections.namedtuple("_Slot", "q v kt og ga gb gate abuf")
_Weights = collections.namedtuple(
    "_Weights", "gmix wqkv wrest wkg bias ghead wgrp pscale wba wbb wout")
_CARRY_ORDER = "FPFPPFPFPFP"


def _dot(a, b):
    return jnp.dot(a, b, preferred_element_type=F32)


def _dot_nt(a, b):
    return lax.dot_general(a, b, (((1,), (1,)), ((), ())), preferred_element_type=F32)


def _rmsnorm(x, g):
    return x * lax.rsqrt(jnp.mean(x * x, axis=-1, keepdims=True) + EPS) * g


def _sigmoid(x):
    return 1.0 / (1.0 + jnp.exp(-x))


def _log_sigmoid(x):
    return jnp.minimum(x, 0.0) - jnp.log1p(jnp.exp(-jnp.abs(x)))


def _masks(tm, seg):
    row = lax.broadcasted_iota(jnp.int32, (tm, tm), 0)
    col = lax.broadcasted_iota(jnp.int32, (tm, tm), 1)
    if tm == seg:
        return col <= row, row <= col
    shift = seg.bit_length() - 1
    same = (row >> shift) == (col >> shift)
    return (col <= row) & same, (row <= col) & same


def _init_wkg(wkg_s, wqkv_ref, wgt_ref):
    wkg_s[0:D_MODEL, :] = wqkv_ref[_K0:_V0, :]
    wkg_s[D_MODEL:D_MODEL + GATE_ROWS, :] = wgt_ref[...].astype(BF16)


def _project(x, w, slot, *, tm, seg):
    nseg = tm // seg
    u = _rmsnorm(x, w.gmix[...]).astype(BF16)
    kg = _dot_nt(w.wkg[...], u)
    slot.kt[...] = kg[0:D_MODEL] * (HEAD_DIM ** -0.5)
    gates = kg[D_MODEL:D_MODEL + GATE_ROWS] + w.bias[...]
    yield
    slot.q[...] = _dot_nt(u, w.wqkv[0:_K0, :]).astype(BF16)
    yield
    slot.v[...] = _dot_nt(u, w.wqkv[_V0:_QKV_ROWS, :]).astype(BF16)
    yield
    slot.og[...] = _dot_nt(u, w.wrest[0:_P0, :])
    yield
    p = _dot_nt(u, w.wrest[_P0:_GA0, :])
    for j in range(nseg):
        slot.abuf[j, HIST_ROWS:HIST_ROWS + seg, :] = p[j * seg:(j + 1) * seg]
    slot.ga[...] = _dot_nt(u, w.wrest[_GA0:_GB0, :])
    yield
    slot.gb[...] = _dot_nt(u, w.wrest[_GB0:_REST_ROWS, :])
    _, upper = _masks(tm, seg)
    lf = _log_sigmoid(gates[8:16])
    hi = lf.astype(BF16).astype(F32)
    r1 = lf - hi
    mid = r1.astype(BF16).astype(F32)
    lo = r1 - mid
    pieces = jnp.concatenate([hi, mid, lo, jnp.zeros_like(lo)], axis=0).astype(BF16)
    cs = _dot(pieces, jnp.where(upper, 1.0, 0.0).astype(BF16))
    b = cs[0:8] + cs[8:16] + cs[16:24]
    slot.gate[0:8, :] = gates[0:8] - b
    slot.gate[8:16, :] = lf
    yield


def _finish(x, out_ref, rows, w, slot, hcat_s, c_rd, n_rd, m_rd, c_out, n_out, m_out, *, tm, seg, pos):
    nseg = tm // seg
    causal, _ = _masks(tm, seg)

    heads = []
    for h in range(N_HEADS):
        sl = slice(h * HEAD_DIM, (h + 1) * HEAD_DIM)
        a_m = jnp.where(causal, jnp.broadcast_to(slot.gate[h:h + 1, :], (tm, tm)), -jnp.inf)
        mprev = jnp.concatenate(
            [jnp.broadcast_to(m_rd[j:j + 1, h * LANES:h * LANES + 1], (seg, 1)) for j in range(nseg)], axis=0)
        g = jnp.maximum(jnp.max(a_m, axis=-1, keepdims=True), mprev)
        dm = jnp.exp(a_m - g)
        lf_b = jnp.broadcast_to(slot.gate[8 + h:9 + h, :], (tm, tm))
        bcol = jnp.sum(jnp.where(causal, lf_b, 0.0), axis=-1, keepdims=True)
        mcol = bcol + g
        wi = jnp.exp(mprev - g)

        qh = slot.q[:, sl]
        vh = slot.v[:, sl]
        kth = slot.kt[sl, :]
        kth_bf = kth.astype(BF16)
        s = _dot(qh, kth_bf) * dm
        qf = qh.astype(F32)
        qc_parts, qn_parts = [], []
        for j in range(nseg):
            rs = slice(j * seg, (j + 1) * seg)
            qc_parts.append(_dot(qh[rs], c_rd[j, h].astype(BF16)))
            qn_parts.append(jnp.sum(qf[rs] * n_rd[j:j + 1, sl], axis=-1, keepdims=True))
        qc = jnp.concatenate(qc_parts, axis=0)
        qn = jnp.concatenate(qn_parts, axis=0)
        heads.append((sl, dm, mcol, wi, vh, kth, kth_bf, s, qc, qn))
    yield

    for h, (sl, dm, mcol, wi, vh, kth, kth_bf, s, qc, qn) in enumerate(heads):
        rowsum = jnp.sum(s, axis=-1, keepdims=True)
        sv = _dot(s.astype(BF16), vh)
        num = wi * qc + sv
        den = wi * qn + rowsum
        hh = num / jnp.maximum(jnp.abs(den), jnp.exp(-mcol))
        hh = hh * lax.rsqrt(jnp.mean(hh * hh, axis=-1, keepdims=True) + EPS) * w.ghead[:, sl]
        hh = hh * _sigmoid(slot.og[:, sl])
        hcat_s[:, sl] = hh.astype(BF16)

        wrows, n_decayed = [], []
        for j in range(nseg):
            r = (j + 1) * seg - 1
            w_row = dm[r:r + 1, :]
            wrows.append(w_row)
            decay = wi[r:r + 1, :]
            kw = (kth * w_row).astype(BF16)
            c_out[j, h] = decay * c_rd[j, h] + _dot(kw, vh)
            n_decayed.append(decay * n_rd[j:j + 1, sl])
            m_out[j:j + 1, h * LANES:(h + 1) * LANES] = jnp.broadcast_to(mcol[r:r + 1, :], (1, LANES))
        ridx = lax.broadcasted_iota(jnp.int32, (GATE_ROWS, tm), 0)
        wmat = jnp.zeros((GATE_ROWS, tm), F32)
        for j in range(nseg):
            wmat = jnp.where(ridx == j, jnp.broadcast_to(wrows[j], (GATE_ROWS, tm)), wmat)
        nupd = _dot_nt(wmat.astype(BF16), kth_bf)
        for j in range(nseg):
            n_out[j:j + 1, sl] = n_decayed[j] + nupd[j:j + 1, :]
    yield

    grp_out = []
    for gi, win in enumerate(POOL_WINDOWS):
        ls = slice(gi * POOL_GROUP, (gi + 1) * POOL_GROUP)
        cnt = jnp.minimum(pos + 1, win).astype(F32)
        dparts = []
        for j in range(nseg):
            tok = slot.abuf[j, HIST_ROWS:HIST_ROWS + seg, ls]
            tot = tok
            for k in range(1, win):
                tot = tot + slot.abuf[j, HIST_ROWS - k:HIST_ROWS - k + seg, ls]
            dparts.append(tot / cnt - tok)
        d = jnp.concatenate(dparts, axis=0).astype(BF16)
        grp_out.append(_dot(d, w.wgrp[gi]))
    pooled = jnp.concatenate(grp_out, axis=-1) * w.pscale[...]
    yield

    branch_a = _dot(hcat_s[...], w.wba[...])
    branch_b = _dot(pooled.astype(BF16), w.wbb[...])
    yield
    mixed = _sigmoid(slot.ga[...]) * branch_a + _sigmoid(slot.gb[...]) * branch_b
    out_ref[rows, :] = x + _dot(mixed.astype(BF16), w.wout[...])
    yield


def _run(order, **gens):
    for name in order:
        next(gens[name], None)
    for gen in gens.values():
        for _ in gen:
            pass


def _carry_kernel(x_ref, xnext_ref, gmix_ref, wqkv_ref, wrest_ref, wgt_ref, bias_ref, ghead_ref, wgrp_ref,
                  pscale_ref, wba_ref, wbb_ref, wout_ref, c_in, n_in, m_in, hist_in,
                  x1_ref, c_out, n_out, m_out, hist_out,
                  wkg_s, hcat_s, *slot_refs, tm):
    i = pl.program_id(0)
    w = _Weights(gmix_ref, wqkv_ref, wrest_ref, wkg_s, bias_ref, ghead_ref, wgrp_ref, pscale_ref,
                 wba_ref, wbb_ref, wout_ref)
    nslot = len(_Slot._fields)
    slot_a, slot_b = _Slot(*slot_refs[:nslot]), _Slot(*slot_refs[nslot:])
    kw = dict(tm=tm, seg=tm)

    @pl.when(i == 0)
    def _():
        _init_wkg(wkg_s, wqkv_ref, wgt_ref)
        c_out[...] = c_in[...]
        n_out[...] = n_in[...]
        m_out[...] = m_in[...]
        hist_out[...] = hist_in[...]
        _run("", P=_project(x_ref[0:tm, :], w, slot_a, **kw))

    row = lax.broadcasted_iota(jnp.int32, (tm, POOL_GROUP), 0)
    state = (c_out, n_out, m_out)

    def finish_beside(rows, slot, tile, x_proj, slot_proj):
        slot.abuf[0, 0:HIST_ROWS, :] = hist_out[0]
        _run(_CARRY_ORDER,
             F=_finish(x_ref[rows, :], x1_ref, rows, w, slot, hcat_s, *state, *state, pos=row + tile * tm, **kw),
             P=_project(x_proj, w, slot_proj, **kw))
        hist_out[0] = slot.abuf[0, tm:tm + HIST_ROWS, :]

    finish_beside(slice(0, tm), slot_a, 2 * i, x_ref[tm:2 * tm, :], slot_b)
    finish_beside(slice(tm, 2 * tm), slot_b, 2 * i + 1, xnext_ref[...], slot_a)


def _streams_kernel(x_ref, gmix_ref, wqkv_ref, wrest_ref, wgt_ref, bias_ref, ghead_ref, wgrp_ref,
                    pscale_ref, wba_ref, wbb_ref, wout_ref, c_in, n_in, m_in, hist_in,
                    x1_ref, c_out, n_out, m_out, hist_out,
                    wkg_s, hcat_s, *slot_refs, tm, seg, pos0):
    i = pl.program_id(0)
    w = _Weights(gmix_ref, wqkv_ref, wrest_ref, wkg_s, bias_ref, ghead_ref, wgrp_ref, pscale_ref,
                 wba_ref, wbb_ref, wout_ref)
    slot = _Slot(*slot_refs)

    @pl.when(i == 0)
    def _():
        _init_wkg(wkg_s, wqkv_ref, wgt_ref)

    x = x_ref[...]
    _run("", P=_project(x, w, slot, tm=tm, seg=seg))
    for j in range(tm // seg):
        slot.abuf[j, 0:HIST_ROWS, :] = hist_in[j]
    pos = lax.broadcasted_iota(jnp.int32, (seg, POOL_GROUP), 0) + pos0
    _run("", F=_finish(x, x1_ref, slice(0, tm), w, slot, hcat_s, c_in, n_in, m_in, c_out, n_out, m_out,
                       tm=tm, seg=seg, pos=pos))
    for j in range(tm // seg):
        hist_out[j] = slot.abuf[j, seg:seg + HIST_ROWS, :]


def _ffn_kernel(x_ref, gffn_ref, wup_ref, wdown_ref, gfin_ref, y_ref, *, sub):
    for r in range(x_ref.shape[0] // sub):
        rows = slice(r * sub, (r + 1) * sub)
        x = x_ref[rows, :]
        u = _rmsnorm(x, gffn_ref[...]).astype(BF16)
        hid = _dot(u, wup_ref[...])
        act = jnp.square(jnp.maximum(hid, 0.0)).astype(BF16)
        x2 = x + _dot(act, wdown_ref[...])
        y_ref[rows, :] = _rmsnorm(x2, gfin_ref[...])


def _resident(shape):
    zeros = (0,) * len(shape)
    return pl.BlockSpec(shape, lambda i: zeros, pipeline_mode=pl.Buffered(1))


def _slot_scratch(tm, seg):
    return [
        pltpu.VMEM((tm, D_MODEL), BF16),
        pltpu.VMEM((tm, D_MODEL), BF16),
        pltpu.VMEM((D_MODEL, tm), F32),
        pltpu.VMEM((tm, D_MODEL), F32),
        pltpu.VMEM((tm, D_MODEL), F32),
        pltpu.VMEM((tm, D_MODEL), F32),
        pltpu.VMEM((GATE_ROWS, tm), F32),
        pltpu.VMEM((tm // seg, HIST_ROWS + seg, D_POOL), F32),
    ]


def _mixer(x, weights, state, *, tm, seg, carry, pos0):
    t = x.shape[0]
    nseg = tm // seg
    c0, n0, m0, hist0 = state
    n_streams = c0.shape[0]
    sblk = 1 if carry else nseg
    smap = (lambda i: 0) if carry else (lambda i: i)
    n0 = n0.reshape(n_streams // sblk, sblk, D_MODEL)
    m0 = jnp.repeat(m0, LANES, axis=-1).reshape(n_streams // sblk, sblk, N_HEADS * LANES)
    hist0 = jnp.pad(hist0, ((0, 0), (1, 0), (0, 0)))
    bias = jnp.broadcast_to(weights["gate_bias"][:, None], (GATE_ROWS, tm))

    if carry:
        assert pos0 == 0 and nseg == 1 and t % (2 * tm) == 0
        steps = t // (2 * tm)
        last_tile = t // tm - 1
        x_specs = [pl.BlockSpec((2 * tm, D_MODEL), lambda i: (i, 0)),
                   pl.BlockSpec((tm, D_MODEL), lambda i: (jnp.minimum(2 * i + 2, last_tile), 0))]
        x_args = [x, x]
        x1_spec = pl.BlockSpec((2 * tm, D_MODEL), lambda i: (i, 0))
        body = functools.partial(_carry_kernel, tm=tm)
        slots = _slot_scratch(tm, seg) + _slot_scratch(tm, seg)
    else:
        steps = t // tm
        x_specs = [pl.BlockSpec((tm, D_MODEL), lambda i: (i, 0))]
        x_args = [x]
        x1_spec = pl.BlockSpec((tm, D_MODEL), lambda i: (i, 0))
        body = functools.partial(_streams_kernel, tm=tm, seg=seg, pos0=pos0)
        slots = _slot_scratch(tm, seg)

    state_specs = [
        pl.BlockSpec((sblk, N_HEADS, HEAD_DIM, HEAD_DIM), lambda i: (smap(i), 0, 0, 0)),
        pl.BlockSpec((None, sblk, D_MODEL), lambda i: (smap(i), 0, 0)),
        pl.BlockSpec((None, sblk, N_HEADS * LANES), lambda i: (smap(i), 0, 0)),
        pl.BlockSpec((sblk, HIST_ROWS, D_POOL), lambda i: (smap(i), 0, 0)),
    ]
    in_specs = x_specs + [
        _resident((1, D_MODEL)),
        _resident((_QKV_ROWS, D_MODEL)),
        _resident((_REST_ROWS, D_MODEL)),
        _resident((GATE_ROWS, D_MODEL)),
        _resident((GATE_ROWS, tm)),
        _resident((1, D_MODEL)),
        _resident((len(POOL_WINDOWS), POOL_GROUP, POOL_GROUP)),
        _resident((1, D_POOL)),
        _resident((D_MODEL, D_MODEL)),
        _resident((D_POOL, D_MODEL)),
        _resident((D_MODEL, D_MODEL)),
    ] + state_specs
    out_shape = [
        jax.ShapeDtypeStruct((t, D_MODEL), F32),
        jax.ShapeDtypeStruct(c0.shape, F32),
        jax.ShapeDtypeStruct(n0.shape, F32),
        jax.ShapeDtypeStruct(m0.shape, F32),
        jax.ShapeDtypeStruct(hist0.shape, F32),
    ]
    scratch = [
        pltpu.VMEM((D_MODEL + GATE_ROWS, D_MODEL), BF16),
        pltpu.VMEM((tm, D_MODEL), BF16),
    ] + slots
    x1, c1, n1, m1, hist1 = pl.pallas_call(
        body,
        grid=(steps,),
        in_specs=in_specs,
        out_specs=[x1_spec] + state_specs,
        out_shape=out_shape,
        scratch_shapes=scratch,
        compiler_params=pltpu.CompilerParams(
            dimension_semantics=("arbitrary",), vmem_limit_bytes=V7X_VMEM_LIMIT_BYTES),
        name="mixer_carry" if carry else "mixer_streams",
    )(*x_args, weights["g_mix"], weights["w_qkv"], weights["w_rest"], weights["wg_t"], bias, weights["g_head"],
      weights["w_grp"], weights["pool_scale"], weights["w_ba"], weights["w_bb"], weights["w_out"], c0, n0, m0, hist0)
    n1 = n1.reshape(n_streams, N_HEADS, HEAD_DIM)
    m1 = m1.reshape(n_streams, N_HEADS, LANES)[:, :, 0]
    return x1, c1, n1, m1, hist1[:, 1:, :]


def _ffn(x, weights, *, tm):
    t = x.shape[0]
    return pl.pallas_call(
        functools.partial(_ffn_kernel, sub=256),
        grid=(t // tm,),
        in_specs=[
            pl.BlockSpec((tm, D_MODEL), lambda i: (i, 0)),
            _resident((1, D_MODEL)),
            _resident((D_MODEL, D_FF)),
            _resident((D_FF, D_MODEL)),
            _resident((1, D_MODEL)),
        ],
        out_specs=pl.BlockSpec((tm, D_MODEL), lambda i: (i, 0)),
        out_shape=jax.ShapeDtypeStruct((t, D_MODEL), F32),
        compiler_params=pltpu.CompilerParams(
            dimension_semantics=("arbitrary",), vmem_limit_bytes=V7X_VMEM_LIMIT_BYTES),
        name="ffn",
    )(x, weights["g_ffn"], weights["w_up"], weights["w_down"], weights["g_final"])


def _pack_weights(w_in, b_igate, b_fgate, g_norm_mix, g_head, w_pool_grp, pool_scale, w_branch_mlstm,
                  w_branch_pool, w_out, g_norm_ffn, w_up, w_down, g_final):
    d = D_MODEL
    w_in_t = w_in.T
    wi_t = w_in_t[_QKV_ROWS:_QKV_ROWS + N_HEADS]
    wf_t = w_in_t[_QKV_ROWS + N_HEADS:_REST0]
    return {
        "w_qkv": w_in_t[0:_QKV_ROWS].astype(BF16),
        "w_rest": w_in_t[_REST0:_REST0 + _REST_ROWS].astype(BF16),
        "wg_t": jnp.concatenate([wi_t, wf_t, wf_t, wi_t], axis=0),
        "gate_bias": jnp.concatenate([b_igate, b_fgate, b_fgate, b_igate]).astype(F32),
        "g_mix": g_norm_mix.reshape(1, d),
        "g_head": g_head.reshape(1, d),
        "w_grp": w_pool_grp.astype(BF16),
        "pool_scale": pool_scale.reshape(1, D_POOL),
        "w_ba": w_branch_mlstm.astype(BF16),
        "w_bb": w_branch_pool.astype(BF16),
        "w_out": w_out.astype(BF16),
        "g_ffn": g_norm_ffn.reshape(1, d),
        "w_up": w_up.astype(BF16),
        "w_down": w_down.astype(BF16),
        "g_final": g_final.reshape(1, d),
    }


def kernel(x_prompt, x_sample, state_C, state_n, state_m, state_pool, w_in, b_igate, b_fgate, g_norm_mix, g_head,
           w_pool_grp, pool_scale, w_branch_mlstm, w_branch_pool, w_out, g_norm_ffn, w_up, w_down, g_final):
    depth = w_in.shape[0]
    bp, sp, d = x_prompt.shape
    bs, ss, _ = x_sample.shape
    assert depth == 1 and bp == 1 and d == D_MODEL
    hp = x_prompt.reshape(bp * sp, d)
    hs = x_sample.reshape(bs * ss, d)
    weights = _pack_weights(w_in[0], b_igate[0], b_fgate[0], g_norm_mix[0], g_head[0], w_pool_grp[0], pool_scale[0],
                            w_branch_mlstm[0], w_branch_pool[0], w_out[0], g_norm_ffn[0], w_up[0], w_down[0],
                            g_final)
    zero_state = (jnp.zeros((bp, N_HEADS, HEAD_DIM, HEAD_DIM), F32), jnp.zeros((bp, D_MODEL), F32),
                  jnp.zeros((bp, N_HEADS), F32), jnp.zeros((bp, HIST_ROWS - 1, D_POOL), F32))
    hp, cp, np_, mp, pp = _mixer(hp, weights, zero_state, tm=256, seg=256, carry=True, pos0=0)
    sample_state = (state_C[0], state_n[0].reshape(bs, D_MODEL), state_m[0], state_pool[0])
    hs, cs, ns, ms, ps = _mixer(hs, weights, sample_state, tm=4 * ss, seg=ss, carry=False, pos0=PAST_LEN)
    y_prompt = _ffn(hp, weights, tm=512).reshape(bp, sp, d)
    y_sample = _ffn(hs, weights, tm=512).reshape(bs, ss, d)
    return (y_prompt, y_sample, cp[None], np_[None], mp[None], pp[None], cs[None], ns[None], ms[None], ps[None])
```

```python
import collections
import functools

import jax
import jax.numpy as jnp
from jax import lax
from jax.experimental import pallas as pl
from jax.experimental.pallas import tpu as pltpu

F32 = jnp.float32
BF16 = jnp.bfloat16

D_MODEL = 1024
N_HEADS = 4
HEAD_DIM = 256
D_POOL = 512
POOL_GROUP = 128
POOL_WINDOWS = (2, 4, 8, 16)
HIST_ROWS = 16
D_FF = 4096
EPS = 1e-6
PAST_LEN = 2048
LANES = 128
GATE_ROWS = 16
V7X_VMEM_LIMIT_BYTES = 56 * 1024 * 1024

_K0, _V0, _QKV_ROWS = 1024, 2048, 3072
_REST0 = _QKV_ROWS + 2 * N_HEADS
_P0, _GA0, _GB0, _REST_ROWS = 1024, 1536, 2560, 3584
_SPLIT_ROWS = 512

_Slot = collections.namedtuple("_Slot", "q v kt ktb og ga gb gate abuf")
_Weights = collections.namedtuple(
    "_Weights", "gmix wqkv wrest wkg bias ghead wgrp pscale wba wbb wout")
_CARRY_ORDER = "FPFPPFPFPFP"


def _dot(a, b):
    return jnp.dot(a, b, preferred_element_type=F32)


def _dot_nt(a, b):
    return lax.dot_general(a, b, (((1,), (1,)), ((), ())), preferred_element_type=F32)


def _rmsnorm(x, g):
    return x * lax.rsqrt(jnp.mean(x * x, axis=-1, keepdims=True) + EPS) * g


def _sigmoid(x):
    return 1.0 / (1.0 + jnp.exp(-x))


def _log_sigmoid(x):
    return jnp.minimum(x, 0.0) - jnp.log1p(jnp.exp(-jnp.abs(x)))


def _masks(tm, seg):
    row = lax.broadcasted_iota(jnp.int32, (tm, tm), 0)
    col = lax.broadcasted_iota(jnp.int32, (tm, tm), 1)
    if tm == seg:
        return col <= row, row <= col
    shift = seg.bit_length() - 1
    same = (row >> shift) == (col >> shift)
    return (col <= row) & same, (row <= col) & same


def _init_wkg(wkg_s, wqkv_ref, wgt_ref):
    wkg_s[0:D_MODEL, :] = wqkv_ref[_K0:_V0, :]
    wkg_s[D_MODEL:D_MODEL + GATE_ROWS, :] = wgt_ref[...].astype(BF16)


def _project(x, w, slot, *, tm, seg):
    nseg = tm // seg
    u = _rmsnorm(x, w.gmix[...]).astype(BF16)
    kg = _dot_nt(w.wkg[...], u)
    kt = kg[0:D_MODEL] * (HEAD_DIM ** -0.5)
    slot.kt[...] = kt
    slot.ktb[...] = kt.astype(BF16)
    gates = kg[D_MODEL:D_MODEL + GATE_ROWS] + w.bias[...]
    yield
    slot.q[...] = _dot_nt(u, w.wqkv[0:_K0, :]).astype(BF16)
    yield
    slot.v[...] = _dot_nt(u, w.wqkv[_V0:_QKV_ROWS, :]).astype(BF16)
    yield
    slot.og[...] = _dot_nt(u, w.wrest[0:_P0, :])
    yield
    p = _dot_nt(u, w.wrest[_P0:_GA0, :])
    for j in range(nseg):
        slot.abuf[j, HIST_ROWS:HIST_ROWS + seg, :] = p[j * seg:(j + 1) * seg]
    slot.ga[...] = _dot_nt(u, w.wrest[_GA0:_GB0, :])
    yield
    slot.gb[...] = _dot_nt(u, w.wrest[_GB0:_REST_ROWS, :])
    _, upper = _masks(tm, seg)
    lf = _log_sigmoid(gates[8:16])
    hi = lf.astype(BF16).astype(F32)
    r1 = lf - hi
    mid = r1.astype(BF16).astype(F32)
    lo = r1 - mid
    pieces = jnp.concatenate([hi, mid, lo, jnp.zeros_like(lo)], axis=0).astype(BF16)
    cs = _dot(pieces, jnp.where(upper, 1.0, 0.0).astype(BF16))
    b = cs[0:8] + cs[8:16] + cs[16:24]
    slot.gate[0:8, :] = gates[0:8] - b
    slot.gate[8:16, :] = lf
    yield


def _finish(x, out_ref, rows, w, slot, hcat_s, c_rd, n_rd, m_rd, c_out, n_out, m_out, *, tm, seg, pos):
    nseg = tm // seg
    causal, _ = _masks(tm, seg)

    heads = []
    for h in range(N_HEADS):
        sl = slice(h * HEAD_DIM, (h + 1) * HEAD_DIM)
        a_m = jnp.where(causal, jnp.broadcast_to(slot.gate[h:h + 1, :], (tm, tm)), -jnp.inf)
        mprev = jnp.concatenate(
            [jnp.broadcast_to(m_rd[j:j + 1, h * LANES:h * LANES + 1], (seg, 1)) for j in range(nseg)], axis=0)
        g = jnp.maximum(jnp.max(a_m, axis=-1, keepdims=True), mprev)
        dm = jnp.exp(a_m - g)
        lf_b = jnp.broadcast_to(slot.gate[8 + h:9 + h, :], (tm, tm))
        bcol = jnp.sum(jnp.where(causal, lf_b, 0.0), axis=-1, keepdims=True)
        mcol = bcol + g
        wi = jnp.exp(mprev - g)

        qh = slot.q[:, sl]
        vh = slot.v[:, sl]
        s = _dot(qh, slot.ktb[sl, :]) * dm
        qf = qh.astype(F32)
        qc_parts, qn_parts = [], []
        for j in range(nseg):
            rs = slice(j * seg, (j + 1) * seg)
            qc_parts.append(_dot(qh[rs], c_rd[j, h].astype(BF16)))
            qn_parts.append(jnp.sum(qf[rs] * n_rd[j:j + 1, sl], axis=-1, keepdims=True))
        qc = jnp.concatenate(qc_parts, axis=0)
        qn = jnp.concatenate(qn_parts, axis=0)
        heads.append((sl, dm, mcol, wi, vh, s, qc, qn))
    yield

    for h, (sl, dm, mcol, wi, vh, s, qc, qn) in enumerate(heads):
        kth = slot.kt[sl, :]
        rowsum = jnp.sum(s, axis=-1, keepdims=True)
        sv = _dot(s.astype(BF16), vh)
        num = wi * qc + sv
        den = wi * qn + rowsum
        hh = num / jnp.maximum(jnp.abs(den), jnp.exp(-mcol))
        hh = hh * lax.rsqrt(jnp.mean(hh * hh, axis=-1, keepdims=True) + EPS) * w.ghead[:, sl]
        hh = hh * _sigmoid(slot.og[:, sl])
        hcat_s[:, sl] = hh.astype(BF16)

        wrows, n_decayed = [], []
        for j in range(nseg):
            r = (j + 1) * seg - 1
            w_row = dm[r:r + 1, :]
            wrows.append(w_row)
            decay = wi[r:r + 1, :]
            kw = (kth * w_row).astype(BF16)
            c_out[j, h] = decay * c_rd[j, h] + _dot(kw, vh)
            n_decayed.append(decay * n_rd[j:j + 1, sl])
            m_out[j:j + 1, h * LANES:(h + 1) * LANES] = jnp.broadcast_to(mcol[r:r + 1, :], (1, LANES))
        ridx = lax.broadcasted_iota(jnp.int32, (GATE_ROWS, tm), 0)
        wmat = jnp.zeros((GATE_ROWS, tm), F32)
        for j in range(nseg):
            wmat = jnp.where(ridx == j, jnp.broadcast_to(wrows[j], (GATE_ROWS, tm)), wmat)
        nupd = _dot_nt(wmat.astype(BF16), slot.ktb[sl, :])
        for j in range(nseg):
            n_out[j:j + 1, sl] = n_decayed[j] + nupd[j:j + 1, :]
    yield

    grp_out = []
    for gi, win in enumerate(POOL_WINDOWS):
        ls = slice(gi * POOL_GROUP, (gi + 1) * POOL_GROUP)
        cnt = jnp.minimum(pos + 1, win).astype(F32)
        dparts = []
        for j in range(nseg):
            tok = slot.abuf[j, HIST_ROWS:HIST_ROWS + seg, ls]
            tot = tok
            for k in range(1, win):
                tot = tot + slot.abuf[j, HIST_ROWS - k:HIST_ROWS - k + seg, ls]
            dparts.append(tot / cnt - tok)
        d = jnp.concatenate(dparts, axis=0).astype(BF16)
        grp_out.append(_dot(d, w.wgrp[gi]))
    pooled = jnp.concatenate(grp_out, axis=-1) * w.pscale[...]
    yield

    branch_a = _dot(hcat_s[...], w.wba[...])
    branch_b = _dot(pooled.astype(BF16), w.wbb[...])
    yield
    mixed = _sigmoid(slot.ga[...]) * branch_a + _sigmoid(slot.gb[...]) * branch_b
    out_ref[rows, :] = x + _dot(mixed.astype(BF16), w.wout[...])
    yield


def _run(order, **gens):
    for name in order:
        next(gens[name], None)
    for gen in gens.values():
        for _ in gen:
            pass


def _carry_kernel(x_ref, xnext_ref, gmix_ref, wqkv_ref, wrest_ref, wgt_ref, bias_ref, ghead_ref, wgrp_ref,
                  pscale_ref, wba_ref, wbb_ref, wout_ref, c_in, n_in, m_in, hist_in,
                  x1_ref, c_out, n_out, m_out, hist_out,
                  wkg_s, hcat_s, *slot_refs, tm):
    i = pl.program_id(0)
    w = _Weights(gmix_ref, wqkv_ref, wrest_ref, wkg_s, bias_ref, ghead_ref, wgrp_ref, pscale_ref,
                 wba_ref, wbb_ref, wout_ref)
    nslot = len(_Slot._fields)
    slot_a, slot_b = _Slot(*slot_refs[:nslot]), _Slot(*slot_refs[nslot:])
    kw = dict(tm=tm, seg=tm)

    @pl.when(i == 0)
    def _():
        _init_wkg(wkg_s, wqkv_ref, wgt_ref)
        c_out[...] = c_in[...]
        n_out[...] = n_in[...]
        m_out[...] = m_in[...]
        hist_out[...] = hist_in[...]
        _run("", P=_project(x_ref[0:tm, :], w, slot_a, **kw))

    row = lax.broadcasted_iota(jnp.int32, (tm, POOL_GROUP), 0)
    state = (c_out, n_out, m_out)

    def finish_beside(rows, slot, tile, x_proj, slot_proj):
        slot.abuf[0, 0:HIST_ROWS, :] = hist_out[0]
        _run(_CARRY_ORDER,
             F=_finish(x_ref[rows, :], x1_ref, rows, w, slot, hcat_s, *state, *state, pos=row + tile * tm, **kw),
             P=_project(x_proj, w, slot_proj, **kw))
        hist_out[0] = slot.abuf[0, tm:tm + HIST_ROWS, :]

    finish_beside(slice(0, tm), slot_a, 2 * i, x_ref[tm:2 * tm, :], slot_b)
    finish_beside(slice(tm, 2 * tm), slot_b, 2 * i + 1, xnext_ref[...], slot_a)


def _streams_kernel(x_ref, gmix_ref, wqkv_ref, wrest_ref, wgt_ref, bias_ref, ghead_ref, wgrp_ref,
                    pscale_ref, wba_ref, wbb_ref, wout_ref, c_in, n_in, m_in, hist_in,
                    x1_ref, c_out, n_out, m_out, hist_out,
                    wkg_s, hcat_s, *slot_refs, tm, seg, pos0):
    i = pl.program_id(0)
    w = _Weights(gmix_ref, wqkv_ref, wrest_ref, wkg_s, bias_ref, ghead_ref, wgrp_ref, pscale_ref,
                 wba_ref, wbb_ref, wout_ref)
    slot = _Slot(*slot_refs)

    @pl.when(i == 0)
    def _():
        _init_wkg(wkg_s, wqkv_ref, wgt_ref)

    x = x_ref[...]
    _run("", P=_project(x, w, slot, tm=tm, seg=seg))
    for j in range(tm // seg):
        slot.abuf[j, 0:HIST_ROWS, :] = hist_in[j]
    pos = lax.broadcasted_iota(jnp.int32, (seg, POOL_GROUP), 0) + pos0
    _run("", F=_finish(x, x1_ref, slice(0, tm), w, slot, hcat_s, c_in, n_in, m_in, c_out, n_out, m_out,
                       tm=tm, seg=seg, pos=pos))
    for j in range(tm // seg):
        hist_out[j] = slot.abuf[j, seg:seg + HIST_ROWS, :]


def _ffn_kernel(x_ref, gffn_ref, wup_ref, wdown_ref, gfin_ref, y_ref, *, sub):
    for r in range(x_ref.shape[0] // sub):
        rows = slice(r * sub, (r + 1) * sub)
        x = x_ref[rows, :]
        u = _rmsnorm(x, gffn_ref[...]).astype(BF16)
        hid = _dot(u, wup_ref[...])
        act = jnp.square(jnp.maximum(hid, 0.0)).astype(BF16)
        x2 = x + _dot(act, wdown_ref[...])
        y_ref[rows, :] = _rmsnorm(x2, gfin_ref[...])


def _resident(shape):
    zeros = (0,) * len(shape)
    return pl.BlockSpec(shape, lambda i: zeros, pipeline_mode=pl.Buffered(1))


def _slot_scratch(tm, seg):
    return [
        pltpu.VMEM((tm, D_MODEL), BF16),
        pltpu.VMEM((tm, D_MODEL), BF16),
        pltpu.VMEM((D_MODEL, tm), F32),
        pltpu.VMEM((D_MODEL, tm), BF16),
        pltpu.VMEM((tm, D_MODEL), F32),
        pltpu.VMEM((tm, D_MODEL), F32),
        pltpu.VMEM((tm, D_MODEL), F32),
        pltpu.VMEM((GATE_ROWS, tm), F32),
        pltpu.VMEM((tm // seg, HIST_ROWS + seg, D_POOL), F32),
    ]


def _mixer(x, weights, state, *, tm, seg, carry, pos0):
    t = x.shape[0]
    nseg = tm // seg
    c0, n0, m0, hist0 = state
    n_streams = c0.shape[0]
    sblk = 1 if carry else nseg
    smap = (lambda i: 0) if carry else (lambda i: i)
    n0 = n0.reshape(n_streams // sblk, sblk, D_MODEL)
    m0 = jnp.repeat(m0, LANES, axis=-1).reshape(n_streams // sblk, sblk, N_HEADS * LANES)
    hist0 = jnp.pad(hist0, ((0, 0), (1, 0), (0, 0)))
    bias = jnp.broadcast_to(weights["gate_bias"][:, None], (GATE_ROWS, tm))

    if carry:
        assert pos0 == 0 and nseg == 1 and t % (2 * tm) == 0
        steps = t // (2 * tm)
        last_tile = t // tm - 1
        x_specs = [pl.BlockSpec((2 * tm, D_MODEL), lambda i: (i, 0)),
                   pl.BlockSpec((tm, D_MODEL), lambda i: (jnp.minimum(2 * i + 2, last_tile), 0))]
        x_args = [x, x]
        x1_spec = pl.BlockSpec((2 * tm, D_MODEL), lambda i: (i, 0))
        body = functools.partial(_carry_kernel, tm=tm)
        slots = _slot_scratch(tm, seg) + _slot_scratch(tm, seg)
    else:
        steps = t // tm
        x_specs = [pl.BlockSpec((tm, D_MODEL), lambda i: (i, 0))]
        x_args = [x]
        x1_spec = pl.BlockSpec((tm, D_MODEL), lambda i: (i, 0))
        body = functools.partial(_streams_kernel, tm=tm, seg=seg, pos0=pos0)
        slots = _slot_scratch(tm, seg)

    state_specs = [
        pl.BlockSpec((sblk, N_HEADS, HEAD_DIM, HEAD_DIM), lambda i: (smap(i), 0, 0, 0)),
        pl.BlockSpec((None, sblk, D_MODEL), lambda i: (smap(i), 0, 0)),
        pl.BlockSpec((None, sblk, N_HEADS * LANES), lambda i: (smap(i), 0, 0)),
        pl.BlockSpec((sblk, HIST_ROWS, D_POOL), lambda i: (smap(i), 0, 0)),
    ]
    in_specs = x_specs + [
        _resident((1, D_MODEL)),
        _resident((_QKV_ROWS, D_MODEL)),
        _resident((_REST_ROWS, D_MODEL)),
        _resident((GATE_ROWS, D_MODEL)),
        _resident((GATE_ROWS, tm)),
        _resident((1, D_MODEL)),
        _resident((len(POOL_WINDOWS), POOL_GROUP, POOL_GROUP)),
        _resident((1, D_POOL)),
        _resident((D_MODEL, D_MODEL)),
        _resident((D_POOL, D_MODEL)),
        _resident((D_MODEL, D_MODEL)),
    ] + state_specs
    out_shape = [
        jax.ShapeDtypeStruct((t, D_MODEL), F32),
        jax.ShapeDtypeStruct(c0.shape, F32),
        jax.ShapeDtypeStruct(n0.shape, F32),
        jax.ShapeDtypeStruct(m0.shape, F32),
        jax.ShapeDtypeStruct(hist0.shape, F32),
    ]
    scratch = [
        pltpu.VMEM((D_MODEL + GATE_ROWS, D_MODEL), BF16),
        pltpu.VMEM((tm, D_MODEL), BF16),
    ] + slots
    x1, c1, n1, m1, hist1 = pl.pallas_call(
        body,
        grid=(steps,),
        in_specs=in_specs,
        out_specs=[x1_spec] + state_specs,
        out_shape=out_shape,
        scratch_shapes=scratch,
        compiler_params=pltpu.CompilerParams(
            dimension_semantics=("arbitrary",), vmem_limit_bytes=V7X_VMEM_LIMIT_BYTES),
        name="mixer_carry" if carry else "mixer_streams",
    )(*x_args, weights["g_mix"], weights["w_qkv"], weights["w_rest"], weights["wg_t"], bias, weights["g_head"],
      weights["w_grp"], weights["pool_scale"], weights["w_ba"], weights["w_bb"], weights["w_out"], c0, n0, m0, hist0)
    n1 = n1.reshape(n_streams, N_HEADS, HEAD_DIM)
    m1 = m1.reshape(n_streams, N_HEADS, LANES)[:, :, 0]
    return x1, c1, n1, m1, hist1[:, 1:, :]


def _ffn(x, weights, *, tm):
    t = x.shape[0]
    return pl.pallas_call(
        functools.partial(_ffn_kernel, sub=256),
        grid=(t // tm,),
        in_specs=[
            pl.BlockSpec((tm, D_MODEL), lambda i: (i, 0)),
            _resident((1, D_MODEL)),
            _resident((D_MODEL, D_FF)),
            _resident((D_FF, D_MODEL)),
            _resident((1, D_MODEL)),
        ],
        out_specs=pl.BlockSpec((tm, D_MODEL), lambda i: (i, 0)),
        out_shape=jax.ShapeDtypeStruct((t, D_MODEL), F32),
        compiler_params=pltpu.CompilerParams(
            dimension_semantics=("arbitrary",), vmem_limit_bytes=V7X_VMEM_LIMIT_BYTES),
        name="ffn",
    )(x, weights["g_ffn"], weights["w_up"], weights["w_down"], weights["g_final"])


def _split_w_in_kernel(w_ref, wqkv_ref, wrest_ref, wg_ref, prev_s):
    i = pl.program_id(0)
    n_qkv = _QKV_ROWS // _SPLIT_ROWS
    cur = w_ref[...]

    @pl.when(i < n_qkv)
    def _():
        wqkv_ref[...] = cur.astype(BF16)

    @pl.when(i == n_qkv)
    def _():
        wg_ref[...] = cur[0:2 * N_HEADS, :]

    @pl.when(i > n_qkv)
    def _():
        wrest_ref[...] = jnp.concatenate([prev_s[2 * N_HEADS:, :], cur[0:2 * N_HEADS, :]], axis=0).astype(BF16)

    prev_s[...] = cur


def _split_w_in(w_in_t):
    n_in = w_in_t.shape[0]
    assert n_in == _REST0 + _REST_ROWS and _QKV_ROWS % _SPLIT_ROWS == 0 and _REST_ROWS % _SPLIT_ROWS == 0
    n_qkv, n_rest = _QKV_ROWS // _SPLIT_ROWS, _REST_ROWS // _SPLIT_ROWS
    return pl.pallas_call(
        _split_w_in_kernel,
        grid=(n_qkv + 1 + n_rest,),
        in_specs=[pl.BlockSpec((_SPLIT_ROWS, D_MODEL), lambda i: (i, 0))],
        out_specs=[
            pl.BlockSpec((_SPLIT_ROWS, D_MODEL), lambda i: (jnp.minimum(i, n_qkv - 1), 0)),
            pl.BlockSpec((_SPLIT_ROWS, D_MODEL), lambda i: (jnp.clip(i - n_qkv - 1, 0, n_rest - 1), 0)),
            pl.BlockSpec((2 * N_HEADS, D_MODEL), lambda i: (0, 0)),
        ],
        out_shape=[
            jax.ShapeDtypeStruct((_QKV_ROWS, D_MODEL), BF16),
            jax.ShapeDtypeStruct((_REST_ROWS, D_MODEL), BF16),
            jax.ShapeDtypeStruct((2 * N_HEADS, D_MODEL), F32),
        ],
        scratch_shapes=[pltpu.VMEM((_SPLIT_ROWS, D_MODEL), F32)],
        compiler_params=pltpu.CompilerParams(dimension_semantics=("arbitrary",)),
        name="split_w_in",
    )(w_in_t)


def _pack_weights(w_in, b_igate, b_fgate, g_norm_mix, g_head, w_pool_grp, pool_scale, w_branch_mlstm,
                  w_branch_pool, w_out, g_norm_ffn, w_up, w_down, g_final):
    d = D_MODEL
    w_qkv, w_rest, wg = _split_w_in(w_in.T)
    wi_t, wf_t = wg[0:N_HEADS], wg[N_HEADS:2 * N_HEADS]
    return {
        "w_qkv": w_qkv,
        "w_rest": w_rest,
        "wg_t": jnp.concatenate([wi_t, wf_t, wf_t, wi_t], axis=0),
        "gate_bias": jnp.concatenate([b_igate, b_fgate, b_fgate, b_igate]).astype(F32),
        "g_mix": g_norm_mix.reshape(1, d),
        "g_head": g_head.reshape(1, d),
        "w_grp": w_pool_grp.astype(BF16),
        "pool_scale": pool_scale.reshape(1, D_POOL),
        "w_ba": w_branch_mlstm.astype(BF16),
        "w_bb": w_branch_pool.astype(BF16),
        "w_out": w_out.astype(BF16),
        "g_ffn": g_norm_ffn.reshape(1, d),
        "w_up": w_up.astype(BF16),
        "w_down": w_down.astype(BF16),
        "g_final": g_final.reshape(1, d),
    }


def kernel(x_prompt, x_sample, state_C, state_n, state_m, state_pool, w_in, b_igate, b_fgate, g_norm_mix, g_head,
           w_pool_grp, pool_scale, w_branch_mlstm, w_branch_pool, w_out, g_norm_ffn, w_up, w_down, g_final):
    depth = w_in.shape[0]
    bp, sp, d = x_prompt.shape
    bs, ss, _ = x_sample.shape
    assert depth == 1 and bp == 1 and d == D_MODEL
    hp = x_prompt.reshape(bp * sp, d)
    hs = x_sample.reshape(bs * ss, d)
    weights = _pack_weights(w_in[0], b_igate[0], b_fgate[0], g_norm_mix[0], g_head[0], w_pool_grp[0], pool_scale[0],
                            w_branch_mlstm[0], w_branch_pool[0], w_out[0], g_norm_ffn[0], w_up[0], w_down[0],
                            g_final)
    zero_state = (jnp.zeros((bp, N_HEADS, HEAD_DIM, HEAD_DIM), F32), jnp.zeros((bp, D_MODEL), F32),
                  jnp.zeros((bp, N_HEADS), F32), jnp.zeros((bp, HIST_ROWS - 1, D_POOL), F32))
    hp, cp, np_, mp, pp = _mixer(hp, weights, zero_state, tm=256, seg=256, carry=True, pos0=0)
    sample_state = (state_C[0], state_n[0].reshape(bs, D_MODEL), state_m[0], state_pool[0])
    hs, cs, ns, ms, ps = _mixer(hs, weights, sample_state, tm=4 * ss, seg=ss, carry=False, pos0=PAST_LEN)
    y_prompt = _ffn(hp, weights, tm=1024).reshape(bp, sp, d)
    y_sample = _ffn(hs, weights, tm=512).reshape(bs, ss, d)
    return (y_prompt, y_sample, cp[None], np_[None], mp[None], pp[None], cs[None], ns[None], ms[None], ps[None])
```

```python
import collections
import functools

import jax
import jax.numpy as jnp
from jax import lax
from jax.experimental import pallas as pl
from jax.experimental.pallas import tpu as pltpu

F32 = jnp.float32
BF16 = jnp.bfloat16

D_MODEL = 1024
N_HEADS = 4
HEAD_DIM = 256
D_POOL = 512
POOL_GROUP = 128
POOL_WINDOWS = (2, 4, 8, 16)
HIST_ROWS = 16
D_FF = 4096
EPS = 1e-6
PAST_LEN = 2048
LANES = 128
GATE_ROWS = 16
STATE_RING_SLOTS = 8
V7X_VMEM_LIMIT_BYTES = 56 * 1024 * 1024

_K0, _V0, _QKV_ROWS = 1024, 2048, 3072
_REST0 = _QKV_ROWS + 2 * N_HEADS
_P0, _GA0, _GB0, _REST_ROWS = 1024, 1536, 2560, 3584
_SPLIT_ROWS = 512

_Slot = collections.namedtuple("_Slot", "q v kt ktb og ga gb gate abuf")
_Weights = collections.namedtuple(
    "_Weights", "gmix wqkv wrest wkg bias ghead wgrp pscale wba wbb wout")
_CARRY_ORDER = "FPFPPFPPFPFP"
_STREAMS_ORDER = "PPPPFPFPFPFF"


def _dot(a, b):
    return jnp.dot(a, b, preferred_element_type=F32)


def _dot_nt(a, b):
    return lax.dot_general(a, b, (((1,), (1,)), ((), ())), preferred_element_type=F32)


def _rmsnorm(x, g):
    return x * lax.rsqrt(jnp.mean(x * x, axis=-1, keepdims=True) + EPS) * g


def _sigmoid(x):
    return 1.0 / (1.0 + jnp.exp(-x))


def _log_sigmoid(x):
    return jnp.minimum(x, 0.0) - jnp.log1p(jnp.exp(-jnp.abs(x)))


def _masks(tm, seg):
    row = lax.broadcasted_iota(jnp.int32, (tm, tm), 0)
    col = lax.broadcasted_iota(jnp.int32, (tm, tm), 1)
    if tm == seg:
        return col <= row, row <= col
    shift = seg.bit_length() - 1
    same = (row >> shift) == (col >> shift)
    return (col <= row) & same, (row <= col) & same


def _init_wkg(wkg_s, wqkv_ref, wgt_ref):
    wkg_s[0:D_MODEL, :] = wqkv_ref[_K0:_V0, :]
    wkg_s[D_MODEL:D_MODEL + GATE_ROWS, :] = wgt_ref[...].astype(BF16)


def _project(x, w, slot, *, tm, seg):
    nseg = tm // seg
    u = _rmsnorm(x, w.gmix[...]).astype(BF16)
    kg = _dot_nt(w.wkg[...], u)
    kt = kg[0:D_MODEL] * (HEAD_DIM ** -0.5)
    slot.kt[...] = kt
    slot.ktb[...] = kt.astype(BF16)
    gates = kg[D_MODEL:D_MODEL + GATE_ROWS] + w.bias[...]
    yield
    slot.q[...] = _dot_nt(u, w.wqkv[0:_K0, :]).astype(BF16)
    yield
    slot.v[...] = _dot_nt(u, w.wqkv[_V0:_QKV_ROWS, :]).astype(BF16)
    yield
    p = _dot_nt(u, w.wrest[_P0:_GA0, :])
    for j in range(nseg):
        slot.abuf[j, HIST_ROWS:HIST_ROWS + seg, :] = p[j * seg:(j + 1) * seg]
    _, upper = _masks(tm, seg)
    lf = _log_sigmoid(gates[8:16])
    hi = lf.astype(BF16).astype(F32)
    r1 = lf - hi
    mid = r1.astype(BF16).astype(F32)
    lo = r1 - mid
    pieces = jnp.concatenate([hi, mid, lo, jnp.zeros_like(lo)], axis=0).astype(BF16)
    cs = _dot(pieces, jnp.where(upper, 1.0, 0.0).astype(BF16))
    b = cs[0:8] + cs[8:16] + cs[16:24]
    slot.gate[0:8, :] = gates[0:8] - b
    slot.gate[8:16, :] = lf
    yield
    slot.og[...] = _dot_nt(u, w.wrest[0:_P0, :])
    yield
    slot.ga[...] = _dot_nt(u, w.wrest[_GA0:_GB0, :])
    yield
    slot.gb[...] = _dot_nt(u, w.wrest[_GB0:_REST_ROWS, :])
    yield


def _finish(x, out_ref, rows, w, slot, hcat_s, c_rd, n_rd, m_rd, c_store, n_out, m_out, *, tm, seg, pos):
    nseg = tm // seg
    causal, _ = _masks(tm, seg)

    heads = []
    for h in range(N_HEADS):
        sl = slice(h * HEAD_DIM, (h + 1) * HEAD_DIM)
        a_m = jnp.where(causal, jnp.broadcast_to(slot.gate[h:h + 1, :], (tm, tm)), -jnp.inf)
        mprev = jnp.concatenate(
            [jnp.broadcast_to(m_rd[j:j + 1, h * LANES:h * LANES + 1], (seg, 1)) for j in range(nseg)], axis=0)
        g = jnp.maximum(jnp.max(a_m, axis=-1, keepdims=True), mprev)
        dm = jnp.exp(a_m - g)
        lf_b = jnp.broadcast_to(slot.gate[8 + h:9 + h, :], (tm, tm))
        bcol = jnp.sum(jnp.where(causal, lf_b, 0.0), axis=-1, keepdims=True)
        mcol = bcol + g
        wi = jnp.exp(mprev - g)

        qh = slot.q[:, sl]
        vh = slot.v[:, sl]
        s = _dot(qh, slot.ktb[sl, :]) * dm
        qf = qh.astype(F32)
        qc_parts, qn_parts = [], []
        for j in range(nseg):
            rs = slice(j * seg, (j + 1) * seg)
            qc_parts.append(_dot(qh[rs], c_rd[j, h].astype(BF16)))
            qn_parts.append(jnp.sum(qf[rs] * n_rd[j:j + 1, sl], axis=-1, keepdims=True))
        qc = jnp.concatenate(qc_parts, axis=0)
        qn = jnp.concatenate(qn_parts, axis=0)
        heads.append((sl, dm, mcol, wi, vh, s, qc, qn))
    yield

    for h, (sl, dm, mcol, wi, vh, s, qc, qn) in enumerate(heads):
        kth = slot.kt[sl, :]
        rowsum = jnp.sum(s, axis=-1, keepdims=True)
        sv = _dot(s.astype(BF16), vh)
        num = wi * qc + sv
        den = wi * qn + rowsum
        hh = num / jnp.maximum(jnp.abs(den), jnp.exp(-mcol))
        hh = hh * lax.rsqrt(jnp.mean(hh * hh, axis=-1, keepdims=True) + EPS) * w.ghead[:, sl]
        hh = hh * _sigmoid(slot.og[:, sl])
        hcat_s[:, sl] = hh.astype(BF16)

        wrows, n_decayed = [], []
        for j in range(nseg):
            r = (j + 1) * seg - 1
            w_row = dm[r:r + 1, :]
            wrows.append(w_row)
            decay = wi[r:r + 1, :]
            kw = (kth * w_row).astype(BF16)
            c_store(j, h, decay * c_rd[j, h] + _dot(kw, vh))
            n_decayed.append(decay * n_rd[j:j + 1, sl])
            m_out[j:j + 1, h * LANES:(h + 1) * LANES] = jnp.broadcast_to(mcol[r:r + 1, :], (1, LANES))
        ridx = lax.broadcasted_iota(jnp.int32, (GATE_ROWS, tm), 0)
        wmat = jnp.zeros((GATE_ROWS, tm), F32)
        for j in range(nseg):
            wmat = jnp.where(ridx == j, jnp.broadcast_to(wrows[j], (GATE_ROWS, tm)), wmat)
        nupd = _dot_nt(wmat.astype(BF16), slot.ktb[sl, :])
        for j in range(nseg):
            n_out[j:j + 1, sl] = n_decayed[j] + nupd[j:j + 1, :]
    yield

    grp_out = []
    for gi, win in enumerate(POOL_WINDOWS):
        ls = slice(gi * POOL_GROUP, (gi + 1) * POOL_GROUP)
        cnt = jnp.minimum(pos + 1, win).astype(F32)
        dparts = []
        for j in range(nseg):
            tok = slot.abuf[j, HIST_ROWS:HIST_ROWS + seg, ls]
            tot = tok
            for k in range(1, win):
                tot = tot + slot.abuf[j, HIST_ROWS - k:HIST_ROWS - k + seg, ls]
            dparts.append(tot / cnt - tok)
        d = jnp.concatenate(dparts, axis=0).astype(BF16)
        grp_out.append(_dot(d, w.wgrp[gi]))
    pooled = jnp.concatenate(grp_out, axis=-1) * w.pscale[...]
    yield

    branch_a = _dot(hcat_s[...], w.wba[...])
    branch_b = _dot(pooled.astype(BF16), w.wbb[...])
    yield
    mixed = _sigmoid(slot.ga[...]) * branch_a + _sigmoid(slot.gb[...]) * branch_b
    out_ref[rows, :] = x + _dot(mixed.astype(BF16), w.wout[...])
    yield


def _run(order, **gens):
    for name in order:
        next(gens[name], None)
    for gen in gens.values():
        for _ in gen:
            pass


class _StateRing:
    def __init__(self, dst_hbm, ring, sems, first_stream):
        self.dst, self.ring, self.sems, self.first = dst_hbm, ring, sems, first_stream
        self.count = 0
        self.in_flight = {}

    def store(self, j, h, value):
        slot = self.count % self.ring.shape[0]
        self.count += 1
        if slot in self.in_flight:
            self.drain()
        self.ring[slot] = value
        copy = pltpu.make_async_copy(self.ring.at[slot], self.dst.at[self.first + j, h], self.sems.at[slot])
        copy.start()
        self.in_flight[slot] = copy

    def drain(self):
        for slot in sorted(self.in_flight):
            self.in_flight.pop(slot).wait()


def _carry_kernel(x_ref, xnext_ref, gmix_ref, wqkv_ref, wrest_ref, wgt_ref, bias_ref, ghead_ref, wgrp_ref,
                  pscale_ref, wba_ref, wbb_ref, wout_ref, c_in, n_in, m_in, hist_in,
                  x1_ref, c_out, n_out, m_out, hist_out,
                  wkg_s, hcat_s, *slot_refs, tm):
    i = pl.program_id(0)
    w = _Weights(gmix_ref, wqkv_ref, wrest_ref, wkg_s, bias_ref, ghead_ref, wgrp_ref, pscale_ref,
                 wba_ref, wbb_ref, wout_ref)
    nslot = len(_Slot._fields)
    slot_a, slot_b = _Slot(*slot_refs[:nslot]), _Slot(*slot_refs[nslot:])
    kw = dict(tm=tm, seg=tm)

    @pl.when(i == 0)
    def _():
        _init_wkg(wkg_s, wqkv_ref, wgt_ref)
        c_out[...] = c_in[...]
        n_out[...] = n_in[...]
        m_out[...] = m_in[...]
        hist_out[...] = hist_in[...]
        _run("", P=_project(x_ref[0:tm, :], w, slot_a, **kw))

    row = lax.broadcasted_iota(jnp.int32, (tm, POOL_GROUP), 0)
    state = (c_out, n_out, m_out)

    def c_store(j, h, value):
        c_out[j, h] = value

    def finish_beside(rows, slot, tile, x_proj, slot_proj):
        slot.abuf[0, 0:HIST_ROWS, :] = hist_out[0]
        _run(_CARRY_ORDER,
             F=_finish(x_ref[rows, :], x1_ref, rows, w, slot, hcat_s, *state, c_store, n_out, m_out,
                       pos=row + tile * tm, **kw),
             P=_project(x_proj, w, slot_proj, **kw))
        hist_out[0] = slot.abuf[0, tm:tm + HIST_ROWS, :]

    finish_beside(slice(0, tm), slot_a, 2 * i, x_ref[tm:2 * tm, :], slot_b)
    finish_beside(slice(tm, 2 * tm), slot_b, 2 * i + 1, xnext_ref[...], slot_a)


def _streams_kernel(x_ref, gmix_ref, wqkv_ref, wrest_ref, wgt_ref, bias_ref, ghead_ref, wgrp_ref,
                    pscale_ref, wba_ref, wbb_ref, wout_ref, c_in, n_in, m_in, hist_in,
                    x1_ref, c_hbm, n_out, m_out, hist_out,
                    wkg_s, hcat_s, cring_s, cring_sem, *slot_refs, tm, seg, pos0):
    i = pl.program_id(0)
    nseg = tm // seg
    w = _Weights(gmix_ref, wqkv_ref, wrest_ref, wkg_s, bias_ref, ghead_ref, wgrp_ref, pscale_ref,
                 wba_ref, wbb_ref, wout_ref)
    slot = _Slot(*slot_refs)

    @pl.when(i == 0)
    def _():
        _init_wkg(wkg_s, wqkv_ref, wgt_ref)

    x = x_ref[...]
    for j in range(nseg):
        slot.abuf[j, 0:HIST_ROWS, :] = hist_in[j]
    pos = lax.broadcasted_iota(jnp.int32, (seg, POOL_GROUP), 0) + pos0
    ring = _StateRing(c_hbm, cring_s, cring_sem, first_stream=i * nseg)
    _run(_STREAMS_ORDER,
         P=_project(x, w, slot, tm=tm, seg=seg),
         F=_finish(x, x1_ref, slice(0, tm), w, slot, hcat_s, c_in, n_in, m_in, ring.store, n_out, m_out,
                   tm=tm, seg=seg, pos=pos))
    ring.drain()
    for j in range(nseg):
        hist_out[j] = slot.abuf[j, seg:seg + HIST_ROWS, :]


def _ffn_kernel(x_ref, gffn_ref, wup_ref, wdown_ref, gfin_ref, y_ref, *, sub):
    for r in range(x_ref.shape[0] // sub):
        rows = slice(r * sub, (r + 1) * sub)
        x = x_ref[rows, :]
        u = _rmsnorm(x, gffn_ref[...]).astype(BF16)
        hid = _dot(u, wup_ref[...])
        act = jnp.square(jnp.maximum(hid, 0.0)).astype(BF16)
        x2 = x + _dot(act, wdown_ref[...])
        y_ref[rows, :] = _rmsnorm(x2, gfin_ref[...])


def _resident(shape):
    zeros = (0,) * len(shape)
    return pl.BlockSpec(shape, lambda i: zeros, pipeline_mode=pl.Buffered(1))


def _slot_scratch(tm, seg):
    return [
        pltpu.VMEM((tm, D_MODEL), BF16),
        pltpu.VMEM((tm, D_MODEL), BF16),
        pltpu.VMEM((D_MODEL, tm), F32),
        pltpu.VMEM((D_MODEL, tm), BF16),
        pltpu.VMEM((tm, D_MODEL), F32),
        pltpu.VMEM((tm, D_MODEL), F32),
        pltpu.VMEM((tm, D_MODEL), F32),
        pltpu.VMEM((GATE_ROWS, tm), F32),
        pltpu.VMEM((tm // seg, HIST_ROWS + seg, D_POOL), F32),
    ]


def _mixer(x, weights, state, *, tm, seg, carry, pos0):
    t = x.shape[0]
    nseg = tm // seg
    c0, n0, m0, hist0 = state
    n_streams = c0.shape[0]
    sblk = 1 if carry else nseg
    smap = (lambda i: 0) if carry else (lambda i: i)
    n0 = n0.reshape(n_streams // sblk, sblk, D_MODEL)
    m0 = jnp.repeat(m0, LANES, axis=-1).reshape(n_streams // sblk, sblk, N_HEADS * LANES)
    hist0 = jnp.pad(hist0, ((0, 0), (1, 0), (0, 0)))
    bias = jnp.broadcast_to(weights["gate_bias"][:, None], (GATE_ROWS, tm))

    if carry:
        assert pos0 == 0 and nseg == 1 and t % (2 * tm) == 0
        steps = t // (2 * tm)
        last_tile = t // tm - 1
        x_specs = [pl.BlockSpec((2 * tm, D_MODEL), lambda i: (i, 0)),
                   pl.BlockSpec((tm, D_MODEL), lambda i: (jnp.minimum(2 * i + 2, last_tile), 0))]
        x_args = [x, x]
        x1_spec = pl.BlockSpec((2 * tm, D_MODEL), lambda i: (i, 0))
        body = functools.partial(_carry_kernel, tm=tm)
        slots = _slot_scratch(tm, seg) + _slot_scratch(tm, seg)
    else:
        steps = t // tm
        x_specs = [pl.BlockSpec((tm, D_MODEL), lambda i: (i, 0))]
        x_args = [x]
        x1_spec = pl.BlockSpec((tm, D_MODEL), lambda i: (i, 0))
        body = functools.partial(_streams_kernel, tm=tm, seg=seg, pos0=pos0)
        slots = [pltpu.VMEM((STATE_RING_SLOTS, HEAD_DIM, HEAD_DIM), F32),
                 pltpu.SemaphoreType.DMA((STATE_RING_SLOTS,))] + _slot_scratch(tm, seg)

    state_specs = [
        pl.BlockSpec((sblk, N_HEADS, HEAD_DIM, HEAD_DIM), lambda i: (smap(i), 0, 0, 0)),
        pl.BlockSpec((None, sblk, D_MODEL), lambda i: (smap(i), 0, 0)),
        pl.BlockSpec((None, sblk, N_HEADS * LANES), lambda i: (smap(i), 0, 0)),
        pl.BlockSpec((sblk, HIST_ROWS, D_POOL), lambda i: (smap(i), 0, 0)),
    ]
    in_specs = x_specs + [
        _resident((1, D_MODEL)),
        _resident((_QKV_ROWS, D_MODEL)),
        _resident((_REST_ROWS, D_MODEL)),
        _resident((GATE_ROWS, D_MODEL)),
        _resident((GATE_ROWS, tm)),
        _resident((1, D_MODEL)),
        _resident((len(POOL_WINDOWS), POOL_GROUP, POOL_GROUP)),
        _resident((1, D_POOL)),
        _resident((D_MODEL, D_MODEL)),
        _resident((D_POOL, D_MODEL)),
        _resident((D_MODEL, D_MODEL)),
    ] + state_specs
    out_shape = [
        jax.ShapeDtypeStruct((t, D_MODEL), F32),
        jax.ShapeDtypeStruct(c0.shape, F32),
        jax.ShapeDtypeStruct(n0.shape, F32),
        jax.ShapeDtypeStruct(m0.shape, F32),
        jax.ShapeDtypeStruct(hist0.shape, F32),
    ]
    scratch = [
        pltpu.VMEM((D_MODEL + GATE_ROWS, D_MODEL), BF16),
        pltpu.VMEM((tm, D_MODEL), BF16),
    ] + slots
    x1, c1, n1, m1, hist1 = pl.pallas_call(
        body,
        grid=(steps,),
        in_specs=in_specs,
        out_specs=[x1_spec, state_specs[0] if carry else pl.BlockSpec(memory_space=pl.ANY)] + state_specs[1:],
        out_shape=out_shape,
        scratch_shapes=scratch,
        compiler_params=pltpu.CompilerParams(
            dimension_semantics=("arbitrary",), vmem_limit_bytes=V7X_VMEM_LIMIT_BYTES),
        name="mixer_carry" if carry else "mixer_streams",
    )(*x_args, weights["g_mix"], weights["w_qkv"], weights["w_rest"], weights["wg_t"], bias, weights["g_head"],
      weights["w_grp"], weights["pool_scale"], weights["w_ba"], weights["w_bb"], weights["w_out"], c0, n0, m0, hist0)
    n1 = n1.reshape(n_streams, N_HEADS, HEAD_DIM)
    m1 = m1.reshape(n_streams, N_HEADS, LANES)[:, :, 0]
    return x1, c1, n1, m1, hist1[:, 1:, :]


def _ffn(x, weights, *, tm):
    t = x.shape[0]
    return pl.pallas_call(
        functools.partial(_ffn_kernel, sub=256),
        grid=(t // tm,),
        in_specs=[
            pl.BlockSpec((tm, D_MODEL), lambda i: (i, 0)),
            _resident((1, D_MODEL)),
            _resident((D_MODEL, D_FF)),
            _resident((D_FF, D_MODEL)),
            _resident((1, D_MODEL)),
        ],
        out_specs=pl.BlockSpec((tm, D_MODEL), lambda i: (i, 0)),
        out_shape=jax.ShapeDtypeStruct((t, D_MODEL), F32),
        compiler_params=pltpu.CompilerParams(
            dimension_semantics=("arbitrary",), vmem_limit_bytes=V7X_VMEM_LIMIT_BYTES),
        name="ffn",
    )(x, weights["g_ffn"], weights["w_up"], weights["w_down"], weights["g_final"])


def _split_w_in_kernel(w_ref, wqkv_ref, wrest_ref, wg_ref, prev_s):
    i = pl.program_id(0)
    n_qkv = _QKV_ROWS // _SPLIT_ROWS
    cur = w_ref[...]

    @pl.when(i < n_qkv)
    def _():
        wqkv_ref[...] = cur.astype(BF16)

    @pl.when(i == n_qkv)
    def _():
        wg_ref[...] = cur[0:2 * N_HEADS, :]

    @pl.when(i > n_qkv)
    def _():
        wrest_ref[...] = jnp.concatenate([prev_s[2 * N_HEADS:, :], cur[0:2 * N_HEADS, :]], axis=0).astype(BF16)

    prev_s[...] = cur


def _split_w_in(w_in_t):
    n_in = w_in_t.shape[0]
    assert n_in == _REST0 + _REST_ROWS and _QKV_ROWS % _SPLIT_ROWS == 0 and _REST_ROWS % _SPLIT_ROWS == 0
    n_qkv, n_rest = _QKV_ROWS // _SPLIT_ROWS, _REST_ROWS // _SPLIT_ROWS
    return pl.pallas_call(
        _split_w_in_kernel,
        grid=(n_qkv + 1 + n_rest,),
        in_specs=[pl.BlockSpec((_SPLIT_ROWS, D_MODEL), lambda i: (i, 0))],
        out_specs=[
            pl.BlockSpec((_SPLIT_ROWS, D_MODEL), lambda i: (jnp.minimum(i, n_qkv - 1), 0)),
            pl.BlockSpec((_SPLIT_ROWS, D_MODEL), lambda i: (jnp.clip(i - n_qkv - 1, 0, n_rest - 1), 0)),
            pl.BlockSpec((2 * N_HEADS, D_MODEL), lambda i: (0, 0)),
        ],
        out_shape=[
            jax.ShapeDtypeStruct((_QKV_ROWS, D_MODEL), BF16),
            jax.ShapeDtypeStruct((_REST_ROWS, D_MODEL), BF16),
            jax.ShapeDtypeStruct((2 * N_HEADS, D_MODEL), F32),
        ],
        scratch_shapes=[pltpu.VMEM((_SPLIT_ROWS, D_MODEL), F32)],
        compiler_params=pltpu.CompilerParams(dimension_semantics=("arbitrary",)),
        name="split_w_in",
    )(w_in_t)


def _pack_weights(w_in, b_igate, b_fgate, g_norm_mix, g_head, w_pool_grp, pool_scale, w_branch_mlstm,
                  w_branch_pool, w_out, g_norm_ffn, w_up, w_down, g_final):
    d = D_MODEL
    w_qkv, w_rest, wg = _split_w_in(w_in.T)
    wi_t, wf_t = wg[0:N_HEADS], wg[N_HEADS:2 * N_HEADS]
    return {
        "w_qkv": w_qkv,
        "w_rest": w_rest,
        "wg_t": jnp.concatenate([wi_t, wf_t, wf_t, wi_t], axis=0),
        "gate_bias": jnp.concatenate([b_igate, b_fgate, b_fgate, b_igate]).astype(F32),
        "g_mix": g_norm_mix.reshape(1, d),
        "g_head": g_head.reshape(1, d),
        "w_grp": w_pool_grp.astype(BF16),
        "pool_scale": pool_scale.reshape(1, D_POOL),
        "w_ba": w_branch_mlstm.astype(BF16),
        "w_bb": w_branch_pool.astype(BF16),
        "w_out": w_out.astype(BF16),
        "g_ffn": g_norm_ffn.reshape(1, d),
        "w_up": w_up.astype(BF16),
        "w_down": w_down.astype(BF16),
        "g_final": g_final.reshape(1, d),
    }


def kernel(x_prompt, x_sample, state_C, state_n, state_m, state_pool, w_in, b_igate, b_fgate, g_norm_mix, g_head,
           w_pool_grp, pool_scale, w_branch_mlstm, w_branch_pool, w_out, g_norm_ffn, w_up, w_down, g_final):
    depth = w_in.shape[0]
    bp, sp, d = x_prompt.shape
    bs, ss, _ = x_sample.shape
    assert depth == 1 and bp == 1 and d == D_MODEL
    hp = x_prompt.reshape(bp * sp, d)
    hs = x_sample.reshape(bs * ss, d)
    weights = _pack_weights(w_in[0], b_igate[0], b_fgate[0], g_norm_mix[0], g_head[0], w_pool_grp[0], pool_scale[0],
                            w_branch_mlstm[0], w_branch_pool[0], w_out[0], g_norm_ffn[0], w_up[0], w_down[0],
                            g_final)
    zero_state = (jnp.zeros((bp, N_HEADS, HEAD_DIM, HEAD_DIM), F32), jnp.zeros((bp, D_MODEL), F32),
                  jnp.zeros((bp, N_HEADS), F32), jnp.zeros((bp, HIST_ROWS - 1, D_POOL), F32))
    hp, cp, np_, mp, pp = _mixer(hp, weights, zero_state, tm=256, seg=256, carry=True, pos0=0)
    sample_state = (state_C[0], state_n[0].reshape(bs, D_MODEL), state_m[0], state_pool[0])
    hs, cs, ns, ms, ps = _mixer(hs, weights, sample_state, tm=8 * ss, seg=ss, carry=False, pos0=PAST_LEN)
    y_prompt = _ffn(hp, weights, tm=1024).reshape(bp, sp, d)
    y_sample = _ffn(hs, weights, tm=512).reshape(bs, ss, d)
    return (y_prompt, y_sample, cp[None], np_[None], mp[None], pp[None], cs[None], ns[None], ms[None], ps[None])
```

```python
import collections
import functools

import jax
import jax.numpy as jnp
from jax import lax
from jax.experimental import pallas as pl
from jax.experimental.pallas import tpu as pltpu

F32 = jnp.float32
BF16 = jnp.bfloat16

D_MODEL = 1024
N_HEADS = 4
HEAD_DIM = 256
D_POOL = 512
POOL_GROUP = 128
POOL_WINDOWS = (2, 4, 8, 16)
HIST_ROWS = 16
POOL_T0 = 32
D_FF = 4096
EPS = 1e-6
PAST_LEN = 2048
LANES = 128
GATE_ROWS = 16
STATE_RING_SLOTS = 8
V7X_VMEM_LIMIT_BYTES = 56 * 1024 * 1024

_K0, _V0, _QKV_ROWS = 1024, 2048, 3072
_REST0 = _QKV_ROWS + 2 * N_HEADS
_P0, _GA0, _GB0, _REST_ROWS = 1024, 1536, 2560, 3584
_SPLIT_ROWS = 512

_Slot = collections.namedtuple("_Slot", "q v kt ktb og ga gb gate abuf pd")
_Weights = collections.namedtuple(
    "_Weights", "gmix wqkv wrest wkg bias ghead wgrp pscale wba wbb wout")
_CARRY_ORDER = "FPFPPFPPFPFP"
_STREAMS_ORDER = "PPPPFPFPFPFF"


def _dot(a, b):
    return jnp.dot(a, b, preferred_element_type=F32)


def _dot_nt(a, b):
    return lax.dot_general(a, b, (((1,), (1,)), ((), ())), preferred_element_type=F32)


def _rmsnorm(x, g):
    return x * lax.rsqrt(jnp.mean(x * x, axis=-1, keepdims=True) + EPS) * g


def _sigmoid(x):
    return 1.0 / (1.0 + jnp.exp(-x))


def _log_sigmoid(x):
    return jnp.minimum(x, 0.0) - jnp.log1p(jnp.exp(-jnp.abs(x)))


def _masks(tm, seg):
    row = lax.broadcasted_iota(jnp.int32, (tm, tm), 0)
    col = lax.broadcasted_iota(jnp.int32, (tm, tm), 1)
    if tm == seg:
        return col <= row, row <= col
    shift = seg.bit_length() - 1
    same = (row >> shift) == (col >> shift)
    return (col <= row) & same, (row <= col) & same


def _init_scratch(wkg_s, wqkv_ref, wgt_ref, *slots):
    wkg_s[0:D_MODEL, :] = wqkv_ref[_K0:_V0, :]
    wkg_s[D_MODEL:D_MODEL + GATE_ROWS, :] = wgt_ref[...].astype(BF16)
    for slot in slots:
        slot.abuf[:, 0:POOL_T0 - HIST_ROWS, :] = jnp.zeros((slot.abuf.shape[0], POOL_T0 - HIST_ROWS, D_POOL), F32)


def _project(x, w, slot, hist, pos, *, tm, seg):
    nseg = tm // seg
    u = _rmsnorm(x, w.gmix[...]).astype(BF16)
    kg = _dot_nt(w.wkg[...], u)
    kt = kg[0:D_MODEL] * (HEAD_DIM ** -0.5)
    slot.kt[...] = kt
    slot.ktb[...] = kt.astype(BF16)
    gates = kg[D_MODEL:D_MODEL + GATE_ROWS] + w.bias[...]
    yield
    p = _dot_nt(u, w.wrest[_P0:_GA0, :])
    for j in range(nseg):
        slot.abuf[j, POOL_T0 - HIST_ROWS:POOL_T0, :] = hist[j]
        slot.abuf[j, POOL_T0:POOL_T0 + seg, :] = p[j * seg:(j + 1) * seg]
    for gi, win in enumerate(POOL_WINDOWS):
        ls = slice(gi * POOL_GROUP, (gi + 1) * POOL_GROUP)
        cnt = jnp.minimum(pos + 1, win).astype(F32)
        for j in range(nseg):
            start = POOL_T0 - 8 * gi
            tot = slot.abuf[j, start:POOL_T0 + seg, ls] + slot.abuf[j, start - 1:POOL_T0 + seg - 1, ls]
            shift = 2
            while shift < win:
                n = tot.shape[0]
                tot = tot[8:n] + tot[8 - shift:n - shift]
                shift *= 2
            tok = slot.abuf[j, POOL_T0:POOL_T0 + seg, ls]
            slot.pd[j * seg:(j + 1) * seg, ls] = (tot / cnt - tok).astype(BF16)
    yield
    slot.q[...] = _dot_nt(u, w.wqkv[0:_K0, :]).astype(BF16)
    _, upper = _masks(tm, seg)
    lf = _log_sigmoid(gates[8:16])
    hi = lf.astype(BF16).astype(F32)
    r1 = lf - hi
    mid = r1.astype(BF16).astype(F32)
    lo = r1 - mid
    pieces = jnp.concatenate([hi, mid, lo, jnp.zeros_like(lo)], axis=0).astype(BF16)
    cs = _dot(pieces, jnp.where(upper, 1.0, 0.0).astype(BF16))
    b = cs[0:8] + cs[8:16] + cs[16:24]
    slot.gate[0:8, :] = gates[0:8] - b
    slot.gate[8:16, :] = lf
    yield
    slot.v[...] = _dot_nt(u, w.wqkv[_V0:_QKV_ROWS, :]).astype(BF16)
    yield
    slot.og[...] = _dot_nt(u, w.wrest[0:_P0, :])
    yield
    slot.ga[...] = _dot_nt(u, w.wrest[_GA0:_GB0, :])
    yield
    slot.gb[...] = _dot_nt(u, w.wrest[_GB0:_REST_ROWS, :])
    yield


def _finish(x, out_ref, rows, w, slot, hcat_s, c_rd, n_rd, m_rd, c_store, n_out, m_out, *, tm, seg):
    nseg = tm // seg
    causal, _ = _masks(tm, seg)

    heads = []
    for h in range(N_HEADS):
        sl = slice(h * HEAD_DIM, (h + 1) * HEAD_DIM)
        a_m = jnp.where(causal, jnp.broadcast_to(slot.gate[h:h + 1, :], (tm, tm)), -jnp.inf)
        mprev = jnp.concatenate(
            [jnp.broadcast_to(m_rd[j:j + 1, h * LANES:h * LANES + 1], (seg, 1)) for j in range(nseg)], axis=0)
        g = jnp.maximum(jnp.max(a_m, axis=-1, keepdims=True), mprev)
        dm = jnp.exp(a_m - g)
        lf_b = jnp.broadcast_to(slot.gate[8 + h:9 + h, :], (tm, tm))
        bcol = jnp.sum(jnp.where(causal, lf_b, 0.0), axis=-1, keepdims=True)
        mcol = bcol + g
        wi = jnp.exp(mprev - g)

        qh = slot.q[:, sl]
        vh = slot.v[:, sl]
        s = _dot(qh, slot.ktb[sl, :]) * dm
        qf = qh.astype(F32)
        qc_parts, qn_parts = [], []
        for j in range(nseg):
            rs = slice(j * seg, (j + 1) * seg)
            qc_parts.append(_dot(qh[rs], c_rd[j, h].astype(BF16)))
            qn_parts.append(jnp.sum(qf[rs] * n_rd[j:j + 1, sl], axis=-1, keepdims=True))
        qc = jnp.concatenate(qc_parts, axis=0)
        qn = jnp.concatenate(qn_parts, axis=0)
        heads.append((sl, dm, mcol, wi, vh, s, qc, qn))
    yield

    for h, (sl, dm, mcol, wi, vh, s, qc, qn) in enumerate(heads):
        kth = slot.kt[sl, :]
        rowsum = jnp.sum(s, axis=-1, keepdims=True)
        sv = _dot(s.astype(BF16), vh)
        num = wi * qc + sv
        den = wi * qn + rowsum
        hh = num / jnp.maximum(jnp.abs(den), jnp.exp(-mcol))
        hh = hh * lax.rsqrt(jnp.mean(hh * hh, axis=-1, keepdims=True) + EPS) * w.ghead[:, sl]
        hh = hh * _sigmoid(slot.og[:, sl])
        hcat_s[:, sl] = hh.astype(BF16)

        wrows, n_decayed = [], []
        for j in range(nseg):
            r = (j + 1) * seg - 1
            w_row = dm[r:r + 1, :]
            wrows.append(w_row)
            decay = wi[r:r + 1, :]
            kw = (kth * w_row).astype(BF16)
            c_store(j, h, decay * c_rd[j, h] + _dot(kw, vh))
            n_decayed.append(decay * n_rd[j:j + 1, sl])
            m_out[j:j + 1, h * LANES:(h + 1) * LANES] = jnp.broadcast_to(mcol[r:r + 1, :], (1, LANES))
        ridx = lax.broadcasted_iota(jnp.int32, (GATE_ROWS, tm), 0)
        wmat = jnp.zeros((GATE_ROWS, tm), F32)
        for j in range(nseg):
            wmat = jnp.where(ridx == j, jnp.broadcast_to(wrows[j], (GATE_ROWS, tm)), wmat)
        nupd = _dot_nt(wmat.astype(BF16), slot.ktb[sl, :])
        for j in range(nseg):
            n_out[j:j + 1, sl] = n_decayed[j] + nupd[j:j + 1, :]
    yield

    grp_out = [_dot(slot.pd[:, gi * POOL_GROUP:(gi + 1) * POOL_GROUP], w.wgrp[gi])
               for gi in range(len(POOL_WINDOWS))]
    pooled = jnp.concatenate(grp_out, axis=-1) * w.pscale[...]
    yield

    branch_a = _dot(hcat_s[...], w.wba[...])
    branch_b = _dot(pooled.astype(BF16), w.wbb[...])
    yield
    mixed = _sigmoid(slot.ga[...]) * branch_a + _sigmoid(slot.gb[...]) * branch_b
    out_ref[rows, :] = x + _dot(mixed.astype(BF16), w.wout[...])
    yield


def _run(order, **gens):
    for name in order:
        next(gens[name], None)
    for gen in gens.values():
        for _ in gen:
            pass


class _StateRing:
    def __init__(self, dst_hbm, ring, sems, first_stream):
        self.dst, self.ring, self.sems, self.first = dst_hbm, ring, sems, first_stream
        self.count = 0
        self.in_flight = {}

    def store(self, j, h, value):
        slot = self.count % self.ring.shape[0]
        self.count += 1
        if slot in self.in_flight:
            self.drain()
        self.ring[slot] = value
        copy = pltpu.make_async_copy(self.ring.at[slot], self.dst.at[self.first + j, h], self.sems.at[slot])
        copy.start()
        self.in_flight[slot] = copy

    def drain(self):
        for slot in sorted(self.in_flight):
            self.in_flight.pop(slot).wait()


def _carry_kernel(x_ref, xnext_ref, gmix_ref, wqkv_ref, wrest_ref, wgt_ref, bias_ref, ghead_ref, wgrp_ref,
                  pscale_ref, wba_ref, wbb_ref, wout_ref, c_in, n_in, m_in, hist_in,
                  x1_ref, c_out, n_out, m_out, hist_out,
                  wkg_s, hcat_s, *slot_refs, tm):
    i = pl.program_id(0)
    w = _Weights(gmix_ref, wqkv_ref, wrest_ref, wkg_s, bias_ref, ghead_ref, wgrp_ref, pscale_ref,
                 wba_ref, wbb_ref, wout_ref)
    nslot = len(_Slot._fields)
    slot_a, slot_b = _Slot(*slot_refs[:nslot]), _Slot(*slot_refs[nslot:])
    kw = dict(tm=tm, seg=tm)

    row = lax.broadcasted_iota(jnp.int32, (tm, POOL_GROUP), 0)

    @pl.when(i == 0)
    def _():
        _init_scratch(wkg_s, wqkv_ref, wgt_ref, slot_a, slot_b)
        c_out[...] = c_in[...]
        n_out[...] = n_in[...]
        m_out[...] = m_in[...]
        _run("", P=_project(x_ref[0:tm, :], w, slot_a, [hist_in[0]], row, **kw))

    state = (c_out, n_out, m_out)

    def c_store(j, h, value):
        c_out[j, h] = value

    def finish_beside(rows, slot, tile, x_proj, slot_proj):
        new_hist = slot.abuf[0, POOL_T0 + tm - HIST_ROWS:POOL_T0 + tm, :]
        hist_out[0] = new_hist
        _run(_CARRY_ORDER,
             F=_finish(x_ref[rows, :], x1_ref, rows, w, slot, hcat_s, *state, c_store, n_out, m_out, **kw),
             P=_project(x_proj, w, slot_proj, [new_hist], row + (tile + 1) * tm, **kw))

    finish_beside(slice(0, tm), slot_a, 2 * i, x_ref[tm:2 * tm, :], slot_b)
    finish_beside(slice(tm, 2 * tm), slot_b, 2 * i + 1, xnext_ref[...], slot_a)


def _streams_kernel(x_ref, gmix_ref, wqkv_ref, wrest_ref, wgt_ref, bias_ref, ghead_ref, wgrp_ref,
                    pscale_ref, wba_ref, wbb_ref, wout_ref, c_in, n_in, m_in, hist_in,
                    x1_ref, c_hbm, n_out, m_out, hist_out,
                    wkg_s, hcat_s, cring_s, cring_sem, *slot_refs, tm, seg, pos0):
    i = pl.program_id(0)
    nseg = tm // seg
    w = _Weights(gmix_ref, wqkv_ref, wrest_ref, wkg_s, bias_ref, ghead_ref, wgrp_ref, pscale_ref,
                 wba_ref, wbb_ref, wout_ref)
    slot = _Slot(*slot_refs)

    @pl.when(i == 0)
    def _():
        _init_scratch(wkg_s, wqkv_ref, wgt_ref, slot)

    x = x_ref[...]
    pos = lax.broadcasted_iota(jnp.int32, (seg, POOL_GROUP), 0) + pos0
    ring = _StateRing(c_hbm, cring_s, cring_sem, first_stream=i * nseg)
    _run(_STREAMS_ORDER,
         P=_project(x, w, slot, [hist_in[j] for j in range(nseg)], pos, tm=tm, seg=seg),
         F=_finish(x, x1_ref, slice(0, tm), w, slot, hcat_s, c_in, n_in, m_in, ring.store, n_out, m_out,
                   tm=tm, seg=seg))
    ring.drain()
    for j in range(nseg):
        hist_out[j] = slot.abuf[j, POOL_T0 + seg - HIST_ROWS:POOL_T0 + seg, :]


def _ffn_kernel(x_ref, gffn_ref, wup_ref, wdown_ref, gfin_ref, y_ref, *, sub):
    for r in range(x_ref.shape[0] // sub):
        rows = slice(r * sub, (r + 1) * sub)
        x = x_ref[rows, :]
        u = _rmsnorm(x, gffn_ref[...]).astype(BF16)
        hid = _dot(u, wup_ref[...])
        act = jnp.square(jnp.maximum(hid, 0.0)).astype(BF16)
        x2 = x + _dot(act, wdown_ref[...])
        y_ref[rows, :] = _rmsnorm(x2, gfin_ref[...])


def _resident(shape):
    zeros = (0,) * len(shape)
    return pl.BlockSpec(shape, lambda i: zeros, pipeline_mode=pl.Buffered(1))


def _slot_scratch(tm, seg):
    return [
        pltpu.VMEM((tm, D_MODEL), BF16),
        pltpu.VMEM((tm, D_MODEL), BF16),
        pltpu.VMEM((D_MODEL, tm), F32),
        pltpu.VMEM((D_MODEL, tm), BF16),
        pltpu.VMEM((tm, D_MODEL), F32),
        pltpu.VMEM((tm, D_MODEL), F32),
        pltpu.VMEM((tm, D_MODEL), F32),
        pltpu.VMEM((GATE_ROWS, tm), F32),
        pltpu.VMEM((tm // seg, POOL_T0 + seg, D_POOL), F32),
        pltpu.VMEM((tm, D_POOL), BF16),
    ]


def _mixer(x, weights, state, *, tm, seg, carry, pos0):
    t = x.shape[0]
    nseg = tm // seg
    c0, n0, m0, hist0 = state
    n_streams = c0.shape[0]
    sblk = 1 if carry else nseg
    smap = (lambda i: 0) if carry else (lambda i: i)
    n0 = n0.reshape(n_streams // sblk, sblk, D_MODEL)
    m0 = jnp.repeat(m0, LANES, axis=-1).reshape(n_streams // sblk, sblk, N_HEADS * LANES)
    hist0 = jnp.pad(hist0, ((0, 0), (1, 0), (0, 0)))
    bias = jnp.broadcast_to(weights["gate_bias"][:, None], (GATE_ROWS, tm))

    if carry:
        assert pos0 == 0 and nseg == 1 and t % (2 * tm) == 0
        steps = t // (2 * tm)
        last_tile = t // tm - 1
        x_specs = [pl.BlockSpec((2 * tm, D_MODEL), lambda i: (i, 0)),
                   pl.BlockSpec((tm, D_MODEL), lambda i: (jnp.minimum(2 * i + 2, last_tile), 0))]
        x_args = [x, x]
        x1_spec = pl.BlockSpec((2 * tm, D_MODEL), lambda i: (i, 0))
        body = functools.partial(_carry_kernel, tm=tm)
        slots = _slot_scratch(tm, seg) + _slot_scratch(tm, seg)
    else:
        steps = t // tm
        x_specs = [pl.BlockSpec((tm, D_MODEL), lambda i: (i, 0))]
        x_args = [x]
        x1_spec = pl.BlockSpec((tm, D_MODEL), lambda i: (i, 0))
        body = functools.partial(_streams_kernel, tm=tm, seg=seg, pos0=pos0)
        slots = [pltpu.VMEM((STATE_RING_SLOTS, HEAD_DIM, HEAD_DIM), F32),
                 pltpu.SemaphoreType.DMA((STATE_RING_SLOTS,))] + _slot_scratch(tm, seg)

    state_specs = [
        pl.BlockSpec((sblk, N_HEADS, HEAD_DIM, HEAD_DIM), lambda i: (smap(i), 0, 0, 0)),
        pl.BlockSpec((None, sblk, D_MODEL), lambda i: (smap(i), 0, 0)),
        pl.BlockSpec((None, sblk, N_HEADS * LANES), lambda i: (smap(i), 0, 0)),
        pl.BlockSpec((sblk, HIST_ROWS, D_POOL), lambda i: (smap(i), 0, 0)),
    ]
    in_specs = x_specs + [
        _resident((1, D_MODEL)),
        _resident((_QKV_ROWS, D_MODEL)),
        _resident((_REST_ROWS, D_MODEL)),
        _resident((GATE_ROWS, D_MODEL)),
        _resident((GATE_ROWS, tm)),
        _resident((1, D_MODEL)),
        _resident((len(POOL_WINDOWS), POOL_GROUP, POOL_GROUP)),
        _resident((1, D_POOL)),
        _resident((D_MODEL, D_MODEL)),
        _resident((D_POOL, D_MODEL)),
        _resident((D_MODEL, D_MODEL)),
    ] + state_specs
    out_shape = [
        jax.ShapeDtypeStruct((t, D_MODEL), F32),
        jax.ShapeDtypeStruct(c0.shape, F32),
        jax.ShapeDtypeStruct(n0.shape, F32),
        jax.ShapeDtypeStruct(m0.shape, F32),
        jax.ShapeDtypeStruct(hist0.shape, F32),
    ]
    scratch = [
        pltpu.VMEM((D_MODEL + GATE_ROWS, D_MODEL), BF16),
        pltpu.VMEM((tm, D_MODEL), BF16),
    ] + slots
    x1, c1, n1, m1, hist1 = pl.pallas_call(
        body,
        grid=(steps,),
        in_specs=in_specs,
        out_specs=[x1_spec, state_specs[0] if carry else pl.BlockSpec(memory_space=pl.ANY)] + state_specs[1:],
        out_shape=out_shape,
        scratch_shapes=scratch,
        compiler_params=pltpu.CompilerParams(
            dimension_semantics=("arbitrary",), vmem_limit_bytes=V7X_VMEM_LIMIT_BYTES),
        name="mixer_carry" if carry else "mixer_streams",
    )(*x_args, weights["g_mix"], weights["w_qkv"], weights["w_rest"], weights["wg_t"], bias, weights["g_head"],
      weights["w_grp"], weights["pool_scale"], weights["w_ba"], weights["w_bb"], weights["w_out"], c0, n0, m0, hist0)
    n1 = n1.reshape(n_streams, N_HEADS, HEAD_DIM)
    m1 = m1.reshape(n_streams, N_HEADS, LANES)[:, :, 0]
    return x1, c1, n1, m1, hist1[:, 1:, :]


def _ffn(x, weights, *, tm):
    t = x.shape[0]
    return pl.pallas_call(
        functools.partial(_ffn_kernel, sub=256),
        grid=(t // tm,),
        in_specs=[
            pl.BlockSpec((tm, D_MODEL), lambda i: (i, 0)),
            _resident((1, D_MODEL)),
            _resident((D_MODEL, D_FF)),
            _resident((D_FF, D_MODEL)),
            _resident((1, D_MODEL)),
        ],
        out_specs=pl.BlockSpec((tm, D_MODEL), lambda i: (i, 0)),
        out_shape=jax.ShapeDtypeStruct((t, D_MODEL), F32),
        compiler_params=pltpu.CompilerParams(
            dimension_semantics=("arbitrary",), vmem_limit_bytes=V7X_VMEM_LIMIT_BYTES),
        name="ffn",
    )(x, weights["g_ffn"], weights["w_up"], weights["w_down"], weights["g_final"])


def _split_w_in_kernel(w_ref, wqkv_ref, wrest_ref, wg_ref, prev_s):
    i = pl.program_id(0)
    n_qkv = _QKV_ROWS // _SPLIT_ROWS
    cur = w_ref[...]

    @pl.when(i < n_qkv)
    def _():
        wqkv_ref[...] = cur.astype(BF16)

    @pl.when(i == n_qkv)
    def _():
        wg_ref[...] = cur[0:2 * N_HEADS, :]

    @pl.when(i > n_qkv)
    def _():
        wrest_ref[...] = jnp.concatenate([prev_s[2 * N_HEADS:, :], cur[0:2 * N_HEADS, :]], axis=0).astype(BF16)

    prev_s[...] = cur


def _split_w_in(w_in_t):
    n_in = w_in_t.shape[0]
    assert n_in == _REST0 + _REST_ROWS and _QKV_ROWS % _SPLIT_ROWS == 0 and _REST_ROWS % _SPLIT_ROWS == 0
    n_qkv, n_rest = _QKV_ROWS // _SPLIT_ROWS, _REST_ROWS // _SPLIT_ROWS
    return pl.pallas_call(
        _split_w_in_kernel,
        grid=(n_qkv + 1 + n_rest,),
        in_specs=[pl.BlockSpec((_SPLIT_ROWS, D_MODEL), lambda i: (i, 0))],
        out_specs=[
            pl.BlockSpec((_SPLIT_ROWS, D_MODEL), lambda i: (jnp.minimum(i, n_qkv - 1), 0)),
            pl.BlockSpec((_SPLIT_ROWS, D_MODEL), lambda i: (jnp.clip(i - n_qkv - 1, 0, n_rest - 1), 0)),
            pl.BlockSpec((2 * N_HEADS, D_MODEL), lambda i: (0, 0)),
        ],
        out_shape=[
            jax.ShapeDtypeStruct((_QKV_ROWS, D_MODEL), BF16),
            jax.ShapeDtypeStruct((_REST_ROWS, D_MODEL), BF16),
            jax.ShapeDtypeStruct((2 * N_HEADS, D_MODEL), F32),
        ],
        scratch_shapes=[pltpu.VMEM((_SPLIT_ROWS, D_MODEL), F32)],
        compiler_params=pltpu.CompilerParams(dimension_semantics=("arbitrary",)),
        name="split_w_in",
    )(w_in_t)


def _pack_weights(w_in, b_igate, b_fgate, g_norm_mix, g_head, w_pool_grp, pool_scale, w_branch_mlstm,
                  w_branch_pool, w_out, g_norm_ffn, w_up, w_down, g_final):
    d = D_MODEL
    w_qkv, w_rest, wg = _split_w_in(w_in.T)
    wi_t, wf_t = wg[0:N_HEADS], wg[N_HEADS:2 * N_HEADS]
    return {
        "w_qkv": w_qkv,
        "w_rest": w_rest,
        "wg_t": jnp.concatenate([wi_t, wf_t, wf_t, wi_t], axis=0),
        "gate_bias": jnp.concatenate([b_igate, b_fgate, b_fgate, b_igate]).astype(F32),
        "g_mix": g_norm_mix.reshape(1, d),
        "g_head": g_head.reshape(1, d),
        "w_grp": w_pool_grp.astype(BF16),
        "pool_scale": pool_scale.reshape(1, D_POOL),
        "w_ba": w_branch_mlstm.astype(BF16),
        "w_bb": w_branch_pool.astype(BF16),
        "w_out": w_out.astype(BF16),
        "g_ffn": g_norm_ffn.reshape(1, d),
        "w_up": w_up.astype(BF16),
        "w_down": w_down.astype(BF16),
        "g_final": g_final.reshape(1, d),
    }


def kernel(x_prompt, x_sample, state_C, state_n, state_m, state_pool, w_in, b_igate, b_fgate, g_norm_mix, g_head,
           w_pool_grp, pool_scale, w_branch_mlstm, w_branch_pool, w_out, g_norm_ffn, w_up, w_down, g_final):
    depth = w_in.shape[0]
    bp, sp, d = x_prompt.shape
    bs, ss, _ = x_sample.shape
    assert depth == 1 and bp == 1 and d == D_MODEL
    hp = x_prompt.reshape(bp * sp, d)
    hs = x_sample.reshape(bs * ss, d)
    weights = _pack_weights(w_in[0], b_igate[0], b_fgate[0], g_norm_mix[0], g_head[0], w_pool_grp[0], pool_scale[0],
                            w_branch_mlstm[0], w_branch_pool[0], w_out[0], g_norm_ffn[0], w_up[0], w_down[0],
                            g_final)
    zero_state = (jnp.zeros((bp, N_HEADS, HEAD_DIM, HEAD_DIM), F32), jnp.zeros((bp, D_MODEL), F32),
                  jnp.zeros((bp, N_HEADS), F32), jnp.zeros((bp, HIST_ROWS - 1, D_POOL), F32))
    hp, cp, np_, mp, pp = _mixer(hp, weights, zero_state, tm=256, seg=256, carry=True, pos0=0)
    sample_state = (state_C[0], state_n[0].reshape(bs, D_MODEL), state_m[0], state_pool[0])
    hs, cs, ns, ms, ps = _mixer(hs, weights, sample_state, tm=8 * ss, seg=ss, carry=False, pos0=PAST_LEN)
    y_prompt = _ffn(hp, weights, tm=1024).reshape(bp, sp, d)
    y_sample = _ffn(hs, weights, tm=512).reshape(bs, ss, d)
    return (y_prompt, y_sample, cp[None], np_[None], mp[None], pp[None], cs[None], ns[None], ms[None], ps[None])
```

```python
import collections
import functools

import jax
import jax.numpy as jnp
from jax import lax
from jax.experimental import pallas as pl
from jax.experimental.pallas import tpu as pltpu

F32 = jnp.float32
BF16 = jnp.bfloat16

D_MODEL = 1024
N_HEADS = 4
HEAD_DIM = 256
D_POOL = 512
POOL_GROUP = 128
POOL_WINDOWS = (2, 4, 8, 16)
HIST_ROWS = 16
POOL_T0 = 32
D_FF = 4096
EPS = 1e-6
PAST_LEN = 2048
LANES = 128
GATE_ROWS = 16
FFN_WEIGHT_STEPS = 8
STATE_RING_SLOTS = 8
V7X_VMEM_LIMIT_BYTES = 56 * 1024 * 1024

_K0, _V0, _QKV_ROWS = 1024, 2048, 3072
_REST0 = _QKV_ROWS + 2 * N_HEADS
_P0, _GA0, _GB0, _REST_ROWS = 1024, 1536, 2560, 3584
_SPLIT_ROWS = 512

_Slot = collections.namedtuple("_Slot", "q v kt ktb og ga gb gate abuf pd")
_Weights = collections.namedtuple(
    "_Weights", "gmix wqkv wrest wkg bias ghead wgrp pscale wba wbb wout")
_CARRY_ORDER = "FPFFFFPPFPPFPFP"
_STREAMS_ORDER = "PPPPFPFPFPFFFFF"


def _dot(a, b):
    return jnp.dot(a, b, preferred_element_type=F32)


def _dot_nt(a, b):
    return lax.dot_general(a, b, (((1,), (1,)), ((), ())), preferred_element_type=F32)


def _rmsnorm(x, g):
    return x * lax.rsqrt(jnp.mean(x * x, axis=-1, keepdims=True) + EPS) * g


def _sigmoid(x):
    return 1.0 / (1.0 + jnp.exp(-x))


def _log_sigmoid(x):
    return jnp.minimum(x, 0.0) - jnp.log1p(jnp.exp(-jnp.abs(x)))


def _masks(tm, seg):
    row = lax.broadcasted_iota(jnp.int32, (tm, tm), 0)
    col = lax.broadcasted_iota(jnp.int32, (tm, tm), 1)
    if tm == seg:
        return col <= row, row <= col
    shift = seg.bit_length() - 1
    same = (row >> shift) == (col >> shift)
    return (col <= row) & same, (row <= col) & same


def _init_scratch(wkg_s, wqkv_ref, wgt_ref, *slots):
    wkg_s[0:D_MODEL, :] = wqkv_ref[_K0:_V0, :]
    wkg_s[D_MODEL:D_MODEL + GATE_ROWS, :] = wgt_ref[...].astype(BF16)
    for slot in slots:
        slot.abuf[:, 0:POOL_T0 - HIST_ROWS, :] = jnp.zeros((slot.abuf.shape[0], POOL_T0 - HIST_ROWS, D_POOL), F32)


def _project(x, w, slot, hist, pos, *, tm, seg):
    nseg = tm // seg
    u = _rmsnorm(x, w.gmix[...]).astype(BF16)
    kg = _dot_nt(w.wkg[...], u)
    kt = kg[0:D_MODEL] * (HEAD_DIM ** -0.5)
    slot.kt[...] = kt
    slot.ktb[...] = kt.astype(BF16)
    gates = kg[D_MODEL:D_MODEL + GATE_ROWS] + w.bias[...]
    yield
    p = _dot_nt(u, w.wrest[_P0:_GA0, :])
    for j in range(nseg):
        slot.abuf[j, POOL_T0 - HIST_ROWS:POOL_T0, :] = hist[j]
        slot.abuf[j, POOL_T0:POOL_T0 + seg, :] = p[j * seg:(j + 1) * seg]
    for gi, win in enumerate(POOL_WINDOWS):
        ls = slice(gi * POOL_GROUP, (gi + 1) * POOL_GROUP)
        cnt = jnp.minimum(pos + 1, win).astype(F32)
        for j in range(nseg):
            start = POOL_T0 - 8 * gi
            tot = slot.abuf[j, start:POOL_T0 + seg, ls] + slot.abuf[j, start - 1:POOL_T0 + seg - 1, ls]
            shift = 2
            while shift < win:
                n = tot.shape[0]
                tot = tot[8:n] + tot[8 - shift:n - shift]
                shift *= 2
            tok = slot.abuf[j, POOL_T0:POOL_T0 + seg, ls]
            slot.pd[j * seg:(j + 1) * seg, ls] = (tot / cnt - tok).astype(BF16)
    yield
    slot.q[...] = _dot_nt(u, w.wqkv[0:_K0, :]).astype(BF16)
    _, upper = _masks(tm, seg)
    lf = _log_sigmoid(gates[8:16])
    hi = lf.astype(BF16).astype(F32)
    r1 = lf - hi
    mid = r1.astype(BF16).astype(F32)
    lo = r1 - mid
    pieces = jnp.concatenate([hi, mid, lo, jnp.zeros_like(lo)], axis=0).astype(BF16)
    cs = _dot(pieces, jnp.where(upper, 1.0, 0.0).astype(BF16))
    b = cs[0:8] + cs[8:16] + cs[16:24]
    slot.gate[0:8, :] = gates[0:8] - b
    slot.gate[8:16, :] = lf
    yield
    slot.v[...] = _dot_nt(u, w.wqkv[_V0:_QKV_ROWS, :]).astype(BF16)
    yield
    slot.og[...] = _dot_nt(u, w.wrest[0:_P0, :])
    yield
    slot.ga[...] = _dot_nt(u, w.wrest[_GA0:_GB0, :])
    yield
    slot.gb[...] = _dot_nt(u, w.wrest[_GB0:_REST_ROWS, :])
    yield


def _finish(x, out_ref, rows, w, slot, hcat_s, c_rd, n_rd, m_rd, c_store, n_out, m_out, *, tm, seg):
    nseg = tm // seg
    causal, _ = _masks(tm, seg)

    heads = []
    for h in range(N_HEADS):
        sl = slice(h * HEAD_DIM, (h + 1) * HEAD_DIM)
        a_m = jnp.where(causal, jnp.broadcast_to(slot.gate[h:h + 1, :], (tm, tm)), -jnp.inf)
        mprev = jnp.concatenate(
            [jnp.broadcast_to(m_rd[j:j + 1, h * LANES:h * LANES + 1], (seg, 1)) for j in range(nseg)], axis=0)
        g = jnp.maximum(jnp.max(a_m, axis=-1, keepdims=True), mprev)
        dm = jnp.exp(a_m - g)
        lf_b = jnp.broadcast_to(slot.gate[8 + h:9 + h, :], (tm, tm))
        bcol = jnp.sum(jnp.where(causal, lf_b, 0.0), axis=-1, keepdims=True)
        mcol = bcol + g
        wi = jnp.exp(mprev - g)

        qh = slot.q[:, sl]
        vh = slot.v[:, sl]
        s = _dot(qh, slot.ktb[sl, :]) * dm
        qf = qh.astype(F32)
        qc_parts, qn_parts = [], []
        for j in range(nseg):
            rs = slice(j * seg, (j + 1) * seg)
            qc_parts.append(_dot(qh[rs], c_rd[j, h].astype(BF16)))
            qn_parts.append(jnp.sum(qf[rs] * n_rd[j:j + 1, sl], axis=-1, keepdims=True))
        qc = jnp.concatenate(qc_parts, axis=0)
        qn = jnp.concatenate(qn_parts, axis=0)
        heads.append((sl, dm, mcol, wi, vh, s, qc, qn))
    yield

    for h, (sl, dm, mcol, wi, vh, s, qc, qn) in enumerate(heads):
        kth = slot.kt[sl, :]
        rowsum = jnp.sum(s, axis=-1, keepdims=True)
        sv = _dot(s.astype(BF16), vh)
        num = wi * qc + sv
        den = wi * qn + rowsum
        hh = num / jnp.maximum(jnp.abs(den), jnp.exp(-mcol))
        hh = hh * lax.rsqrt(jnp.mean(hh * hh, axis=-1, keepdims=True) + EPS) * w.ghead[:, sl]
        hh = hh * _sigmoid(slot.og[:, sl])
        hcat_s[:, sl] = hh.astype(BF16)

        wrows, n_decayed = [], []
        for j in range(nseg):
            r = (j + 1) * seg - 1
            w_row = dm[r:r + 1, :]
            wrows.append(w_row)
            decay = wi[r:r + 1, :]
            kw = (kth * w_row).astype(BF16)
            c_store(j, h, decay * c_rd[j, h] + _dot(kw, vh))
            n_decayed.append(decay * n_rd[j:j + 1, sl])
            m_out[j:j + 1, h * LANES:(h + 1) * LANES] = jnp.broadcast_to(mcol[r:r + 1, :], (1, LANES))
        ridx = lax.broadcasted_iota(jnp.int32, (GATE_ROWS, tm), 0)
        wmat = jnp.zeros((GATE_ROWS, tm), F32)
        for j in range(nseg):
            wmat = jnp.where(ridx == j, jnp.broadcast_to(wrows[j], (GATE_ROWS, tm)), wmat)
        nupd = _dot_nt(wmat.astype(BF16), slot.ktb[sl, :])
        for j in range(nseg):
            n_out[j:j + 1, sl] = n_decayed[j] + nupd[j:j + 1, :]
        yield

    grp_out = [_dot(slot.pd[:, gi * POOL_GROUP:(gi + 1) * POOL_GROUP], w.wgrp[gi])
               for gi in range(len(POOL_WINDOWS))]
    pooled = jnp.concatenate(grp_out, axis=-1) * w.pscale[...]
    yield

    branch_a = _dot(hcat_s[...], w.wba[...])
    branch_b = _dot(pooled.astype(BF16), w.wbb[...])
    yield
    mixed = _sigmoid(slot.ga[...]) * branch_a + _sigmoid(slot.gb[...]) * branch_b
    out_ref[rows, :] = x + _dot(mixed.astype(BF16), w.wout[...])
    yield


def _run(order, **gens):
    for name in order:
        next(gens[name], None)
    for gen in gens.values():
        for _ in gen:
            pass


class _StateRing:
    def __init__(self, dst_hbm, ring, sems, first_stream):
        self.dst, self.ring, self.sems, self.first = dst_hbm, ring, sems, first_stream
        self.count = 0
        self.in_flight = {}

    def store(self, j, h, value):
        slot = self.count % self.ring.shape[0]
        self.count += 1
        if slot in self.in_flight:
            self.drain()
        self.ring[slot] = value
        copy = pltpu.make_async_copy(self.ring.at[slot], self.dst.at[self.first + j, h], self.sems.at[slot])
        copy.start()
        self.in_flight[slot] = copy

    def drain(self):
        for slot in sorted(self.in_flight):
            self.in_flight.pop(slot).wait()


def _carry_kernel(x_ref, xnext_ref, gmix_ref, wqkv_ref, wrest_ref, wgt_ref, bias_ref, ghead_ref, wgrp_ref,
                  pscale_ref, wba_ref, wbb_ref, wout_ref, c_in, n_in, m_in, hist_in,
                  x1_ref, c_out, n_out, m_out, hist_out,
                  wkg_s, hcat_s, *slot_refs, tm):
    i = pl.program_id(0)
    w = _Weights(gmix_ref, wqkv_ref, wrest_ref, wkg_s, bias_ref, ghead_ref, wgrp_ref, pscale_ref,
                 wba_ref, wbb_ref, wout_ref)
    nslot = len(_Slot._fields)
    slot_a, slot_b = _Slot(*slot_refs[:nslot]), _Slot(*slot_refs[nslot:])
    kw = dict(tm=tm, seg=tm)

    row = lax.broadcasted_iota(jnp.int32, (tm, POOL_GROUP), 0)

    @pl.when(i == 0)
    def _():
        _init_scratch(wkg_s, wqkv_ref, wgt_ref, slot_a, slot_b)
        c_out[...] = c_in[...]
        n_out[...] = n_in[...]
        m_out[...] = m_in[...]
        _run("", P=_project(x_ref[0:tm, :], w, slot_a, [hist_in[0]], row, **kw))

    state = (c_out, n_out, m_out)

    def c_store(j, h, value):
        c_out[j, h] = value

    def finish_beside(rows, slot, tile, x_proj, slot_proj):
        new_hist = slot.abuf[0, POOL_T0 + tm - HIST_ROWS:POOL_T0 + tm, :]
        hist_out[0] = new_hist
        _run(_CARRY_ORDER,
             F=_finish(x_ref[rows, :], x1_ref, rows, w, slot, hcat_s, *state, c_store, n_out, m_out, **kw),
             P=_project(x_proj, w, slot_proj, [new_hist], row + (tile + 1) * tm, **kw))

    finish_beside(slice(0, tm), slot_a, 2 * i, x_ref[tm:2 * tm, :], slot_b)
    finish_beside(slice(tm, 2 * tm), slot_b, 2 * i + 1, xnext_ref[...], slot_a)


def _streams_kernel(x_ref, gmix_ref, wqkv_ref, wrest_ref, wgt_ref, bias_ref, ghead_ref, wgrp_ref,
                    pscale_ref, wba_ref, wbb_ref, wout_ref, c_in, n_in, m_in, hist_in,
                    x1_ref, c_hbm, n_out, m_out, hist_out,
                    wkg_s, hcat_s, cring_s, cring_sem, *slot_refs, tm, seg, pos0):
    i = pl.program_id(0)
    nseg = tm // seg
    w = _Weights(gmix_ref, wqkv_ref, wrest_ref, wkg_s, bias_ref, ghead_ref, wgrp_ref, pscale_ref,
                 wba_ref, wbb_ref, wout_ref)
    slot = _Slot(*slot_refs)

    @pl.when(i == 0)
    def _():
        _init_scratch(wkg_s, wqkv_ref, wgt_ref, slot)

    x = x_ref[...]
    pos = lax.broadcasted_iota(jnp.int32, (seg, POOL_GROUP), 0) + pos0
    ring = _StateRing(c_hbm, cring_s, cring_sem, first_stream=i * nseg)
    _run(_STREAMS_ORDER,
         P=_project(x, w, slot, [hist_in[j] for j in range(nseg)], pos, tm=tm, seg=seg),
         F=_finish(x, x1_ref, slice(0, tm), w, slot, hcat_s, c_in, n_in, m_in, ring.store, n_out, m_out,
                   tm=tm, seg=seg))
    ring.drain()
    for j in range(nseg):
        hist_out[j] = slot.abuf[j, POOL_T0 + seg - HIST_ROWS:POOL_T0 + seg, :]


def _ffn_kernel(xs_ref, xp_ref, gffn_ref, wup_ref, wdown_ref, gfin_ref, ys_ref, yp_ref, wup_s, wdown_s,
                *, sub, n_w, n_s):
    i = pl.program_id(0)

    @pl.when(i < n_w)
    def _():
        up_rows, down_rows = wup_ref.shape[0], wdown_ref.shape[0]
        wup_s[pl.ds(pl.multiple_of(i * up_rows, up_rows), up_rows), :] = wup_ref[...].astype(BF16)
        wdown_s[pl.ds(pl.multiple_of(i * down_rows, down_rows), down_rows), :] = wdown_ref[...].astype(BF16)

    @pl.when(i >= n_w)
    def _():
        is_sample = i < n_w + n_s
        for r in range(xp_ref.shape[0] // sub):
            rows = slice(r * sub, (r + 1) * sub)
            x = jnp.where(is_sample, xs_ref[rows, :], xp_ref[rows, :])
            u = _rmsnorm(x, gffn_ref[...]).astype(BF16)
            hid = _dot(u, wup_s[...])
            act = jnp.square(jnp.maximum(hid, 0.0)).astype(BF16)
            x2 = x + _dot(act, wdown_s[...])
            y = _rmsnorm(x2, gfin_ref[...])
            ys_ref[rows, :] = y
            yp_ref[rows, :] = y


def _resident(shape):
    zeros = (0,) * len(shape)
    return pl.BlockSpec(shape, lambda i: zeros, pipeline_mode=pl.Buffered(1))


def _slot_scratch(tm, seg):
    return [
        pltpu.VMEM((tm, D_MODEL), BF16),
        pltpu.VMEM((tm, D_MODEL), BF16),
        pltpu.VMEM((D_MODEL, tm), F32),
        pltpu.VMEM((D_MODEL, tm), BF16),
        pltpu.VMEM((tm, D_MODEL), F32),
        pltpu.VMEM((tm, D_MODEL), F32),
        pltpu.VMEM((tm, D_MODEL), F32),
        pltpu.VMEM((GATE_ROWS, tm), F32),
        pltpu.VMEM((tm // seg, POOL_T0 + seg, D_POOL), F32),
        pltpu.VMEM((tm, D_POOL), BF16),
    ]


def _mixer(x, weights, state, *, tm, seg, carry, pos0):
    t = x.shape[0]
    nseg = tm // seg
    c0, n0, m0, hist0 = state
    n_streams = c0.shape[0]
    sblk = 1 if carry else nseg
    smap = (lambda i: 0) if carry else (lambda i: i)
    n0 = n0.reshape(n_streams // sblk, sblk, D_MODEL)
    m0 = jnp.repeat(m0, LANES, axis=-1).reshape(n_streams // sblk, sblk, N_HEADS * LANES)
    hist0 = jnp.pad(hist0, ((0, 0), (1, 0), (0, 0)))
    bias = jnp.broadcast_to(weights["gate_bias"][:, None], (GATE_ROWS, tm))

    if carry:
        assert pos0 == 0 and nseg == 1 and t % (2 * tm) == 0
        steps = t // (2 * tm)
        last_tile = t // tm - 1
        x_specs = [pl.BlockSpec((2 * tm, D_MODEL), lambda i: (i, 0)),
                   pl.BlockSpec((tm, D_MODEL), lambda i: (jnp.minimum(2 * i + 2, last_tile), 0))]
        x_args = [x, x]
        x1_spec = pl.BlockSpec((2 * tm, D_MODEL), lambda i: (i, 0))
        body = functools.partial(_carry_kernel, tm=tm)
        slots = _slot_scratch(tm, seg) + _slot_scratch(tm, seg)
    else:
        steps = t // tm
        x_specs = [pl.BlockSpec((tm, D_MODEL), lambda i: (i, 0))]
        x_args = [x]
        x1_spec = pl.BlockSpec((tm, D_MODEL), lambda i: (i, 0))
        body = functools.partial(_streams_kernel, tm=tm, seg=seg, pos0=pos0)
        slots = [pltpu.VMEM((STATE_RING_SLOTS, HEAD_DIM, HEAD_DIM), F32),
                 pltpu.SemaphoreType.DMA((STATE_RING_SLOTS,))] + _slot_scratch(tm, seg)

    state_specs = [
        pl.BlockSpec((sblk, N_HEADS, HEAD_DIM, HEAD_DIM), lambda i: (smap(i), 0, 0, 0)),
        pl.BlockSpec((None, sblk, D_MODEL), lambda i: (smap(i), 0, 0)),
        pl.BlockSpec((None, sblk, N_HEADS * LANES), lambda i: (smap(i), 0, 0)),
        pl.BlockSpec((sblk, HIST_ROWS, D_POOL), lambda i: (smap(i), 0, 0)),
    ]
    in_specs = x_specs + [
        _resident((1, D_MODEL)),
        _resident((_QKV_ROWS, D_MODEL)),
        _resident((_REST_ROWS, D_MODEL)),
        _resident((GATE_ROWS, D_MODEL)),
        _resident((GATE_ROWS, tm)),
        _resident((1, D_MODEL)),
        _resident((len(POOL_WINDOWS), POOL_GROUP, POOL_GROUP)),
        _resident((1, D_POOL)),
        _resident((D_MODEL, D_MODEL)),
        _resident((D_POOL, D_MODEL)),
        _resident((D_MODEL, D_MODEL)),
    ] + state_specs
    out_shape = [
        jax.ShapeDtypeStruct((t, D_MODEL), F32),
        jax.ShapeDtypeStruct(c0.shape, F32),
        jax.ShapeDtypeStruct(n0.shape, F32),
        jax.ShapeDtypeStruct(m0.shape, F32),
        jax.ShapeDtypeStruct(hist0.shape, F32),
    ]
    scratch = [
        pltpu.VMEM((D_MODEL + GATE_ROWS, D_MODEL), BF16),
        pltpu.VMEM((tm, D_MODEL), BF16),
    ] + slots
    x1, c1, n1, m1, hist1 = pl.pallas_call(
        body,
        grid=(steps,),
        in_specs=in_specs,
        out_specs=[x1_spec, state_specs[0] if carry else pl.BlockSpec(memory_space=pl.ANY)] + state_specs[1:],
        out_shape=out_shape,
        scratch_shapes=scratch,
        compiler_params=pltpu.CompilerParams(
            dimension_semantics=("arbitrary",), vmem_limit_bytes=V7X_VMEM_LIMIT_BYTES),
        name="mixer_carry" if carry else "mixer_streams",
    )(*x_args, weights["g_mix"], weights["w_qkv"], weights["w_rest"], weights["wg_t"], bias, weights["g_head"],
      weights["w_grp"], weights["pool_scale"], weights["w_ba"], weights["w_bb"], weights["w_out"], c0, n0, m0, hist0)
    n1 = n1.reshape(n_streams, N_HEADS, HEAD_DIM)
    m1 = m1.reshape(n_streams, N_HEADS, LANES)[:, :, 0]
    return x1, c1, n1, m1, hist1[:, 1:, :]


def _ffn(x_sample, x_prompt, weights, *, tm):
    ts, tp = x_sample.shape[0], x_prompt.shape[0]
    n_w, n_s, n_p = FFN_WEIGHT_STEPS, ts // tm, tp // tm
    clamp = lambda v, hi: jnp.clip(v, 0, hi)
    ys, yp = pl.pallas_call(
        functools.partial(_ffn_kernel, sub=256, n_w=n_w, n_s=n_s),
        grid=(n_w + n_s + n_p,),
        in_specs=[
            pl.BlockSpec((tm, D_MODEL), lambda i: (clamp(i - n_w, n_s - 1), 0)),
            pl.BlockSpec((tm, D_MODEL), lambda i: (clamp(i - n_w - n_s, n_p - 1), 0)),
            _resident((1, D_MODEL)),
            pl.BlockSpec((D_MODEL // n_w, D_FF), lambda i: (jnp.minimum(i, n_w - 1), 0)),
            pl.BlockSpec((D_FF // n_w, D_MODEL), lambda i: (jnp.minimum(i, n_w - 1), 0)),
            _resident((1, D_MODEL)),
        ],
        out_specs=[
            pl.BlockSpec((tm, D_MODEL), lambda i: (clamp(i - n_w, n_s), 0)),
            pl.BlockSpec((tm, D_MODEL), lambda i: (clamp(i - n_w - n_s, n_p - 1), 0)),
        ],
        out_shape=[jax.ShapeDtypeStruct((ts + tm, D_MODEL), F32), jax.ShapeDtypeStruct((tp, D_MODEL), F32)],
        scratch_shapes=[pltpu.VMEM((D_MODEL, D_FF), BF16), pltpu.VMEM((D_FF, D_MODEL), BF16)],
        compiler_params=pltpu.CompilerParams(
            dimension_semantics=("arbitrary",), vmem_limit_bytes=V7X_VMEM_LIMIT_BYTES),
        name="ffn",
    )(x_sample, x_prompt, weights["g_ffn"], weights["w_up"], weights["w_down"], weights["g_final"])
    return ys[:ts], yp


def _split_w_in_kernel(w_ref, wqkv_ref, wrest_ref, wg_ref, prev_s):
    i = pl.program_id(0)
    n_qkv = _QKV_ROWS // _SPLIT_ROWS
    cur = w_ref[...]

    @pl.when(i < n_qkv)
    def _():
        wqkv_ref[...] = cur.astype(BF16)

    @pl.when(i == n_qkv)
    def _():
        wg_ref[...] = cur[0:2 * N_HEADS, :]

    @pl.when(i > n_qkv)
    def _():
        wrest_ref[...] = jnp.concatenate([prev_s[2 * N_HEADS:, :], cur[0:2 * N_HEADS, :]], axis=0).astype(BF16)

    prev_s[...] = cur


def _split_w_in(w_in_t):
    n_in = w_in_t.shape[0]
    assert n_in == _REST0 + _REST_ROWS and _QKV_ROWS % _SPLIT_ROWS == 0 and _REST_ROWS % _SPLIT_ROWS == 0
    n_qkv, n_rest = _QKV_ROWS // _SPLIT_ROWS, _REST_ROWS // _SPLIT_ROWS
    return pl.pallas_call(
        _split_w_in_kernel,
        grid=(n_qkv + 1 + n_rest,),
        in_specs=[pl.BlockSpec((_SPLIT_ROWS, D_MODEL), lambda i: (i, 0))],
        out_specs=[
            pl.BlockSpec((_SPLIT_ROWS, D_MODEL), lambda i: (jnp.minimum(i, n_qkv - 1), 0)),
            pl.BlockSpec((_SPLIT_ROWS, D_MODEL), lambda i: (jnp.clip(i - n_qkv - 1, 0, n_rest - 1), 0)),
            pl.BlockSpec((2 * N_HEADS, D_MODEL), lambda i: (0, 0)),
        ],
        out_shape=[
            jax.ShapeDtypeStruct((_QKV_ROWS, D_MODEL), BF16),
            jax.ShapeDtypeStruct((_REST_ROWS, D_MODEL), BF16),
            jax.ShapeDtypeStruct((2 * N_HEADS, D_MODEL), F32),
        ],
        scratch_shapes=[pltpu.VMEM((_SPLIT_ROWS, D_MODEL), F32)],
        compiler_params=pltpu.CompilerParams(dimension_semantics=("arbitrary",)),
        name="split_w_in",
    )(w_in_t)


def _pack_weights(w_in, b_igate, b_fgate, g_norm_mix, g_head, w_pool_grp, pool_scale, w_branch_mlstm,
                  w_branch_pool, w_out, g_norm_ffn, w_up, w_down, g_final):
    d = D_MODEL
    w_qkv, w_rest, wg = _split_w_in(w_in.T)
    wi_t, wf_t = wg[0:N_HEADS], wg[N_HEADS:2 * N_HEADS]
    return {
        "w_qkv": w_qkv,
        "w_rest": w_rest,
        "wg_t": jnp.concatenate([wi_t, wf_t, wf_t, wi_t], axis=0),
        "gate_bias": jnp.concatenate([b_igate, b_fgate, b_fgate, b_igate]).astype(F32),
        "g_mix": g_norm_mix.reshape(1, d),
        "g_head": g_head.reshape(1, d),
        "w_grp": w_pool_grp.astype(BF16),
        "pool_scale": pool_scale.reshape(1, D_POOL),
        "w_ba": w_branch_mlstm.astype(BF16),
        "w_bb": w_branch_pool.astype(BF16),
        "w_out": w_out.astype(BF16),
        "g_ffn": g_norm_ffn.reshape(1, d),
        "w_up": w_up,
        "w_down": w_down,
        "g_final": g_final.reshape(1, d),
    }


def kernel(x_prompt, x_sample, state_C, state_n, state_m, state_pool, w_in, b_igate, b_fgate, g_norm_mix, g_head,
           w_pool_grp, pool_scale, w_branch_mlstm, w_branch_pool, w_out, g_norm_ffn, w_up, w_down, g_final):
    depth = w_in.shape[0]
    bp, sp, d = x_prompt.shape
    bs, ss, _ = x_sample.shape
    assert depth == 1 and bp == 1 and d == D_MODEL
    hp = x_prompt.reshape(bp * sp, d)
    hs = x_sample.reshape(bs * ss, d)
    weights = _pack_weights(w_in[0], b_igate[0], b_fgate[0], g_norm_mix[0], g_head[0], w_pool_grp[0], pool_scale[0],
                            w_branch_mlstm[0], w_branch_pool[0], w_out[0], g_norm_ffn[0], w_up[0], w_down[0],
                            g_final)
    zero_state = (jnp.zeros((bp, N_HEADS, HEAD_DIM, HEAD_DIM), F32), jnp.zeros((bp, D_MODEL), F32),
                  jnp.zeros((bp, N_HEADS), F32), jnp.zeros((bp, HIST_ROWS - 1, D_POOL), F32))
    hp, cp, np_, mp, pp = _mixer(hp, weights, zero_state, tm=256, seg=256, carry=True, pos0=0)
    sample_state = (state_C[0], state_n[0].reshape(bs, D_MODEL), state_m[0], state_pool[0])
    hs, cs, ns, ms, ps = _mixer(hs, weights, sample_state, tm=8 * ss, seg=ss, carry=False, pos0=PAST_LEN)
    y_sample, y_prompt = _ffn(hs, hp, weights, tm=512)
    y_prompt = y_prompt.reshape(bp, sp, d)
    y_sample = y_sample.reshape(bs, ss, d)
    return (y_prompt, y_sample, cp[None], np_[None], mp[None], pp[None], cs[None], ns[None], ms[None], ps[None])
```

```python
import collections
import functools

import jax
import jax.numpy as jnp
from jax import lax
from jax.experimental import pallas as pl
from jax.experimental.pallas import tpu as pltpu

F32 = jnp.float32
BF16 = jnp.bfloat16

D_MODEL = 1024
N_HEADS = 4
HEAD_DIM = 256
D_POOL = 512
POOL_GROUP = 128
POOL_WINDOWS = (2, 4, 8, 16)
HIST_ROWS = 16
POOL_T0 = 32
D_FF = 4096
EPS = 1e-6
PAST_LEN = 2048
LANES = 128
GATE_ROWS = 16
STATE_RING_SLOTS = 8
V7X_VMEM_LIMIT_BYTES = 56 * 1024 * 1024

_K0, _V0, _QKV_ROWS = 1024, 2048, 3072
_REST0 = _QKV_ROWS + 2 * N_HEADS
_P0, _GA0, _GB0, _REST_ROWS = 1024, 1536, 2560, 3584
_SPLIT_ROWS = 512

_Slot = collections.namedtuple("_Slot", "q v kt ktb og ga gb gate abuf pd")
_Weights = collections.namedtuple(
    "_Weights", "gmix wqkv wrest wkg bias ghead wgrp pscale wba wbb wout")
_CARRY_ORDER = "FPFFFFPPFPPFPFP"
_STREAMS_ORDER = "PPPPFPFPFPFFFFF"


def _dot(a, b):
    return jnp.dot(a, b, preferred_element_type=F32)


def _dot_nt(a, b):
    return lax.dot_general(a, b, (((1,), (1,)), ((), ())), preferred_element_type=F32)


def _rmsnorm(x, g):
    return x * lax.rsqrt(jnp.mean(x * x, axis=-1, keepdims=True) + EPS) * g


def _sigmoid(x):
    return 1.0 / (1.0 + jnp.exp(-x))


def _log_sigmoid(x):
    return jnp.minimum(x, 0.0) - jnp.log1p(jnp.exp(-jnp.abs(x)))


def _masks(tm, seg):
    row = lax.broadcasted_iota(jnp.int32, (tm, tm), 0)
    col = lax.broadcasted_iota(jnp.int32, (tm, tm), 1)
    if tm == seg:
        return col <= row, row <= col
    shift = seg.bit_length() - 1
    same = (row >> shift) == (col >> shift)
    return (col <= row) & same, (row <= col) & same


def _init_scratch(wkg_s, wqkv_ref, wgt_ref, *slots):
    wkg_s[0:D_MODEL, :] = wqkv_ref[_K0:_V0, :]
    wkg_s[D_MODEL:D_MODEL + GATE_ROWS, :] = wgt_ref[...].astype(BF16)
    for slot in slots:
        slot.abuf[:, 0:POOL_T0 - HIST_ROWS, :] = jnp.zeros((slot.abuf.shape[0], POOL_T0 - HIST_ROWS, D_POOL), F32)


def _project(x, w, slot, hist, pos, *, tm, seg):
    nseg = tm // seg
    u = _rmsnorm(x, w.gmix[...]).astype(BF16)
    kg = _dot_nt(w.wkg[...], u)
    kt = kg[0:D_MODEL] * (HEAD_DIM ** -0.5)
    slot.kt[...] = kt
    slot.ktb[...] = kt.astype(BF16)
    gates = kg[D_MODEL:D_MODEL + GATE_ROWS] + w.bias[...]
    yield
    p = _dot_nt(u, w.wrest[_P0:_GA0, :])
    for j in range(nseg):
        slot.abuf[j, POOL_T0 - HIST_ROWS:POOL_T0, :] = hist[j]
        slot.abuf[j, POOL_T0:POOL_T0 + seg, :] = p[j * seg:(j + 1) * seg]
    for gi, win in enumerate(POOL_WINDOWS):
        ls = slice(gi * POOL_GROUP, (gi + 1) * POOL_GROUP)
        cnt = jnp.minimum(pos + 1, win).astype(F32)
        for j in range(nseg):
            start = POOL_T0 - 8 * gi
            tot = slot.abuf[j, start:POOL_T0 + seg, ls] + slot.abuf[j, start - 1:POOL_T0 + seg - 1, ls]
            shift = 2
            while shift < win:
                n = tot.shape[0]
                tot = tot[8:n] + tot[8 - shift:n - shift]
                shift *= 2
            tok = slot.abuf[j, POOL_T0:POOL_T0 + seg, ls]
            slot.pd[j * seg:(j + 1) * seg, ls] = (tot / cnt - tok).astype(BF16)
    yield
    slot.q[...] = _dot_nt(u, w.wqkv[0:_K0, :]).astype(BF16)
    _, upper = _masks(tm, seg)
    lf = _log_sigmoid(gates[8:16])
    hi = lf.astype(BF16).astype(F32)
    r1 = lf - hi
    mid = r1.astype(BF16).astype(F32)
    lo = r1 - mid
    pieces = jnp.concatenate([hi, mid, lo, jnp.zeros_like(lo)], axis=0).astype(BF16)
    cs = _dot(pieces, jnp.where(upper, 1.0, 0.0).astype(BF16))
    b = cs[0:8] + cs[8:16] + cs[16:24]
    slot.gate[0:8, :] = gates[0:8] - b
    slot.gate[8:16, :] = lf
    yield
    slot.v[...] = _dot_nt(u, w.wqkv[_V0:_QKV_ROWS, :]).astype(BF16)
    yield
    slot.og[...] = _dot_nt(u, w.wrest[0:_P0, :])
    yield
    slot.ga[...] = _dot_nt(u, w.wrest[_GA0:_GB0, :])
    yield
    slot.gb[...] = _dot_nt(u, w.wrest[_GB0:_REST_ROWS, :])
    yield


def _finish(x, out_ref, rows, w, slot, hcat_s, c_rd, n_rd, m_rd, c_store, n_out, m_out, *, tm, seg):
    nseg = tm // seg
    causal, _ = _masks(tm, seg)

    heads = []
    for h in range(N_HEADS):
        sl = slice(h * HEAD_DIM, (h + 1) * HEAD_DIM)
        a_m = jnp.where(causal, jnp.broadcast_to(slot.gate[h:h + 1, :], (tm, tm)), -jnp.inf)
        mprev = jnp.concatenate(
            [jnp.broadcast_to(m_rd[j:j + 1, h * LANES:h * LANES + 1], (seg, 1)) for j in range(nseg)], axis=0)
        g = jnp.maximum(jnp.max(a_m, axis=-1, keepdims=True), mprev)
        dm = jnp.exp(a_m - g)
        lf_b = jnp.broadcast_to(slot.gate[8 + h:9 + h, :], (tm, tm))
        bcol = jnp.sum(jnp.where(causal, lf_b, 0.0), axis=-1, keepdims=True)
        mcol = bcol + g
        wi = jnp.exp(mprev - g)

        qh = slot.q[:, sl]
        vh = slot.v[:, sl]
        s = _dot(qh, slot.ktb[sl, :]) * dm
        qf = qh.astype(F32)
        qc_parts, qn_parts = [], []
        for j in range(nseg):
            rs = slice(j * seg, (j + 1) * seg)
            qc_parts.append(_dot(qh[rs], c_rd[j, h].astype(BF16)))
            qn_parts.append(jnp.sum(qf[rs] * n_rd[j:j + 1, sl], axis=-1, keepdims=True))
        qc = jnp.concatenate(qc_parts, axis=0)
        qn = jnp.concatenate(qn_parts, axis=0)
        heads.append((sl, dm, mcol, wi, vh, s, qc, qn))
    yield

    for h, (sl, dm, mcol, wi, vh, s, qc, qn) in enumerate(heads):
        kth = slot.kt[sl, :]
        rowsum = jnp.sum(s, axis=-1, keepdims=True)
        sv = _dot(s.astype(BF16), vh)
        num = wi * qc + sv
        den = wi * qn + rowsum
        hh = num / jnp.maximum(jnp.abs(den), jnp.exp(-mcol))
        hh = hh * lax.rsqrt(jnp.mean(hh * hh, axis=-1, keepdims=True) + EPS) * w.ghead[:, sl]
        hh = hh * _sigmoid(slot.og[:, sl])
        hcat_s[:, sl] = hh.astype(BF16)

        wrows, n_decayed = [], []
        for j in range(nseg):
            r = (j + 1) * seg - 1
            w_row = dm[r:r + 1, :]
            wrows.append(w_row)
            decay = wi[r:r + 1, :]
            kw = (kth * w_row).astype(BF16)
            c_store(j, h, decay * c_rd[j, h] + _dot(kw, vh))
            n_decayed.append(decay * n_rd[j:j + 1, sl])
            m_out[j:j + 1, h * LANES:(h + 1) * LANES] = jnp.broadcast_to(mcol[r:r + 1, :], (1, LANES))
        ridx = lax.broadcasted_iota(jnp.int32, (GATE_ROWS, tm), 0)
        wmat = jnp.zeros((GATE_ROWS, tm), F32)
        for j in range(nseg):
            wmat = jnp.where(ridx == j, jnp.broadcast_to(wrows[j], (GATE_ROWS, tm)), wmat)
        nupd = _dot_nt(wmat.astype(BF16), slot.ktb[sl, :])
        for j in range(nseg):
            n_out[j:j + 1, sl] = n_decayed[j] + nupd[j:j + 1, :]
        yield

    grp_out = [_dot(slot.pd[:, gi * POOL_GROUP:(gi + 1) * POOL_GROUP], w.wgrp[gi])
               for gi in range(len(POOL_WINDOWS))]
    pooled = jnp.concatenate(grp_out, axis=-1) * w.pscale[...]
    yield

    branch_a = _dot(hcat_s[...], w.wba[...])
    branch_b = _dot(pooled.astype(BF16), w.wbb[...])
    yield
    mixed = _sigmoid(slot.ga[...]) * branch_a + _sigmoid(slot.gb[...]) * branch_b
    out_ref[rows, :] = x + _dot(mixed.astype(BF16), w.wout[...])
    yield


def _run(order, **gens):
    for name in order:
        next(gens[name], None)
    for gen in gens.values():
        for _ in gen:
            pass


class _StateRing:
    def __init__(self, dst_hbm, ring, sems, first_stream):
        self.dst, self.ring, self.sems, self.first = dst_hbm, ring, sems, first_stream
        self.count = 0
        self.in_flight = {}

    def store(self, j, h, value):
        slot = self.count % self.ring.shape[0]
        self.count += 1
        if slot in self.in_flight:
            self.drain()
        self.ring[slot] = value
        copy = pltpu.make_async_copy(self.ring.at[slot], self.dst.at[self.first + j, h], self.sems.at[slot])
        copy.start()
        self.in_flight[slot] = copy

    def drain(self):
        for slot in sorted(self.in_flight):
            self.in_flight.pop(slot).wait()


def _carry_kernel(x_ref, xnext_ref, gmix_ref, wqkv_ref, wrest_ref, wgt_ref, bias_ref, ghead_ref, wgrp_ref,
                  pscale_ref, wba_ref, wbb_ref, wout_ref, c_in, n_in, m_in, hist_in, *rest, tm, n_riders):
    i = pl.program_id(0)
    rider_in, rest = rest[:n_riders], rest[n_riders:]
    x1_ref, c_out, n_out, m_out, hist_out = rest[:5]
    rider_out, (wkg_s, hcat_s, *slot_refs) = rest[5:5 + n_riders], rest[5 + n_riders:]
    w = _Weights(gmix_ref, wqkv_ref, wrest_ref, wkg_s, bias_ref, ghead_ref, wgrp_ref, pscale_ref,
                 wba_ref, wbb_ref, wout_ref)
    nslot = len(_Slot._fields)
    slot_a, slot_b = _Slot(*slot_refs[:nslot]), _Slot(*slot_refs[nslot:])
    kw = dict(tm=tm, seg=tm)
    for src, dst in zip(rider_in, rider_out):
        dst[...] = src[...].astype(BF16)

    row = lax.broadcasted_iota(jnp.int32, (tm, POOL_GROUP), 0)

    @pl.when(i == 0)
    def _():
        _init_scratch(wkg_s, wqkv_ref, wgt_ref, slot_a, slot_b)
        c_out[...] = c_in[...]
        n_out[...] = n_in[...]
        m_out[...] = m_in[...]
        _run("", P=_project(x_ref[0:tm, :], w, slot_a, [hist_in[0]], row, **kw))

    state = (c_out, n_out, m_out)

    def c_store(j, h, value):
        c_out[j, h] = value

    def finish_beside(rows, slot, tile, x_proj, slot_proj):
        new_hist = slot.abuf[0, POOL_T0 + tm - HIST_ROWS:POOL_T0 + tm, :]
        hist_out[0] = new_hist
        _run(_CARRY_ORDER,
             F=_finish(x_ref[rows, :], x1_ref, rows, w, slot, hcat_s, *state, c_store, n_out, m_out, **kw),
             P=_project(x_proj, w, slot_proj, [new_hist], row + (tile + 1) * tm, **kw))

    finish_beside(slice(0, tm), slot_a, 2 * i, x_ref[tm:2 * tm, :], slot_b)
    finish_beside(slice(tm, 2 * tm), slot_b, 2 * i + 1, xnext_ref[...], slot_a)


def _streams_kernel(x_ref, gmix_ref, wqkv_ref, wrest_ref, wgt_ref, bias_ref, ghead_ref, wgrp_ref,
                    pscale_ref, wba_ref, wbb_ref, wout_ref, c_in, n_in, m_in, hist_in,
                    x1_ref, c_hbm, n_out, m_out, hist_out,
                    wkg_s, hcat_s, cring_s, cring_sem, *slot_refs, tm, seg, pos0):
    i = pl.program_id(0)
    nseg = tm // seg
    w = _Weights(gmix_ref, wqkv_ref, wrest_ref, wkg_s, bias_ref, ghead_ref, wgrp_ref, pscale_ref,
                 wba_ref, wbb_ref, wout_ref)
    slot = _Slot(*slot_refs)

    @pl.when(i == 0)
    def _():
        _init_scratch(wkg_s, wqkv_ref, wgt_ref, slot)

    x = x_ref[...]
    pos = lax.broadcasted_iota(jnp.int32, (seg, POOL_GROUP), 0) + pos0
    ring = _StateRing(c_hbm, cring_s, cring_sem, first_stream=i * nseg)
    _run(_STREAMS_ORDER,
         P=_project(x, w, slot, [hist_in[j] for j in range(nseg)], pos, tm=tm, seg=seg),
         F=_finish(x, x1_ref, slice(0, tm), w, slot, hcat_s, c_in, n_in, m_in, ring.store, n_out, m_out,
                   tm=tm, seg=seg))
    ring.drain()
    for j in range(nseg):
        hist_out[j] = slot.abuf[j, POOL_T0 + seg - HIST_ROWS:POOL_T0 + seg, :]


def _ffn_tile(x_ref, y_ref, gffn_ref, wup_ref, wdown_ref, gfin_ref, sub):
    for r in range(x_ref.shape[0] // sub):
        rows = slice(r * sub, (r + 1) * sub)
        x = x_ref[rows, :]
        u = _rmsnorm(x, gffn_ref[...]).astype(BF16)
        hid = _dot(u, wup_ref[...])
        act = jnp.square(jnp.maximum(hid, 0.0)).astype(BF16)
        x2 = x + _dot(act, wdown_ref[...])
        y_ref[rows, :] = _rmsnorm(x2, gfin_ref[...])


def _ffn_kernel(xs_ref, xp_ref, gffn_ref, wup_ref, wdown_ref, gfin_ref, ys_ref, yp_ref, *, sub, n_s):
    i = pl.program_id(0)

    @pl.when(i < n_s)
    def _():
        _ffn_tile(xs_ref, ys_ref, gffn_ref, wup_ref, wdown_ref, gfin_ref, sub)

    @pl.when(i >= n_s)
    def _():
        _ffn_tile(xp_ref, yp_ref, gffn_ref, wup_ref, wdown_ref, gfin_ref, sub)


def _resident(shape):
    zeros = (0,) * len(shape)
    return pl.BlockSpec(shape, lambda i: zeros, pipeline_mode=pl.Buffered(1))


def _slot_scratch(tm, seg):
    return [
        pltpu.VMEM((tm, D_MODEL), BF16),
        pltpu.VMEM((tm, D_MODEL), BF16),
        pltpu.VMEM((D_MODEL, tm), F32),
        pltpu.VMEM((D_MODEL, tm), BF16),
        pltpu.VMEM((tm, D_MODEL), F32),
        pltpu.VMEM((tm, D_MODEL), F32),
        pltpu.VMEM((tm, D_MODEL), F32),
        pltpu.VMEM((GATE_ROWS, tm), F32),
        pltpu.VMEM((tm // seg, POOL_T0 + seg, D_POOL), F32),
        pltpu.VMEM((tm, D_POOL), BF16),
    ]


def _mixer(x, weights, state, *, tm, seg, carry, pos0, riders=()):
    t = x.shape[0]
    nseg = tm // seg
    c0, n0, m0, hist0 = state
    n_streams = c0.shape[0]
    sblk = 1 if carry else nseg
    smap = (lambda i: 0) if carry else (lambda i: i)
    n0 = n0.reshape(n_streams // sblk, sblk, D_MODEL)
    m0 = jnp.repeat(m0, LANES, axis=-1).reshape(n_streams // sblk, sblk, N_HEADS * LANES)
    hist0 = jnp.pad(hist0, ((0, 0), (1, 0), (0, 0)))
    bias = jnp.broadcast_to(weights["gate_bias"][:, None], (GATE_ROWS, tm))

    if carry:
        assert pos0 == 0 and nseg == 1 and t % (2 * tm) == 0
        steps = t // (2 * tm)
        last_tile = t // tm - 1
        x_specs = [pl.BlockSpec((2 * tm, D_MODEL), lambda i: (i, 0)),
                   pl.BlockSpec((tm, D_MODEL), lambda i: (jnp.minimum(2 * i + 2, last_tile), 0))]
        x_args = [x, x]
        x1_spec = pl.BlockSpec((2 * tm, D_MODEL), lambda i: (i, 0))
        body = functools.partial(_carry_kernel, tm=tm, n_riders=len(riders))
        slots = _slot_scratch(tm, seg) + _slot_scratch(tm, seg)
        assert all(r.shape[0] % (16 * steps) == 0 for r in riders)
    else:
        steps = t // tm
        x_specs = [pl.BlockSpec((tm, D_MODEL), lambda i: (i, 0))]
        x_args = [x]
        x1_spec = pl.BlockSpec((tm, D_MODEL), lambda i: (i, 0))
        body = functools.partial(_streams_kernel, tm=tm, seg=seg, pos0=pos0)
        slots = [pltpu.VMEM((STATE_RING_SLOTS, HEAD_DIM, HEAD_DIM), F32),
                 pltpu.SemaphoreType.DMA((STATE_RING_SLOTS,))] + _slot_scratch(tm, seg)

    state_specs = [
        pl.BlockSpec((sblk, N_HEADS, HEAD_DIM, HEAD_DIM), lambda i: (smap(i), 0, 0, 0)),
        pl.BlockSpec((None, sblk, D_MODEL), lambda i: (smap(i), 0, 0)),
        pl.BlockSpec((None, sblk, N_HEADS * LANES), lambda i: (smap(i), 0, 0)),
        pl.BlockSpec((sblk, HIST_ROWS, D_POOL), lambda i: (smap(i), 0, 0)),
    ]
    in_specs = x_specs + [
        _resident((1, D_MODEL)),
        _resident((_QKV_ROWS, D_MODEL)),
        _resident((_REST_ROWS, D_MODEL)),
        _resident((GATE_ROWS, D_MODEL)),
        _resident((GATE_ROWS, tm)),
        _resident((1, D_MODEL)),
        _resident((len(POOL_WINDOWS), POOL_GROUP, POOL_GROUP)),
        _resident((1, D_POOL)),
        _resident((D_MODEL, D_MODEL)),
        _resident((D_POOL, D_MODEL)),
        _resident((D_MODEL, D_MODEL)),
    ] + state_specs
    rider_specs = [pl.BlockSpec((r.shape[0] // steps, r.shape[1]), lambda i: (i, 0)) for r in riders]
    out_shape = [
        jax.ShapeDtypeStruct((t, D_MODEL), F32),
        jax.ShapeDtypeStruct(c0.shape, F32),
        jax.ShapeDtypeStruct(n0.shape, F32),
        jax.ShapeDtypeStruct(m0.shape, F32),
        jax.ShapeDtypeStruct(hist0.shape, F32),
    ] + [jax.ShapeDtypeStruct(r.shape, BF16) for r in riders]
    scratch = [
        pltpu.VMEM((D_MODEL + GATE_ROWS, D_MODEL), BF16),
        pltpu.VMEM((tm, D_MODEL), BF16),
    ] + slots
    x1, c1, n1, m1, hist1, *cast = pl.pallas_call(
        body,
        grid=(steps,),
        in_specs=in_specs + rider_specs,
        out_specs=([x1_spec, state_specs[0] if carry else pl.BlockSpec(memory_space=pl.ANY)] + state_specs[1:]
                   + rider_specs),
        out_shape=out_shape,
        scratch_shapes=scratch,
        compiler_params=pltpu.CompilerParams(
            dimension_semantics=("arbitrary",), vmem_limit_bytes=V7X_VMEM_LIMIT_BYTES),
        name="mixer_carry" if carry else "mixer_streams",
    )(*x_args, weights["g_mix"], weights["w_qkv"], weights["w_rest"], weights["wg_t"], bias, weights["g_head"],
      weights["w_grp"], weights["pool_scale"], weights["w_ba"], weights["w_bb"], weights["w_out"], c0, n0, m0, hist0,
      *riders)
    n1 = n1.reshape(n_streams, N_HEADS, HEAD_DIM)
    m1 = m1.reshape(n_streams, N_HEADS, LANES)[:, :, 0]
    return (x1, c1, n1, m1, hist1[:, 1:, :], *cast)


def _ffn(x_sample, x_prompt, w_up, w_down, weights, *, tm):
    ts, tp = x_sample.shape[0], x_prompt.shape[0]
    n_s, n_p = ts // tm, tp // tm
    clamp = lambda v, hi: jnp.clip(v, 0, hi)
    return pl.pallas_call(
        functools.partial(_ffn_kernel, sub=256, n_s=n_s),
        grid=(n_s + n_p,),
        in_specs=[
            pl.BlockSpec((tm, D_MODEL), lambda i: (clamp(i, n_s - 1), 0)),
            pl.BlockSpec((tm, D_MODEL), lambda i: (clamp(i - n_s, n_p - 1), 0)),
            _resident((1, D_MODEL)),
            _resident((D_MODEL, D_FF)),
            _resident((D_FF, D_MODEL)),
            _resident((1, D_MODEL)),
        ],
        out_specs=[
            pl.BlockSpec((tm, D_MODEL), lambda i: (clamp(i, n_s - 1), 0)),
            pl.BlockSpec((tm, D_MODEL), lambda i: (clamp(i - n_s, n_p - 1), 0)),
        ],
        out_shape=[jax.ShapeDtypeStruct((ts, D_MODEL), F32), jax.ShapeDtypeStruct((tp, D_MODEL), F32)],
        compiler_params=pltpu.CompilerParams(
            dimension_semantics=("arbitrary",), vmem_limit_bytes=V7X_VMEM_LIMIT_BYTES),
        name="ffn",
    )(x_sample, x_prompt, weights["g_ffn"], w_up, w_down, weights["g_final"])


def _split_w_in_kernel(w_ref, wqkv_ref, wrest_ref, wg_ref, prev_s):
    i = pl.program_id(0)
    n_qkv = _QKV_ROWS // _SPLIT_ROWS
    cur = w_ref[...]

    @pl.when(i < n_qkv)
    def _():
        wqkv_ref[...] = cur.astype(BF16)

    @pl.when(i == n_qkv)
    def _():
        wg_ref[...] = cur[0:2 * N_HEADS, :]

    @pl.when(i > n_qkv)
    def _():
        wrest_ref[...] = jnp.concatenate([prev_s[2 * N_HEADS:, :], cur[0:2 * N_HEADS, :]], axis=0).astype(BF16)

    prev_s[...] = cur


def _split_w_in(w_in_t):
    n_in = w_in_t.shape[0]
    assert n_in == _REST0 + _REST_ROWS and _QKV_ROWS % _SPLIT_ROWS == 0 and _REST_ROWS % _SPLIT_ROWS == 0
    n_qkv, n_rest = _QKV_ROWS // _SPLIT_ROWS, _REST_ROWS // _SPLIT_ROWS
    return pl.pallas_call(
        _split_w_in_kernel,
        grid=(n_qkv + 1 + n_rest,),
        in_specs=[pl.BlockSpec((_SPLIT_ROWS, D_MODEL), lambda i: (i, 0))],
        out_specs=[
            pl.BlockSpec((_SPLIT_ROWS, D_MODEL), lambda i: (jnp.minimum(i, n_qkv - 1), 0)),
            pl.BlockSpec((_SPLIT_ROWS, D_MODEL), lambda i: (jnp.clip(i - n_qkv - 1, 0, n_rest - 1), 0)),
            pl.BlockSpec((2 * N_HEADS, D_MODEL), lambda i: (0, 0)),
        ],
        out_shape=[
            jax.ShapeDtypeStruct((_QKV_ROWS, D_MODEL), BF16),
            jax.ShapeDtypeStruct((_REST_ROWS, D_MODEL), BF16),
            jax.ShapeDtypeStruct((2 * N_HEADS, D_MODEL), F32),
        ],
        scratch_shapes=[pltpu.VMEM((_SPLIT_ROWS, D_MODEL), F32)],
        compiler_params=pltpu.CompilerParams(dimension_semantics=("arbitrary",)),
        name="split_w_in",
    )(w_in_t)


def _pack_weights(w_in, b_igate, b_fgate, g_norm_mix, g_head, w_pool_grp, pool_scale, w_branch_mlstm,
                  w_branch_pool, w_out, g_norm_ffn, w_up, w_down, g_final):
    d = D_MODEL
    w_qkv, w_rest, wg = _split_w_in(w_in.T)
    wi_t, wf_t = wg[0:N_HEADS], wg[N_HEADS:2 * N_HEADS]
    return {
        "w_qkv": w_qkv,
        "w_rest": w_rest,
        "wg_t": jnp.concatenate([wi_t, wf_t, wf_t, wi_t], axis=0),
        "gate_bias": jnp.concatenate([b_igate, b_fgate, b_fgate, b_igate]).astype(F32),
        "g_mix": g_norm_mix.reshape(1, d),
        "g_head": g_head.reshape(1, d),
        "w_grp": w_pool_grp.astype(BF16),
        "pool_scale": pool_scale.reshape(1, D_POOL),
        "w_ba": w_branch_mlstm.astype(BF16),
        "w_bb": w_branch_pool.astype(BF16),
        "w_out": w_out.astype(BF16),
        "g_ffn": g_norm_ffn.reshape(1, d),
        "g_final": g_final.reshape(1, d),
    }


def kernel(x_prompt, x_sample, state_C, state_n, state_m, state_pool, w_in, b_igate, b_fgate, g_norm_mix, g_head,
           w_pool_grp, pool_scale, w_branch_mlstm, w_branch_pool, w_out, g_norm_ffn, w_up, w_down, g_final):
    depth = w_in.shape[0]
    bp, sp, d = x_prompt.shape
    bs, ss, _ = x_sample.shape
    assert depth == 1 and bp == 1 and d == D_MODEL
    hp = x_prompt.reshape(bp * sp, d)
    hs = x_sample.reshape(bs * ss, d)
    weights = _pack_weights(w_in[0], b_igate[0], b_fgate[0], g_norm_mix[0], g_head[0], w_pool_grp[0], pool_scale[0],
                            w_branch_mlstm[0], w_branch_pool[0], w_out[0], g_norm_ffn[0], w_up[0], w_down[0],
                            g_final)
    zero_state = (jnp.zeros((bp, N_HEADS, HEAD_DIM, HEAD_DIM), F32), jnp.zeros((bp, D_MODEL), F32),
                  jnp.zeros((bp, N_HEADS), F32), jnp.zeros((bp, HIST_ROWS - 1, D_POOL), F32))
    hp, cp, np_, mp, pp, w_up_b, w_down_b = _mixer(hp, weights, zero_state, tm=256, seg=256, carry=True, pos0=0,
                                                   riders=(w_up[0], w_down[0]))
    sample_state = (state_C[0], state_n[0].reshape(bs, D_MODEL), state_m[0], state_pool[0])
    hs, cs, ns, ms, ps = _mixer(hs, weights, sample_state, tm=8 * ss, seg=ss, carry=False, pos0=PAST_LEN)
    y_sample, y_prompt = _ffn(hs, hp, w_up_b, w_down_b, weights, tm=512)
    y_prompt = y_prompt.reshape(bp, sp, d)
    y_sample = y_sample.reshape(bs, ss, d)
    return (y_prompt, y_sample, cp[None], np_[None], mp[None], pp[None], cs[None], ns[None], ms[None], ps[None])
```

```python
import collections
import functools

import jax
import jax.numpy as jnp
from jax import lax
from jax.experimental import pallas as pl
from jax.experimental.pallas import tpu as pltpu

F32 = jnp.float32
BF16 = jnp.bfloat16

D_MODEL = 1024
N_HEADS = 4
HEAD_DIM = 256
D_POOL = 512
POOL_GROUP = 128
POOL_WINDOWS = (2, 4, 8, 16)
HIST_ROWS = 16
POOL_T0 = 32
D_FF = 4096
EPS = 1e-6
PAST_LEN = 2048
LANES = 128
GATE_ROWS = 16
STATE_RING_SLOTS = 8
V7X_VMEM_LIMIT_BYTES = 56 * 1024 * 1024

_K0, _V0, _QKV_ROWS = 1024, 2048, 3072
_REST0 = _QKV_ROWS + 2 * N_HEADS
_P0, _GA0, _GB0, _REST_ROWS = 1024, 1536, 2560, 3584
_SPLIT_ROWS = 512

_Slot = collections.namedtuple("_Slot", "q v kt ktb og ga gb gate abuf pd")
_Weights = collections.namedtuple(
    "_Weights", "gmix wqkv wrest wkg bias ghead wgrp pscale wba wbb wout")
_CARRY_ORDER = "FPFFFFPPFPPFPFP"
_STREAMS_ORDER = "PPPPFPFPFPFFFFF"


def _dot(a, b):
    return jnp.dot(a, b, preferred_element_type=F32)


def _dot_nt(a, b):
    return lax.dot_general(a, b, (((1,), (1,)), ((), ())), preferred_element_type=F32)


def _rmsnorm(x, g):
    return x * lax.rsqrt(jnp.mean(x * x, axis=-1, keepdims=True) + EPS) * g


def _sigmoid(x):
    return 1.0 / (1.0 + jnp.exp(-x))


def _log_sigmoid(x):
    return jnp.minimum(x, 0.0) - jnp.log1p(jnp.exp(-jnp.abs(x)))


def _masks(tm, seg):
    row = lax.broadcasted_iota(jnp.int32, (tm, tm), 0)
    col = lax.broadcasted_iota(jnp.int32, (tm, tm), 1)
    if tm == seg:
        return col <= row, row <= col
    shift = seg.bit_length() - 1
    same = (row >> shift) == (col >> shift)
    return (col <= row) & same, (row <= col) & same


def _init_scratch(wkg_s, wqkv_ref, wgt_ref, *slots):
    wkg_s[0:D_MODEL, :] = wqkv_ref[_K0:_V0, :]
    wkg_s[D_MODEL:D_MODEL + GATE_ROWS, :] = wgt_ref[...].astype(BF16)
    for slot in slots:
        slot.abuf[:, 0:POOL_T0 - HIST_ROWS, :] = jnp.zeros((slot.abuf.shape[0], POOL_T0 - HIST_ROWS, D_POOL), F32)


def _project(x, w, slot, hist, pos, *, tm, seg):
    nseg = tm // seg
    u = _rmsnorm(x, w.gmix[...]).astype(BF16)
    kg = _dot_nt(w.wkg[...], u)
    kt = kg[0:D_MODEL] * (HEAD_DIM ** -0.5)
    slot.kt[...] = kt
    slot.ktb[...] = kt.astype(BF16)
    gates = kg[D_MODEL:D_MODEL + GATE_ROWS] + w.bias[...]
    yield
    p = _dot_nt(u, w.wrest[_P0:_GA0, :])
    for j in range(nseg):
        slot.abuf[j, POOL_T0 - HIST_ROWS:POOL_T0, :] = hist[j]
        slot.abuf[j, POOL_T0:POOL_T0 + seg, :] = p[j * seg:(j + 1) * seg]
    for gi, win in enumerate(POOL_WINDOWS):
        ls = slice(gi * POOL_GROUP, (gi + 1) * POOL_GROUP)
        cnt = jnp.minimum(pos + 1, win).astype(F32)
        for j in range(nseg):
            start = POOL_T0 - 8 * gi
            tot = slot.abuf[j, start:POOL_T0 + seg, ls] + slot.abuf[j, start - 1:POOL_T0 + seg - 1, ls]
            shift = 2
            while shift < win:
                n = tot.shape[0]
                tot = tot[8:n] + tot[8 - shift:n - shift]
                shift *= 2
            tok = slot.abuf[j, POOL_T0:POOL_T0 + seg, ls]
            slot.pd[j * seg:(j + 1) * seg, ls] = (tot / cnt - tok).astype(BF16)
    yield
    slot.q[...] = _dot_nt(u, w.wqkv[0:_K0, :]).astype(BF16)
    _, upper = _masks(tm, seg)
    lf = _log_sigmoid(gates[8:16])
    hi = lf.astype(BF16).astype(F32)
    r1 = lf - hi
    mid = r1.astype(BF16).astype(F32)
    lo = r1 - mid
    pieces = jnp.concatenate([hi, mid, lo, jnp.zeros_like(lo)], axis=0).astype(BF16)
    cs = _dot(pieces, jnp.where(upper, 1.0, 0.0).astype(BF16))
    b = cs[0:8] + cs[8:16] + cs[16:24]
    slot.gate[0:8, :] = gates[0:8] - b
    slot.gate[8:16, :] = lf
    yield
    slot.v[...] = _dot_nt(u, w.wqkv[_V0:_QKV_ROWS, :]).astype(BF16)
    yield
    slot.og[...] = _dot_nt(u, w.wrest[0:_P0, :])
    yield
    slot.ga[...] = _dot_nt(u, w.wrest[_GA0:_GB0, :])
    yield
    slot.gb[...] = _dot_nt(u, w.wrest[_GB0:_REST_ROWS, :])
    yield


def _finish(x, out_ref, rows, w, slot, hcat_s, c_rd, n_rd, m_rd, c_store, n_out, m_out, *, tm, seg):
    nseg = tm // seg
    causal, _ = _masks(tm, seg)

    heads = []
    for h in range(N_HEADS):
        sl = slice(h * HEAD_DIM, (h + 1) * HEAD_DIM)
        a_m = jnp.where(causal, jnp.broadcast_to(slot.gate[h:h + 1, :], (tm, tm)), -jnp.inf)
        mprev = jnp.concatenate(
            [jnp.broadcast_to(m_rd[j:j + 1, h * LANES:h * LANES + 1], (seg, 1)) for j in range(nseg)], axis=0)
        g = jnp.maximum(jnp.max(a_m, axis=-1, keepdims=True), mprev)
        dm = jnp.exp(a_m - g)
        lf_b = jnp.broadcast_to(slot.gate[8 + h:9 + h, :], (tm, tm))
        bcol = jnp.sum(jnp.where(causal, lf_b, 0.0), axis=-1, keepdims=True)
        mcol = bcol + g
        wi = jnp.exp(mprev - g)

        qh = slot.q[:, sl]
        vh = slot.v[:, sl]
        s = _dot(qh, slot.ktb[sl, :]) * dm
        qf = qh.astype(F32)
        qc_parts, qn_parts = [], []
        for j in range(nseg):
            rs = slice(j * seg, (j + 1) * seg)
            qc_parts.append(_dot(qh[rs], c_rd[j, h].astype(BF16)))
            qn_parts.append(jnp.sum(qf[rs] * n_rd[j:j + 1, sl], axis=-1, keepdims=True))
        qc = jnp.concatenate(qc_parts, axis=0)
        qn = jnp.concatenate(qn_parts, axis=0)
        heads.append((sl, dm, mcol, wi, vh, s, qc, qn))
    yield

    for h, (sl, dm, mcol, wi, vh, s, qc, qn) in enumerate(heads):
        kth = slot.kt[sl, :]
        rowsum = jnp.sum(s, axis=-1, keepdims=True)
        sv = _dot(s.astype(BF16), vh)
        num = wi * qc + sv
        den = wi * qn + rowsum
        hh = num / jnp.maximum(jnp.abs(den), jnp.exp(-mcol))
        hh = hh * lax.rsqrt(jnp.mean(hh * hh, axis=-1, keepdims=True) + EPS) * w.ghead[:, sl]
        hh = hh * _sigmoid(slot.og[:, sl])
        hcat_s[:, sl] = hh.astype(BF16)

        wrows, n_decayed = [], []
        for j in range(nseg):
            r = (j + 1) * seg - 1
            w_row = dm[r:r + 1, :]
            wrows.append(w_row)
            decay = wi[r:r + 1, :]
            kw = (kth * w_row).astype(BF16)
            c_store(j, h, decay * c_rd[j, h] + _dot(kw, vh))
            n_decayed.append(decay * n_rd[j:j + 1, sl])
            m_out[j:j + 1, h * LANES:(h + 1) * LANES] = jnp.broadcast_to(mcol[r:r + 1, :], (1, LANES))
        ridx = lax.broadcasted_iota(jnp.int32, (GATE_ROWS, tm), 0)
        wmat = jnp.zeros((GATE_ROWS, tm), F32)
        for j in range(nseg):
            wmat = jnp.where(ridx == j, jnp.broadcast_to(wrows[j], (GATE_ROWS, tm)), wmat)
        nupd = _dot_nt(wmat.astype(BF16), slot.ktb[sl, :])
        for j in range(nseg):
            n_out[j:j + 1, sl] = n_decayed[j] + nupd[j:j + 1, :]
        yield

    grp_out = [_dot(slot.pd[:, gi * POOL_GROUP:(gi + 1) * POOL_GROUP], w.wgrp[gi])
               for gi in range(len(POOL_WINDOWS))]
    pooled = jnp.concatenate(grp_out, axis=-1) * w.pscale[...]
    yield

    branch_a = _dot(hcat_s[...], w.wba[...])
    branch_b = _dot(pooled.astype(BF16), w.wbb[...])
    yield
    mixed = _sigmoid(slot.ga[...]) * branch_a + _sigmoid(slot.gb[...]) * branch_b
    out_ref[rows, :] = x + _dot(mixed.astype(BF16), w.wout[...])
    yield


def _run(order, **gens):
    for name in order:
        next(gens[name], None)
    for gen in gens.values():
        for _ in gen:
            pass


class _StateRing:
    def __init__(self, dst_hbm, ring, sems, first_stream):
        self.dst, self.ring, self.sems, self.first = dst_hbm, ring, sems, first_stream
        self.count = 0
        self.in_flight = {}

    def store(self, j, h, value):
        slot = self.count % self.ring.shape[0]
        self.count += 1
        if slot in self.in_flight:
            self.drain()
        self.ring[slot] = value
        copy = pltpu.make_async_copy(self.ring.at[slot], self.dst.at[self.first + j, h], self.sems.at[slot])
        copy.start()
        self.in_flight[slot] = copy

    def drain(self):
        for slot in sorted(self.in_flight):
            self.in_flight.pop(slot).wait()


def _carry_kernel(x_ref, xnext_ref, gmix_ref, wqkv_ref, wrest_ref, wgt_ref, bias_ref, ghead_ref, wgrp_ref,
                  pscale_ref, wba_ref, wbb_ref, wout_ref, c_in, n_in, m_in, hist_in, *rest, tm, n_riders):
    i = pl.program_id(0)
    rider_in, rest = rest[:n_riders], rest[n_riders:]
    x1_ref, c_out, n_out, m_out, hist_out = rest[:5]
    rider_out, (wkg_s, hcat_s, *slot_refs) = rest[5:5 + n_riders], rest[5 + n_riders:]
    w = _Weights(gmix_ref, wqkv_ref, wrest_ref, wkg_s, bias_ref, ghead_ref, wgrp_ref, pscale_ref,
                 wba_ref, wbb_ref, wout_ref)
    nslot = len(_Slot._fields)
    slot_a, slot_b = _Slot(*slot_refs[:nslot]), _Slot(*slot_refs[nslot:])
    kw = dict(tm=tm, seg=tm)
    for src, dst in zip(rider_in, rider_out):
        dst[...] = src[...].astype(BF16)

    row = lax.broadcasted_iota(jnp.int32, (tm, POOL_GROUP), 0)

    @pl.when(i == 0)
    def _():
        _init_scratch(wkg_s, wqkv_ref, wgt_ref, slot_a, slot_b)
        c_out[...] = c_in[...]
        n_out[...] = n_in[...]
        m_out[...] = m_in[...]
        _run("", P=_project(x_ref[0:tm, :], w, slot_a, [hist_in[0]], row, **kw))

    state = (c_out, n_out, m_out)

    def c_store(j, h, value):
        c_out[j, h] = value

    def finish_beside(rows, slot, tile, x_proj, slot_proj):
        new_hist = slot.abuf[0, POOL_T0 + tm - HIST_ROWS:POOL_T0 + tm, :]
        hist_out[0] = new_hist
        _run(_CARRY_ORDER,
             F=_finish(x_ref[rows, :], x1_ref, rows, w, slot, hcat_s, *state, c_store, n_out, m_out, **kw),
             P=_project(x_proj, w, slot_proj, [new_hist], row + (tile + 1) * tm, **kw))

    finish_beside(slice(0, tm), slot_a, 2 * i, x_ref[tm:2 * tm, :], slot_b)
    finish_beside(slice(tm, 2 * tm), slot_b, 2 * i + 1, xnext_ref[...], slot_a)


def _streams_kernel(x_ref, gmix_ref, wqkv_ref, wrest_ref, wgt_ref, bias_ref, ghead_ref, wgrp_ref,
                    pscale_ref, wba_ref, wbb_ref, wout_ref, c_in, n_in, m_in, hist_in,
                    x1_ref, c_hbm, n_out, m_out, hist_out,
                    wkg_s, hcat_s, cring_s, cring_sem, *slot_refs, tm, seg, pos0):
    i = pl.program_id(0)
    nseg = tm // seg
    w = _Weights(gmix_ref, wqkv_ref, wrest_ref, wkg_s, bias_ref, ghead_ref, wgrp_ref, pscale_ref,
                 wba_ref, wbb_ref, wout_ref)
    slot = _Slot(*slot_refs)

    @pl.when(i == 0)
    def _():
        _init_scratch(wkg_s, wqkv_ref, wgt_ref, slot)

    x = x_ref[...]
    pos = lax.broadcasted_iota(jnp.int32, (seg, POOL_GROUP), 0) + pos0
    ring = _StateRing(c_hbm, cring_s, cring_sem, first_stream=i * nseg)
    _run(_STREAMS_ORDER,
         P=_project(x, w, slot, [hist_in[j] for j in range(nseg)], pos, tm=tm, seg=seg),
         F=_finish(x, x1_ref, slice(0, tm), w, slot, hcat_s, c_in, n_in, m_in, ring.store, n_out, m_out,
                   tm=tm, seg=seg))
    ring.drain()
    for j in range(nseg):
        hist_out[j] = slot.abuf[j, POOL_T0 + seg - HIST_ROWS:POOL_T0 + seg, :]


def _ffn_tile(x_ref, y_ref, gffn_ref, wup_ref, wdown_ref, gfin_ref, sub):
    for r in range(x_ref.shape[0] // sub):
        rows = slice(r * sub, (r + 1) * sub)
        x = x_ref[rows, :]
        u = _rmsnorm(x, gffn_ref[...]).astype(BF16)
        hid = _dot(u, wup_ref[...])
        act = jnp.square(jnp.maximum(hid, 0.0)).astype(BF16)
        x2 = x + _dot(act, wdown_ref[...])
        y_ref[rows, :] = _rmsnorm(x2, gfin_ref[...])


def _ffn_kernel(xs_ref, xp_ref, gffn_ref, wup_ref, wdown_ref, gfin_ref, ys_ref, yp_ref, *, sub, n_s):
    i = pl.program_id(0)

    @pl.when(i < n_s)
    def _():
        _ffn_tile(xs_ref, ys_ref, gffn_ref, wup_ref, wdown_ref, gfin_ref, sub)

    @pl.when(i >= n_s)
    def _():
        _ffn_tile(xp_ref, yp_ref, gffn_ref, wup_ref, wdown_ref, gfin_ref, sub)


def _resident(shape):
    zeros = (0,) * len(shape)
    return pl.BlockSpec(shape, lambda i: zeros, pipeline_mode=pl.Buffered(1))


def _slot_scratch(tm, seg):
    return [
        pltpu.VMEM((tm, D_MODEL), BF16),
        pltpu.VMEM((tm, D_MODEL), BF16),
        pltpu.VMEM((D_MODEL, tm), F32),
        pltpu.VMEM((D_MODEL, tm), BF16),
        pltpu.VMEM((tm, D_MODEL), F32),
        pltpu.VMEM((tm, D_MODEL), F32),
        pltpu.VMEM((tm, D_MODEL), F32),
        pltpu.VMEM((GATE_ROWS, tm), F32),
        pltpu.VMEM((tm // seg, POOL_T0 + seg, D_POOL), F32),
        pltpu.VMEM((tm, D_POOL), BF16),
    ]


def _mixer(x, weights, state, *, tm, seg, carry, pos0, riders=()):
    t = x.shape[0]
    nseg = tm // seg
    c0, n0, m0, hist0 = state
    n_streams = c0.shape[0]
    sblk = 1 if carry else nseg
    smap = (lambda i: 0) if carry else (lambda i: i)
    n0 = n0.reshape(n_streams // sblk, sblk, D_MODEL)
    m0 = jnp.repeat(m0, LANES, axis=-1).reshape(n_streams // sblk, sblk, N_HEADS * LANES)
    hist0 = jnp.pad(hist0, ((0, 0), (1, 0), (0, 0)))
    bias = jnp.broadcast_to(weights["gate_bias"][:, None], (GATE_ROWS, tm))

    if carry:
        assert pos0 == 0 and nseg == 1 and t % (2 * tm) == 0
        steps = t // (2 * tm)
        last_tile = t // tm - 1
        x_specs = [pl.BlockSpec((2 * tm, D_MODEL), lambda i: (i, 0)),
                   pl.BlockSpec((tm, D_MODEL), lambda i: (jnp.minimum(2 * i + 2, last_tile), 0))]
        x_args = [x, x]
        x1_spec = pl.BlockSpec((2 * tm, D_MODEL), lambda i: (i, 0))
        body = functools.partial(_carry_kernel, tm=tm, n_riders=len(riders))
        slots = _slot_scratch(tm, seg) + _slot_scratch(tm, seg)
        assert all(r.shape[0] % (16 * steps) == 0 for r in riders)
    else:
        steps = t // tm
        x_specs = [pl.BlockSpec((tm, D_MODEL), lambda i: (i, 0))]
        x_args = [x]
        x1_spec = pl.BlockSpec((tm, D_MODEL), lambda i: (i, 0))
        body = functools.partial(_streams_kernel, tm=tm, seg=seg, pos0=pos0)
        slots = [pltpu.VMEM((STATE_RING_SLOTS, HEAD_DIM, HEAD_DIM), F32),
                 pltpu.SemaphoreType.DMA((STATE_RING_SLOTS,))] + _slot_scratch(tm, seg)

    state_specs = [
        pl.BlockSpec((sblk, N_HEADS, HEAD_DIM, HEAD_DIM), lambda i: (smap(i), 0, 0, 0)),
        pl.BlockSpec((None, sblk, D_MODEL), lambda i: (smap(i), 0, 0)),
        pl.BlockSpec((None, sblk, N_HEADS * LANES), lambda i: (smap(i), 0, 0)),
        pl.BlockSpec((sblk, HIST_ROWS, D_POOL), lambda i: (smap(i), 0, 0)),
    ]
    in_specs = x_specs + [
        _resident((1, D_MODEL)),
        _resident((_QKV_ROWS, D_MODEL)),
        _resident((_REST_ROWS, D_MODEL)),
        _resident((GATE_ROWS, D_MODEL)),
        _resident((GATE_ROWS, tm)),
        _resident((1, D_MODEL)),
        _resident((len(POOL_WINDOWS), POOL_GROUP, POOL_GROUP)),
        _resident((1, D_POOL)),
        _resident((D_MODEL, D_MODEL)),
        _resident((D_POOL, D_MODEL)),
        _resident((D_MODEL, D_MODEL)),
    ] + state_specs
    rider_specs = [pl.BlockSpec((r.shape[0] // steps, r.shape[1]), lambda i: (i, 0)) for r in riders]
    out_shape = [
        jax.ShapeDtypeStruct((t, D_MODEL), F32),
        jax.ShapeDtypeStruct(c0.shape, F32),
        jax.ShapeDtypeStruct(n0.shape, F32),
        jax.ShapeDtypeStruct(m0.shape, F32),
        jax.ShapeDtypeStruct(hist0.shape, F32),
    ] + [jax.ShapeDtypeStruct(r.shape, BF16) for r in riders]
    scratch = [
        pltpu.VMEM((D_MODEL + GATE_ROWS, D_MODEL), BF16),
        pltpu.VMEM((tm, D_MODEL), BF16),
    ] + slots
    x1, c1, n1, m1, hist1, *cast = pl.pallas_call(
        body,
        grid=(steps,),
        in_specs=in_specs + rider_specs,
        out_specs=([x1_spec, state_specs[0] if carry else pl.BlockSpec(memory_space=pl.ANY)] + state_specs[1:]
                   + rider_specs),
        out_shape=out_shape,
        scratch_shapes=scratch,
        compiler_params=pltpu.CompilerParams(
            dimension_semantics=("arbitrary",), vmem_limit_bytes=V7X_VMEM_LIMIT_BYTES),
        name="mixer_carry" if carry else "mixer_streams",
    )(*x_args, weights["g_mix"], weights["w_qkv"], weights["w_rest"], weights["wg_t"], bias, weights["g_head"],
      weights["w_grp"], weights["pool_scale"], weights["w_ba"], weights["w_bb"], weights["w_out"], c0, n0, m0, hist0,
      *riders)
    n1 = n1.reshape(n_streams, N_HEADS, HEAD_DIM)
    m1 = m1.reshape(n_streams, N_HEADS, LANES)[:, :, 0]
    return (x1, c1, n1, m1, hist1[:, 1:, :], *cast)


def _ffn(x_sample, x_prompt, w_up, w_down, weights, *, tm_s, tm_p):
    ts, tp = x_sample.shape[0], x_prompt.shape[0]
    n_s, n_p = ts // tm_s, tp // tm_p
    clamp = lambda v, hi: jnp.clip(v, 0, hi)
    return pl.pallas_call(
        functools.partial(_ffn_kernel, sub=256, n_s=n_s),
        grid=(n_s + n_p,),
        in_specs=[
            pl.BlockSpec((tm_s, D_MODEL), lambda i: (clamp(i, n_s - 1), 0)),
            pl.BlockSpec((tm_p, D_MODEL), lambda i: (clamp(i - n_s, n_p - 1), 0)),
            _resident((1, D_MODEL)),
            _resident((D_MODEL, D_FF)),
            _resident((D_FF, D_MODEL)),
            _resident((1, D_MODEL)),
        ],
        out_specs=[
            pl.BlockSpec((tm_s, D_MODEL), lambda i: (clamp(i, n_s - 1), 0)),
            pl.BlockSpec((tm_p, D_MODEL), lambda i: (clamp(i - n_s, n_p - 1), 0)),
        ],
        out_shape=[jax.ShapeDtypeStruct((ts, D_MODEL), F32), jax.ShapeDtypeStruct((tp, D_MODEL), F32)],
        compiler_params=pltpu.CompilerParams(
            dimension_semantics=("arbitrary",), vmem_limit_bytes=V7X_VMEM_LIMIT_BYTES),
        name="ffn",
    )(x_sample, x_prompt, weights["g_ffn"], w_up, w_down, weights["g_final"])


def _split_w_in_kernel(w_ref, wqkv_ref, wrest_ref, wg_ref, prev_s):
    i = pl.program_id(0)
    n_qkv = _QKV_ROWS // _SPLIT_ROWS
    cur = w_ref[...]

    @pl.when(i < n_qkv)
    def _():
        wqkv_ref[...] = cur.astype(BF16)

    @pl.when(i == n_qkv)
    def _():
        wg_ref[...] = cur[0:2 * N_HEADS, :]

    @pl.when(i > n_qkv)
    def _():
        wrest_ref[...] = jnp.concatenate([prev_s[2 * N_HEADS:, :], cur[0:2 * N_HEADS, :]], axis=0).astype(BF16)

    prev_s[...] = cur


def _split_w_in(w_in_t):
    n_in = w_in_t.shape[0]
    assert n_in == _REST0 + _REST_ROWS and _QKV_ROWS % _SPLIT_ROWS == 0 and _REST_ROWS % _SPLIT_ROWS == 0
    n_qkv, n_rest = _QKV_ROWS // _SPLIT_ROWS, _REST_ROWS // _SPLIT_ROWS
    return pl.pallas_call(
        _split_w_in_kernel,
        grid=(n_qkv + 1 + n_rest,),
        in_specs=[pl.BlockSpec((_SPLIT_ROWS, D_MODEL), lambda i: (i, 0))],
        out_specs=[
            pl.BlockSpec((_SPLIT_ROWS, D_MODEL), lambda i: (jnp.minimum(i, n_qkv - 1), 0)),
            pl.BlockSpec((_SPLIT_ROWS, D_MODEL), lambda i: (jnp.clip(i - n_qkv - 1, 0, n_rest - 1), 0)),
            pl.BlockSpec((2 * N_HEADS, D_MODEL), lambda i: (0, 0)),
        ],
        out_shape=[
            jax.ShapeDtypeStruct((_QKV_ROWS, D_MODEL), BF16),
            jax.ShapeDtypeStruct((_REST_ROWS, D_MODEL), BF16),
            jax.ShapeDtypeStruct((2 * N_HEADS, D_MODEL), F32),
        ],
        scratch_shapes=[pltpu.VMEM((_SPLIT_ROWS, D_MODEL), F32)],
        compiler_params=pltpu.CompilerParams(dimension_semantics=("arbitrary",)),
        name="split_w_in",
    )(w_in_t)


def _pack_weights(w_in, b_igate, b_fgate, g_norm_mix, g_head, w_pool_grp, pool_scale, w_branch_mlstm,
                  w_branch_pool, w_out, g_norm_ffn, w_up, w_down, g_final):
    d = D_MODEL
    w_qkv, w_rest, wg = _split_w_in(w_in.T)
    wi_t, wf_t = wg[0:N_HEADS], wg[N_HEADS:2 * N_HEADS]
    return {
        "w_qkv": w_qkv,
        "w_rest": w_rest,
        "wg_t": jnp.concatenate([wi_t, wf_t, wf_t, wi_t], axis=0),
        "gate_bias": jnp.concatenate([b_igate, b_fgate, b_fgate, b_igate]).astype(F32),
        "g_mix": g_norm_mix.reshape(1, d),
        "g_head": g_head.reshape(1, d),
        "w_grp": w_pool_grp.astype(BF16),
        "pool_scale": pool_scale.reshape(1, D_POOL),
        "w_ba": w_branch_mlstm.astype(BF16),
        "w_bb": w_branch_pool.astype(BF16),
        "w_out": w_out.astype(BF16),
        "g_ffn": g_norm_ffn.reshape(1, d),
        "g_final": g_final.reshape(1, d),
    }


def kernel(x_prompt, x_sample, state_C, state_n, state_m, state_pool, w_in, b_igate, b_fgate, g_norm_mix, g_head,
           w_pool_grp, pool_scale, w_branch_mlstm, w_branch_pool, w_out, g_norm_ffn, w_up, w_down, g_final):
    depth = w_in.shape[0]
    bp, sp, d = x_prompt.shape
    bs, ss, _ = x_sample.shape
    assert depth == 1 and bp == 1 and d == D_MODEL
    hp = x_prompt.reshape(bp * sp, d)
    hs = x_sample.reshape(bs * ss, d)
    weights = _pack_weights(w_in[0], b_igate[0], b_fgate[0], g_norm_mix[0], g_head[0], w_pool_grp[0], pool_scale[0],
                            w_branch_mlstm[0], w_branch_pool[0], w_out[0], g_norm_ffn[0], w_up[0], w_down[0],
                            g_final)
    zero_state = (jnp.zeros((bp, N_HEADS, HEAD_DIM, HEAD_DIM), F32), jnp.zeros((bp, D_MODEL), F32),
                  jnp.zeros((bp, N_HEADS), F32), jnp.zeros((bp, HIST_ROWS - 1, D_POOL), F32))
    hp, cp, np_, mp, pp, w_up_b, w_down_b = _mixer(hp, weights, zero_state, tm=256, seg=256, carry=True, pos0=0,
                                                   riders=(w_up[0], w_down[0]))
    sample_state = (state_C[0], state_n[0].reshape(bs, D_MODEL), state_m[0], state_pool[0])
    hs, cs, ns, ms, ps = _mixer(hs, weights, sample_state, tm=8 * ss, seg=ss, carry=False, pos0=PAST_LEN)
    y_sample, y_prompt = _ffn(hs, hp, w_up_b, w_down_b, weights, tm_s=512, tm_p=1024)
    y_prompt = y_prompt.reshape(bp, sp, d)
    y_sample = y_sample.reshape(bs, ss, d)
    return (y_prompt, y_sample, cp[None], np_[None], mp[None], pp[None], cs[None], ns[None], ms[None], ps[None])
```

```python
import collections
import functools

import jax
import jax.numpy as jnp
from jax import lax
from jax.experimental import pallas as pl
from jax.experimental.pallas import tpu as pltpu

F32 = jnp.float32
BF16 = jnp.bfloat16

D_MODEL = 1024
N_HEADS = 4
HEAD_DIM = 256
D_POOL = 512
POOL_GROUP = 128
POOL_WINDOWS = (2, 4, 8, 16)
HIST_ROWS = 16
POOL_T0 = 32
D_FF = 4096
EPS = 1e-6
PAST_LEN = 2048
LANES = 128
GATE_ROWS = 16
STATE_RING_SLOTS = 16
V7X_VMEM_LIMIT_BYTES = 58 * 1024 * 1024

_K0, _V0, _QKV_ROWS = 1024, 2048, 3072
_REST0 = _QKV_ROWS + 2 * N_HEADS
_P0, _GA0, _GB0, _REST_ROWS = 1024, 1536, 2560, 3584
_SPLIT_ROWS = 512

_Slot = collections.namedtuple("_Slot", "q v kt ktb og ga gb gate abuf pd")
_Weights = collections.namedtuple(
    "_Weights", "gmix wqkv wrest wkg bias ghead wgrp pscale wba wbb wout")
_CARRY_ORDER = "FPFFFFPPFPPFPFP"
_STREAMS_ORDER = "PPPPFPFPFPFFFFF"


def _dot(a, b):
    return jnp.dot(a, b, preferred_element_type=F32)


def _dot_nt(a, b):
    return lax.dot_general(a, b, (((1,), (1,)), ((), ())), preferred_element_type=F32)


def _rmsnorm(x, g):
    return x * lax.rsqrt(jnp.mean(x * x, axis=-1, keepdims=True) + EPS) * g


def _sigmoid(x):
    return 1.0 / (1.0 + jnp.exp(-x))


def _log_sigmoid(x):
    return jnp.minimum(x, 0.0) - jnp.log1p(jnp.exp(-jnp.abs(x)))


def _masks(tm, seg):
    row = lax.broadcasted_iota(jnp.int32, (tm, tm), 0)
    col = lax.broadcasted_iota(jnp.int32, (tm, tm), 1)
    if tm == seg:
        return col <= row, row <= col
    shift = seg.bit_length() - 1
    same = (row >> shift) == (col >> shift)
    return (col <= row) & same, (row <= col) & same


def _init_scratch(wkg_s, wqkv_ref, wgt_ref, *slots):
    wkg_s[0:D_MODEL, :] = wqkv_ref[_K0:_V0, :]
    wkg_s[D_MODEL:D_MODEL + GATE_ROWS, :] = wgt_ref[...].astype(BF16)
    for slot in slots:
        slot.abuf[:, 0:POOL_T0 - HIST_ROWS, :] = jnp.zeros((slot.abuf.shape[0], POOL_T0 - HIST_ROWS, D_POOL), F32)


def _project(x, w, slot, hist, pos, *, tm, seg):
    nseg = tm // seg
    u = _rmsnorm(x, w.gmix[...]).astype(BF16)
    kg = _dot_nt(w.wkg[...], u)
    kt = kg[0:D_MODEL] * (HEAD_DIM ** -0.5)
    slot.kt[...] = kt
    slot.ktb[...] = kt.astype(BF16)
    gates = kg[D_MODEL:D_MODEL + GATE_ROWS] + w.bias[...]
    yield
    p = _dot_nt(u, w.wrest[_P0:_GA0, :])
    for j in range(nseg):
        slot.abuf[j, POOL_T0 - HIST_ROWS:POOL_T0, :] = hist[j]
        slot.abuf[j, POOL_T0:POOL_T0 + seg, :] = p[j * seg:(j + 1) * seg]
    for gi, win in enumerate(POOL_WINDOWS):
        ls = slice(gi * POOL_GROUP, (gi + 1) * POOL_GROUP)
        cnt = jnp.minimum(pos + 1, win).astype(F32)
        for j in range(nseg):
            start = POOL_T0 - 8 * gi
            tot = slot.abuf[j, start:POOL_T0 + seg, ls] + slot.abuf[j, start - 1:POOL_T0 + seg - 1, ls]
            shift = 2
            while shift < win:
                n = tot.shape[0]
                tot = tot[8:n] + tot[8 - shift:n - shift]
                shift *= 2
            tok = slot.abuf[j, POOL_T0:POOL_T0 + seg, ls]
            slot.pd[j * seg:(j + 1) * seg, ls] = (tot / cnt - tok).astype(BF16)
    yield
    slot.q[...] = _dot_nt(u, w.wqkv[0:_K0, :]).astype(BF16)
    _, upper = _masks(tm, seg)
    lf = _log_sigmoid(gates[8:16])
    hi = lf.astype(BF16).astype(F32)
    r1 = lf - hi
    mid = r1.astype(BF16).astype(F32)
    lo = r1 - mid
    pieces = jnp.concatenate([hi, mid, lo, jnp.zeros_like(lo)], axis=0).astype(BF16)
    cs = _dot(pieces, jnp.where(upper, 1.0, 0.0).astype(BF16))
    b = cs[0:8] + cs[8:16] + cs[16:24]
    slot.gate[0:8, :] = gates[0:8] - b
    slot.gate[8:16, :] = lf
    yield
    slot.v[...] = _dot_nt(u, w.wqkv[_V0:_QKV_ROWS, :]).astype(BF16)
    yield
    slot.og[...] = _dot_nt(u, w.wrest[0:_P0, :])
    yield
    slot.ga[...] = _dot_nt(u, w.wrest[_GA0:_GB0, :])
    yield
    slot.gb[...] = _dot_nt(u, w.wrest[_GB0:_REST_ROWS, :])
    yield


def _finish(x, out_ref, rows, w, slot, hcat_s, c_rd, n_rd, m_rd, c_store, n_out, m_out, *, tm, seg):
    nseg = tm // seg
    causal, _ = _masks(tm, seg)

    heads = []
    for h in range(N_HEADS):
        sl = slice(h * HEAD_DIM, (h + 1) * HEAD_DIM)
        a_m = jnp.where(causal, jnp.broadcast_to(slot.gate[h:h + 1, :], (tm, tm)), -jnp.inf)
        mprev = jnp.concatenate(
            [jnp.broadcast_to(m_rd[j:j + 1, h * LANES:h * LANES + 1], (seg, 1)) for j in range(nseg)], axis=0)
        g = jnp.maximum(jnp.max(a_m, axis=-1, keepdims=True), mprev)
        dm = jnp.exp(a_m - g)
        lf_b = jnp.broadcast_to(slot.gate[8 + h:9 + h, :], (tm, tm))
        bcol = jnp.sum(jnp.where(causal, lf_b, 0.0), axis=-1, keepdims=True)
        mcol = bcol + g
        wi = jnp.exp(mprev - g)

        qh = slot.q[:, sl]
        vh = slot.v[:, sl]
        s = _dot(qh, slot.ktb[sl, :]) * dm
        qf = qh.astype(F32)
        qc_parts, qn_parts = [], []
        for j in range(nseg):
            rs = slice(j * seg, (j + 1) * seg)
            qc_parts.append(_dot(qh[rs], c_rd[j, h].astype(BF16)))
            qn_parts.append(jnp.sum(qf[rs] * n_rd[j:j + 1, sl], axis=-1, keepdims=True))
        qc = jnp.concatenate(qc_parts, axis=0)
        qn = jnp.concatenate(qn_parts, axis=0)
        heads.append((sl, dm, mcol, wi, vh, s, qc, qn))
    yield

    for h, (sl, dm, mcol, wi, vh, s, qc, qn) in enumerate(heads):
        kth = slot.kt[sl, :]
        rowsum = jnp.sum(s, axis=-1, keepdims=True)
        sv = _dot(s.astype(BF16), vh)
        num = wi * qc + sv
        den = wi * qn + rowsum
        hh = num / jnp.maximum(jnp.abs(den), jnp.exp(-mcol))
        hh = hh * lax.rsqrt(jnp.mean(hh * hh, axis=-1, keepdims=True) + EPS) * w.ghead[:, sl]
        hh = hh * _sigmoid(slot.og[:, sl])
        hcat_s[:, sl] = hh.astype(BF16)

        wrows, n_decayed = [], []
        for j in range(nseg):
            r = (j + 1) * seg - 1
            w_row = dm[r:r + 1, :]
            wrows.append(w_row)
            decay = wi[r:r + 1, :]
            kw = (kth * w_row).astype(BF16)
            c_store(j, h, decay * c_rd[j, h] + _dot(kw, vh))
            n_decayed.append(decay * n_rd[j:j + 1, sl])
            m_out[j:j + 1, h * LANES:(h + 1) * LANES] = jnp.broadcast_to(mcol[r:r + 1, :], (1, LANES))
        ridx = lax.broadcasted_iota(jnp.int32, (GATE_ROWS, tm), 0)
        wmat = jnp.zeros((GATE_ROWS, tm), F32)
        for j in range(nseg):
            wmat = jnp.where(ridx == j, jnp.broadcast_to(wrows[j], (GATE_ROWS, tm)), wmat)
        nupd = _dot_nt(wmat.astype(BF16), slot.ktb[sl, :])
        for j in range(nseg):
            n_out[j:j + 1, sl] = n_decayed[j] + nupd[j:j + 1, :]
        yield

    grp_out = [_dot(slot.pd[:, gi * POOL_GROUP:(gi + 1) * POOL_GROUP], w.wgrp[gi])
               for gi in range(len(POOL_WINDOWS))]
    pooled = jnp.concatenate(grp_out, axis=-1) * w.pscale[...]
    yield

    branch_a = _dot(hcat_s[...], w.wba[...])
    branch_b = _dot(pooled.astype(BF16), w.wbb[...])
    yield
    mixed = _sigmoid(slot.ga[...]) * branch_a + _sigmoid(slot.gb[...]) * branch_b
    out_ref[rows, :] = x + _dot(mixed.astype(BF16), w.wout[...])
    yield


def _run(order, **gens):
    for name in order:
        next(gens[name], None)
    for gen in gens.values():
        for _ in gen:
            pass


class _StateRing:
    def __init__(self, dst_hbm, ring, sems, first_stream, step, last_step, stores_per_step):
        self.dst, self.ring, self.sems, self.first = dst_hbm, ring, sems, first_stream
        self.step, self.last_step = step, last_step
        self.bank = ring.shape[0] // 2
        assert stores_per_step % ring.shape[0] == 0
        self.count = 0

    def _copy(self, slot, stream, h):
        return pltpu.make_async_copy(self.ring.at[slot], self.dst.at[stream, h], self.sems.at[slot])

    def _wait_bank(self, bank):
        for slot in range(bank * self.bank, (bank + 1) * self.bank):
            self._copy(slot, 0, 0).wait()

    def store(self, j, h, value):
        slot = self.count % self.ring.shape[0]
        if slot % self.bank == 0:
            bank = slot // self.bank
            if self.count < self.ring.shape[0]:
                pl.when(self.step > 0)(functools.partial(self._wait_bank, bank))
            else:
                self._wait_bank(bank)
        self.count += 1
        self.ring[slot] = value
        self._copy(slot, self.first + j, h).start()

    def finish(self):
        @pl.when(self.step == self.last_step)
        def _():
            self._wait_bank(0)
            self._wait_bank(1)


def _carry_kernel(x_ref, xnext_ref, gmix_ref, wqkv_ref, wrest_ref, wgt_ref, bias_ref, ghead_ref, wgrp_ref,
                  pscale_ref, wba_ref, wbb_ref, wout_ref, c_in, n_in, m_in, hist_in, *rest, tm, n_riders):
    i = pl.program_id(0)
    rider_in, rest = rest[:n_riders], rest[n_riders:]
    x1_ref, c_out, n_out, m_out, hist_out = rest[:5]
    rider_out, (wkg_s, hcat_s, *slot_refs) = rest[5:5 + n_riders], rest[5 + n_riders:]
    w = _Weights(gmix_ref, wqkv_ref, wrest_ref, wkg_s, bias_ref, ghead_ref, wgrp_ref, pscale_ref,
                 wba_ref, wbb_ref, wout_ref)
    nslot = len(_Slot._fields)
    slot_a, slot_b = _Slot(*slot_refs[:nslot]), _Slot(*slot_refs[nslot:])
    kw = dict(tm=tm, seg=tm)
    for src, dst in zip(rider_in, rider_out):
        dst[...] = src[...].astype(BF16)

    row = lax.broadcasted_iota(jnp.int32, (tm, POOL_GROUP), 0)

    @pl.when(i == 0)
    def _():
        _init_scratch(wkg_s, wqkv_ref, wgt_ref, slot_a, slot_b)
        c_out[...] = c_in[...]
        n_out[...] = n_in[...]
        m_out[...] = m_in[...]
        _run("", P=_project(x_ref[0:tm, :], w, slot_a, [hist_in[0]], row, **kw))

    state = (c_out, n_out, m_out)

    def c_store(j, h, value):
        c_out[j, h] = value

    def finish_beside(rows, slot, tile, x_proj, slot_proj):
        new_hist = slot.abuf[0, POOL_T0 + tm - HIST_ROWS:POOL_T0 + tm, :]
        hist_out[0] = new_hist
        _run(_CARRY_ORDER,
             F=_finish(x_ref[rows, :], x1_ref, rows, w, slot, hcat_s, *state, c_store, n_out, m_out, **kw),
             P=_project(x_proj, w, slot_proj, [new_hist], row + (tile + 1) * tm, **kw))

    finish_beside(slice(0, tm), slot_a, 2 * i, x_ref[tm:2 * tm, :], slot_b)
    finish_beside(slice(tm, 2 * tm), slot_b, 2 * i + 1, xnext_ref[...], slot_a)


def _streams_kernel(x_ref, gmix_ref, wqkv_ref, wrest_ref, wgt_ref, bias_ref, ghead_ref, wgrp_ref,
                    pscale_ref, wba_ref, wbb_ref, wout_ref, c_in, n_in, m_in, hist_in,
                    x1_ref, c_hbm, n_out, m_out, hist_out,
                    wkg_s, hcat_s, cring_s, cring_sem, *slot_refs, tm, seg, pos0):
    i = pl.program_id(0)
    nseg = tm // seg
    w = _Weights(gmix_ref, wqkv_ref, wrest_ref, wkg_s, bias_ref, ghead_ref, wgrp_ref, pscale_ref,
                 wba_ref, wbb_ref, wout_ref)
    slot = _Slot(*slot_refs)

    @pl.when(i == 0)
    def _():
        _init_scratch(wkg_s, wqkv_ref, wgt_ref, slot)

    x = x_ref[...]
    pos = lax.broadcasted_iota(jnp.int32, (seg, POOL_GROUP), 0) + pos0
    ring = _StateRing(c_hbm, cring_s, cring_sem, first_stream=i * nseg, step=i, last_step=pl.num_programs(0) - 1,
                      stores_per_step=nseg * N_HEADS)
    _run(_STREAMS_ORDER,
         P=_project(x, w, slot, [hist_in[j] for j in range(nseg)], pos, tm=tm, seg=seg),
         F=_finish(x, x1_ref, slice(0, tm), w, slot, hcat_s, c_in, n_in, m_in, ring.store, n_out, m_out,
                   tm=tm, seg=seg))
    ring.finish()
    for j in range(nseg):
        hist_out[j] = slot.abuf[j, POOL_T0 + seg - HIST_ROWS:POOL_T0 + seg, :]


def _ffn_tile(x_ref, y_ref, gffn_ref, wup_ref, wdown_ref, gfin_ref, sub):
    for r in range(x_ref.shape[0] // sub):
        rows = slice(r * sub, (r + 1) * sub)
        x = x_ref[rows, :]
        u = _rmsnorm(x, gffn_ref[...]).astype(BF16)
        hid = _dot(u, wup_ref[...])
        act = jnp.square(jnp.maximum(hid, 0.0)).astype(BF16)
        x2 = x + _dot(act, wdown_ref[...])
        y_ref[rows, :] = _rmsnorm(x2, gfin_ref[...])


def _ffn_kernel(xs_ref, xp_ref, gffn_ref, wup_ref, wdown_ref, gfin_ref, ys_ref, yp_ref, *, sub, n_s):
    i = pl.program_id(0)

    @pl.when(i < n_s)
    def _():
        _ffn_tile(xs_ref, ys_ref, gffn_ref, wup_ref, wdown_ref, gfin_ref, sub)

    @pl.when(i >= n_s)
    def _():
        _ffn_tile(xp_ref, yp_ref, gffn_ref, wup_ref, wdown_ref, gfin_ref, sub)


def _resident(shape):
    zeros = (0,) * len(shape)
    return pl.BlockSpec(shape, lambda i: zeros, pipeline_mode=pl.Buffered(1))


def _slot_scratch(tm, seg):
    return [
        pltpu.VMEM((tm, D_MODEL), BF16),
        pltpu.VMEM((tm, D_MODEL), BF16),
        pltpu.VMEM((D_MODEL, tm), F32),
        pltpu.VMEM((D_MODEL, tm), BF16),
        pltpu.VMEM((tm, D_MODEL), F32),
        pltpu.VMEM((tm, D_MODEL), F32),
        pltpu.VMEM((tm, D_MODEL), F32),
        pltpu.VMEM((GATE_ROWS, tm), F32),
        pltpu.VMEM((tm // seg, POOL_T0 + seg, D_POOL), F32),
        pltpu.VMEM((tm, D_POOL), BF16),
    ]


def _mixer(x, weights, state, *, tm, seg, carry, pos0, riders=()):
    t = x.shape[0]
    nseg = tm // seg
    c0, n0, m0, hist0 = state
    n_streams = c0.shape[0]
    sblk = 1 if carry else nseg
    smap = (lambda i: 0) if carry else (lambda i: i)
    n0 = n0.reshape(n_streams // sblk, sblk, D_MODEL)
    m0 = jnp.repeat(m0, LANES, axis=-1).reshape(n_streams // sblk, sblk, N_HEADS * LANES)
    hist0 = jnp.pad(hist0, ((0, 0), (1, 0), (0, 0)))
    bias = jnp.broadcast_to(weights["gate_bias"][:, None], (GATE_ROWS, tm))

    if carry:
        assert pos0 == 0 and nseg == 1 and t % (2 * tm) == 0
        steps = t // (2 * tm)
        last_tile = t // tm - 1
        x_specs = [pl.BlockSpec((2 * tm, D_MODEL), lambda i: (i, 0)),
                   pl.BlockSpec((tm, D_MODEL), lambda i: (jnp.minimum(2 * i + 2, last_tile), 0))]
        x_args = [x, x]
        x1_spec = pl.BlockSpec((2 * tm, D_MODEL), lambda i: (i, 0))
        body = functools.partial(_carry_kernel, tm=tm, n_riders=len(riders))
        slots = _slot_scratch(tm, seg) + _slot_scratch(tm, seg)
        assert all(r.shape[0] % (16 * steps) == 0 for r in riders)
    else:
        steps = t // tm
        x_specs = [pl.BlockSpec((tm, D_MODEL), lambda i: (i, 0))]
        x_args = [x]
        x1_spec = pl.BlockSpec((tm, D_MODEL), lambda i: (i, 0))
        body = functools.partial(_streams_kernel, tm=tm, seg=seg, pos0=pos0)
        slots = [pltpu.VMEM((STATE_RING_SLOTS, HEAD_DIM, HEAD_DIM), F32),
                 pltpu.SemaphoreType.DMA((STATE_RING_SLOTS,))] + _slot_scratch(tm, seg)

    state_specs = [
        pl.BlockSpec((sblk, N_HEADS, HEAD_DIM, HEAD_DIM), lambda i: (smap(i), 0, 0, 0)),
        pl.BlockSpec((None, sblk, D_MODEL), lambda i: (smap(i), 0, 0)),
        pl.BlockSpec((None, sblk, N_HEADS * LANES), lambda i: (smap(i), 0, 0)),
        pl.BlockSpec((sblk, HIST_ROWS, D_POOL), lambda i: (smap(i), 0, 0)),
    ]
    in_specs = x_specs + [
        _resident((1, D_MODEL)),
        _resident((_QKV_ROWS, D_MODEL)),
        _resident((_REST_ROWS, D_MODEL)),
        _resident((GATE_ROWS, D_MODEL)),
        _resident((GATE_ROWS, tm)),
        _resident((1, D_MODEL)),
        _resident((len(POOL_WINDOWS), POOL_GROUP, POOL_GROUP)),
        _resident((1, D_POOL)),
        _resident((D_MODEL, D_MODEL)),
        _resident((D_POOL, D_MODEL)),
        _resident((D_MODEL, D_MODEL)),
    ] + state_specs
    rider_specs = [pl.BlockSpec((r.shape[0] // steps, r.shape[1]), lambda i: (i, 0)) for r in riders]
    out_shape = [
        jax.ShapeDtypeStruct((t, D_MODEL), F32),
        jax.ShapeDtypeStruct(c0.shape, F32),
        jax.ShapeDtypeStruct(n0.shape, F32),
        jax.ShapeDtypeStruct(m0.shape, F32),
        jax.ShapeDtypeStruct(hist0.shape, F32),
    ] + [jax.ShapeDtypeStruct(r.shape, BF16) for r in riders]
    scratch = [
        pltpu.VMEM((D_MODEL + GATE_ROWS, D_MODEL), BF16),
        pltpu.VMEM((tm, D_MODEL), BF16),
    ] + slots
    x1, c1, n1, m1, hist1, *cast = pl.pallas_call(
        body,
        grid=(steps,),
        in_specs=in_specs + rider_specs,
        out_specs=([x1_spec, state_specs[0] if carry else pl.BlockSpec(memory_space=pl.ANY)] + state_specs[1:]
                   + rider_specs),
        out_shape=out_shape,
        scratch_shapes=scratch,
        compiler_params=pltpu.CompilerParams(
            dimension_semantics=("arbitrary",), vmem_limit_bytes=V7X_VMEM_LIMIT_BYTES),
        name="mixer_carry" if carry else "mixer_streams",
    )(*x_args, weights["g_mix"], weights["w_qkv"], weights["w_rest"], weights["wg_t"], bias, weights["g_head"],
      weights["w_grp"], weights["pool_scale"], weights["w_ba"], weights["w_bb"], weights["w_out"], c0, n0, m0, hist0,
      *riders)
    n1 = n1.reshape(n_streams, N_HEADS, HEAD_DIM)
    m1 = m1.reshape(n_streams, N_HEADS, LANES)[:, :, 0]
    return (x1, c1, n1, m1, hist1[:, 1:, :], *cast)


def _ffn(x_sample, x_prompt, w_up, w_down, weights, *, tm_s, tm_p):
    ts, tp = x_sample.shape[0], x_prompt.shape[0]
    n_s, n_p = ts // tm_s, tp // tm_p
    clamp = lambda v, hi: jnp.clip(v, 0, hi)
    return pl.pallas_call(
        functools.partial(_ffn_kernel, sub=256, n_s=n_s),
        grid=(n_s + n_p,),
        in_specs=[
            pl.BlockSpec((tm_s, D_MODEL), lambda i: (clamp(i, n_s - 1), 0)),
            pl.BlockSpec((tm_p, D_MODEL), lambda i: (clamp(i - n_s, n_p - 1), 0)),
            _resident((1, D_MODEL)),
            _resident((D_MODEL, D_FF)),
            _resident((D_FF, D_MODEL)),
            _resident((1, D_MODEL)),
        ],
        out_specs=[
            pl.BlockSpec((tm_s, D_MODEL), lambda i: (clamp(i, n_s - 1), 0)),
            pl.BlockSpec((tm_p, D_MODEL), lambda i: (clamp(i - n_s, n_p - 1), 0)),
        ],
        out_shape=[jax.ShapeDtypeStruct((ts, D_MODEL), F32), jax.ShapeDtypeStruct((tp, D_MODEL), F32)],
        compiler_params=pltpu.CompilerParams(
            dimension_semantics=("arbitrary",), vmem_limit_bytes=V7X_VMEM_LIMIT_BYTES),
        name="ffn",
    )(x_sample, x_prompt, weights["g_ffn"], w_up, w_down, weights["g_final"])


def _split_w_in_kernel(w_ref, wqkv_ref, wrest_ref, wg_ref, prev_s):
    i = pl.program_id(0)
    n_qkv = _QKV_ROWS // _SPLIT_ROWS
    cur = w_ref[...]

    @pl.when(i < n_qkv)
    def _():
        wqkv_ref[...] = cur.astype(BF16)

    @pl.when(i == n_qkv)
    def _():
        wg_ref[...] = cur[0:2 * N_HEADS, :]

    @pl.when(i > n_qkv)
    def _():
        wrest_ref[...] = jnp.concatenate([prev_s[2 * N_HEADS:, :], cur[0:2 * N_HEADS, :]], axis=0).astype(BF16)

    prev_s[...] = cur


def _split_w_in(w_in_t):
    n_in = w_in_t.shape[0]
    assert n_in == _REST0 + _REST_ROWS and _QKV_ROWS % _SPLIT_ROWS == 0 and _REST_ROWS % _SPLIT_ROWS == 0
    n_qkv, n_rest = _QKV_ROWS // _SPLIT_ROWS, _REST_ROWS // _SPLIT_ROWS
    return pl.pallas_call(
        _split_w_in_kernel,
        grid=(n_qkv + 1 + n_rest,),
        in_specs=[pl.BlockSpec((_SPLIT_ROWS, D_MODEL), lambda i: (i, 0))],
        out_specs=[
            pl.BlockSpec((_SPLIT_ROWS, D_MODEL), lambda i: (jnp.minimum(i, n_qkv - 1), 0)),
            pl.BlockSpec((_SPLIT_ROWS, D_MODEL), lambda i: (jnp.clip(i - n_qkv - 1, 0, n_rest - 1), 0)),
            pl.BlockSpec((2 * N_HEADS, D_MODEL), lambda i: (0, 0)),
        ],
        out_shape=[
            jax.ShapeDtypeStruct((_QKV_ROWS, D_MODEL), BF16),
            jax.ShapeDtypeStruct((_REST_ROWS, D_MODEL), BF16),
            jax.ShapeDtypeStruct((2 * N_HEADS, D_MODEL), F32),
        ],
        scratch_shapes=[pltpu.VMEM((_SPLIT_ROWS, D_MODEL), F32)],
        compiler_params=pltpu.CompilerParams(dimension_semantics=("arbitrary",)),
        name="split_w_in",
    )(w_in_t)


def _pack_weights(w_in, b_igate, b_fgate, g_norm_mix, g_head, w_pool_grp, pool_scale, w_branch_mlstm,
                  w_branch_pool, w_out, g_norm_ffn, w_up, w_down, g_final):
    d = D_MODEL
    w_qkv, w_rest, wg = _split_w_in(w_in.T)
    wi_t, wf_t = wg[0:N_HEADS], wg[N_HEADS:2 * N_HEADS]
    return {
        "w_qkv": w_qkv,
        "w_rest": w_rest,
        "wg_t": jnp.concatenate([wi_t, wf_t, wf_t, wi_t], axis=0),
        "gate_bias": jnp.concatenate([b_igate, b_fgate, b_fgate, b_igate]).astype(F32),
        "g_mix": g_norm_mix.reshape(1, d),
        "g_head": g_head.reshape(1, d),
        "w_grp": w_pool_grp.astype(BF16),
        "pool_scale": pool_scale.reshape(1, D_POOL),
        "w_ba": w_branch_mlstm.astype(BF16),
        "w_bb": w_branch_pool.astype(BF16),
        "w_out": w_out.astype(BF16),
        "g_ffn": g_norm_ffn.reshape(1, d),
        "g_final": g_final.reshape(1, d),
    }


def kernel(x_prompt, x_sample, state_C, state_n, state_m, state_pool, w_in, b_igate, b_fgate, g_norm_mix, g_head,
           w_pool_grp, pool_scale, w_branch_mlstm, w_branch_pool, w_out, g_norm_ffn, w_up, w_down, g_final):
    depth = w_in.shape[0]
    bp, sp, d = x_prompt.shape
    bs, ss, _ = x_sample.shape
    assert depth == 1 and bp == 1 and d == D_MODEL
    hp = x_prompt.reshape(bp * sp, d)
    hs = x_sample.reshape(bs * ss, d)
    weights = _pack_weights(w_in[0], b_igate[0], b_fgate[0], g_norm_mix[0], g_head[0], w_pool_grp[0], pool_scale[0],
                            w_branch_mlstm[0], w_branch_pool[0], w_out[0], g_norm_ffn[0], w_up[0], w_down[0],
                            g_final)
    zero_state = (jnp.zeros((bp, N_HEADS, HEAD_DIM, HEAD_DIM), F32), jnp.zeros((bp, D_MODEL), F32),
                  jnp.zeros((bp, N_HEADS), F32), jnp.zeros((bp, HIST_ROWS - 1, D_POOL), F32))
    hp, cp, np_, mp, pp, w_up_b, w_down_b = _mixer(hp, weights, zero_state, tm=256, seg=256, carry=True, pos0=0,
                                                   riders=(w_up[0], w_down[0]))
    sample_state = (state_C[0], state_n[0].reshape(bs, D_MODEL), state_m[0], state_pool[0])
    hs, cs, ns, ms, ps = _mixer(hs, weights, sample_state, tm=8 * ss, seg=ss, carry=False, pos0=PAST_LEN)
    y_sample, y_prompt = _ffn(hs, hp, w_up_b, w_down_b, weights, tm_s=512, tm_p=1024)
    y_prompt = y_prompt.reshape(bp, sp, d)
    y_sample = y_sample.reshape(bs, ss, d)
    return (y_prompt, y_sample, cp[None], np_[None], mp[None], pp[None], cs[None], ns[None], ms[None], ps[None])
```

```python
import collections
import functools

import jax
import jax.numpy as jnp
from jax import lax
from jax.experimental import pallas as pl
from jax.experimental.pallas import tpu as pltpu

F32 = jnp.float32
BF16 = jnp.bfloat16

D_MODEL = 1024
N_HEADS = 4
HEAD_DIM = 256
D_POOL = 512
POOL_GROUP = 128
POOL_WINDOWS = (2, 4, 8, 16)
HIST_ROWS = 16
POOL_T0 = 32
D_FF = 4096
EPS = 1e-6
PAST_LEN = 2048
LANES = 128
GATE_ROWS = 16
RIDER_HOLD_STEPS = 2
STATE_RING_SLOTS = 16
V7X_VMEM_LIMIT_BYTES = 58 * 1024 * 1024

_K0, _V0, _QKV_ROWS = 1024, 2048, 3072
_REST0 = _QKV_ROWS + 2 * N_HEADS
_P0, _GA0, _GB0, _REST_ROWS = 1024, 1536, 2560, 3584
_SPLIT_ROWS = 512

_Slot = collections.namedtuple("_Slot", "q v kt ktb og ga gb gate abuf pd")
_Weights = collections.namedtuple(
    "_Weights", "gmix wqkv wrest wkg bias ghead wgrp pscale wba wbb wout")
_CARRY_ORDER = "FPFFFFPPFPPFPFP"
_STREAMS_ORDER = "PPPPFPFPFPFFFFF"


def _dot(a, b):
    return jnp.dot(a, b, preferred_element_type=F32)


def _dot_nt(a, b):
    return lax.dot_general(a, b, (((1,), (1,)), ((), ())), preferred_element_type=F32)


def _rmsnorm(x, g):
    return x * lax.rsqrt(jnp.mean(x * x, axis=-1, keepdims=True) + EPS) * g


def _sigmoid(x):
    return 1.0 / (1.0 + jnp.exp(-x))


def _log_sigmoid(x):
    return jnp.minimum(x, 0.0) - jnp.log1p(jnp.exp(-jnp.abs(x)))


def _masks(tm, seg):
    row = lax.broadcasted_iota(jnp.int32, (tm, tm), 0)
    col = lax.broadcasted_iota(jnp.int32, (tm, tm), 1)
    if tm == seg:
        return col <= row, row <= col
    shift = seg.bit_length() - 1
    same = (row >> shift) == (col >> shift)
    return (col <= row) & same, (row <= col) & same


def _init_scratch(wkg_s, wqkv_ref, wgt_ref, *slots):
    wkg_s[0:D_MODEL, :] = wqkv_ref[_K0:_V0, :]
    wkg_s[D_MODEL:D_MODEL + GATE_ROWS, :] = wgt_ref[...].astype(BF16)
    for slot in slots:
        slot.abuf[:, 0:POOL_T0 - HIST_ROWS, :] = jnp.zeros((slot.abuf.shape[0], POOL_T0 - HIST_ROWS, D_POOL), F32)


def _project(x, w, slot, hist, pos, *, tm, seg):
    nseg = tm // seg
    u = _rmsnorm(x, w.gmix[...]).astype(BF16)
    kg = _dot_nt(w.wkg[...], u)
    kt = kg[0:D_MODEL] * (HEAD_DIM ** -0.5)
    slot.kt[...] = kt
    slot.ktb[...] = kt.astype(BF16)
    gates = kg[D_MODEL:D_MODEL + GATE_ROWS] + w.bias[...]
    yield
    p = _dot_nt(u, w.wrest[_P0:_GA0, :])
    for j in range(nseg):
        slot.abuf[j, POOL_T0 - HIST_ROWS:POOL_T0, :] = hist[j]
        slot.abuf[j, POOL_T0:POOL_T0 + seg, :] = p[j * seg:(j + 1) * seg]
    for gi, win in enumerate(POOL_WINDOWS):
        ls = slice(gi * POOL_GROUP, (gi + 1) * POOL_GROUP)
        cnt = jnp.minimum(pos + 1, win).astype(F32)
        for j in range(nseg):
            start = POOL_T0 - 8 * gi
            tot = slot.abuf[j, start:POOL_T0 + seg, ls] + slot.abuf[j, start - 1:POOL_T0 + seg - 1, ls]
            shift = 2
            while shift < win:
                n = tot.shape[0]
                tot = tot[8:n] + tot[8 - shift:n - shift]
                shift *= 2
            tok = slot.abuf[j, POOL_T0:POOL_T0 + seg, ls]
            slot.pd[j * seg:(j + 1) * seg, ls] = (tot / cnt - tok).astype(BF16)
    yield
    slot.q[...] = _dot_nt(u, w.wqkv[0:_K0, :]).astype(BF16)
    _, upper = _masks(tm, seg)
    lf = _log_sigmoid(gates[8:16])
    hi = lf.astype(BF16).astype(F32)
    r1 = lf - hi
    mid = r1.astype(BF16).astype(F32)
    lo = r1 - mid
    pieces = jnp.concatenate([hi, mid, lo, jnp.zeros_like(lo)], axis=0).astype(BF16)
    cs = _dot(pieces, jnp.where(upper, 1.0, 0.0).astype(BF16))
    b = cs[0:8] + cs[8:16] + cs[16:24]
    slot.gate[0:8, :] = gates[0:8] - b
    slot.gate[8:16, :] = lf
    yield
    slot.v[...] = _dot_nt(u, w.wqkv[_V0:_QKV_ROWS, :]).astype(BF16)
    yield
    slot.og[...] = _dot_nt(u, w.wrest[0:_P0, :])
    yield
    slot.ga[...] = _dot_nt(u, w.wrest[_GA0:_GB0, :])
    yield
    slot.gb[...] = _dot_nt(u, w.wrest[_GB0:_REST_ROWS, :])
    yield


def _finish(x, out_ref, rows, w, slot, hcat_s, c_rd, n_rd, m_rd, c_store, n_out, m_out, *, tm, seg):
    nseg = tm // seg
    causal, _ = _masks(tm, seg)

    heads = []
    for h in range(N_HEADS):
        sl = slice(h * HEAD_DIM, (h + 1) * HEAD_DIM)
        a_m = jnp.where(causal, jnp.broadcast_to(slot.gate[h:h + 1, :], (tm, tm)), -jnp.inf)
        mprev = jnp.concatenate(
            [jnp.broadcast_to(m_rd[j:j + 1, h * LANES:h * LANES + 1], (seg, 1)) for j in range(nseg)], axis=0)
        g = jnp.maximum(jnp.max(a_m, axis=-1, keepdims=True), mprev)
        dm = jnp.exp(a_m - g)
        lf_b = jnp.broadcast_to(slot.gate[8 + h:9 + h, :], (tm, tm))
        bcol = jnp.sum(jnp.where(causal, lf_b, 0.0), axis=-1, keepdims=True)
        mcol = bcol + g
        wi = jnp.exp(mprev - g)

        qh = slot.q[:, sl]
        vh = slot.v[:, sl]
        s = _dot(qh, slot.ktb[sl, :]) * dm
        qf = qh.astype(F32)
        qc_parts, qn_parts = [], []
        for j in range(nseg):
            rs = slice(j * seg, (j + 1) * seg)
            qc_parts.append(_dot(qh[rs], c_rd[j, h].astype(BF16)))
            qn_parts.append(jnp.sum(qf[rs] * n_rd[j:j + 1, sl], axis=-1, keepdims=True))
        qc = jnp.concatenate(qc_parts, axis=0)
        qn = jnp.concatenate(qn_parts, axis=0)
        heads.append((sl, dm, mcol, wi, vh, s, qc, qn))
    yield

    for h, (sl, dm, mcol, wi, vh, s, qc, qn) in enumerate(heads):
        kth = slot.kt[sl, :]
        rowsum = jnp.sum(s, axis=-1, keepdims=True)
        sv = _dot(s.astype(BF16), vh)
        num = wi * qc + sv
        den = wi * qn + rowsum
        hh = num / jnp.maximum(jnp.abs(den), jnp.exp(-mcol))
        hh = hh * lax.rsqrt(jnp.mean(hh * hh, axis=-1, keepdims=True) + EPS) * w.ghead[:, sl]
        hh = hh * _sigmoid(slot.og[:, sl])
        hcat_s[:, sl] = hh.astype(BF16)

        wrows, n_decayed = [], []
        for j in range(nseg):
            r = (j + 1) * seg - 1
            w_row = dm[r:r + 1, :]
            wrows.append(w_row)
            decay = wi[r:r + 1, :]
            kw = (kth * w_row).astype(BF16)
            c_store(j, h, decay * c_rd[j, h] + _dot(kw, vh))
            n_decayed.append(decay * n_rd[j:j + 1, sl])
            m_out[j:j + 1, h * LANES:(h + 1) * LANES] = jnp.broadcast_to(mcol[r:r + 1, :], (1, LANES))
        ridx = lax.broadcasted_iota(jnp.int32, (GATE_ROWS, tm), 0)
        wmat = jnp.zeros((GATE_ROWS, tm), F32)
        for j in range(nseg):
            wmat = jnp.where(ridx == j, jnp.broadcast_to(wrows[j], (GATE_ROWS, tm)), wmat)
        nupd = _dot_nt(wmat.astype(BF16), slot.ktb[sl, :])
        for j in range(nseg):
            n_out[j:j + 1, sl] = n_decayed[j] + nupd[j:j + 1, :]
        yield

    grp_out = [_dot(slot.pd[:, gi * POOL_GROUP:(gi + 1) * POOL_GROUP], w.wgrp[gi])
               for gi in range(len(POOL_WINDOWS))]
    pooled = jnp.concatenate(grp_out, axis=-1) * w.pscale[...]
    yield

    branch_a = _dot(hcat_s[...], w.wba[...])
    branch_b = _dot(pooled.astype(BF16), w.wbb[...])
    yield
    mixed = _sigmoid(slot.ga[...]) * branch_a + _sigmoid(slot.gb[...]) * branch_b
    out_ref[rows, :] = x + _dot(mixed.astype(BF16), w.wout[...])
    yield


def _run(order, **gens):
    for name in order:
        next(gens[name], None)
    for gen in gens.values():
        for _ in gen:
            pass


class _StateRing:
    def __init__(self, dst_hbm, ring, sems, first_stream, step, last_step, stores_per_step):
        self.dst, self.ring, self.sems, self.first = dst_hbm, ring, sems, first_stream
        self.step, self.last_step = step, last_step
        self.bank = ring.shape[0] // 2
        assert stores_per_step % ring.shape[0] == 0
        self.count = 0

    def _copy(self, slot, stream, h):
        return pltpu.make_async_copy(self.ring.at[slot], self.dst.at[stream, h], self.sems.at[slot])

    def _wait_bank(self, bank):
        for slot in range(bank * self.bank, (bank + 1) * self.bank):
            self._copy(slot, 0, 0).wait()

    def store(self, j, h, value):
        slot = self.count % self.ring.shape[0]
        if slot % self.bank == 0:
            bank = slot // self.bank
            if self.count < self.ring.shape[0]:
                pl.when(self.step > 0)(functools.partial(self._wait_bank, bank))
            else:
                self._wait_bank(bank)
        self.count += 1
        self.ring[slot] = value
        self._copy(slot, self.first + j, h).start()

    def finish(self):
        @pl.when(self.step == self.last_step)
        def _():
            self._wait_bank(0)
            self._wait_bank(1)


def _carry_kernel(x_ref, xnext_ref, gmix_ref, wqkv_ref, wrest_ref, wgt_ref, bias_ref, ghead_ref, wgrp_ref,
                  pscale_ref, wba_ref, wbb_ref, wout_ref, c_in, n_in, m_in, hist_in, *rest, tm, n_riders):
    i = pl.program_id(0)
    rider_in, rest = rest[:n_riders], rest[n_riders:]
    x1_ref, c_out, n_out, m_out, hist_out = rest[:5]
    rider_out, (wkg_s, hcat_s, *slot_refs) = rest[5:5 + n_riders], rest[5 + n_riders:]
    w = _Weights(gmix_ref, wqkv_ref, wrest_ref, wkg_s, bias_ref, ghead_ref, wgrp_ref, pscale_ref,
                 wba_ref, wbb_ref, wout_ref)
    nslot = len(_Slot._fields)
    slot_a, slot_b = _Slot(*slot_refs[:nslot]), _Slot(*slot_refs[nslot:])
    kw = dict(tm=tm, seg=tm)
    for src, dst in zip(rider_in, rider_out):
        dst[...] = src[...].astype(BF16)

    row = lax.broadcasted_iota(jnp.int32, (tm, POOL_GROUP), 0)

    @pl.when(i == 0)
    def _():
        _init_scratch(wkg_s, wqkv_ref, wgt_ref, slot_a, slot_b)
        c_out[...] = c_in[...]
        n_out[...] = n_in[...]
        m_out[...] = m_in[...]
        _run("", P=_project(x_ref[0:tm, :], w, slot_a, [hist_in[0]], row, **kw))

    state = (c_out, n_out, m_out)

    def c_store(j, h, value):
        c_out[j, h] = value

    def finish_beside(rows, slot, tile, x_proj, slot_proj):
        new_hist = slot.abuf[0, POOL_T0 + tm - HIST_ROWS:POOL_T0 + tm, :]
        hist_out[0] = new_hist
        _run(_CARRY_ORDER,
             F=_finish(x_ref[rows, :], x1_ref, rows, w, slot, hcat_s, *state, c_store, n_out, m_out, **kw),
             P=_project(x_proj, w, slot_proj, [new_hist], row + (tile + 1) * tm, **kw))

    finish_beside(slice(0, tm), slot_a, 2 * i, x_ref[tm:2 * tm, :], slot_b)
    finish_beside(slice(tm, 2 * tm), slot_b, 2 * i + 1, xnext_ref[...], slot_a)


def _streams_kernel(x_ref, gmix_ref, wqkv_ref, wrest_ref, wgt_ref, bias_ref, ghead_ref, wgrp_ref,
                    pscale_ref, wba_ref, wbb_ref, wout_ref, c_in, n_in, m_in, hist_in,
                    x1_ref, c_hbm, n_out, m_out, hist_out,
                    wkg_s, hcat_s, cring_s, cring_sem, *slot_refs, tm, seg, pos0):
    i = pl.program_id(0)
    nseg = tm // seg
    w = _Weights(gmix_ref, wqkv_ref, wrest_ref, wkg_s, bias_ref, ghead_ref, wgrp_ref, pscale_ref,
                 wba_ref, wbb_ref, wout_ref)
    slot = _Slot(*slot_refs)

    @pl.when(i == 0)
    def _():
        _init_scratch(wkg_s, wqkv_ref, wgt_ref, slot)

    x = x_ref[...]
    pos = lax.broadcasted_iota(jnp.int32, (seg, POOL_GROUP), 0) + pos0
    ring = _StateRing(c_hbm, cring_s, cring_sem, first_stream=i * nseg, step=i, last_step=pl.num_programs(0) - 1,
                      stores_per_step=nseg * N_HEADS)
    _run(_STREAMS_ORDER,
         P=_project(x, w, slot, [hist_in[j] for j in range(nseg)], pos, tm=tm, seg=seg),
         F=_finish(x, x1_ref, slice(0, tm), w, slot, hcat_s, c_in, n_in, m_in, ring.store, n_out, m_out,
                   tm=tm, seg=seg))
    ring.finish()
    for j in range(nseg):
        hist_out[j] = slot.abuf[j, POOL_T0 + seg - HIST_ROWS:POOL_T0 + seg, :]


def _ffn_tile(x_ref, y_ref, gffn_ref, wup_ref, wdown_ref, gfin_ref, sub):
    n_sub = x_ref.shape[0] // sub
    rows = [slice(r * sub, (r + 1) * sub) for r in range(n_sub)]
    act = {}

    def up(r):
        u = _rmsnorm(x_ref[rows[r], :], gffn_ref[...]).astype(BF16)
        act[r] = jnp.square(jnp.maximum(_dot(u, wup_ref[...]), 0.0)).astype(BF16)

    def down(r):
        x2 = x_ref[rows[r], :] + _dot(act.pop(r), wdown_ref[...])
        y_ref[rows[r], :] = _rmsnorm(x2, gfin_ref[...])

    up(0)
    for r in range(1, n_sub):
        up(r)
        down(r - 1)
    down(n_sub - 1)


def _ffn_kernel(xs_ref, xp_ref, gffn_ref, wup_ref, wdown_ref, gfin_ref, ys_ref, yp_ref, *, sub, n_s):
    i = pl.program_id(0)

    @pl.when(i < n_s)
    def _():
        _ffn_tile(xs_ref, ys_ref, gffn_ref, wup_ref, wdown_ref, gfin_ref, sub)

    @pl.when(i >= n_s)
    def _():
        _ffn_tile(xp_ref, yp_ref, gffn_ref, wup_ref, wdown_ref, gfin_ref, sub)


def _resident(shape):
    zeros = (0,) * len(shape)
    return pl.BlockSpec(shape, lambda i: zeros, pipeline_mode=pl.Buffered(1))


def _slot_scratch(tm, seg):
    return [
        pltpu.VMEM((tm, D_MODEL), BF16),
        pltpu.VMEM((tm, D_MODEL), BF16),
        pltpu.VMEM((D_MODEL, tm), F32),
        pltpu.VMEM((D_MODEL, tm), BF16),
        pltpu.VMEM((tm, D_MODEL), F32),
        pltpu.VMEM((tm, D_MODEL), F32),
        pltpu.VMEM((tm, D_MODEL), F32),
        pltpu.VMEM((GATE_ROWS, tm), F32),
        pltpu.VMEM((tm // seg, POOL_T0 + seg, D_POOL), F32),
        pltpu.VMEM((tm, D_POOL), BF16),
    ]


def _mixer(x, weights, state, *, tm, seg, carry, pos0, riders=()):
    t = x.shape[0]
    nseg = tm // seg
    c0, n0, m0, hist0 = state
    n_streams = c0.shape[0]
    sblk = 1 if carry else nseg
    smap = (lambda i: 0) if carry else (lambda i: i)
    n0 = n0.reshape(n_streams // sblk, sblk, D_MODEL)
    m0 = jnp.repeat(m0, LANES, axis=-1).reshape(n_streams // sblk, sblk, N_HEADS * LANES)
    hist0 = jnp.pad(hist0, ((0, 0), (1, 0), (0, 0)))
    bias = jnp.broadcast_to(weights["gate_bias"][:, None], (GATE_ROWS, tm))

    if carry:
        assert pos0 == 0 and nseg == 1 and t % (2 * tm) == 0
        steps = t // (2 * tm)
        last_tile = t // tm - 1
        x_specs = [pl.BlockSpec((2 * tm, D_MODEL), lambda i: (i, 0)),
                   pl.BlockSpec((tm, D_MODEL), lambda i: (jnp.minimum(2 * i + 2, last_tile), 0))]
        x_args = [x, x]
        x1_spec = pl.BlockSpec((2 * tm, D_MODEL), lambda i: (i, 0))
        body = functools.partial(_carry_kernel, tm=tm, n_riders=len(riders))
        slots = _slot_scratch(tm, seg) + _slot_scratch(tm, seg)
        assert steps % RIDER_HOLD_STEPS == 0 and all(r.shape[0] % (16 * steps) == 0 for r in riders)
    else:
        steps = t // tm
        x_specs = [pl.BlockSpec((tm, D_MODEL), lambda i: (i, 0))]
        x_args = [x]
        x1_spec = pl.BlockSpec((tm, D_MODEL), lambda i: (i, 0))
        body = functools.partial(_streams_kernel, tm=tm, seg=seg, pos0=pos0)
        slots = [pltpu.VMEM((STATE_RING_SLOTS, HEAD_DIM, HEAD_DIM), F32),
                 pltpu.SemaphoreType.DMA((STATE_RING_SLOTS,))] + _slot_scratch(tm, seg)

    state_specs = [
        pl.BlockSpec((sblk, N_HEADS, HEAD_DIM, HEAD_DIM), lambda i: (smap(i), 0, 0, 0)),
        pl.BlockSpec((None, sblk, D_MODEL), lambda i: (smap(i), 0, 0)),
        pl.BlockSpec((None, sblk, N_HEADS * LANES), lambda i: (smap(i), 0, 0)),
        pl.BlockSpec((sblk, HIST_ROWS, D_POOL), lambda i: (smap(i), 0, 0)),
    ]
    in_specs = x_specs + [
        _resident((1, D_MODEL)),
        _resident((_QKV_ROWS, D_MODEL)),
        _resident((_REST_ROWS, D_MODEL)),
        _resident((GATE_ROWS, D_MODEL)),
        _resident((GATE_ROWS, tm)),
        _resident((1, D_MODEL)),
        _resident((len(POOL_WINDOWS), POOL_GROUP, POOL_GROUP)),
        _resident((1, D_POOL)),
        _resident((D_MODEL, D_MODEL)),
        _resident((D_POOL, D_MODEL)),
        _resident((D_MODEL, D_MODEL)),
    ] + state_specs
    rider_specs = [pl.BlockSpec((r.shape[0] * RIDER_HOLD_STEPS // steps, r.shape[1]),
                                lambda i: (i // RIDER_HOLD_STEPS, 0)) for r in riders]
    out_shape = [
        jax.ShapeDtypeStruct((t, D_MODEL), F32),
        jax.ShapeDtypeStruct(c0.shape, F32),
        jax.ShapeDtypeStruct(n0.shape, F32),
        jax.ShapeDtypeStruct(m0.shape, F32),
        jax.ShapeDtypeStruct(hist0.shape, F32),
    ] + [jax.ShapeDtypeStruct(r.shape, BF16) for r in riders]
    scratch = [
        pltpu.VMEM((D_MODEL + GATE_ROWS, D_MODEL), BF16),
        pltpu.VMEM((tm, D_MODEL), BF16),
    ] + slots
    x1, c1, n1, m1, hist1, *cast = pl.pallas_call(
        body,
        grid=(steps,),
        in_specs=in_specs + rider_specs,
        out_specs=([x1_spec, state_specs[0] if carry else pl.BlockSpec(memory_space=pl.ANY)] + state_specs[1:]
                   + rider_specs),
        out_shape=out_shape,
        scratch_shapes=scratch,
        compiler_params=pltpu.CompilerParams(
            dimension_semantics=("arbitrary",), vmem_limit_bytes=V7X_VMEM_LIMIT_BYTES),
        name="mixer_carry" if carry else "mixer_streams",
    )(*x_args, weights["g_mix"], weights["w_qkv"], weights["w_rest"], weights["wg_t"], bias, weights["g_head"],
      weights["w_grp"], weights["pool_scale"], weights["w_ba"], weights["w_bb"], weights["w_out"], c0, n0, m0, hist0,
      *riders)
    n1 = n1.reshape(n_streams, N_HEADS, HEAD_DIM)
    m1 = m1.reshape(n_streams, N_HEADS, LANES)[:, :, 0]
    return (x1, c1, n1, m1, hist1[:, 1:, :], *cast)


def _ffn(x_sample, x_prompt, w_up, w_down, weights, *, tm_s, tm_p):
    ts, tp = x_sample.shape[0], x_prompt.shape[0]
    n_s, n_p = ts // tm_s, tp // tm_p
    clamp = lambda v, hi: jnp.clip(v, 0, hi)
    return pl.pallas_call(
        functools.partial(_ffn_kernel, sub=256, n_s=n_s),
        grid=(n_s + n_p,),
        in_specs=[
            pl.BlockSpec((tm_s, D_MODEL), lambda i: (clamp(i, n_s - 1), 0)),
            pl.BlockSpec((tm_p, D_MODEL), lambda i: (clamp(i - n_s, n_p - 1), 0)),
            _resident((1, D_MODEL)),
            _resident((D_MODEL, D_FF)),
            _resident((D_FF, D_MODEL)),
            _resident((1, D_MODEL)),
        ],
        out_specs=[
            pl.BlockSpec((tm_s, D_MODEL), lambda i: (clamp(i, n_s - 1), 0)),
            pl.BlockSpec((tm_p, D_MODEL), lambda i: (clamp(i - n_s, n_p - 1), 0)),
        ],
        out_shape=[jax.ShapeDtypeStruct((ts, D_MODEL), F32), jax.ShapeDtypeStruct((tp, D_MODEL), F32)],
        compiler_params=pltpu.CompilerParams(
            dimension_semantics=("arbitrary",), vmem_limit_bytes=V7X_VMEM_LIMIT_BYTES),
        name="ffn",
    )(x_sample, x_prompt, weights["g_ffn"], w_up, w_down, weights["g_final"])


def _split_w_in_kernel(w_ref, wqkv_ref, wrest_ref, wg_ref, prev_s):
    i = pl.program_id(0)
    n_qkv = _QKV_ROWS // _SPLIT_ROWS
    cur = w_ref[...]

    @pl.when(i < n_qkv)
    def _():
        wqkv_ref[...] = cur.astype(BF16)

    @pl.when(i == n_qkv)
    def _():
        wg_ref[...] = cur[0:2 * N_HEADS, :]

    @pl.when(i > n_qkv)
    def _():
        wrest_ref[...] = jnp.concatenate([prev_s[2 * N_HEADS:, :], cur[0:2 * N_HEADS, :]], axis=0).astype(BF16)

    prev_s[...] = cur


def _split_w_in(w_in_t):
    n_in = w_in_t.shape[0]
    assert n_in == _REST0 + _REST_ROWS and _QKV_ROWS % _SPLIT_ROWS == 0 and _REST_ROWS % _SPLIT_ROWS == 0
    n_qkv, n_rest = _QKV_ROWS // _SPLIT_ROWS, _REST_ROWS // _SPLIT_ROWS
    return pl.pallas_call(
        _split_w_in_kernel,
        grid=(n_qkv + 1 + n_rest,),
        in_specs=[pl.BlockSpec((_SPLIT_ROWS, D_MODEL), lambda i: (i, 0))],
        out_specs=[
            pl.BlockSpec((_SPLIT_ROWS, D_MODEL), lambda i: (jnp.minimum(i, n_qkv - 1), 0)),
            pl.BlockSpec((_SPLIT_ROWS, D_MODEL), lambda i: (jnp.clip(i - n_qkv - 1, 0, n_rest - 1), 0)),
            pl.BlockSpec((2 * N_HEADS, D_MODEL), lambda i: (0, 0)),
        ],
        out_shape=[
            jax.ShapeDtypeStruct((_QKV_ROWS, D_MODEL), BF16),
            jax.ShapeDtypeStruct((_REST_ROWS, D_MODEL), BF16),
            jax.ShapeDtypeStruct((2 * N_HEADS, D_MODEL), F32),
        ],
        scratch_shapes=[pltpu.VMEM((_SPLIT_ROWS, D_MODEL), F32)],
        compiler_params=pltpu.CompilerParams(dimension_semantics=("arbitrary",)),
        name="split_w_in",
    )(w_in_t)


def _pack_weights(w_in, b_igate, b_fgate, g_norm_mix, g_head, w_pool_grp, pool_scale, w_branch_mlstm,
                  w_branch_pool, w_out, g_norm_ffn, w_up, w_down, g_final):
    d = D_MODEL
    w_qkv, w_rest, wg = _split_w_in(w_in.T)
    wi_t, wf_t = wg[0:N_HEADS], wg[N_HEADS:2 * N_HEADS]
    return {
        "w_qkv": w_qkv,
        "w_rest": w_rest,
        "wg_t": jnp.concatenate([wi_t, wf_t, wf_t, wi_t], axis=0),
        "gate_bias": jnp.concatenate([b_igate, b_fgate, b_fgate, b_igate]).astype(F32),
        "g_mix": g_norm_mix.reshape(1, d),
        "g_head": g_head.reshape(1, d),
        "w_grp": w_pool_grp.astype(BF16),
        "pool_scale": pool_scale.reshape(1, D_POOL),
        "w_ba": w_branch_mlstm.astype(BF16),
        "w_bb": w_branch_pool.astype(BF16),
        "w_out": w_out.astype(BF16),
        "g_ffn": g_norm_ffn.reshape(1, d),
        "g_final": g_final.reshape(1, d),
    }


def kernel(x_prompt, x_sample, state_C, state_n, state_m, state_pool, w_in, b_igate, b_fgate, g_norm_mix, g_head,
           w_pool_grp, pool_scale, w_branch_mlstm, w_branch_pool, w_out, g_norm_ffn, w_up, w_down, g_final):
    depth = w_in.shape[0]
    bp, sp, d = x_prompt.shape
    bs, ss, _ = x_sample.shape
    assert depth == 1 and bp == 1 and d == D_MODEL
    hp = x_prompt.reshape(bp * sp, d)
    hs = x_sample.reshape(bs * ss, d)
    weights = _pack_weights(w_in[0], b_igate[0], b_fgate[0], g_norm_mix[0], g_head[0], w_pool_grp[0], pool_scale[0],
                            w_branch_mlstm[0], w_branch_pool[0], w_out[0], g_norm_ffn[0], w_up[0], w_down[0],
                            g_final)
    zero_state = (jnp.zeros((bp, N_HEADS, HEAD_DIM, HEAD_DIM), F32), jnp.zeros((bp, D_MODEL), F32),
                  jnp.zeros((bp, N_HEADS), F32), jnp.zeros((bp, HIST_ROWS - 1, D_POOL), F32))
    hp, cp, np_, mp, pp, w_up_b, w_down_b = _mixer(hp, weights, zero_state, tm=256, seg=256, carry=True, pos0=0,
                                                   riders=(w_up[0], w_down[0]))
    sample_state = (state_C[0], state_n[0].reshape(bs, D_MODEL), state_m[0], state_pool[0])
    hs, cs, ns, ms, ps = _mixer(hs, weights, sample_state, tm=8 * ss, seg=ss, carry=False, pos0=PAST_LEN)
    y_sample, y_prompt = _ffn(hs, hp, w_up_b, w_down_b, weights, tm_s=512, tm_p=1024)
    y_prompt = y_prompt.reshape(bp, sp, d)
    y_sample = y_sample.reshape(bs, ss, d)
    return (y_prompt, y_sample, cp[None], np_[None], mp[None], pp[None], cs[None], ns[None], ms[None], ps[None])
```

```python
import collections
import functools

import jax
import jax.numpy as jnp
from jax import lax
from jax.experimental import pallas as pl
from jax.experimental.pallas import tpu as pltpu

F32 = jnp.float32
BF16 = jnp.bfloat16

D_MODEL = 1024
N_HEADS = 4
HEAD_DIM = 256
D_POOL = 512
POOL_GROUP = 128
POOL_WINDOWS = (2, 4, 8, 16)
HIST_ROWS = 16
POOL_T0 = 32
D_FF = 4096
EPS = 1e-6
PAST_LEN = 2048
LANES = 128
GATE_ROWS = 16
RIDER_HOLD_STEPS = 2
STATE_RING_SLOTS = 16
V7X_VMEM_LIMIT_BYTES = 58 * 1024 * 1024

_K0, _V0, _QKV_ROWS = 1024, 2048, 3072
_REST0 = _QKV_ROWS + 2 * N_HEADS
_P0, _GA0, _GB0, _REST_ROWS = 1024, 1536, 2560, 3584
_SPLIT_ROWS = 1024

_Slot = collections.namedtuple("_Slot", "q v kt ktb og ga gb gate abuf pd")
_Weights = collections.namedtuple(
    "_Weights", "gmix wqkv wrest wkg bias ghead wgrp pscale wba wbb wout")
_CARRY_ORDER = "FPFFFFPPFPPFPFP"
_STREAMS_ORDER = "PPPPFPFPFPFFFFF"


def _dot(a, b):
    return jnp.dot(a, b, preferred_element_type=F32)


def _dot_nt(a, b):
    return lax.dot_general(a, b, (((1,), (1,)), ((), ())), preferred_element_type=F32)


def _rmsnorm(x, g):
    return x * lax.rsqrt(jnp.mean(x * x, axis=-1, keepdims=True) + EPS) * g


def _sigmoid(x):
    return 1.0 / (1.0 + jnp.exp(-x))


def _log_sigmoid(x):
    return jnp.minimum(x, 0.0) - jnp.log1p(jnp.exp(-jnp.abs(x)))


def _masks(tm, seg):
    row = lax.broadcasted_iota(jnp.int32, (tm, tm), 0)
    col = lax.broadcasted_iota(jnp.int32, (tm, tm), 1)
    if tm == seg:
        return col <= row, row <= col
    shift = seg.bit_length() - 1
    same = (row >> shift) == (col >> shift)
    return (col <= row) & same, (row <= col) & same


def _init_scratch(wkg_s, wqkv_ref, wgt_ref, *slots):
    wkg_s[0:D_MODEL, :] = wqkv_ref[_K0:_V0, :]
    wkg_s[D_MODEL:D_MODEL + GATE_ROWS, :] = wgt_ref[...].astype(BF16)
    for slot in slots:
        slot.abuf[:, 0:POOL_T0 - HIST_ROWS, :] = jnp.zeros((slot.abuf.shape[0], POOL_T0 - HIST_ROWS, D_POOL), F32)


def _project(x, w, slot, hist, pos, *, tm, seg):
    nseg = tm // seg
    u = _rmsnorm(x, w.gmix[...]).astype(BF16)
    kg = _dot_nt(w.wkg[...], u)
    kt = kg[0:D_MODEL] * (HEAD_DIM ** -0.5)
    slot.kt[...] = kt
    slot.ktb[...] = kt.astype(BF16)
    gates = kg[D_MODEL:D_MODEL + GATE_ROWS] + w.bias[...]
    yield
    p = _dot_nt(u, w.wrest[_P0:_GA0, :])
    for j in range(nseg):
        slot.abuf[j, POOL_T0 - HIST_ROWS:POOL_T0, :] = hist[j]
        slot.abuf[j, POOL_T0:POOL_T0 + seg, :] = p[j * seg:(j + 1) * seg]
    for gi, win in enumerate(POOL_WINDOWS):
        ls = slice(gi * POOL_GROUP, (gi + 1) * POOL_GROUP)
        cnt = jnp.minimum(pos + 1, win).astype(F32)
        for j in range(nseg):
            start = POOL_T0 - 8 * gi
            tot = slot.abuf[j, start:POOL_T0 + seg, ls] + slot.abuf[j, start - 1:POOL_T0 + seg - 1, ls]
            shift = 2
            while shift < win:
                n = tot.shape[0]
                tot = tot[8:n] + tot[8 - shift:n - shift]
                shift *= 2
            tok = slot.abuf[j, POOL_T0:POOL_T0 + seg, ls]
            slot.pd[j * seg:(j + 1) * seg, ls] = (tot / cnt - tok).astype(BF16)
    yield
    slot.q[...] = _dot_nt(u, w.wqkv[0:_K0, :]).astype(BF16)
    _, upper = _masks(tm, seg)
    lf = _log_sigmoid(gates[8:16])
    hi = lf.astype(BF16).astype(F32)
    r1 = lf - hi
    mid = r1.astype(BF16).astype(F32)
    lo = r1 - mid
    pieces = jnp.concatenate([hi, mid, lo, jnp.zeros_like(lo)], axis=0).astype(BF16)
    cs = _dot(pieces, jnp.where(upper, 1.0, 0.0).astype(BF16))
    b = cs[0:8] + cs[8:16] + cs[16:24]
    slot.gate[0:8, :] = gates[0:8] - b
    slot.gate[8:16, :] = lf
    yield
    slot.v[...] = _dot_nt(u, w.wqkv[_V0:_QKV_ROWS, :]).astype(BF16)
    yield
    slot.og[...] = _dot_nt(u, w.wrest[0:_P0, :])
    yield
    slot.ga[...] = _dot_nt(u, w.wrest[_GA0:_GB0, :])
    yield
    slot.gb[...] = _dot_nt(u, w.wrest[_GB0:_REST_ROWS, :])
    yield


def _finish(x, out_ref, rows, w, slot, hcat_s, c_rd, n_rd, m_rd, c_store, n_out, m_out, *, tm, seg):
    nseg = tm // seg
    causal, _ = _masks(tm, seg)

    heads = []
    for h in range(N_HEADS):
        sl = slice(h * HEAD_DIM, (h + 1) * HEAD_DIM)
        a_m = jnp.where(causal, jnp.broadcast_to(slot.gate[h:h + 1, :], (tm, tm)), -jnp.inf)
        mprev = jnp.concatenate(
            [jnp.broadcast_to(m_rd[j:j + 1, h * LANES:h * LANES + 1], (seg, 1)) for j in range(nseg)], axis=0)
        g = jnp.maximum(jnp.max(a_m, axis=-1, keepdims=True), mprev)
        dm = jnp.exp(a_m - g)
        lf_b = jnp.broadcast_to(slot.gate[8 + h:9 + h, :], (tm, tm))
        bcol = jnp.sum(jnp.where(causal, lf_b, 0.0), axis=-1, keepdims=True)
        mcol = bcol + g
        wi = jnp.exp(mprev - g)

        qh = slot.q[:, sl]
        vh = slot.v[:, sl]
        s = _dot(qh, slot.ktb[sl, :]) * dm
        qf = qh.astype(F32)
        qc_parts, qn_parts = [], []
        for j in range(nseg):
            rs = slice(j * seg, (j + 1) * seg)
            qc_parts.append(_dot(qh[rs], c_rd[j, h].astype(BF16)))
            qn_parts.append(jnp.sum(qf[rs] * n_rd[j:j + 1, sl], axis=-1, keepdims=True))
        qc = jnp.concatenate(qc_parts, axis=0)
        qn = jnp.concatenate(qn_parts, axis=0)
        heads.append((sl, dm, mcol, wi, vh, s, qc, qn))
    yield

    for h, (sl, dm, mcol, wi, vh, s, qc, qn) in enumerate(heads):
        kth = slot.kt[sl, :]
        rowsum = jnp.sum(s, axis=-1, keepdims=True)
        sv = _dot(s.astype(BF16), vh)
        num = wi * qc + sv
        den = wi * qn + rowsum
        hh = num / jnp.maximum(jnp.abs(den), jnp.exp(-mcol))
        hh = hh * lax.rsqrt(jnp.mean(hh * hh, axis=-1, keepdims=True) + EPS) * w.ghead[:, sl]
        hh = hh * _sigmoid(slot.og[:, sl])
        hcat_s[:, sl] = hh.astype(BF16)

        wrows, n_decayed = [], []
        for j in range(nseg):
            r = (j + 1) * seg - 1
            w_row = dm[r:r + 1, :]
            wrows.append(w_row)
            decay = wi[r:r + 1, :]
            kw = (kth * w_row).astype(BF16)
            c_store(j, h, decay * c_rd[j, h] + _dot(kw, vh))
            n_decayed.append(decay * n_rd[j:j + 1, sl])
            m_out[j:j + 1, h * LANES:(h + 1) * LANES] = jnp.broadcast_to(mcol[r:r + 1, :], (1, LANES))
        ridx = lax.broadcasted_iota(jnp.int32, (GATE_ROWS, tm), 0)
        wmat = jnp.zeros((GATE_ROWS, tm), F32)
        for j in range(nseg):
            wmat = jnp.where(ridx == j, jnp.broadcast_to(wrows[j], (GATE_ROWS, tm)), wmat)
        nupd = _dot_nt(wmat.astype(BF16), slot.ktb[sl, :])
        for j in range(nseg):
            n_out[j:j + 1, sl] = n_decayed[j] + nupd[j:j + 1, :]
        yield

    grp_out = [_dot(slot.pd[:, gi * POOL_GROUP:(gi + 1) * POOL_GROUP], w.wgrp[gi])
               for gi in range(len(POOL_WINDOWS))]
    pooled = jnp.concatenate(grp_out, axis=-1) * w.pscale[...]
    yield

    branch_a = _dot(hcat_s[...], w.wba[...])
    branch_b = _dot(pooled.astype(BF16), w.wbb[...])
    yield
    mixed = _sigmoid(slot.ga[...]) * branch_a + _sigmoid(slot.gb[...]) * branch_b
    out_ref[rows, :] = x + _dot(mixed.astype(BF16), w.wout[...])
    yield


def _run(order, **gens):
    for name in order:
        next(gens[name], None)
    for gen in gens.values():
        for _ in gen:
            pass


class _StateRing:
    def __init__(self, dst_hbm, ring, sems, first_stream, step, last_step, stores_per_step):
        self.dst, self.ring, self.sems, self.first = dst_hbm, ring, sems, first_stream
        self.step, self.last_step = step, last_step
        self.bank = ring.shape[0] // 2
        assert stores_per_step % ring.shape[0] == 0
        self.count = 0

    def _copy(self, slot, stream, h):
        return pltpu.make_async_copy(self.ring.at[slot], self.dst.at[stream, h], self.sems.at[slot])

    def _wait_bank(self, bank):
        for slot in range(bank * self.bank, (bank + 1) * self.bank):
            self._copy(slot, 0, 0).wait()

    def store(self, j, h, value):
        slot = self.count % self.ring.shape[0]
        if slot % self.bank == 0:
            bank = slot // self.bank
            if self.count < self.ring.shape[0]:
                pl.when(self.step > 0)(functools.partial(self._wait_bank, bank))
            else:
                self._wait_bank(bank)
        self.count += 1
        self.ring[slot] = value
        self._copy(slot, self.first + j, h).start()

    def finish(self):
        @pl.when(self.step == self.last_step)
        def _():
            self._wait_bank(0)
            self._wait_bank(1)


def _carry_kernel(x_ref, xnext_ref, gmix_ref, wqkv_ref, wrest_ref, wgt_ref, bias_ref, ghead_ref, wgrp_ref,
                  pscale_ref, wba_ref, wbb_ref, wout_ref, c_in, n_in, m_in, hist_in, *rest, tm, n_riders):
    i = pl.program_id(0)
    rider_in, rest = rest[:n_riders], rest[n_riders:]
    x1_ref, c_out, n_out, m_out, hist_out = rest[:5]
    rider_out, (wkg_s, hcat_s, *slot_refs) = rest[5:5 + n_riders], rest[5 + n_riders:]
    w = _Weights(gmix_ref, wqkv_ref, wrest_ref, wkg_s, bias_ref, ghead_ref, wgrp_ref, pscale_ref,
                 wba_ref, wbb_ref, wout_ref)
    nslot = len(_Slot._fields)
    slot_a, slot_b = _Slot(*slot_refs[:nslot]), _Slot(*slot_refs[nslot:])
    kw = dict(tm=tm, seg=tm)
    for src, dst in zip(rider_in, rider_out):
        dst[...] = src[...].astype(BF16)

    row = lax.broadcasted_iota(jnp.int32, (tm, POOL_GROUP), 0)

    @pl.when(i == 0)
    def _():
        _init_scratch(wkg_s, wqkv_ref, wgt_ref, slot_a, slot_b)
        c_out[...] = c_in[...]
        n_out[...] = n_in[...]
        m_out[...] = m_in[...]
        _run("", P=_project(x_ref[0:tm, :], w, slot_a, [hist_in[0]], row, **kw))

    state = (c_out, n_out, m_out)

    def c_store(j, h, value):
        c_out[j, h] = value

    def finish_beside(rows, slot, tile, x_proj, slot_proj):
        new_hist = slot.abuf[0, POOL_T0 + tm - HIST_ROWS:POOL_T0 + tm, :]
        hist_out[0] = new_hist
        _run(_CARRY_ORDER,
             F=_finish(x_ref[rows, :], x1_ref, rows, w, slot, hcat_s, *state, c_store, n_out, m_out, **kw),
             P=_project(x_proj, w, slot_proj, [new_hist], row + (tile + 1) * tm, **kw))

    finish_beside(slice(0, tm), slot_a, 2 * i, x_ref[tm:2 * tm, :], slot_b)
    finish_beside(slice(tm, 2 * tm), slot_b, 2 * i + 1, xnext_ref[...], slot_a)


def _streams_kernel(x_ref, gmix_ref, wqkv_ref, wrest_ref, wgt_ref, bias_ref, ghead_ref, wgrp_ref,
                    pscale_ref, wba_ref, wbb_ref, wout_ref, c_in, n_in, m_in, hist_in,
                    x1_ref, c_hbm, n_out, m_out, hist_out,
                    wkg_s, hcat_s, cring_s, cring_sem, *slot_refs, tm, seg, pos0):
    i = pl.program_id(0)
    nseg = tm // seg
    w = _Weights(gmix_ref, wqkv_ref, wrest_ref, wkg_s, bias_ref, ghead_ref, wgrp_ref, pscale_ref,
                 wba_ref, wbb_ref, wout_ref)
    slot = _Slot(*slot_refs)

    @pl.when(i == 0)
    def _():
        _init_scratch(wkg_s, wqkv_ref, wgt_ref, slot)

    x = x_ref[...]
    pos = lax.broadcasted_iota(jnp.int32, (seg, POOL_GROUP), 0) + pos0
    ring = _StateRing(c_hbm, cring_s, cring_sem, first_stream=i * nseg, step=i, last_step=pl.num_programs(0) - 1,
                      stores_per_step=nseg * N_HEADS)
    _run(_STREAMS_ORDER,
         P=_project(x, w, slot, [hist_in[j] for j in range(nseg)], pos, tm=tm, seg=seg),
         F=_finish(x, x1_ref, slice(0, tm), w, slot, hcat_s, c_in, n_in, m_in, ring.store, n_out, m_out,
                   tm=tm, seg=seg))
    ring.finish()
    for j in range(nseg):
        hist_out[j] = slot.abuf[j, POOL_T0 + seg - HIST_ROWS:POOL_T0 + seg, :]


def _ffn_tile(x_ref, y_ref, gffn_ref, wup_ref, wdown_ref, gfin_ref, sub):
    n_sub = x_ref.shape[0] // sub
    rows = [slice(r * sub, (r + 1) * sub) for r in range(n_sub)]
    act = {}

    def up(r):
        u = _rmsnorm(x_ref[rows[r], :], gffn_ref[...]).astype(BF16)
        act[r] = jnp.square(jnp.maximum(_dot(u, wup_ref[...]), 0.0)).astype(BF16)

    def down(r):
        x2 = x_ref[rows[r], :] + _dot(act.pop(r), wdown_ref[...])
        y_ref[rows[r], :] = _rmsnorm(x2, gfin_ref[...])

    up(0)
    for r in range(1, n_sub):
        up(r)
        down(r - 1)
    down(n_sub - 1)


def _ffn_kernel(xs_ref, xp_ref, gffn_ref, wup_ref, wdown_ref, gfin_ref, ys_ref, yp_ref, *, sub, n_s):
    i = pl.program_id(0)

    @pl.when(i < n_s)
    def _():
        _ffn_tile(xs_ref, ys_ref, gffn_ref, wup_ref, wdown_ref, gfin_ref, sub)

    @pl.when(i >= n_s)
    def _():
        _ffn_tile(xp_ref, yp_ref, gffn_ref, wup_ref, wdown_ref, gfin_ref, sub)


def _resident(shape):
    zeros = (0,) * len(shape)
    return pl.BlockSpec(shape, lambda i: zeros, pipeline_mode=pl.Buffered(1))


def _slot_scratch(tm, seg):
    return [
        pltpu.VMEM((tm, D_MODEL), BF16),
        pltpu.VMEM((tm, D_MODEL), BF16),
        pltpu.VMEM((D_MODEL, tm), F32),
        pltpu.VMEM((D_MODEL, tm), BF16),
        pltpu.VMEM((tm, D_MODEL), F32),
        pltpu.VMEM((tm, D_MODEL), F32),
        pltpu.VMEM((tm, D_MODEL), F32),
        pltpu.VMEM((GATE_ROWS, tm), F32),
        pltpu.VMEM((tm // seg, POOL_T0 + seg, D_POOL), F32),
        pltpu.VMEM((tm, D_POOL), BF16),
    ]


def _mixer(x, weights, state, *, tm, seg, carry, pos0, riders=()):
    t = x.shape[0]
    nseg = tm // seg
    c0, n0, m0, hist0 = state
    n_streams = c0.shape[0]
    sblk = 1 if carry else nseg
    smap = (lambda i: 0) if carry else (lambda i: i)
    n0 = n0.reshape(n_streams // sblk, sblk, D_MODEL)
    m0 = jnp.repeat(m0, LANES, axis=-1).reshape(n_streams // sblk, sblk, N_HEADS * LANES)
    hist0 = jnp.pad(hist0, ((0, 0), (1, 0), (0, 0)))
    bias = jnp.broadcast_to(weights["gate_bias"][:, None], (GATE_ROWS, tm))

    if carry:
        assert pos0 == 0 and nseg == 1 and t % (2 * tm) == 0
        steps = t // (2 * tm)
        last_tile = t // tm - 1
        x_specs = [pl.BlockSpec((2 * tm, D_MODEL), lambda i: (i, 0)),
                   pl.BlockSpec((tm, D_MODEL), lambda i: (jnp.minimum(2 * i + 2, last_tile), 0))]
        x_args = [x, x]
        x1_spec = pl.BlockSpec((2 * tm, D_MODEL), lambda i: (i, 0))
        body = functools.partial(_carry_kernel, tm=tm, n_riders=len(riders))
        slots = _slot_scratch(tm, seg) + _slot_scratch(tm, seg)
        assert steps % RIDER_HOLD_STEPS == 0 and all(r.shape[0] % (16 * steps) == 0 for r in riders)
    else:
        steps = t // tm
        x_specs = [pl.BlockSpec((tm, D_MODEL), lambda i: (i, 0))]
        x_args = [x]
        x1_spec = pl.BlockSpec((tm, D_MODEL), lambda i: (i, 0))
        body = functools.partial(_streams_kernel, tm=tm, seg=seg, pos0=pos0)
        slots = [pltpu.VMEM((STATE_RING_SLOTS, HEAD_DIM, HEAD_DIM), F32),
                 pltpu.SemaphoreType.DMA((STATE_RING_SLOTS,))] + _slot_scratch(tm, seg)

    state_specs = [
        pl.BlockSpec((sblk, N_HEADS, HEAD_DIM, HEAD_DIM), lambda i: (smap(i), 0, 0, 0)),
        pl.BlockSpec((None, sblk, D_MODEL), lambda i: (smap(i), 0, 0)),
        pl.BlockSpec((None, sblk, N_HEADS * LANES), lambda i: (smap(i), 0, 0)),
        pl.BlockSpec((sblk, HIST_ROWS, D_POOL), lambda i: (smap(i), 0, 0)),
    ]
    in_specs = x_specs + [
        _resident((1, D_MODEL)),
        _resident((_QKV_ROWS, D_MODEL)),
        _resident((_REST_ROWS, D_MODEL)),
        _resident((GATE_ROWS, D_MODEL)),
        _resident((GATE_ROWS, tm)),
        _resident((1, D_MODEL)),
        _resident((len(POOL_WINDOWS), POOL_GROUP, POOL_GROUP)),
        _resident((1, D_POOL)),
        _resident((D_MODEL, D_MODEL)),
        _resident((D_POOL, D_MODEL)),
        _resident((D_MODEL, D_MODEL)),
    ] + state_specs
    rider_specs = [pl.BlockSpec((r.shape[0] * RIDER_HOLD_STEPS // steps, r.shape[1]),
                                lambda i: (i // RIDER_HOLD_STEPS, 0)) for r in riders]
    out_shape = [
        jax.ShapeDtypeStruct((t, D_MODEL), F32),
        jax.ShapeDtypeStruct(c0.shape, F32),
        jax.ShapeDtypeStruct(n0.shape, F32),
        jax.ShapeDtypeStruct(m0.shape, F32),
        jax.ShapeDtypeStruct(hist0.shape, F32),
    ] + [jax.ShapeDtypeStruct(r.shape, BF16) for r in riders]
    scratch = [
        pltpu.VMEM((D_MODEL + GATE_ROWS, D_MODEL), BF16),
        pltpu.VMEM((tm, D_MODEL), BF16),
    ] + slots
    x1, c1, n1, m1, hist1, *cast = pl.pallas_call(
        body,
        grid=(steps,),
        in_specs=in_specs + rider_specs,
        out_specs=([x1_spec, state_specs[0] if carry else pl.BlockSpec(memory_space=pl.ANY)] + state_specs[1:]
                   + rider_specs),
        out_shape=out_shape,
        scratch_shapes=scratch,
        compiler_params=pltpu.CompilerParams(
            dimension_semantics=("arbitrary",), vmem_limit_bytes=V7X_VMEM_LIMIT_BYTES),
        name="mixer_carry" if carry else "mixer_streams",
    )(*x_args, weights["g_mix"], weights["w_qkv"], weights["w_rest"], weights["wg_t"], bias, weights["g_head"],
      weights["w_grp"], weights["pool_scale"], weights["w_ba"], weights["w_bb"], weights["w_out"], c0, n0, m0, hist0,
      *riders)
    n1 = n1.reshape(n_streams, N_HEADS, HEAD_DIM)
    m1 = m1.reshape(n_streams, N_HEADS, LANES)[:, :, 0]
    return (x1, c1, n1, m1, hist1[:, 1:, :], *cast)


def _ffn(x_sample, x_prompt, w_up, w_down, weights, *, tm_s, tm_p):
    ts, tp = x_sample.shape[0], x_prompt.shape[0]
    n_s, n_p = ts // tm_s, tp // tm_p
    clamp = lambda v, hi: jnp.clip(v, 0, hi)
    return pl.pallas_call(
        functools.partial(_ffn_kernel, sub=256, n_s=n_s),
        grid=(n_s + n_p,),
        in_specs=[
            pl.BlockSpec((tm_s, D_MODEL), lambda i: (clamp(i, n_s - 1), 0)),
            pl.BlockSpec((tm_p, D_MODEL), lambda i: (clamp(i - n_s, n_p - 1), 0)),
            _resident((1, D_MODEL)),
            _resident((D_MODEL, D_FF)),
            _resident((D_FF, D_MODEL)),
            _resident((1, D_MODEL)),
        ],
        out_specs=[
            pl.BlockSpec((tm_s, D_MODEL), lambda i: (clamp(i, n_s - 1), 0)),
            pl.BlockSpec((tm_p, D_MODEL), lambda i: (clamp(i - n_s, n_p - 1), 0)),
        ],
        out_shape=[jax.ShapeDtypeStruct((ts, D_MODEL), F32), jax.ShapeDtypeStruct((tp, D_MODEL), F32)],
        compiler_params=pltpu.CompilerParams(
            dimension_semantics=("arbitrary",), vmem_limit_bytes=V7X_VMEM_LIMIT_BYTES),
        name="ffn",
    )(x_sample, x_prompt, weights["g_ffn"], w_up, w_down, weights["g_final"])


def _split_w_in_kernel(w_ref, wqkv_ref, wrest_ref, wg_ref, prev_s):
    i = pl.program_id(0)
    n_qkv = _QKV_ROWS // _SPLIT_ROWS
    cur = w_ref[...]

    @pl.when(i < n_qkv)
    def _():
        wqkv_ref[...] = cur.astype(BF16)

    @pl.when(i == n_qkv)
    def _():
        wg_ref[...] = cur[0:2 * N_HEADS, :]

    @pl.when(i > n_qkv)
    def _():
        wrest_ref[...] = jnp.concatenate([prev_s[2 * N_HEADS:, :], cur[0:2 * N_HEADS, :]], axis=0).astype(BF16)

    prev_s[...] = cur


def _split_w_in(w_in_t):
    n_in = w_in_t.shape[0]
    assert n_in == _REST0 + _REST_ROWS and _QKV_ROWS % _SPLIT_ROWS == 0
    n_qkv, n_rest = _QKV_ROWS // _SPLIT_ROWS, pl.cdiv(_REST_ROWS, _SPLIT_ROWS)
    last_chunk = pl.cdiv(n_in, _SPLIT_ROWS) - 1
    return pl.pallas_call(
        _split_w_in_kernel,
        grid=(n_qkv + 1 + n_rest,),
        in_specs=[pl.BlockSpec((_SPLIT_ROWS, D_MODEL), lambda i: (jnp.minimum(i, last_chunk), 0))],
        out_specs=[
            pl.BlockSpec((_SPLIT_ROWS, D_MODEL), lambda i: (jnp.minimum(i, n_qkv - 1), 0)),
            pl.BlockSpec((_SPLIT_ROWS, D_MODEL), lambda i: (jnp.clip(i - n_qkv - 1, 0, n_rest - 1), 0)),
            pl.BlockSpec((2 * N_HEADS, D_MODEL), lambda i: (0, 0)),
        ],
        out_shape=[
            jax.ShapeDtypeStruct((_QKV_ROWS, D_MODEL), BF16),
            jax.ShapeDtypeStruct((_REST_ROWS, D_MODEL), BF16),
            jax.ShapeDtypeStruct((2 * N_HEADS, D_MODEL), F32),
        ],
        scratch_shapes=[pltpu.VMEM((_SPLIT_ROWS, D_MODEL), F32)],
        compiler_params=pltpu.CompilerParams(dimension_semantics=("arbitrary",)),
        name="split_w_in",
    )(w_in_t)


def _pack_weights(w_in, b_igate, b_fgate, g_norm_mix, g_head, w_pool_grp, pool_scale, w_branch_mlstm,
                  w_branch_pool, w_out, g_norm_ffn, w_up, w_down, g_final):
    d = D_MODEL
    w_qkv, w_rest, wg = _split_w_in(w_in.T)
    wi_t, wf_t = wg[0:N_HEADS], wg[N_HEADS:2 * N_HEADS]
    return {
        "w_qkv": w_qkv,
        "w_rest": w_rest,
        "wg_t": jnp.concatenate([wi_t, wf_t, wf_t, wi_t], axis=0),
        "gate_bias": jnp.concatenate([b_igate, b_fgate, b_fgate, b_igate]).astype(F32),
        "g_mix": g_norm_mix.reshape(1, d),
        "g_head": g_head.reshape(1, d),
        "w_grp": w_pool_grp.astype(BF16),
        "pool_scale": pool_scale.reshape(1, D_POOL),
        "w_ba": w_branch_mlstm.astype(BF16),
        "w_bb": w_branch_pool.astype(BF16),
        "w_out": w_out.astype(BF16),
        "g_ffn": g_norm_ffn.reshape(1, d),
        "g_final": g_final.reshape(1, d),
    }


def kernel(x_prompt, x_sample, state_C, state_n, state_m, state_pool, w_in, b_igate, b_fgate, g_norm_mix, g_head,
           w_pool_grp, pool_scale, w_branch_mlstm, w_branch_pool, w_out, g_norm_ffn, w_up, w_down, g_final):
    depth = w_in.shape[0]
    bp, sp, d = x_prompt.shape
    bs, ss, _ = x_sample.shape
    assert depth == 1 and bp == 1 and d == D_MODEL
    hp = x_prompt.reshape(bp * sp, d)
    hs = x_sample.reshape(bs * ss, d)
    weights = _pack_weights(w_in[0], b_igate[0], b_fgate[0], g_norm_mix[0], g_head[0], w_pool_grp[0], pool_scale[0],
                            w_branch_mlstm[0], w_branch_pool[0], w_out[0], g_norm_ffn[0], w_up[0], w_down[0],
                            g_final)
    zero_state = (jnp.zeros((bp, N_HEADS, HEAD_DIM, HEAD_DIM), F32), jnp.zeros((bp, D_MODEL), F32),
                  jnp.zeros((bp, N_HEADS), F32), jnp.zeros((bp, HIST_ROWS - 1, D_POOL), F32))
    hp, cp, np_, mp, pp, w_up_b, w_down_b = _mixer(hp, weights, zero_state, tm=256, seg=256, carry=True, pos0=0,
                                                   riders=(w_up[0], w_down[0]))
    sample_state = (state_C[0], state_n[0].reshape(bs, D_MODEL), state_m[0], state_pool[0])
    hs, cs, ns, ms, ps = _mixer(hs, weights, sample_state, tm=8 * ss, seg=ss, carry=False, pos0=PAST_LEN)
    y_sample, y_prompt = _ffn(hs, hp, w_up_b, w_down_b, weights, tm_s=512, tm_p=1024)
    y_prompt = y_prompt.reshape(bp, sp, d)
    y_sample = y_sample.reshape(bs, ss, d)
    return (y_prompt, y_sample, cp[None], np_[None], mp[None], pp[None], cs[None], ns[None], ms[None], ps[None])
```

```python
import collections
import functools

import jax
import jax.numpy as jnp
from jax import lax
from jax.experimental import pallas as pl
from jax.experimental.pallas import tpu as pltpu

F32 = jnp.float32
BF16 = jnp.bfloat16

D_MODEL = 1024
N_HEADS = 4
HEAD_DIM = 256
D_POOL = 512
POOL_GROUP = 128
POOL_WINDOWS = (2, 4, 8, 16)
HIST_ROWS = 16
POOL_T0 = 32
D_FF = 4096
EPS = 1e-6
PAST_LEN = 2048
LANES = 128
GATE_ROWS = 16
RIDER_HOLD_STEPS = 2
STATE_RING_SLOTS = 16
V7X_VMEM_LIMIT_BYTES = 58 * 1024 * 1024

_K0, _V0, _QKV_ROWS = 1024, 2048, 3072
_REST0 = _QKV_ROWS + 2 * N_HEADS
_P0, _GA0, _GB0, _REST_ROWS = 1024, 1536, 2560, 3584
_SPLIT_ROWS = 1024

_Slot = collections.namedtuple("_Slot", "q v kt ktb og ga gb gate abuf pd")
_Weights = collections.namedtuple(
    "_Weights", "gmix wqv wrest wkg bias ghead wgrp pscale wba wbb wout")
_CARRY_ORDER = "FPFFFFPPFPPFPFP"
_STREAMS_ORDER = "PPPPFPFPFPFFFFF"


def _dot(a, b):
    return jnp.dot(a, b, preferred_element_type=F32)


def _dot_nt(a, b):
    return lax.dot_general(a, b, (((1,), (1,)), ((), ())), preferred_element_type=F32)


def _rmsnorm(x, g):
    return x * lax.rsqrt(jnp.mean(x * x, axis=-1, keepdims=True) + EPS) * g


def _sigmoid(x):
    return 1.0 / (1.0 + jnp.exp(-x))


def _log_sigmoid(x):
    return jnp.minimum(x, 0.0) - jnp.log1p(jnp.exp(-jnp.abs(x)))


def _masks(tm, seg):
    row = lax.broadcasted_iota(jnp.int32, (tm, tm), 0)
    col = lax.broadcasted_iota(jnp.int32, (tm, tm), 1)
    if tm == seg:
        return col <= row, row <= col
    shift = seg.bit_length() - 1
    same = (row >> shift) == (col >> shift)
    return (col <= row) & same, (row <= col) & same


def _init_scratch(*slots):
    for slot in slots:
        slot.abuf[:, 0:POOL_T0 - HIST_ROWS, :] = jnp.zeros((slot.abuf.shape[0], POOL_T0 - HIST_ROWS, D_POOL), F32)


def _project(x, w, slot, hist, pos, *, tm, seg):
    nseg = tm // seg
    u = _rmsnorm(x, w.gmix[...]).astype(BF16)
    kg = _dot_nt(w.wkg[...], u)
    kt = kg[0:D_MODEL] * (HEAD_DIM ** -0.5)
    slot.kt[...] = kt
    slot.ktb[...] = kt.astype(BF16)
    gates = kg[D_MODEL:D_MODEL + GATE_ROWS] + w.bias[...]
    yield
    p = _dot(u, w.wrest[:, _P0:_GA0])
    for j in range(nseg):
        slot.abuf[j, POOL_T0 - HIST_ROWS:POOL_T0, :] = hist[j]
        slot.abuf[j, POOL_T0:POOL_T0 + seg, :] = p[j * seg:(j + 1) * seg]
    for gi, win in enumerate(POOL_WINDOWS):
        ls = slice(gi * POOL_GROUP, (gi + 1) * POOL_GROUP)
        cnt = jnp.minimum(pos + 1, win).astype(F32)
        for j in range(nseg):
            start = POOL_T0 - 8 * gi
            tot = slot.abuf[j, start:POOL_T0 + seg, ls] + slot.abuf[j, start - 1:POOL_T0 + seg - 1, ls]
            shift = 2
            while shift < win:
                n = tot.shape[0]
                tot = tot[8:n] + tot[8 - shift:n - shift]
                shift *= 2
            tok = slot.abuf[j, POOL_T0:POOL_T0 + seg, ls]
            slot.pd[j * seg:(j + 1) * seg, ls] = (tot / cnt - tok).astype(BF16)
    yield
    slot.q[...] = _dot(u, w.wqv[:, 0:D_MODEL]).astype(BF16)
    _, upper = _masks(tm, seg)
    lf = _log_sigmoid(gates[8:16])
    hi = lf.astype(BF16).astype(F32)
    r1 = lf - hi
    mid = r1.astype(BF16).astype(F32)
    lo = r1 - mid
    pieces = jnp.concatenate([hi, mid, lo, jnp.zeros_like(lo)], axis=0).astype(BF16)
    cs = _dot(pieces, jnp.where(upper, 1.0, 0.0).astype(BF16))
    b = cs[0:8] + cs[8:16] + cs[16:24]
    slot.gate[0:8, :] = gates[0:8] - b
    slot.gate[8:16, :] = lf
    yield
    slot.v[...] = _dot(u, w.wqv[:, D_MODEL:2 * D_MODEL]).astype(BF16)
    yield
    slot.og[...] = _dot(u, w.wrest[:, 0:_P0])
    yield
    slot.ga[...] = _dot(u, w.wrest[:, _GA0:_GB0])
    yield
    slot.gb[...] = _dot(u, w.wrest[:, _GB0:_REST_ROWS])
    yield


def _finish(x, out_ref, rows, w, slot, hcat_s, c_rd, n_rd, m_rd, c_store, n_out, m_out, *, tm, seg):
    nseg = tm // seg
    causal, _ = _masks(tm, seg)

    heads = []
    for h in range(N_HEADS):
        sl = slice(h * HEAD_DIM, (h + 1) * HEAD_DIM)
        a_m = jnp.where(causal, jnp.broadcast_to(slot.gate[h:h + 1, :], (tm, tm)), -jnp.inf)
        mprev = jnp.concatenate(
            [jnp.broadcast_to(m_rd[j:j + 1, h * LANES:h * LANES + 1], (seg, 1)) for j in range(nseg)], axis=0)
        g = jnp.maximum(jnp.max(a_m, axis=-1, keepdims=True), mprev)
        dm = jnp.exp(a_m - g)
        lf_b = jnp.broadcast_to(slot.gate[8 + h:9 + h, :], (tm, tm))
        bcol = jnp.sum(jnp.where(causal, lf_b, 0.0), axis=-1, keepdims=True)
        mcol = bcol + g
        wi = jnp.exp(mprev - g)

        qh = slot.q[:, sl]
        vh = slot.v[:, sl]
        s = _dot(qh, slot.ktb[sl, :]) * dm
        qf = qh.astype(F32)
        qc_parts, qn_parts = [], []
        for j in range(nseg):
            rs = slice(j * seg, (j + 1) * seg)
            qc_parts.append(_dot(qh[rs], c_rd[j, h].astype(BF16)))
            qn_parts.append(jnp.sum(qf[rs] * n_rd[j:j + 1, sl], axis=-1, keepdims=True))
        qc = jnp.concatenate(qc_parts, axis=0)
        qn = jnp.concatenate(qn_parts, axis=0)
        heads.append((sl, dm, mcol, wi, vh, s, qc, qn))
    yield

    for h, (sl, dm, mcol, wi, vh, s, qc, qn) in enumerate(heads):
        kth = slot.kt[sl, :]
        rowsum = jnp.sum(s, axis=-1, keepdims=True)
        sv = _dot(s.astype(BF16), vh)
        num = wi * qc + sv
        den = wi * qn + rowsum
        hh = num / jnp.maximum(jnp.abs(den), jnp.exp(-mcol))
        hh = hh * lax.rsqrt(jnp.mean(hh * hh, axis=-1, keepdims=True) + EPS) * w.ghead[:, sl]
        hh = hh * _sigmoid(slot.og[:, sl])
        hcat_s[:, sl] = hh.astype(BF16)

        wrows, n_decayed = [], []
        for j in range(nseg):
            r = (j + 1) * seg - 1
            w_row = dm[r:r + 1, :]
            wrows.append(w_row)
            decay = wi[r:r + 1, :]
            kw = (kth * w_row).astype(BF16)
            c_store(j, h, decay * c_rd[j, h] + _dot(kw, vh))
            n_decayed.append(decay * n_rd[j:j + 1, sl])
            m_out[j:j + 1, h * LANES:(h + 1) * LANES] = jnp.broadcast_to(mcol[r:r + 1, :], (1, LANES))
        ridx = lax.broadcasted_iota(jnp.int32, (GATE_ROWS, tm), 0)
        wmat = jnp.zeros((GATE_ROWS, tm), F32)
        for j in range(nseg):
            wmat = jnp.where(ridx == j, jnp.broadcast_to(wrows[j], (GATE_ROWS, tm)), wmat)
        nupd = _dot_nt(wmat.astype(BF16), slot.ktb[sl, :])
        for j in range(nseg):
            n_out[j:j + 1, sl] = n_decayed[j] + nupd[j:j + 1, :]
        yield

    grp_out = [_dot(slot.pd[:, gi * POOL_GROUP:(gi + 1) * POOL_GROUP], w.wgrp[gi])
               for gi in range(len(POOL_WINDOWS))]
    pooled = jnp.concatenate(grp_out, axis=-1) * w.pscale[...]
    yield

    branch_a = _dot(hcat_s[...], w.wba[...])
    branch_b = _dot(pooled.astype(BF16), w.wbb[...])
    yield
    mixed = _sigmoid(slot.ga[...]) * branch_a + _sigmoid(slot.gb[...]) * branch_b
    out_ref[rows, :] = x + _dot(mixed.astype(BF16), w.wout[...])
    yield


def _run(order, **gens):
    for name in order:
        next(gens[name], None)
    for gen in gens.values():
        for _ in gen:
            pass


class _StateRing:
    def __init__(self, dst_hbm, ring, sems, first_stream, step, last_step, stores_per_step):
        self.dst, self.ring, self.sems, self.first = dst_hbm, ring, sems, first_stream
        self.step, self.last_step = step, last_step
        self.bank = ring.shape[0] // 2
        assert stores_per_step % ring.shape[0] == 0
        self.count = 0

    def _copy(self, slot, stream, h):
        return pltpu.make_async_copy(self.ring.at[slot], self.dst.at[stream, h], self.sems.at[slot])

    def _wait_bank(self, bank):
        for slot in range(bank * self.bank, (bank + 1) * self.bank):
            self._copy(slot, 0, 0).wait()

    def store(self, j, h, value):
        slot = self.count % self.ring.shape[0]
        if slot % self.bank == 0:
            bank = slot // self.bank
            if self.count < self.ring.shape[0]:
                pl.when(self.step > 0)(functools.partial(self._wait_bank, bank))
            else:
                self._wait_bank(bank)
        self.count += 1
        self.ring[slot] = value
        self._copy(slot, self.first + j, h).start()

    def finish(self):
        @pl.when(self.step == self.last_step)
        def _():
            self._wait_bank(0)
            self._wait_bank(1)


def _carry_kernel(x_ref, xnext_ref, gmix_ref, wqv_ref, wrest_ref, wkg_ref, bias_ref, ghead_ref, wgrp_ref,
                  pscale_ref, wba_ref, wbb_ref, wout_ref, c_in, n_in, m_in, hist_in, *rest, tm, n_riders):
    i = pl.program_id(0)
    rider_in, rest = rest[:n_riders], rest[n_riders:]
    x1_ref, c_out, n_out, m_out, hist_out = rest[:5]
    rider_out, (hcat_s, *slot_refs) = rest[5:5 + n_riders], rest[5 + n_riders:]
    w = _Weights(gmix_ref, wqv_ref, wrest_ref, wkg_ref, bias_ref, ghead_ref, wgrp_ref, pscale_ref,
                 wba_ref, wbb_ref, wout_ref)
    nslot = len(_Slot._fields)
    slot_a, slot_b = _Slot(*slot_refs[:nslot]), _Slot(*slot_refs[nslot:])
    kw = dict(tm=tm, seg=tm)
    for src, dst in zip(rider_in, rider_out):
        dst[...] = src[...].astype(BF16)

    row = lax.broadcasted_iota(jnp.int32, (tm, POOL_GROUP), 0)

    @pl.when(i == 0)
    def _():
        _init_scratch(slot_a, slot_b)
        c_out[...] = c_in[...]
        n_out[...] = n_in[...]
        m_out[...] = m_in[...]
        _run("", P=_project(x_ref[0:tm, :], w, slot_a, [hist_in[0]], row, **kw))

    state = (c_out, n_out, m_out)

    def c_store(j, h, value):
        c_out[j, h] = value

    def finish_beside(rows, slot, tile, x_proj, slot_proj):
        new_hist = slot.abuf[0, POOL_T0 + tm - HIST_ROWS:POOL_T0 + tm, :]
        hist_out[0] = new_hist
        _run(_CARRY_ORDER,
             F=_finish(x_ref[rows, :], x1_ref, rows, w, slot, hcat_s, *state, c_store, n_out, m_out, **kw),
             P=_project(x_proj, w, slot_proj, [new_hist], row + (tile + 1) * tm, **kw))

    finish_beside(slice(0, tm), slot_a, 2 * i, x_ref[tm:2 * tm, :], slot_b)
    finish_beside(slice(tm, 2 * tm), slot_b, 2 * i + 1, xnext_ref[...], slot_a)


def _streams_kernel(x_ref, gmix_ref, wqv_ref, wrest_ref, wkg_ref, bias_ref, ghead_ref, wgrp_ref,
                    pscale_ref, wba_ref, wbb_ref, wout_ref, c_in, n_in, m_in, hist_in,
                    x1_ref, c_hbm, n_out, m_out, hist_out,
                    hcat_s, cring_s, cring_sem, *slot_refs, tm, seg, pos0):
    i = pl.program_id(0)
    nseg = tm // seg
    w = _Weights(gmix_ref, wqv_ref, wrest_ref, wkg_ref, bias_ref, ghead_ref, wgrp_ref, pscale_ref,
                 wba_ref, wbb_ref, wout_ref)
    slot = _Slot(*slot_refs)

    @pl.when(i == 0)
    def _():
        _init_scratch(slot)

    x = x_ref[...]
    pos = lax.broadcasted_iota(jnp.int32, (seg, POOL_GROUP), 0) + pos0
    ring = _StateRing(c_hbm, cring_s, cring_sem, first_stream=i * nseg, step=i, last_step=pl.num_programs(0) - 1,
                      stores_per_step=nseg * N_HEADS)
    _run(_STREAMS_ORDER,
         P=_project(x, w, slot, [hist_in[j] for j in range(nseg)], pos, tm=tm, seg=seg),
         F=_finish(x, x1_ref, slice(0, tm), w, slot, hcat_s, c_in, n_in, m_in, ring.store, n_out, m_out,
                   tm=tm, seg=seg))
    ring.finish()
    for j in range(nseg):
        hist_out[j] = slot.abuf[j, POOL_T0 + seg - HIST_ROWS:POOL_T0 + seg, :]


def _ffn_tile(x_ref, y_ref, gffn_ref, wup_ref, wdown_ref, gfin_ref, sub):
    n_sub = x_ref.shape[0] // sub
    rows = [slice(r * sub, (r + 1) * sub) for r in range(n_sub)]
    act = {}

    def up(r):
        u = _rmsnorm(x_ref[rows[r], :], gffn_ref[...]).astype(BF16)
        act[r] = jnp.square(jnp.maximum(_dot(u, wup_ref[...]), 0.0)).astype(BF16)

    def down(r):
        x2 = x_ref[rows[r], :] + _dot(act.pop(r), wdown_ref[...])
        y_ref[rows[r], :] = _rmsnorm(x2, gfin_ref[...])

    up(0)
    for r in range(1, n_sub):
        up(r)
        down(r - 1)
    down(n_sub - 1)


def _ffn_kernel(xs_ref, xp_ref, gffn_ref, wup_ref, wdown_ref, gfin_ref, ys_ref, yp_ref, *, sub, n_s):
    i = pl.program_id(0)

    @pl.when(i < n_s)
    def _():
        _ffn_tile(xs_ref, ys_ref, gffn_ref, wup_ref, wdown_ref, gfin_ref, sub)

    @pl.when(i >= n_s)
    def _():
        _ffn_tile(xp_ref, yp_ref, gffn_ref, wup_ref, wdown_ref, gfin_ref, sub)


def _resident(shape):
    zeros = (0,) * len(shape)
    return pl.BlockSpec(shape, lambda i: zeros, pipeline_mode=pl.Buffered(1))


def _slot_scratch(tm, seg):
    return [
        pltpu.VMEM((tm, D_MODEL), BF16),
        pltpu.VMEM((tm, D_MODEL), BF16),
        pltpu.VMEM((D_MODEL, tm), F32),
        pltpu.VMEM((D_MODEL, tm), BF16),
        pltpu.VMEM((tm, D_MODEL), F32),
        pltpu.VMEM((tm, D_MODEL), F32),
        pltpu.VMEM((tm, D_MODEL), F32),
        pltpu.VMEM((GATE_ROWS, tm), F32),
        pltpu.VMEM((tm // seg, POOL_T0 + seg, D_POOL), F32),
        pltpu.VMEM((tm, D_POOL), BF16),
    ]


def _mixer(x, weights, state, *, tm, seg, carry, pos0, riders=()):
    t = x.shape[0]
    nseg = tm // seg
    c0, n0, m0, hist0 = state
    n_streams = c0.shape[0]
    sblk = 1 if carry else nseg
    smap = (lambda i: 0) if carry else (lambda i: i)
    n0 = n0.reshape(n_streams // sblk, sblk, D_MODEL)
    m0 = jnp.repeat(m0, LANES, axis=-1).reshape(n_streams // sblk, sblk, N_HEADS * LANES)
    hist0 = jnp.pad(hist0, ((0, 0), (1, 0), (0, 0)))
    bias = jnp.broadcast_to(weights["gate_bias"][:, None], (GATE_ROWS, tm))

    if carry:
        assert pos0 == 0 and nseg == 1 and t % (2 * tm) == 0
        steps = t // (2 * tm)
        last_tile = t // tm - 1
        x_specs = [pl.BlockSpec((2 * tm, D_MODEL), lambda i: (i, 0)),
                   pl.BlockSpec((tm, D_MODEL), lambda i: (jnp.minimum(2 * i + 2, last_tile), 0))]
        x_args = [x, x]
        x1_spec = pl.BlockSpec((2 * tm, D_MODEL), lambda i: (i, 0))
        body = functools.partial(_carry_kernel, tm=tm, n_riders=len(riders))
        slots = _slot_scratch(tm, seg) + _slot_scratch(tm, seg)
        assert steps % RIDER_HOLD_STEPS == 0 and all(r.shape[0] % (16 * steps) == 0 for r in riders)
    else:
        steps = t // tm
        x_specs = [pl.BlockSpec((tm, D_MODEL), lambda i: (i, 0))]
        x_args = [x]
        x1_spec = pl.BlockSpec((tm, D_MODEL), lambda i: (i, 0))
        body = functools.partial(_streams_kernel, tm=tm, seg=seg, pos0=pos0)
        slots = [pltpu.VMEM((STATE_RING_SLOTS, HEAD_DIM, HEAD_DIM), F32),
                 pltpu.SemaphoreType.DMA((STATE_RING_SLOTS,))] + _slot_scratch(tm, seg)

    state_specs = [
        pl.BlockSpec((sblk, N_HEADS, HEAD_DIM, HEAD_DIM), lambda i: (smap(i), 0, 0, 0)),
        pl.BlockSpec((None, sblk, D_MODEL), lambda i: (smap(i), 0, 0)),
        pl.BlockSpec((None, sblk, N_HEADS * LANES), lambda i: (smap(i), 0, 0)),
        pl.BlockSpec((sblk, HIST_ROWS, D_POOL), lambda i: (smap(i), 0, 0)),
    ]
    in_specs = x_specs + [
        _resident((1, D_MODEL)),
        _resident((D_MODEL, 2 * D_MODEL)),
        _resident((D_MODEL, _REST_ROWS)),
        _resident((D_MODEL + GATE_ROWS, D_MODEL)),
        _resident((GATE_ROWS, tm)),
        _resident((1, D_MODEL)),
        _resident((len(POOL_WINDOWS), POOL_GROUP, POOL_GROUP)),
        _resident((1, D_POOL)),
        _resident((D_MODEL, D_MODEL)),
        _resident((D_POOL, D_MODEL)),
        _resident((D_MODEL, D_MODEL)),
    ] + state_specs
    rider_specs = [pl.BlockSpec((r.shape[0] * RIDER_HOLD_STEPS // steps, r.shape[1]),
                                lambda i: (i // RIDER_HOLD_STEPS, 0)) for r in riders]
    out_shape = [
        jax.ShapeDtypeStruct((t, D_MODEL), F32),
        jax.ShapeDtypeStruct(c0.shape, F32),
        jax.ShapeDtypeStruct(n0.shape, F32),
        jax.ShapeDtypeStruct(m0.shape, F32),
        jax.ShapeDtypeStruct(hist0.shape, F32),
    ] + [jax.ShapeDtypeStruct(r.shape, BF16) for r in riders]
    scratch = [pltpu.VMEM((tm, D_MODEL), BF16)] + slots
    x1, c1, n1, m1, hist1, *cast = pl.pallas_call(
        body,
        grid=(steps,),
        in_specs=in_specs + rider_specs,
        out_specs=([x1_spec, state_specs[0] if carry else pl.BlockSpec(memory_space=pl.ANY)] + state_specs[1:]
                   + rider_specs),
        out_shape=out_shape,
        scratch_shapes=scratch,
        compiler_params=pltpu.CompilerParams(
            dimension_semantics=("arbitrary",), vmem_limit_bytes=V7X_VMEM_LIMIT_BYTES),
        name="mixer_carry" if carry else "mixer_streams",
    )(*x_args, weights["g_mix"], weights["w_qv"], weights["w_rest"], weights["w_kg"], bias, weights["g_head"],
      weights["w_grp"], weights["pool_scale"], weights["w_ba"], weights["w_bb"], weights["w_out"], c0, n0, m0, hist0,
      *riders)
    n1 = n1.reshape(n_streams, N_HEADS, HEAD_DIM)
    m1 = m1.reshape(n_streams, N_HEADS, LANES)[:, :, 0]
    return (x1, c1, n1, m1, hist1[:, 1:, :], *cast)


def _ffn(x_sample, x_prompt, w_up, w_down, weights, *, tm_s, tm_p):
    ts, tp = x_sample.shape[0], x_prompt.shape[0]
    n_s, n_p = ts // tm_s, tp // tm_p
    clamp = lambda v, hi: jnp.clip(v, 0, hi)
    return pl.pallas_call(
        functools.partial(_ffn_kernel, sub=256, n_s=n_s),
        grid=(n_s + n_p,),
        in_specs=[
            pl.BlockSpec((tm_s, D_MODEL), lambda i: (clamp(i, n_s - 1), 0)),
            pl.BlockSpec((tm_p, D_MODEL), lambda i: (clamp(i - n_s, n_p - 1), 0)),
            _resident((1, D_MODEL)),
            _resident((D_MODEL, D_FF)),
            _resident((D_FF, D_MODEL)),
            _resident((1, D_MODEL)),
        ],
        out_specs=[
            pl.BlockSpec((tm_s, D_MODEL), lambda i: (clamp(i, n_s - 1), 0)),
            pl.BlockSpec((tm_p, D_MODEL), lambda i: (clamp(i - n_s, n_p - 1), 0)),
        ],
        out_shape=[jax.ShapeDtypeStruct((ts, D_MODEL), F32), jax.ShapeDtypeStruct((tp, D_MODEL), F32)],
        compiler_params=pltpu.CompilerParams(
            dimension_semantics=("arbitrary",), vmem_limit_bytes=V7X_VMEM_LIMIT_BYTES),
        name="ffn",
    )(x_sample, x_prompt, weights["g_ffn"], w_up, w_down, weights["g_final"])


def _split_w_in_kernel(w_ref, wqv_ref, wkg_ref, wrest_ref, prev_s):
    i = pl.program_id(0)
    n_qkv = _QKV_ROWS // _SPLIT_ROWS
    cur = w_ref[...]

    @pl.when((i == 0) | (i == 2))
    def _():
        wqv_ref[...] = cur.T.astype(BF16)

    @pl.when(i == 1)
    def _():
        wkg_ref[0:D_MODEL, :] = cur.astype(BF16)

    @pl.when(i == n_qkv)
    def _():
        gate = cur[0:2 * N_HEADS, :]
        wkg_ref[D_MODEL:D_MODEL + GATE_ROWS, :] = jnp.concatenate(
            [gate, pltpu.roll(gate, N_HEADS, axis=0)], axis=0).astype(BF16)

    @pl.when(i > n_qkv)
    def _():
        rows = jnp.concatenate([prev_s[2 * N_HEADS:, :], cur[0:2 * N_HEADS, :]], axis=0)
        wrest_ref[...] = rows.T.astype(BF16)

    prev_s[...] = cur


def _split_w_in(w_in_t):
    n_in = w_in_t.shape[0]
    assert n_in == _REST0 + _REST_ROWS and _SPLIT_ROWS == D_MODEL and _QKV_ROWS == 3 * _SPLIT_ROWS
    n_qkv, n_rest = _QKV_ROWS // _SPLIT_ROWS, pl.cdiv(_REST_ROWS, _SPLIT_ROWS)
    last_chunk = pl.cdiv(n_in, _SPLIT_ROWS) - 1
    return pl.pallas_call(
        _split_w_in_kernel,
        grid=(n_qkv + 1 + n_rest,),
        in_specs=[pl.BlockSpec((_SPLIT_ROWS, D_MODEL), lambda i: (jnp.minimum(i, last_chunk), 0))],
        out_specs=[
            pl.BlockSpec((D_MODEL, _SPLIT_ROWS), lambda i: (0, jnp.where(i < 2, 0, 1))),
            pl.BlockSpec((D_MODEL + GATE_ROWS, D_MODEL), lambda i: (0, 0)),
            pl.BlockSpec((D_MODEL, _SPLIT_ROWS), lambda i: (0, jnp.clip(i - n_qkv - 1, 0, n_rest - 1))),
        ],
        out_shape=[
            jax.ShapeDtypeStruct((D_MODEL, 2 * D_MODEL), BF16),
            jax.ShapeDtypeStruct((D_MODEL + GATE_ROWS, D_MODEL), BF16),
            jax.ShapeDtypeStruct((D_MODEL, _REST_ROWS), BF16),
        ],
        scratch_shapes=[pltpu.VMEM((_SPLIT_ROWS, D_MODEL), F32)],
        compiler_params=pltpu.CompilerParams(
            dimension_semantics=("arbitrary",), vmem_limit_bytes=V7X_VMEM_LIMIT_BYTES),
        name="split_w_in",
    )(w_in_t)


def _pack_weights(w_in, b_igate, b_fgate, g_norm_mix, g_head, w_pool_grp, pool_scale, w_branch_mlstm,
                  w_branch_pool, w_out, g_norm_ffn, w_up, w_down, g_final):
    d = D_MODEL
    w_qv, w_kg, w_rest = _split_w_in(w_in.T)
    return {
        "w_qv": w_qv,
        "w_kg": w_kg,
        "w_rest": w_rest,
        "gate_bias": jnp.concatenate([b_igate, b_fgate, b_fgate, b_igate]).astype(F32),
        "g_mix": g_norm_mix.reshape(1, d),
        "g_head": g_head.reshape(1, d),
        "w_grp": w_pool_grp.astype(BF16),
        "pool_scale": pool_scale.reshape(1, D_POOL),
        "w_ba": w_branch_mlstm.astype(BF16),
        "w_bb": w_branch_pool.astype(BF16),
        "w_out": w_out.astype(BF16),
        "g_ffn": g_norm_ffn.reshape(1, d),
        "g_final": g_final.reshape(1, d),
    }


def kernel(x_prompt, x_sample, state_C, state_n, state_m, state_pool, w_in, b_igate, b_fgate, g_norm_mix, g_head,
           w_pool_grp, pool_scale, w_branch_mlstm, w_branch_pool, w_out, g_norm_ffn, w_up, w_down, g_final):
    depth = w_in.shape[0]
    bp, sp, d = x_prompt.shape
    bs, ss, _ = x_sample.shape
    assert depth == 1 and bp == 1 and d == D_MODEL
    hp = x_prompt.reshape(bp * sp, d)
    hs = x_sample.reshape(bs * ss, d)
    weights = _pack_weights(w_in[0], b_igate[0], b_fgate[0], g_norm_mix[0], g_head[0], w_pool_grp[0], pool_scale[0],
                            w_branch_mlstm[0], w_branch_pool[0], w_out[0], g_norm_ffn[0], w_up[0], w_down[0],
                            g_final)
    zero_state = (jnp.zeros((bp, N_HEADS, HEAD_DIM, HEAD_DIM), F32), jnp.zeros((bp, D_MODEL), F32),
                  jnp.zeros((bp, N_HEADS), F32), jnp.zeros((bp, HIST_ROWS - 1, D_POOL), F32))
    hp, cp, np_, mp, pp, w_up_b, w_down_b = _mixer(hp, weights, zero_state, tm=256, seg=256, carry=True, pos0=0,
                                                   riders=(w_up[0], w_down[0]))
    sample_state = (state_C[0], state_n[0].reshape(bs, D_MODEL), state_m[0], state_pool[0])
    hs, cs, ns, ms, ps = _mixer(hs, weights, sample_state, tm=8 * ss, seg=ss, carry=False, pos0=PAST_LEN)
    y_sample, y_prompt = _ffn(hs, hp, w_up_b, w_down_b, weights, tm_s=512, tm_p=1024)
    y_prompt = y_prompt.reshape(bp, sp, d)
    y_sample = y_sample.reshape(bs, ss, d)
    return (y_prompt, y_sample, cp[None], np_[None], mp[None], pp[None], cs[None], ns[None], ms[None], ps[None])
```

```python
import collections
import functools

import jax
import jax.numpy as jnp
from jax import lax
from jax.experimental import pallas as pl
from jax.experimental.pallas import tpu as pltpu

F32 = jnp.float32
BF16 = jnp.bfloat16

D_MODEL = 1024
N_HEADS = 4
HEAD_DIM = 256
D_POOL = 512
POOL_GROUP = 128
POOL_WINDOWS = (2, 4, 8, 16)
HIST_ROWS = 16
POOL_T0 = 32
D_FF = 4096
EPS = 1e-6
PAST_LEN = 2048
LANES = 128
GATE_ROWS = 16
RIDER_HOLD_STEPS = 2
STATE_RING_SLOTS = 16
V7X_VMEM_LIMIT_BYTES = 58 * 1024 * 1024

_K0, _V0, _QKV_ROWS = 1024, 2048, 3072
_REST0 = _QKV_ROWS + 2 * N_HEADS
_P0, _GA0, _GB0, _REST_ROWS = 1024, 1536, 2560, 3584
_SPLIT_ROWS = 1024

_Slot = collections.namedtuple("_Slot", "q v kt ktb og ga gb gate abuf pd")
_Weights = collections.namedtuple(
    "_Weights", "gmix wqv wrest wkg bias ghead wgrp pscale wba wbb wout")
_CARRY_ORDER = "FPFFFFPPFPPFPFP"
_STREAMS_ORDER = "PPPPFPFPFPFFFFF"


def _dot(a, b):
    return jnp.dot(a, b, preferred_element_type=F32)


def _dot_nt(a, b):
    return lax.dot_general(a, b, (((1,), (1,)), ((), ())), preferred_element_type=F32)


def _rmsnorm(x, g):
    return x * lax.rsqrt(jnp.mean(x * x, axis=-1, keepdims=True) + EPS) * g


def _sigmoid(x):
    return 1.0 / (1.0 + jnp.exp(-x))


def _log_sigmoid(x):
    return jnp.minimum(x, 0.0) - jnp.log1p(jnp.exp(-jnp.abs(x)))


def _masks(tm, seg):
    row = lax.broadcasted_iota(jnp.int32, (tm, tm), 0)
    col = lax.broadcasted_iota(jnp.int32, (tm, tm), 1)
    if tm == seg:
        return col <= row, row <= col
    shift = seg.bit_length() - 1
    same = (row >> shift) == (col >> shift)
    return (col <= row) & same, (row <= col) & same


def _init_scratch(*slots):
    for slot in slots:
        slot.abuf[:, 0:POOL_T0 - HIST_ROWS, :] = jnp.zeros((slot.abuf.shape[0], POOL_T0 - HIST_ROWS, D_POOL), F32)


def _project(x, w, slot, hist, pos, *, tm, seg):
    nseg = tm // seg
    u = _rmsnorm(x, w.gmix[...]).astype(BF16)
    kg = _dot_nt(w.wkg[...], u)
    kt = kg[0:D_MODEL] * (HEAD_DIM ** -0.5)
    slot.kt[...] = kt
    slot.ktb[...] = kt.astype(BF16)
    gates = kg[D_MODEL:D_MODEL + GATE_ROWS] + w.bias[...]
    yield
    p = _dot(u, w.wrest[:, _P0:_GA0])
    for j in range(nseg):
        slot.abuf[j, POOL_T0 - HIST_ROWS:POOL_T0, :] = hist[j]
        slot.abuf[j, POOL_T0:POOL_T0 + seg, :] = p[j * seg:(j + 1) * seg]
    for gi, win in enumerate(POOL_WINDOWS):
        ls = slice(gi * POOL_GROUP, (gi + 1) * POOL_GROUP)
        cnt = jnp.minimum(pos + 1, win).astype(F32)
        for j in range(nseg):
            start = POOL_T0 - 8 * gi
            tot = slot.abuf[j, start:POOL_T0 + seg, ls] + slot.abuf[j, start - 1:POOL_T0 + seg - 1, ls]
            shift = 2
            while shift < win:
                n = tot.shape[0]
                tot = tot[8:n] + tot[8 - shift:n - shift]
                shift *= 2
            tok = slot.abuf[j, POOL_T0:POOL_T0 + seg, ls]
            slot.pd[j * seg:(j + 1) * seg, ls] = (tot / cnt - tok).astype(BF16)
    yield
    slot.q[...] = _dot(u, w.wqv[:, 0:D_MODEL]).astype(BF16)
    _, upper = _masks(tm, seg)
    lf = _log_sigmoid(gates[8:16])
    hi = lf.astype(BF16).astype(F32)
    r1 = lf - hi
    mid = r1.astype(BF16).astype(F32)
    lo = r1 - mid
    pieces = jnp.concatenate([hi, mid, lo, jnp.zeros_like(lo)], axis=0).astype(BF16)
    cs = _dot(pieces, jnp.where(upper, 1.0, 0.0).astype(BF16))
    b = cs[0:8] + cs[8:16] + cs[16:24]
    slot.gate[0:8, :] = gates[0:8] - b
    slot.gate[8:16, :] = lf
    yield
    slot.v[...] = _dot(u, w.wqv[:, D_MODEL:2 * D_MODEL]).astype(BF16)
    yield
    slot.og[...] = _dot(u, w.wrest[:, 0:_P0])
    yield
    slot.ga[...] = _dot(u, w.wrest[:, _GA0:_GB0])
    yield
    slot.gb[...] = _dot(u, w.wrest[:, _GB0:_REST_ROWS])
    yield


def _finish(x, out_ref, rows, w, slot, hcat_s, c_rd, n_rd, m_rd, c_store, n_out, m_out, *, tm, seg):
    nseg = tm // seg
    causal, _ = _masks(tm, seg)

    heads = []
    for h in range(N_HEADS):
        sl = slice(h * HEAD_DIM, (h + 1) * HEAD_DIM)
        a_m = jnp.where(causal, jnp.broadcast_to(slot.gate[h:h + 1, :], (tm, tm)), -jnp.inf)
        mprev = jnp.concatenate(
            [jnp.broadcast_to(m_rd[j:j + 1, h * LANES:h * LANES + 1], (seg, 1)) for j in range(nseg)], axis=0)
        g = jnp.maximum(jnp.max(a_m, axis=-1, keepdims=True), mprev)
        dm = jnp.exp(a_m - g)
        lf_b = jnp.broadcast_to(slot.gate[8 + h:9 + h, :], (tm, tm))
        bcol = jnp.sum(jnp.where(causal, lf_b, 0.0), axis=-1, keepdims=True)
        mcol = bcol + g
        wi = jnp.exp(mprev - g)

        qh = slot.q[:, sl]
        vh = slot.v[:, sl]
        s = _dot(qh, slot.ktb[sl, :]) * dm
        qf = qh.astype(F32)
        qc_parts, qn_parts = [], []
        for j in range(nseg):
            rs = slice(j * seg, (j + 1) * seg)
            qc_parts.append(_dot(qh[rs], c_rd[j, h].astype(BF16)))
            qn_parts.append(jnp.sum(qf[rs] * n_rd[j:j + 1, sl], axis=-1, keepdims=True))
        qc = jnp.concatenate(qc_parts, axis=0)
        qn = jnp.concatenate(qn_parts, axis=0)
        heads.append((sl, dm, mcol, wi, vh, s, qc, qn))
    yield

    for h, (sl, dm, mcol, wi, vh, s, qc, qn) in enumerate(heads):
        kth = slot.kt[sl, :]
        rowsum = jnp.sum(s, axis=-1, keepdims=True)
        sv = _dot(s.astype(BF16), vh)
        num = wi * qc + sv
        den = wi * qn + rowsum
        hh = num / jnp.maximum(jnp.abs(den), jnp.exp(-mcol))
        hh = hh * lax.rsqrt(jnp.mean(hh * hh, axis=-1, keepdims=True) + EPS) * w.ghead[:, sl]
        hh = hh * _sigmoid(slot.og[:, sl])
        hcat_s[:, sl] = hh.astype(BF16)

        wrows, n_decayed = [], []
        for j in range(nseg):
            r = (j + 1) * seg - 1
            w_row = dm[r:r + 1, :]
            wrows.append(w_row)
            decay = wi[r:r + 1, :]
            kw = (kth * w_row).astype(BF16)
            c_store(j, h, decay * c_rd[j, h] + _dot(kw, vh))
            n_decayed.append(decay * n_rd[j:j + 1, sl])
            m_out[j:j + 1, h * LANES:(h + 1) * LANES] = jnp.broadcast_to(mcol[r:r + 1, :], (1, LANES))
        ridx = lax.broadcasted_iota(jnp.int32, (GATE_ROWS, tm), 0)
        wmat = jnp.zeros((GATE_ROWS, tm), F32)
        for j in range(nseg):
            wmat = jnp.where(ridx == j, jnp.broadcast_to(wrows[j], (GATE_ROWS, tm)), wmat)
        nupd = _dot_nt(wmat.astype(BF16), slot.ktb[sl, :])
        for j in range(nseg):
            n_out[j:j + 1, sl] = n_decayed[j] + nupd[j:j + 1, :]
        yield

    grp_out = [_dot(slot.pd[:, gi * POOL_GROUP:(gi + 1) * POOL_GROUP], w.wgrp[gi])
               for gi in range(len(POOL_WINDOWS))]
    pooled = jnp.concatenate(grp_out, axis=-1) * w.pscale[...]
    yield

    branch_a = _dot(hcat_s[...], w.wba[...])
    branch_b = _dot(pooled.astype(BF16), w.wbb[...])
    yield
    mixed = _sigmoid(slot.ga[...]) * branch_a + _sigmoid(slot.gb[...]) * branch_b
    out_ref[rows, :] = x + _dot(mixed.astype(BF16), w.wout[...])
    yield


def _run(order, **gens):
    for name in order:
        next(gens[name], None)
    for gen in gens.values():
        for _ in gen:
            pass


class _StateRing:
    def __init__(self, dst_hbm, ring, sems, first_stream, step, last_step, stores_per_step):
        self.dst, self.ring, self.sems, self.first = dst_hbm, ring, sems, first_stream
        self.step, self.last_step = step, last_step
        self.bank = ring.shape[0] // 2
        assert stores_per_step % ring.shape[0] == 0
        self.count = 0

    def _copy(self, slot, stream, h):
        return pltpu.make_async_copy(self.ring.at[slot], self.dst.at[stream, h], self.sems.at[slot])

    def _wait_bank(self, bank):
        for slot in range(bank * self.bank, (bank + 1) * self.bank):
            self._copy(slot, 0, 0).wait()

    def store(self, j, h, value):
        slot = self.count % self.ring.shape[0]
        if slot % self.bank == 0:
            bank = slot // self.bank
            if self.count < self.ring.shape[0]:
                pl.when(self.step > 0)(functools.partial(self._wait_bank, bank))
            else:
                self._wait_bank(bank)
        self.count += 1
        self.ring[slot] = value
        self._copy(slot, self.first + j, h).start()

    def finish(self):
        @pl.when(self.step == self.last_step)
        def _():
            self._wait_bank(0)
            self._wait_bank(1)


def _carry_kernel(x_ref, xnext_ref, gmix_ref, wqv_ref, wrest_ref, wkg_ref, bias_ref, ghead_ref, wgrp_ref,
                  pscale_ref, wba_ref, wbb_ref, wout_ref, c_in, n_in, m_in, hist_in, *rest, tm, n_riders):
    i = pl.program_id(0)
    rider_in, rest = rest[:n_riders], rest[n_riders:]
    x1_ref, c_out, n_out, m_out, hist_out = rest[:5]
    rider_out, (hcat_s, *slot_refs) = rest[5:5 + n_riders], rest[5 + n_riders:]
    w = _Weights(gmix_ref, wqv_ref, wrest_ref, wkg_ref, bias_ref, ghead_ref, wgrp_ref, pscale_ref,
                 wba_ref, wbb_ref, wout_ref)
    nslot = len(_Slot._fields)
    slot_a, slot_b = _Slot(*slot_refs[:nslot]), _Slot(*slot_refs[nslot:])
    kw = dict(tm=tm, seg=tm)
    for src, dst in zip(rider_in, rider_out):
        dst[...] = src[...].astype(BF16)

    row = lax.broadcasted_iota(jnp.int32, (tm, POOL_GROUP), 0)

    @pl.when(i == 0)
    def _():
        _init_scratch(slot_a, slot_b)
        c_out[...] = c_in[...]
        n_out[...] = n_in[...]
        m_out[...] = m_in[...]
        _run("", P=_project(x_ref[0:tm, :], w, slot_a, [hist_in[0]], row, **kw))

    state = (c_out, n_out, m_out)

    def c_store(j, h, value):
        c_out[j, h] = value

    def finish_beside(rows, slot, tile, x_proj, slot_proj):
        new_hist = slot.abuf[0, POOL_T0 + tm - HIST_ROWS:POOL_T0 + tm, :]
        hist_out[0] = new_hist
        _run(_CARRY_ORDER,
             F=_finish(x_ref[rows, :], x1_ref, rows, w, slot, hcat_s, *state, c_store, n_out, m_out, **kw),
             P=_project(x_proj, w, slot_proj, [new_hist], row + (tile + 1) * tm, **kw))

    finish_beside(slice(0, tm), slot_a, 2 * i, x_ref[tm:2 * tm, :], slot_b)
    finish_beside(slice(tm, 2 * tm), slot_b, 2 * i + 1, xnext_ref[...], slot_a)


def _streams_kernel(x_ref, gmix_ref, wqv_ref, wrest_ref, wkg_ref, bias_ref, ghead_ref, wgrp_ref,
                    pscale_ref, wba_ref, wbb_ref, wout_ref, c_in, n_in, m_in, hist_in,
                    x1_ref, c_hbm, n_out, m_out, hist_out,
                    hcat_s, cring_s, cring_sem, *slot_refs, tm, seg, pos0):
    i = pl.program_id(0)
    nseg = tm // seg
    w = _Weights(gmix_ref, wqv_ref, wrest_ref, wkg_ref, bias_ref, ghead_ref, wgrp_ref, pscale_ref,
                 wba_ref, wbb_ref, wout_ref)
    slot = _Slot(*slot_refs)

    @pl.when(i == 0)
    def _():
        _init_scratch(slot)

    x = x_ref[...]
    pos = lax.broadcasted_iota(jnp.int32, (seg, POOL_GROUP), 0) + pos0
    ring = _StateRing(c_hbm, cring_s, cring_sem, first_stream=i * nseg, step=i, last_step=pl.num_programs(0) - 1,
                      stores_per_step=nseg * N_HEADS)
    _run(_STREAMS_ORDER,
         P=_project(x, w, slot, [hist_in[j] for j in range(nseg)], pos, tm=tm, seg=seg),
         F=_finish(x, x1_ref, slice(0, tm), w, slot, hcat_s, c_in, n_in, m_in, ring.store, n_out, m_out,
                   tm=tm, seg=seg))
    ring.finish()
    for j in range(nseg):
        hist_out[j] = slot.abuf[j, POOL_T0 + seg - HIST_ROWS:POOL_T0 + seg, :]


def _ffn_tile(x_ref, y_ref, gffn_ref, wup_ref, wdown_ref, gfin_ref, sub):
    n_sub = x_ref.shape[0] // sub
    rows = [slice(r * sub, (r + 1) * sub) for r in range(n_sub)]
    act = {}

    def up(r):
        u = _rmsnorm(x_ref[rows[r], :], gffn_ref[...]).astype(BF16)
        act[r] = jnp.square(jnp.maximum(_dot(u, wup_ref[...]), 0.0)).astype(BF16)

    def down(r):
        x2 = x_ref[rows[r], :] + _dot(act.pop(r), wdown_ref[...])
        y_ref[rows[r], :] = _rmsnorm(x2, gfin_ref[...])

    up(0)
    for r in range(1, n_sub):
        up(r)
        down(r - 1)
    down(n_sub - 1)


def _ffn_kernel(xs_ref, xp_ref, gffn_ref, wup_ref, wdown_ref, gfin_ref, ys_ref, yp_ref, *, sub, n_s):
    i = pl.program_id(0)

    @pl.when(i < n_s)
    def _():
        _ffn_tile(xs_ref, ys_ref, gffn_ref, wup_ref, wdown_ref, gfin_ref, sub)

    @pl.when(i >= n_s)
    def _():
        _ffn_tile(xp_ref, yp_ref, gffn_ref, wup_ref, wdown_ref, gfin_ref, sub)


def _resident(shape):
    zeros = (0,) * len(shape)
    return pl.BlockSpec(shape, lambda i: zeros, pipeline_mode=pl.Buffered(1))


def _slot_scratch(tm, seg):
    return [
        pltpu.VMEM((tm, D_MODEL), BF16),
        pltpu.VMEM((tm, D_MODEL), BF16),
        pltpu.VMEM((D_MODEL, tm), F32),
        pltpu.VMEM((D_MODEL, tm), BF16),
        pltpu.VMEM((tm, D_MODEL), F32),
        pltpu.VMEM((tm, D_MODEL), F32),
        pltpu.VMEM((tm, D_MODEL), F32),
        pltpu.VMEM((GATE_ROWS, tm), F32),
        pltpu.VMEM((tm // seg, POOL_T0 + seg, D_POOL), F32),
        pltpu.VMEM((tm, D_POOL), BF16),
    ]


def _mixer(x, weights, state, *, tm, seg, carry, pos0, riders=()):
    t = x.shape[0]
    nseg = tm // seg
    c0, n0, m0, hist0 = state
    n_streams = c0.shape[0]
    sblk = 1 if carry else nseg
    smap = (lambda i: 0) if carry else (lambda i: i)
    n0 = n0.reshape(n_streams // sblk, sblk, D_MODEL)
    m0 = jnp.repeat(m0, LANES, axis=-1).reshape(n_streams // sblk, sblk, N_HEADS * LANES)
    hist0 = jnp.pad(hist0, ((0, 0), (1, 0), (0, 0)))
    bias = jnp.broadcast_to(weights["gate_bias"][:, None], (GATE_ROWS, tm))

    if carry:
        assert pos0 == 0 and nseg == 1 and t % (2 * tm) == 0
        steps = t // (2 * tm)
        last_tile = t // tm - 1
        x_specs = [pl.BlockSpec((2 * tm, D_MODEL), lambda i: (i, 0)),
                   pl.BlockSpec((tm, D_MODEL), lambda i: (jnp.minimum(2 * i + 2, last_tile), 0))]
        x_args = [x, x]
        x1_spec = pl.BlockSpec((2 * tm, D_MODEL), lambda i: (i, 0))
        body = functools.partial(_carry_kernel, tm=tm, n_riders=len(riders))
        slots = _slot_scratch(tm, seg) + _slot_scratch(tm, seg)
        assert steps % RIDER_HOLD_STEPS == 0 and all(r.shape[0] % (16 * steps) == 0 for r in riders)
    else:
        steps = t // tm
        x_specs = [pl.BlockSpec((tm, D_MODEL), lambda i: (i, 0))]
        x_args = [x]
        x1_spec = pl.BlockSpec((tm, D_MODEL), lambda i: (i, 0))
        body = functools.partial(_streams_kernel, tm=tm, seg=seg, pos0=pos0)
        slots = [pltpu.VMEM((STATE_RING_SLOTS, HEAD_DIM, HEAD_DIM), F32),
                 pltpu.SemaphoreType.DMA((STATE_RING_SLOTS,))] + _slot_scratch(tm, seg)

    state_specs = [
        pl.BlockSpec((sblk, N_HEADS, HEAD_DIM, HEAD_DIM), lambda i: (smap(i), 0, 0, 0)),
        pl.BlockSpec((None, sblk, D_MODEL), lambda i: (smap(i), 0, 0)),
        pl.BlockSpec((None, sblk, N_HEADS * LANES), lambda i: (smap(i), 0, 0)),
        pl.BlockSpec((sblk, HIST_ROWS, D_POOL), lambda i: (smap(i), 0, 0)),
    ]
    in_specs = x_specs + [
        _resident((1, D_MODEL)),
        _resident((D_MODEL, 2 * D_MODEL)),
        _resident((D_MODEL, _REST_ROWS)),
        _resident((D_MODEL + GATE_ROWS, D_MODEL)),
        _resident((GATE_ROWS, tm)),
        _resident((1, D_MODEL)),
        _resident((len(POOL_WINDOWS), POOL_GROUP, POOL_GROUP)),
        _resident((1, D_POOL)),
        _resident((D_MODEL, D_MODEL)),
        _resident((D_POOL, D_MODEL)),
        _resident((D_MODEL, D_MODEL)),
    ] + state_specs
    rider_specs = [pl.BlockSpec((r.shape[0] * RIDER_HOLD_STEPS // steps, r.shape[1]),
                                lambda i: (i // RIDER_HOLD_STEPS, 0)) for r in riders]
    out_shape = [
        jax.ShapeDtypeStruct((t, D_MODEL), F32),
        jax.ShapeDtypeStruct(c0.shape, F32),
        jax.ShapeDtypeStruct(n0.shape, F32),
        jax.ShapeDtypeStruct(m0.shape, F32),
        jax.ShapeDtypeStruct(hist0.shape, F32),
    ] + [jax.ShapeDtypeStruct(r.shape, BF16) for r in riders]
    scratch = [pltpu.VMEM((tm, D_MODEL), BF16)] + slots
    x1, c1, n1, m1, hist1, *cast = pl.pallas_call(
        body,
        grid=(steps,),
        in_specs=in_specs + rider_specs,
        out_specs=([x1_spec, state_specs[0] if carry else pl.BlockSpec(memory_space=pl.ANY)] + state_specs[1:]
                   + rider_specs),
        out_shape=out_shape,
        scratch_shapes=scratch,
        compiler_params=pltpu.CompilerParams(
            dimension_semantics=("arbitrary",), vmem_limit_bytes=V7X_VMEM_LIMIT_BYTES),
        name="mixer_carry" if carry else "mixer_streams",
    )(*x_args, weights["g_mix"], weights["w_qv"], weights["w_rest"], weights["w_kg"], bias, weights["g_head"],
      weights["w_grp"], weights["pool_scale"], weights["w_ba"], weights["w_bb"], weights["w_out"], c0, n0, m0, hist0,
      *riders)
    n1 = n1.reshape(n_streams, N_HEADS, HEAD_DIM)
    m1 = m1.reshape(n_streams, N_HEADS, LANES)[:, :, 0]
    return (x1, c1, n1, m1, hist1[:, 1:, :], *cast)


def _ffn(x_sample, x_prompt, w_up, w_down, weights, *, tm_s, tm_p):
    ts, tp = x_sample.shape[0], x_prompt.shape[0]
    n_s, n_p = ts // tm_s, tp // tm_p
    clamp = lambda v, hi: jnp.clip(v, 0, hi)
    return pl.pallas_call(
        functools.partial(_ffn_kernel, sub=256, n_s=n_s),
        grid=(n_s + n_p,),
        in_specs=[
            pl.BlockSpec((tm_s, D_MODEL), lambda i: (clamp(i, n_s - 1), 0)),
            pl.BlockSpec((tm_p, D_MODEL), lambda i: (clamp(i - n_s, n_p - 1), 0)),
            _resident((1, D_MODEL)),
            _resident((D_MODEL, D_FF)),
            _resident((D_FF, D_MODEL)),
            _resident((1, D_MODEL)),
        ],
        out_specs=[
            pl.BlockSpec((tm_s, D_MODEL), lambda i: (clamp(i, n_s - 1), 0)),
            pl.BlockSpec((tm_p, D_MODEL), lambda i: (clamp(i - n_s, n_p - 1), 0)),
        ],
        out_shape=[jax.ShapeDtypeStruct((ts, D_MODEL), F32), jax.ShapeDtypeStruct((tp, D_MODEL), F32)],
        compiler_params=pltpu.CompilerParams(
            dimension_semantics=("arbitrary",), vmem_limit_bytes=V7X_VMEM_LIMIT_BYTES),
        name="ffn",
    )(x_sample, x_prompt, weights["g_ffn"], w_up, w_down, weights["g_final"])


def _split_w_in_kernel(w_ref, *refs, n_riders):
    i = pl.program_id(0)
    rider_in, (wqv_ref, wkg_ref, wrest_ref) = refs[:n_riders], refs[n_riders:n_riders + 3]
    rider_out, prev_s = refs[n_riders + 3:2 * n_riders + 3], refs[2 * n_riders + 3]
    n_qkv = _QKV_ROWS // _SPLIT_ROWS
    cur = w_ref[...]
    for src, dst in zip(rider_in, rider_out):
        dst[...] = src[...].astype(BF16)

    @pl.when((i == 0) | (i == 2))
    def _():
        wqv_ref[...] = cur.T.astype(BF16)

    @pl.when(i == 1)
    def _():
        wkg_ref[0:D_MODEL, :] = cur.astype(BF16)

    @pl.when(i == n_qkv)
    def _():
        gate = cur[0:2 * N_HEADS, :]
        wkg_ref[D_MODEL:D_MODEL + GATE_ROWS, :] = jnp.concatenate(
            [gate, pltpu.roll(gate, N_HEADS, axis=0)], axis=0).astype(BF16)

    @pl.when(i > n_qkv)
    def _():
        rows = jnp.concatenate([prev_s[2 * N_HEADS:, :], cur[0:2 * N_HEADS, :]], axis=0)
        wrest_ref[...] = rows.T.astype(BF16)

    prev_s[...] = cur


def _split_w_in(w_in_t, riders=()):
    n_in = w_in_t.shape[0]
    assert n_in == _REST0 + _REST_ROWS and _SPLIT_ROWS == D_MODEL and _QKV_ROWS == 3 * _SPLIT_ROWS
    n_qkv, n_rest = _QKV_ROWS // _SPLIT_ROWS, pl.cdiv(_REST_ROWS, _SPLIT_ROWS)
    steps = n_qkv + 1 + n_rest
    last_chunk = pl.cdiv(n_in, _SPLIT_ROWS) - 1
    assert all(r.shape[0] % (16 * steps) == 0 for r in riders)
    rider_specs = [pl.BlockSpec((r.shape[0] // steps, r.shape[1]), lambda i: (i, 0)) for r in riders]
    return pl.pallas_call(
        functools.partial(_split_w_in_kernel, n_riders=len(riders)),
        grid=(steps,),
        in_specs=[pl.BlockSpec((_SPLIT_ROWS, D_MODEL), lambda i: (jnp.minimum(i, last_chunk), 0))] + rider_specs,
        out_specs=[
            pl.BlockSpec((D_MODEL, _SPLIT_ROWS), lambda i: (0, jnp.where(i < 2, 0, 1))),
            pl.BlockSpec((D_MODEL + GATE_ROWS, D_MODEL), lambda i: (0, 0)),
            pl.BlockSpec((D_MODEL, _SPLIT_ROWS), lambda i: (0, jnp.clip(i - n_qkv - 1, 0, n_rest - 1))),
        ] + rider_specs,
        out_shape=[
            jax.ShapeDtypeStruct((D_MODEL, 2 * D_MODEL), BF16),
            jax.ShapeDtypeStruct((D_MODEL + GATE_ROWS, D_MODEL), BF16),
            jax.ShapeDtypeStruct((D_MODEL, _REST_ROWS), BF16),
        ] + [jax.ShapeDtypeStruct(r.shape, BF16) for r in riders],
        scratch_shapes=[pltpu.VMEM((_SPLIT_ROWS, D_MODEL), F32)],
        compiler_params=pltpu.CompilerParams(
            dimension_semantics=("arbitrary",), vmem_limit_bytes=V7X_VMEM_LIMIT_BYTES),
        name="split_w_in",
    )(w_in_t, *riders)


def _pack_weights(w_in, b_igate, b_fgate, g_norm_mix, g_head, w_pool_grp, pool_scale, w_branch_mlstm,
                  w_branch_pool, w_out, g_norm_ffn, w_up, w_down, g_final):
    d = D_MODEL
    w_qv, w_kg, w_rest, w_ba, w_bb, w_o = _split_w_in(w_in.T, riders=(w_branch_mlstm, w_branch_pool, w_out))
    return {
        "w_qv": w_qv,
        "w_kg": w_kg,
        "w_rest": w_rest,
        "gate_bias": jnp.concatenate([b_igate, b_fgate, b_fgate, b_igate]).astype(F32),
        "g_mix": g_norm_mix.reshape(1, d),
        "g_head": g_head.reshape(1, d),
        "w_grp": w_pool_grp.astype(BF16),
        "pool_scale": pool_scale.reshape(1, D_POOL),
        "w_ba": w_ba,
        "w_bb": w_bb,
        "w_out": w_o,
        "g_ffn": g_norm_ffn.reshape(1, d),
        "g_final": g_final.reshape(1, d),
    }


def kernel(x_prompt, x_sample, state_C, state_n, state_m, state_pool, w_in, b_igate, b_fgate, g_norm_mix, g_head,
           w_pool_grp, pool_scale, w_branch_mlstm, w_branch_pool, w_out, g_norm_ffn, w_up, w_down, g_final):
    depth = w_in.shape[0]
    bp, sp, d = x_prompt.shape
    bs, ss, _ = x_sample.shape
    assert depth == 1 and bp == 1 and d == D_MODEL
    hp = x_prompt.reshape(bp * sp, d)
    hs = x_sample.reshape(bs * ss, d)
    weights = _pack_weights(w_in[0], b_igate[0], b_fgate[0], g_norm_mix[0], g_head[0], w_pool_grp[0], pool_scale[0],
                            w_branch_mlstm[0], w_branch_pool[0], w_out[0], g_norm_ffn[0], w_up[0], w_down[0],
                            g_final)
    zero_state = (jnp.zeros((bp, N_HEADS, HEAD_DIM, HEAD_DIM), F32), jnp.zeros((bp, D_MODEL), F32),
                  jnp.zeros((bp, N_HEADS), F32), jnp.zeros((bp, HIST_ROWS - 1, D_POOL), F32))
    hp, cp, np_, mp, pp, w_up_b, w_down_b = _mixer(hp, weights, zero_state, tm=256, seg=256, carry=True, pos0=0,
                                                   riders=(w_up[0], w_down[0]))
    sample_state = (state_C[0], state_n[0].reshape(bs, D_MODEL), state_m[0], state_pool[0])
    hs, cs, ns, ms, ps = _mixer(hs, weights, sample_state, tm=8 * ss, seg=ss, carry=False, pos0=PAST_LEN)
    y_sample, y_prompt = _ffn(hs, hp, w_up_b, w_down_b, weights, tm_s=512, tm_p=1024)
    y_prompt = y_prompt.reshape(bp, sp, d)
    y_sample = y_sample.reshape(bs, ss, d)
    return (y_prompt, y_sample, cp[None], np_[None], mp[None], pp[None], cs[None], ns[None], ms[None], ps[None])
```

```python
import collections
import functools

import jax
import jax.numpy as jnp
from jax import lax
from jax.experimental import pallas as pl
from jax.experimental.pallas import tpu as pltpu

F32 = jnp.float32
BF16 = jnp.bfloat16

D_MODEL = 1024
N_HEADS = 4
HEAD_DIM = 256
D_POOL = 512
POOL_GROUP = 128
POOL_WINDOWS = (2, 4, 8, 16)
POOL_HIST = max(POOL_WINDOWS) - 1
HIST_ROWS = 16
POOL_T0 = 32
D_FF = 4096
EPS = 1e-6
PAST_LEN = 2048
GATE_ROWS = 16
RIDER_HOLD_STEPS = 2
STATE_RING_SLOTS = 16
V7X_VMEM_LIMIT_BYTES = 58 * 1024 * 1024

_K0, _V0, _QKV_ROWS = 1024, 2048, 3072
_REST0 = _QKV_ROWS + 2 * N_HEADS
_P0, _GA0, _GB0, _REST_ROWS = 1024, 1536, 2560, 3584
_SPLIT_ROWS = 1024

_Slot = collections.namedtuple("_Slot", "q v kt ktb og ga gb gate abuf pd")
_Weights = collections.namedtuple(
    "_Weights", "gmix wqv wrest wkg bias ghead wgrp pscale wba wbb wout")
_CARRY_ORDER = "FPFFFFPPFPPFPFP"
_STREAMS_ORDER = "PPPPFPFPFPFFFFF"


def _dot(a, b):
    return jnp.dot(a, b, preferred_element_type=F32)


def _dot_nt(a, b):
    return lax.dot_general(a, b, (((1,), (1,)), ((), ())), preferred_element_type=F32)


def _rmsnorm(x, g):
    return x * lax.rsqrt(jnp.mean(x * x, axis=-1, keepdims=True) + EPS) * g


def _sigmoid(x):
    return 1.0 / (1.0 + jnp.exp(-x))


def _log_sigmoid(x):
    return jnp.minimum(x, 0.0) - jnp.log1p(jnp.exp(-jnp.abs(x)))


def _masks(tm, seg):
    row = lax.broadcasted_iota(jnp.int32, (tm, tm), 0)
    col = lax.broadcasted_iota(jnp.int32, (tm, tm), 1)
    if tm == seg:
        return col <= row, row <= col
    shift = seg.bit_length() - 1
    same = (row >> shift) == (col >> shift)
    return (col <= row) & same, (row <= col) & same


def _init_scratch(*slots):
    for slot in slots:
        slot.abuf[:, 0:POOL_T0, :] = jnp.zeros((slot.abuf.shape[0], POOL_T0, D_POOL), F32)


def _project(x, w, slot, hist, pos, *, tm, seg):
    nseg = tm // seg
    u = _rmsnorm(x, w.gmix[...]).astype(BF16)
    kg = _dot_nt(w.wkg[...], u)
    kt = kg[0:D_MODEL] * (HEAD_DIM ** -0.5)
    slot.kt[...] = kt
    slot.ktb[...] = kt.astype(BF16)
    gates = kg[D_MODEL:D_MODEL + GATE_ROWS] + w.bias[...]
    yield
    p = _dot(u, w.wrest[:, _P0:_GA0])
    for j in range(nseg):
        slot.abuf[j, POOL_T0 - hist[j].shape[0]:POOL_T0, :] = hist[j]
        slot.abuf[j, POOL_T0:POOL_T0 + seg, :] = p[j * seg:(j + 1) * seg]
    for gi, win in enumerate(POOL_WINDOWS):
        ls = slice(gi * POOL_GROUP, (gi + 1) * POOL_GROUP)
        cnt = jnp.minimum(pos + 1, win).astype(F32)
        for j in range(nseg):
            start = POOL_T0 - 8 * gi
            tot = slot.abuf[j, start:POOL_T0 + seg, ls] + slot.abuf[j, start - 1:POOL_T0 + seg - 1, ls]
            shift = 2
            while shift < win:
                n = tot.shape[0]
                tot = tot[8:n] + tot[8 - shift:n - shift]
                shift *= 2
            tok = slot.abuf[j, POOL_T0:POOL_T0 + seg, ls]
            slot.pd[j * seg:(j + 1) * seg, ls] = (tot / cnt - tok).astype(BF16)
    yield
    slot.q[...] = _dot(u, w.wqv[:, 0:D_MODEL]).astype(BF16)
    _, upper = _masks(tm, seg)
    lf = _log_sigmoid(gates[8:16])
    hi = lf.astype(BF16).astype(F32)
    r1 = lf - hi
    mid = r1.astype(BF16).astype(F32)
    lo = r1 - mid
    pieces = jnp.concatenate([hi, mid, lo, jnp.zeros_like(lo)], axis=0).astype(BF16)
    cs = _dot(pieces, jnp.where(upper, 1.0, 0.0).astype(BF16))
    b = cs[0:8] + cs[8:16] + cs[16:24]
    slot.gate[0:8, :] = gates[0:8] - b
    slot.gate[8:16, :] = lf
    yield
    slot.v[...] = _dot(u, w.wqv[:, D_MODEL:2 * D_MODEL]).astype(BF16)
    yield
    slot.og[...] = _dot(u, w.wrest[:, 0:_P0])
    yield
    slot.ga[...] = _dot(u, w.wrest[:, _GA0:_GB0])
    yield
    slot.gb[...] = _dot(u, w.wrest[:, _GB0:_REST_ROWS])
    yield


def _finish(x, out_ref, rows, w, slot, hcat_s, c_rd, n_rd, m_rd, c_store, n_out, m_out, *, tm, seg):
    nseg = tm // seg
    causal, _ = _masks(tm, seg)

    heads = []
    for h in range(N_HEADS):
        sl = slice(h * HEAD_DIM, (h + 1) * HEAD_DIM)
        a_m = jnp.where(causal, jnp.broadcast_to(slot.gate[h:h + 1, :], (tm, tm)), -jnp.inf)
        mprev = jnp.concatenate(
            [jnp.broadcast_to(m_rd[j:j + 1, h:h + 1], (seg, 1)) for j in range(nseg)], axis=0)
        g = jnp.maximum(jnp.max(a_m, axis=-1, keepdims=True), mprev)
        dm = jnp.exp(a_m - g)
        lf_b = jnp.broadcast_to(slot.gate[8 + h:9 + h, :], (tm, tm))
        bcol = jnp.sum(jnp.where(causal, lf_b, 0.0), axis=-1, keepdims=True)
        mcol = bcol + g
        wi = jnp.exp(mprev - g)

        qh = slot.q[:, sl]
        vh = slot.v[:, sl]
        s = _dot(qh, slot.ktb[sl, :]) * dm
        qf = qh.astype(F32)
        qc_parts, qn_parts = [], []
        for j in range(nseg):
            rs = slice(j * seg, (j + 1) * seg)
            qc_parts.append(_dot(qh[rs], c_rd[j, h].astype(BF16)))
            qn_parts.append(jnp.sum(qf[rs] * n_rd[j, h:h + 1, :], axis=-1, keepdims=True))
        qc = jnp.concatenate(qc_parts, axis=0)
        qn = jnp.concatenate(qn_parts, axis=0)
        heads.append((sl, dm, mcol, wi, vh, s, qc, qn))
    yield

    for h, (sl, dm, mcol, wi, vh, s, qc, qn) in enumerate(heads):
        kth = slot.kt[sl, :]
        rowsum = jnp.sum(s, axis=-1, keepdims=True)
        sv = _dot(s.astype(BF16), vh)
        num = wi * qc + sv
        den = wi * qn + rowsum
        hh = num / jnp.maximum(jnp.abs(den), jnp.exp(-mcol))
        hh = hh * lax.rsqrt(jnp.mean(hh * hh, axis=-1, keepdims=True) + EPS) * w.ghead[:, sl]
        hh = hh * _sigmoid(slot.og[:, sl])
        hcat_s[:, sl] = hh.astype(BF16)

        wrows, n_decayed = [], []
        for j in range(nseg):
            r = (j + 1) * seg - 1
            w_row = dm[r:r + 1, :]
            wrows.append(w_row)
            decay = wi[r:r + 1, :]
            kw = (kth * w_row).astype(BF16)
            c_store(j, h, decay * c_rd[j, h] + _dot(kw, vh))
            n_decayed.append(decay * n_rd[j, h:h + 1, :])
            m_out[j:j + 1, h:h + 1] = mcol[r:r + 1, :]
        ridx = lax.broadcasted_iota(jnp.int32, (GATE_ROWS, tm), 0)
        wmat = jnp.zeros((GATE_ROWS, tm), F32)
        for j in range(nseg):
            wmat = jnp.where(ridx == j, jnp.broadcast_to(wrows[j], (GATE_ROWS, tm)), wmat)
        nupd = _dot_nt(wmat.astype(BF16), slot.ktb[sl, :])
        for j in range(nseg):
            n_out[j, h:h + 1, :] = n_decayed[j] + nupd[j:j + 1, :]
        yield

    grp_out = [_dot(slot.pd[:, gi * POOL_GROUP:(gi + 1) * POOL_GROUP], w.wgrp[gi])
               for gi in range(len(POOL_WINDOWS))]
    pooled = jnp.concatenate(grp_out, axis=-1) * w.pscale[...]
    yield

    branch_a = _dot(hcat_s[...], w.wba[...])
    branch_b = _dot(pooled.astype(BF16), w.wbb[...])
    yield
    mixed = _sigmoid(slot.ga[...]) * branch_a + _sigmoid(slot.gb[...]) * branch_b
    out_ref[rows, :] = x + _dot(mixed.astype(BF16), w.wout[...])
    yield


def _run(order, **gens):
    for name in order:
        next(gens[name], None)
    for gen in gens.values():
        for _ in gen:
            pass


class _StateRing:
    def __init__(self, dst_hbm, ring, sems, first_stream, step, last_step, stores_per_step):
        self.dst, self.ring, self.sems, self.first = dst_hbm, ring, sems, first_stream
        self.step, self.last_step = step, last_step
        self.bank = ring.shape[0] // 2
        assert stores_per_step % ring.shape[0] == 0
        self.count = 0

    def _copy(self, slot, stream, h):
        return pltpu.make_async_copy(self.ring.at[slot], self.dst.at[stream, h], self.sems.at[slot])

    def _wait_bank(self, bank):
        for slot in range(bank * self.bank, (bank + 1) * self.bank):
            self._copy(slot, 0, 0).wait()

    def store(self, j, h, value):
        slot = self.count % self.ring.shape[0]
        if slot % self.bank == 0:
            bank = slot // self.bank
            if self.count < self.ring.shape[0]:
                pl.when(self.step > 0)(functools.partial(self._wait_bank, bank))
            else:
                self._wait_bank(bank)
        self.count += 1
        self.ring[slot] = value
        self._copy(slot, self.first + j, h).start()

    def finish(self):
        @pl.when(self.step == self.last_step)
        def _():
            self._wait_bank(0)
            self._wait_bank(1)


def _carry_kernel(x_ref, xnext_ref, gmix_ref, wqv_ref, wrest_ref, wkg_ref, bias_ref, ghead_ref, wgrp_ref,
                  pscale_ref, wba_ref, wbb_ref, wout_ref, *rest, tm, n_riders):
    i = pl.program_id(0)
    rider_in, rest = rest[:n_riders], rest[n_riders:]
    x1_ref, c_out, n_out, m_out, hist_out = rest[:5]
    rider_out, (hcat_s, *slot_refs) = rest[5:5 + n_riders], rest[5 + n_riders:]
    w = _Weights(gmix_ref, wqv_ref, wrest_ref, wkg_ref, bias_ref, ghead_ref, wgrp_ref, pscale_ref,
                 wba_ref, wbb_ref, wout_ref)
    nslot = len(_Slot._fields)
    slot_a, slot_b = _Slot(*slot_refs[:nslot]), _Slot(*slot_refs[nslot:])
    kw = dict(tm=tm, seg=tm)
    for src, dst in zip(rider_in, rider_out):
        dst[...] = src[...].astype(BF16)

    row = lax.broadcasted_iota(jnp.int32, (tm, POOL_GROUP), 0)

    @pl.when(i == 0)
    def _():
        _init_scratch(slot_a, slot_b)
        c_out[...] = jnp.zeros(c_out.shape, F32)
        n_out[...] = jnp.zeros(n_out.shape, F32)
        m_out[...] = jnp.zeros(m_out.shape, F32)
        _run("", P=_project(x_ref[0:tm, :], w, slot_a, [jnp.zeros((HIST_ROWS, D_POOL), F32)], row, **kw))

    state = (c_out, n_out, m_out)

    def c_store(j, h, value):
        c_out[j, h] = value

    def finish_beside(rows, slot, tile, x_proj, slot_proj):
        new_hist = slot.abuf[0, POOL_T0 + tm - HIST_ROWS:POOL_T0 + tm, :]
        _run(_CARRY_ORDER,
             F=_finish(x_ref[rows, :], x1_ref, rows, w, slot, hcat_s, *state, c_store, n_out, m_out, **kw),
             P=_project(x_proj, w, slot_proj, [new_hist], row + (tile + 1) * tm, **kw))

    finish_beside(slice(0, tm), slot_a, 2 * i, x_ref[tm:2 * tm, :], slot_b)
    finish_beside(slice(tm, 2 * tm), slot_b, 2 * i + 1, xnext_ref[...], slot_a)
    hist_out[0] = slot_b.abuf[0, POOL_T0 + tm - POOL_HIST:POOL_T0 + tm, :]


def _streams_kernel(x_ref, gmix_ref, wqv_ref, wrest_ref, wkg_ref, bias_ref, ghead_ref, wgrp_ref,
                    pscale_ref, wba_ref, wbb_ref, wout_ref, c_in, n_in, m_in, hist_in,
                    x1_ref, c_hbm, n_out, m_out, hist_out,
                    hcat_s, cring_s, cring_sem, *slot_refs, tm, seg, pos0):
    i = pl.program_id(0)
    nseg = tm // seg
    w = _Weights(gmix_ref, wqv_ref, wrest_ref, wkg_ref, bias_ref, ghead_ref, wgrp_ref, pscale_ref,
                 wba_ref, wbb_ref, wout_ref)
    slot = _Slot(*slot_refs)

    @pl.when(i == 0)
    def _():
        _init_scratch(slot)

    x = x_ref[...]
    pos = lax.broadcasted_iota(jnp.int32, (seg, POOL_GROUP), 0) + pos0
    ring = _StateRing(c_hbm, cring_s, cring_sem, first_stream=i * nseg, step=i, last_step=pl.num_programs(0) - 1,
                      stores_per_step=nseg * N_HEADS)
    _run(_STREAMS_ORDER,
         P=_project(x, w, slot, [hist_in[j] for j in range(nseg)], pos, tm=tm, seg=seg),
         F=_finish(x, x1_ref, slice(0, tm), w, slot, hcat_s, c_in, n_in, m_in, ring.store, n_out, m_out,
                   tm=tm, seg=seg))
    ring.finish()
    for j in range(nseg):
        hist_out[j] = slot.abuf[j, POOL_T0 + seg - POOL_HIST:POOL_T0 + seg, :]


def _ffn_tile(x_ref, y_ref, gffn_ref, wup_ref, wdown_ref, gfin_ref, sub):
    n_sub = x_ref.shape[0] // sub
    rows = [slice(r * sub, (r + 1) * sub) for r in range(n_sub)]
    act = {}

    def up(r):
        u = _rmsnorm(x_ref[rows[r], :], gffn_ref[...]).astype(BF16)
        act[r] = jnp.square(jnp.maximum(_dot(u, wup_ref[...]), 0.0)).astype(BF16)

    def down(r):
        x2 = x_ref[rows[r], :] + _dot(act.pop(r), wdown_ref[...])
        y_ref[rows[r], :] = _rmsnorm(x2, gfin_ref[...])

    up(0)
    for r in range(1, n_sub):
        up(r)
        down(r - 1)
    down(n_sub - 1)


def _ffn_kernel(xs_ref, xp_ref, gffn_ref, wup_ref, wdown_ref, gfin_ref, ys_ref, yp_ref, *, sub, n_s):
    i = pl.program_id(0)

    @pl.when(i < n_s)
    def _():
        _ffn_tile(xs_ref, ys_ref, gffn_ref, wup_ref, wdown_ref, gfin_ref, sub)

    @pl.when(i >= n_s)
    def _():
        _ffn_tile(xp_ref, yp_ref, gffn_ref, wup_ref, wdown_ref, gfin_ref, sub)


def _resident(shape):
    zeros = (0,) * len(shape)
    return pl.BlockSpec(shape, lambda i: zeros, pipeline_mode=pl.Buffered(1))


def _slot_scratch(tm, seg):
    return [
        pltpu.VMEM((tm, D_MODEL), BF16),
        pltpu.VMEM((tm, D_MODEL), BF16),
        pltpu.VMEM((D_MODEL, tm), F32),
        pltpu.VMEM((D_MODEL, tm), BF16),
        pltpu.VMEM((tm, D_MODEL), F32),
        pltpu.VMEM((tm, D_MODEL), F32),
        pltpu.VMEM((tm, D_MODEL), F32),
        pltpu.VMEM((GATE_ROWS, tm), F32),
        pltpu.VMEM((tm // seg, POOL_T0 + seg, D_POOL), F32),
        pltpu.VMEM((tm, D_POOL), BF16),
    ]


def _mixer(x, weights, state, *, tm, seg, carry, pos0, riders=()):
    t = x.shape[0]
    nseg = tm // seg
    n_streams = 1 if carry else t // seg
    sblk = 1 if carry else nseg
    smap = (lambda i: 0) if carry else (lambda i: i)
    bias = jnp.broadcast_to(weights["gate_bias"][:, None], (GATE_ROWS, tm))

    if carry:
        assert pos0 == 0 and nseg == 1 and t % (2 * tm) == 0
        steps = t // (2 * tm)
        last_tile = t // tm - 1
        x_specs = [pl.BlockSpec((2 * tm, D_MODEL), lambda i: (i, 0)),
                   pl.BlockSpec((tm, D_MODEL), lambda i: (jnp.minimum(2 * i + 2, last_tile), 0))]
        x_args = [x, x]
        x1_spec = pl.BlockSpec((2 * tm, D_MODEL), lambda i: (i, 0))
        body = functools.partial(_carry_kernel, tm=tm, n_riders=len(riders))
        slots = _slot_scratch(tm, seg) + _slot_scratch(tm, seg)
        assert steps % RIDER_HOLD_STEPS == 0 and all(r.shape[0] % (16 * steps) == 0 for r in riders)
    else:
        steps = t // tm
        x_specs = [pl.BlockSpec((tm, D_MODEL), lambda i: (i, 0))]
        x_args = [x]
        x1_spec = pl.BlockSpec((tm, D_MODEL), lambda i: (i, 0))
        body = functools.partial(_streams_kernel, tm=tm, seg=seg, pos0=pos0)
        slots = [pltpu.VMEM((STATE_RING_SLOTS, HEAD_DIM, HEAD_DIM), F32),
                 pltpu.SemaphoreType.DMA((STATE_RING_SLOTS,))] + _slot_scratch(tm, seg)

    state_specs = [
        pl.BlockSpec((sblk, N_HEADS, HEAD_DIM, HEAD_DIM), lambda i: (smap(i), 0, 0, 0)),
        pl.BlockSpec((sblk, N_HEADS, HEAD_DIM), lambda i: (smap(i), 0, 0)),
        pl.BlockSpec((sblk, N_HEADS), lambda i: (smap(i), 0)),
        pl.BlockSpec((sblk, POOL_HIST, D_POOL), lambda i: (smap(i), 0, 0)),
    ]
    in_specs = x_specs + [
        _resident((1, D_MODEL)),
        _resident((D_MODEL, 2 * D_MODEL)),
        _resident((D_MODEL, _REST_ROWS)),
        _resident((D_MODEL + GATE_ROWS, D_MODEL)),
        _resident((GATE_ROWS, tm)),
        _resident((1, D_MODEL)),
        _resident((len(POOL_WINDOWS), POOL_GROUP, POOL_GROUP)),
        _resident((1, D_POOL)),
        _resident((D_MODEL, D_MODEL)),
        _resident((D_POOL, D_MODEL)),
        _resident((D_MODEL, D_MODEL)),
    ] + ([] if carry else state_specs)
    rider_specs = [pl.BlockSpec((r.shape[0] * RIDER_HOLD_STEPS // steps, r.shape[1]),
                                lambda i: (i // RIDER_HOLD_STEPS, 0)) for r in riders]
    out_shape = [
        jax.ShapeDtypeStruct((t, D_MODEL), F32),
        jax.ShapeDtypeStruct((n_streams, N_HEADS, HEAD_DIM, HEAD_DIM), F32),
        jax.ShapeDtypeStruct((n_streams, N_HEADS, HEAD_DIM), F32),
        jax.ShapeDtypeStruct((n_streams, N_HEADS), F32),
        jax.ShapeDtypeStruct((n_streams, POOL_HIST, D_POOL), F32),
    ] + [jax.ShapeDtypeStruct(r.shape, BF16) for r in riders]
    scratch = [pltpu.VMEM((tm, D_MODEL), BF16)] + slots
    return pl.pallas_call(
        body,
        grid=(steps,),
        in_specs=in_specs + rider_specs,
        out_specs=([x1_spec, state_specs[0] if carry else pl.BlockSpec(memory_space=pl.ANY)] + state_specs[1:]
                   + rider_specs),
        out_shape=out_shape,
        scratch_shapes=scratch,
        compiler_params=pltpu.CompilerParams(
            dimension_semantics=("arbitrary",), vmem_limit_bytes=V7X_VMEM_LIMIT_BYTES),
        name="mixer_carry" if carry else "mixer_streams",
    )(*x_args, weights["g_mix"], weights["w_qv"], weights["w_rest"], weights["w_kg"], bias, weights["g_head"],
      weights["w_grp"], weights["pool_scale"], weights["w_ba"], weights["w_bb"], weights["w_out"], *state, *riders)


def _ffn(x_sample, x_prompt, w_up, w_down, weights, *, tm_s, tm_p):
    ts, tp = x_sample.shape[0], x_prompt.shape[0]
    n_s, n_p = ts // tm_s, tp // tm_p
    clamp = lambda v, hi: jnp.clip(v, 0, hi)
    return pl.pallas_call(
        functools.partial(_ffn_kernel, sub=256, n_s=n_s),
        grid=(n_s + n_p,),
        in_specs=[
            pl.BlockSpec((tm_s, D_MODEL), lambda i: (clamp(i, n_s - 1), 0)),
            pl.BlockSpec((tm_p, D_MODEL), lambda i: (clamp(i - n_s, n_p - 1), 0)),
            _resident((1, D_MODEL)),
            _resident((D_MODEL, D_FF)),
            _resident((D_FF, D_MODEL)),
            _resident((1, D_MODEL)),
        ],
        out_specs=[
            pl.BlockSpec((tm_s, D_MODEL), lambda i: (clamp(i, n_s - 1), 0)),
            pl.BlockSpec((tm_p, D_MODEL), lambda i: (clamp(i - n_s, n_p - 1), 0)),
        ],
        out_shape=[jax.ShapeDtypeStruct((ts, D_MODEL), F32), jax.ShapeDtypeStruct((tp, D_MODEL), F32)],
        compiler_params=pltpu.CompilerParams(
            dimension_semantics=("arbitrary",), vmem_limit_bytes=V7X_VMEM_LIMIT_BYTES),
        name="ffn",
    )(x_sample, x_prompt, weights["g_ffn"], w_up, w_down, weights["g_final"])


def _split_w_in_kernel(w_ref, *refs, n_riders):
    i = pl.program_id(0)
    rider_in, (wqv_ref, wkg_ref, wrest_ref) = refs[:n_riders], refs[n_riders:n_riders + 3]
    rider_out, prev_s = refs[n_riders + 3:2 * n_riders + 3], refs[2 * n_riders + 3]
    n_qkv = _QKV_ROWS // _SPLIT_ROWS
    cur = w_ref[...]
    for src, dst in zip(rider_in, rider_out):
        dst[...] = src[...].astype(BF16)

    @pl.when((i == 0) | (i == 2))
    def _():
        wqv_ref[...] = cur.T.astype(BF16)

    @pl.when(i == 1)
    def _():
        wkg_ref[0:D_MODEL, :] = cur.astype(BF16)

    @pl.when(i == n_qkv)
    def _():
        gate = cur[0:2 * N_HEADS, :]
        wkg_ref[D_MODEL:D_MODEL + GATE_ROWS, :] = jnp.concatenate(
            [gate, pltpu.roll(gate, N_HEADS, axis=0)], axis=0).astype(BF16)

    @pl.when(i > n_qkv)
    def _():
        rows = jnp.concatenate([prev_s[2 * N_HEADS:, :], cur[0:2 * N_HEADS, :]], axis=0)
        wrest_ref[...] = rows.T.astype(BF16)

    prev_s[...] = cur


def _split_w_in(w_in_t, riders=()):
    n_in = w_in_t.shape[0]
    assert n_in == _REST0 + _REST_ROWS and _SPLIT_ROWS == D_MODEL and _QKV_ROWS == 3 * _SPLIT_ROWS
    n_qkv, n_rest = _QKV_ROWS // _SPLIT_ROWS, pl.cdiv(_REST_ROWS, _SPLIT_ROWS)
    steps = n_qkv + 1 + n_rest
    last_chunk = pl.cdiv(n_in, _SPLIT_ROWS) - 1
    assert all(r.shape[0] % (16 * steps) == 0 for r in riders)
    rider_specs = [pl.BlockSpec((r.shape[0] // steps, r.shape[1]), lambda i: (i, 0)) for r in riders]
    return pl.pallas_call(
        functools.partial(_split_w_in_kernel, n_riders=len(riders)),
        grid=(steps,),
        in_specs=[pl.BlockSpec((_SPLIT_ROWS, D_MODEL), lambda i: (jnp.minimum(i, last_chunk), 0))] + rider_specs,
        out_specs=[
            pl.BlockSpec((D_MODEL, _SPLIT_ROWS), lambda i: (0, jnp.where(i < 2, 0, 1))),
            pl.BlockSpec((D_MODEL + GATE_ROWS, D_MODEL), lambda i: (0, 0)),
            pl.BlockSpec((D_MODEL, _SPLIT_ROWS), lambda i: (0, jnp.clip(i - n_qkv - 1, 0, n_rest - 1))),
        ] + rider_specs,
        out_shape=[
            jax.ShapeDtypeStruct((D_MODEL, 2 * D_MODEL), BF16),
            jax.ShapeDtypeStruct((D_MODEL + GATE_ROWS, D_MODEL), BF16),
            jax.ShapeDtypeStruct((D_MODEL, _REST_ROWS), BF16),
        ] + [jax.ShapeDtypeStruct(r.shape, BF16) for r in riders],
        scratch_shapes=[pltpu.VMEM((_SPLIT_ROWS, D_MODEL), F32)],
        compiler_params=pltpu.CompilerParams(
            dimension_semantics=("arbitrary",), vmem_limit_bytes=V7X_VMEM_LIMIT_BYTES),
        name="split_w_in",
    )(w_in_t, *riders)


def _pack_weights(w_in, b_igate, b_fgate, g_norm_mix, g_head, w_pool_grp, pool_scale, w_branch_mlstm,
                  w_branch_pool, w_out, g_norm_ffn, w_up, w_down, g_final):
    d = D_MODEL
    w_qv, w_kg, w_rest, w_ba, w_bb, w_o, w_grp = _split_w_in(
        w_in.T, riders=(w_branch_mlstm, w_branch_pool, w_out, w_pool_grp.reshape(-1, POOL_GROUP)))
    return {
        "w_qv": w_qv,
        "w_kg": w_kg,
        "w_rest": w_rest,
        "gate_bias": jnp.concatenate([b_igate, b_fgate, b_fgate, b_igate]).astype(F32),
        "g_mix": g_norm_mix.reshape(1, d),
        "g_head": g_head.reshape(1, d),
        "w_grp": w_grp.reshape(w_pool_grp.shape),
        "pool_scale": pool_scale.reshape(1, D_POOL),
        "w_ba": w_ba,
        "w_bb": w_bb,
        "w_out": w_o,
        "g_ffn": g_norm_ffn.reshape(1, d),
        "g_final": g_final.reshape(1, d),
    }


def kernel(x_prompt, x_sample, state_C, state_n, state_m, state_pool, w_in, b_igate, b_fgate, g_norm_mix, g_head,
           w_pool_grp, pool_scale, w_branch_mlstm, w_branch_pool, w_out, g_norm_ffn, w_up, w_down, g_final):
    depth = w_in.shape[0]
    bp, sp, d = x_prompt.shape
    bs, ss, _ = x_sample.shape
    assert depth == 1 and bp == 1 and d == D_MODEL
    hp = x_prompt.reshape(bp * sp, d)
    hs = x_sample.reshape(bs * ss, d)
    weights = _pack_weights(w_in[0], b_igate[0], b_fgate[0], g_norm_mix[0], g_head[0], w_pool_grp[0], pool_scale[0],
                            w_branch_mlstm[0], w_branch_pool[0], w_out[0], g_norm_ffn[0], w_up[0], w_down[0],
                            g_final)
    hp, cp, np_, mp, pp, w_up_b, w_down_b = _mixer(hp, weights, (), tm=256, seg=256, carry=True, pos0=0,
                                                   riders=(w_up[0], w_down[0]))
    sample_state = (state_C[0], state_n[0], state_m[0], state_pool[0])
    hs, cs, ns, ms, ps = _mixer(hs, weights, sample_state, tm=8 * ss, seg=ss, carry=False, pos0=PAST_LEN)
    y_sample, y_prompt = _ffn(hs, hp, w_up_b, w_down_b, weights, tm_s=512, tm_p=1024)
    y_prompt = y_prompt.reshape(bp, sp, d)
    y_sample = y_sample.reshape(bs, ss, d)
    return (y_prompt, y_sample, cp[None], np_[None], mp[None], pp[None], cs[None], ns[None], ms[None], ps[None])
```

```python
import collections
import functools

import jax
import jax.numpy as jnp
from jax import lax
from jax.experimental import pallas as pl
from jax.experimental.pallas import tpu as pltpu

F32 = jnp.float32
BF16 = jnp.bfloat16

D_MODEL = 1024
N_HEADS = 4
HEAD_DIM = 256
D_POOL = 512
POOL_GROUP = 128
POOL_WINDOWS = (2, 4, 8, 16)
POOL_HIST = max(POOL_WINDOWS) - 1
HIST_ROWS = 16
POOL_T0 = 32
D_FF = 4096
EPS = 1e-6
PAST_LEN = 2048
LANES = 128
GATE_ROWS = 16
RIDER_HOLD_STEPS = 2
STATE_RING_SLOTS = 16
V7X_VMEM_LIMIT_BYTES = 58 * 1024 * 1024

_K0, _V0, _QKV_ROWS = 1024, 2048, 3072
_REST0 = _QKV_ROWS + 2 * N_HEADS
_P0, _GA0, _GB0, _REST_ROWS = 1024, 1536, 2560, 3584
_SPLIT_ROWS = 1024

_Slot = collections.namedtuple("_Slot", "q v kt ktb og ga gb gate abuf pd")
_Weights = collections.namedtuple(
    "_Weights", "gmix wqv wrest wkg bias ghead wgrp pscale wba wbb wout")
_CARRY_ORDER = "FPFFFFPPFPPFPFP"
_STREAMS_ORDER = "PPPPFPFPFPFFFFF"


def _dot(a, b):
    return jnp.dot(a, b, preferred_element_type=F32)


def _dot_nt(a, b):
    return lax.dot_general(a, b, (((1,), (1,)), ((), ())), preferred_element_type=F32)


def _rmsnorm(x, g):
    return x * lax.rsqrt(jnp.mean(x * x, axis=-1, keepdims=True) + EPS) * g


def _sigmoid(x):
    return 1.0 / (1.0 + jnp.exp(-x))


def _log_sigmoid(x):
    return jnp.minimum(x, 0.0) - jnp.log1p(jnp.exp(-jnp.abs(x)))


def _masks(tm, seg):
    row = lax.broadcasted_iota(jnp.int32, (tm, tm), 0)
    col = lax.broadcasted_iota(jnp.int32, (tm, tm), 1)
    if tm == seg:
        return col <= row, row <= col
    shift = seg.bit_length() - 1
    same = (row >> shift) == (col >> shift)
    return (col <= row) & same, (row <= col) & same


def _init_scratch(*slots):
    for slot in slots:
        slot.abuf[:, 0:POOL_T0, :] = jnp.zeros((slot.abuf.shape[0], POOL_T0, D_POOL), F32)


def _project(x, w, slot, hist, pos, *, tm, seg):
    nseg = tm // seg
    u = _rmsnorm(x, w.gmix[...]).astype(BF16)
    kg = _dot_nt(w.wkg[...], u)
    kt = kg[0:D_MODEL] * (HEAD_DIM ** -0.5)
    slot.kt[...] = kt
    slot.ktb[...] = kt.astype(BF16)
    gates = kg[D_MODEL:D_MODEL + GATE_ROWS] + w.bias[...]
    yield
    p = _dot(u, w.wrest[:, _P0:_GA0])
    for j in range(nseg):
        slot.abuf[j, POOL_T0 - hist[j].shape[0]:POOL_T0, :] = hist[j]
        slot.abuf[j, POOL_T0:POOL_T0 + seg, :] = p[j * seg:(j + 1) * seg]
    for gi, win in enumerate(POOL_WINDOWS):
        ls = slice(gi * POOL_GROUP, (gi + 1) * POOL_GROUP)
        cnt = jnp.minimum(pos + 1, win).astype(F32)
        for j in range(nseg):
            start = POOL_T0 - 8 * gi
            tot = slot.abuf[j, start:POOL_T0 + seg, ls] + slot.abuf[j, start - 1:POOL_T0 + seg - 1, ls]
            shift = 2
            while shift < win:
                n = tot.shape[0]
                tot = tot[8:n] + tot[8 - shift:n - shift]
                shift *= 2
            tok = slot.abuf[j, POOL_T0:POOL_T0 + seg, ls]
            slot.pd[j * seg:(j + 1) * seg, ls] = (tot / cnt - tok).astype(BF16)
    yield
    slot.q[...] = _dot(u, w.wqv[:, 0:D_MODEL]).astype(BF16)
    _, upper = _masks(tm, seg)
    lf = _log_sigmoid(gates[8:16])
    hi = lf.astype(BF16).astype(F32)
    r1 = lf - hi
    mid = r1.astype(BF16).astype(F32)
    lo = r1 - mid
    pieces = jnp.concatenate([hi, mid, lo, jnp.zeros_like(lo)], axis=0).astype(BF16)
    cs = _dot(pieces, jnp.where(upper, 1.0, 0.0).astype(BF16))
    b = cs[0:8] + cs[8:16] + cs[16:24]
    slot.gate[0:8, :] = gates[0:8] - b
    slot.gate[8:16, :] = lf
    yield
    slot.v[...] = _dot(u, w.wqv[:, D_MODEL:2 * D_MODEL]).astype(BF16)
    yield
    slot.og[...] = _dot(u, w.wrest[:, 0:_P0])
    yield
    slot.ga[...] = _dot(u, w.wrest[:, _GA0:_GB0])
    yield
    slot.gb[...] = _dot(u, w.wrest[:, _GB0:_REST_ROWS])
    yield


def _finish(x, out_ref, rows, w, slot, hcat_s, c_rd, n_rd, m_rd, c_store, n_store, m_store, *, tm, seg):
    nseg = tm // seg
    causal, _ = _masks(tm, seg)

    heads = []
    for h in range(N_HEADS):
        sl = slice(h * HEAD_DIM, (h + 1) * HEAD_DIM)
        a_m = jnp.where(causal, jnp.broadcast_to(slot.gate[h:h + 1, :], (tm, tm)), -jnp.inf)
        mprev = jnp.concatenate(
            [jnp.broadcast_to(m_rd(j, h), (seg, 1)) for j in range(nseg)], axis=0)
        g = jnp.maximum(jnp.max(a_m, axis=-1, keepdims=True), mprev)
        dm = jnp.exp(a_m - g)
        lf_b = jnp.broadcast_to(slot.gate[8 + h:9 + h, :], (tm, tm))
        bcol = jnp.sum(jnp.where(causal, lf_b, 0.0), axis=-1, keepdims=True)
        mcol = bcol + g
        wi = jnp.exp(mprev - g)

        qh = slot.q[:, sl]
        vh = slot.v[:, sl]
        s = _dot(qh, slot.ktb[sl, :]) * dm
        qf = qh.astype(F32)
        qc_parts, qn_parts = [], []
        for j in range(nseg):
            rs = slice(j * seg, (j + 1) * seg)
            qc_parts.append(_dot(qh[rs], c_rd[j, h].astype(BF16)))
            qn_parts.append(jnp.sum(qf[rs] * n_rd(j, h), axis=-1, keepdims=True))
        qc = jnp.concatenate(qc_parts, axis=0)
        qn = jnp.concatenate(qn_parts, axis=0)
        heads.append((sl, dm, mcol, wi, vh, s, qc, qn))
    yield

    for h, (sl, dm, mcol, wi, vh, s, qc, qn) in enumerate(heads):
        kth = slot.kt[sl, :]
        rowsum = jnp.sum(s, axis=-1, keepdims=True)
        sv = _dot(s.astype(BF16), vh)
        num = wi * qc + sv
        den = wi * qn + rowsum
        hh = num / jnp.maximum(jnp.abs(den), jnp.exp(-mcol))
        hh = hh * lax.rsqrt(jnp.mean(hh * hh, axis=-1, keepdims=True) + EPS) * w.ghead[:, sl]
        hh = hh * _sigmoid(slot.og[:, sl])
        hcat_s[:, sl] = hh.astype(BF16)

        wrows, n_decayed = [], []
        for j in range(nseg):
            r = (j + 1) * seg - 1
            w_row = dm[r:r + 1, :]
            wrows.append(w_row)
            decay = wi[r:r + 1, :]
            kw = (kth * w_row).astype(BF16)
            c_store(j, h, decay * c_rd[j, h] + _dot(kw, vh))
            n_decayed.append(decay * n_rd(j, h))
            m_store(j, h, mcol[r:r + 1, :])
        ridx = lax.broadcasted_iota(jnp.int32, (GATE_ROWS, tm), 0)
        wmat = jnp.zeros((GATE_ROWS, tm), F32)
        for j in range(nseg):
            wmat = jnp.where(ridx == j, jnp.broadcast_to(wrows[j], (GATE_ROWS, tm)), wmat)
        nupd = _dot_nt(wmat.astype(BF16), slot.ktb[sl, :])
        for j in range(nseg):
            n_store(j, h, n_decayed[j] + nupd[j:j + 1, :])
        yield

    grp_out = [_dot(slot.pd[:, gi * POOL_GROUP:(gi + 1) * POOL_GROUP], w.wgrp[gi])
               for gi in range(len(POOL_WINDOWS))]
    pooled = jnp.concatenate(grp_out, axis=-1) * w.pscale[...]
    yield

    branch_a = _dot(hcat_s[...], w.wba[...])
    branch_b = _dot(pooled.astype(BF16), w.wbb[...])
    yield
    mixed = _sigmoid(slot.ga[...]) * branch_a + _sigmoid(slot.gb[...]) * branch_b
    out_ref[rows, :] = x + _dot(mixed.astype(BF16), w.wout[...])
    yield


def _run(order, **gens):
    for name in order:
        next(gens[name], None)
    for gen in gens.values():
        for _ in gen:
            pass


class _StateRing:
    def __init__(self, dst_hbm, ring, sems, first_stream, step, last_step, stores_per_step):
        self.dst, self.ring, self.sems, self.first = dst_hbm, ring, sems, first_stream
        self.step, self.last_step = step, last_step
        self.bank = ring.shape[0] // 2
        assert stores_per_step % ring.shape[0] == 0
        self.count = 0

    def _copy(self, slot, stream, h):
        return pltpu.make_async_copy(self.ring.at[slot], self.dst.at[stream, h], self.sems.at[slot])

    def _wait_bank(self, bank):
        for slot in range(bank * self.bank, (bank + 1) * self.bank):
            self._copy(slot, 0, 0).wait()

    def store(self, j, h, value):
        slot = self.count % self.ring.shape[0]
        if slot % self.bank == 0:
            bank = slot // self.bank
            if self.count < self.ring.shape[0]:
                pl.when(self.step > 0)(functools.partial(self._wait_bank, bank))
            else:
                self._wait_bank(bank)
        self.count += 1
        self.ring[slot] = value
        self._copy(slot, self.first + j, h).start()

    def finish(self):
        @pl.when(self.step == self.last_step)
        def _():
            self._wait_bank(0)
            self._wait_bank(1)


def _carry_kernel(x_ref, xnext_ref, gmix_ref, wqv_ref, wrest_ref, wkg_ref, bias_ref, ghead_ref, wgrp_ref,
                  pscale_ref, wba_ref, wbb_ref, wout_ref, *rest, tm, n_riders):
    i = pl.program_id(0)
    rider_in, rest = rest[:n_riders], rest[n_riders:]
    x1_ref, c_out, n_out, m_out, hist_out = rest[:5]
    rider_out, (hcat_s, n_s, m_s, *slot_refs) = rest[5:5 + n_riders], rest[5 + n_riders:]
    w = _Weights(gmix_ref, wqv_ref, wrest_ref, wkg_ref, bias_ref, ghead_ref, wgrp_ref, pscale_ref,
                 wba_ref, wbb_ref, wout_ref)
    nslot = len(_Slot._fields)
    slot_a, slot_b = _Slot(*slot_refs[:nslot]), _Slot(*slot_refs[nslot:])
    kw = dict(tm=tm, seg=tm)
    for src, dst in zip(rider_in, rider_out):
        dst[...] = src[...].astype(BF16)

    row = lax.broadcasted_iota(jnp.int32, (tm, POOL_GROUP), 0)

    @pl.when(i == 0)
    def _():
        _init_scratch(slot_a, slot_b)
        c_out[...] = jnp.zeros(c_out.shape, F32)
        n_s[...] = jnp.zeros(n_s.shape, F32)
        m_s[...] = jnp.zeros(m_s.shape, F32)
        _run("", P=_project(x_ref[0:tm, :], w, slot_a, [jnp.zeros((HIST_ROWS, D_POOL), F32)], row, **kw))

    head_cols = [slice(h * HEAD_DIM, (h + 1) * HEAD_DIM) for h in range(N_HEADS)]
    head_lanes = [slice(h * LANES, (h + 1) * LANES) for h in range(N_HEADS)]

    def c_store(j, h, value):
        c_out[j, h] = value

    def n_store(j, h, value):
        n_s[:, head_cols[h]] = value

    def m_store(j, h, value):
        m_s[:, head_lanes[h]] = jnp.broadcast_to(value, (1, LANES))

    state_io = (c_out, lambda j, h: n_s[:, head_cols[h]], lambda j, h: m_s[:, h * LANES:h * LANES + 1],
                c_store, n_store, m_store)

    def finish_beside(rows, slot, tile, x_proj, slot_proj):
        new_hist = slot.abuf[0, POOL_T0 + tm - HIST_ROWS:POOL_T0 + tm, :]
        _run(_CARRY_ORDER,
             F=_finish(x_ref[rows, :], x1_ref, rows, w, slot, hcat_s, *state_io, **kw),
             P=_project(x_proj, w, slot_proj, [new_hist], row + (tile + 1) * tm, **kw))

    finish_beside(slice(0, tm), slot_a, 2 * i, x_ref[tm:2 * tm, :], slot_b)
    finish_beside(slice(tm, 2 * tm), slot_b, 2 * i + 1, xnext_ref[...], slot_a)
    for h in range(N_HEADS):
        n_out[0, h:h + 1, :] = n_s[:, head_cols[h]]
        m_out[:, h:h + 1] = m_s[:, h * LANES:h * LANES + 1]
    hist_out[0] = slot_b.abuf[0, POOL_T0 + tm - POOL_HIST:POOL_T0 + tm, :]


def _streams_kernel(x_ref, gmix_ref, wqv_ref, wrest_ref, wkg_ref, bias_ref, ghead_ref, wgrp_ref,
                    pscale_ref, wba_ref, wbb_ref, wout_ref, c_in, n_in, m_in, hist_in,
                    x1_ref, c_hbm, n_out, m_out, hist_out,
                    hcat_s, cring_s, cring_sem, *slot_refs, tm, seg, pos0):
    i = pl.program_id(0)
    nseg = tm // seg
    w = _Weights(gmix_ref, wqv_ref, wrest_ref, wkg_ref, bias_ref, ghead_ref, wgrp_ref, pscale_ref,
                 wba_ref, wbb_ref, wout_ref)
    slot = _Slot(*slot_refs)

    @pl.when(i == 0)
    def _():
        _init_scratch(slot)

    x = x_ref[...]
    pos = lax.broadcasted_iota(jnp.int32, (seg, POOL_GROUP), 0) + pos0
    ring = _StateRing(c_hbm, cring_s, cring_sem, first_stream=i * nseg, step=i, last_step=pl.num_programs(0) - 1,
                      stores_per_step=nseg * N_HEADS)

    def n_store(j, h, value):
        n_out[j, h:h + 1, :] = value

    def m_store(j, h, value):
        m_out[j:j + 1, h:h + 1] = value

    _run(_STREAMS_ORDER,
         P=_project(x, w, slot, [hist_in[j] for j in range(nseg)], pos, tm=tm, seg=seg),
         F=_finish(x, x1_ref, slice(0, tm), w, slot, hcat_s, c_in, lambda j, h: n_in[j, h:h + 1, :],
                   lambda j, h: m_in[j:j + 1, h:h + 1], ring.store, n_store, m_store, tm=tm, seg=seg))
    ring.finish()
    for j in range(nseg):
        hist_out[j] = slot.abuf[j, POOL_T0 + seg - POOL_HIST:POOL_T0 + seg, :]


def _ffn_tile(x_ref, y_ref, gffn_ref, wup_ref, wdown_ref, gfin_ref, sub):
    n_sub = x_ref.shape[0] // sub
    rows = [slice(r * sub, (r + 1) * sub) for r in range(n_sub)]
    act = {}

    def up(r):
        u = _rmsnorm(x_ref[rows[r], :], gffn_ref[...]).astype(BF16)
        act[r] = jnp.square(jnp.maximum(_dot(u, wup_ref[...]), 0.0)).astype(BF16)

    def down(r):
        x2 = x_ref[rows[r], :] + _dot(act.pop(r), wdown_ref[...])
        y_ref[rows[r], :] = _rmsnorm(x2, gfin_ref[...])

    up(0)
    for r in range(1, n_sub):
        up(r)
        down(r - 1)
    down(n_sub - 1)


def _ffn_kernel(xs_ref, xp_ref, gffn_ref, wup_ref, wdown_ref, gfin_ref, ys_ref, yp_ref, *, sub, n_s):
    i = pl.program_id(0)

    @pl.when(i < n_s)
    def _():
        _ffn_tile(xs_ref, ys_ref, gffn_ref, wup_ref, wdown_ref, gfin_ref, sub)

    @pl.when(i >= n_s)
    def _():
        _ffn_tile(xp_ref, yp_ref, gffn_ref, wup_ref, wdown_ref, gfin_ref, sub)


def _resident(shape):
    zeros = (0,) * len(shape)
    return pl.BlockSpec(shape, lambda i: zeros, pipeline_mode=pl.Buffered(1))


def _slot_scratch(tm, seg):
    return [
        pltpu.VMEM((tm, D_MODEL), BF16),
        pltpu.VMEM((tm, D_MODEL), BF16),
        pltpu.VMEM((D_MODEL, tm), F32),
        pltpu.VMEM((D_MODEL, tm), BF16),
        pltpu.VMEM((tm, D_MODEL), F32),
        pltpu.VMEM((tm, D_MODEL), F32),
        pltpu.VMEM((tm, D_MODEL), F32),
        pltpu.VMEM((GATE_ROWS, tm), F32),
        pltpu.VMEM((tm // seg, POOL_T0 + seg, D_POOL), F32),
        pltpu.VMEM((tm, D_POOL), BF16),
    ]


def _mixer(x, weights, state, *, tm, seg, carry, pos0, riders=()):
    t = x.shape[0]
    nseg = tm // seg
    n_streams = 1 if carry else t // seg
    sblk = 1 if carry else nseg
    smap = (lambda i: 0) if carry else (lambda i: i)
    bias = jnp.broadcast_to(weights["gate_bias"][:, None], (GATE_ROWS, tm))

    if carry:
        assert pos0 == 0 and nseg == 1 and t % (2 * tm) == 0
        steps = t // (2 * tm)
        last_tile = t // tm - 1
        x_specs = [pl.BlockSpec((2 * tm, D_MODEL), lambda i: (i, 0)),
                   pl.BlockSpec((tm, D_MODEL), lambda i: (jnp.minimum(2 * i + 2, last_tile), 0))]
        x_args = [x, x]
        x1_spec = pl.BlockSpec((2 * tm, D_MODEL), lambda i: (i, 0))
        body = functools.partial(_carry_kernel, tm=tm, n_riders=len(riders))
        slots = [pltpu.VMEM((1, D_MODEL), F32), pltpu.VMEM((1, N_HEADS * LANES), F32)] + 2 * _slot_scratch(tm, seg)
        assert steps % RIDER_HOLD_STEPS == 0 and all(r.shape[0] % (16 * steps) == 0 for r in riders)
    else:
        steps = t // tm
        x_specs = [pl.BlockSpec((tm, D_MODEL), lambda i: (i, 0))]
        x_args = [x]
        x1_spec = pl.BlockSpec((tm, D_MODEL), lambda i: (i, 0))
        body = functools.partial(_streams_kernel, tm=tm, seg=seg, pos0=pos0)
        slots = [pltpu.VMEM((STATE_RING_SLOTS, HEAD_DIM, HEAD_DIM), F32),
                 pltpu.SemaphoreType.DMA((STATE_RING_SLOTS,))] + _slot_scratch(tm, seg)

    state_specs = [
        pl.BlockSpec((sblk, N_HEADS, HEAD_DIM, HEAD_DIM), lambda i: (smap(i), 0, 0, 0)),
        pl.BlockSpec((sblk, N_HEADS, HEAD_DIM), lambda i: (smap(i), 0, 0)),
        pl.BlockSpec((sblk, N_HEADS), lambda i: (smap(i), 0)),
        pl.BlockSpec((sblk, POOL_HIST, D_POOL), lambda i: (smap(i), 0, 0)),
    ]
    in_specs = x_specs + [
        _resident((1, D_MODEL)),
        _resident((D_MODEL, 2 * D_MODEL)),
        _resident((D_MODEL, _REST_ROWS)),
        _resident((D_MODEL + GATE_ROWS, D_MODEL)),
        _resident((GATE_ROWS, tm)),
        _resident((1, D_MODEL)),
        _resident((len(POOL_WINDOWS), POOL_GROUP, POOL_GROUP)),
        _resident((1, D_POOL)),
        _resident((D_MODEL, D_MODEL)),
        _resident((D_POOL, D_MODEL)),
        _resident((D_MODEL, D_MODEL)),
    ] + ([] if carry else state_specs)
    rider_specs = [pl.BlockSpec((r.shape[0] * RIDER_HOLD_STEPS // steps, r.shape[1]),
                                lambda i: (i // RIDER_HOLD_STEPS, 0)) for r in riders]
    out_shape = [
        jax.ShapeDtypeStruct((t, D_MODEL), F32),
        jax.ShapeDtypeStruct((n_streams, N_HEADS, HEAD_DIM, HEAD_DIM), F32),
        jax.ShapeDtypeStruct((n_streams, N_HEADS, HEAD_DIM), F32),
        jax.ShapeDtypeStruct((n_streams, N_HEADS), F32),
        jax.ShapeDtypeStruct((n_streams, POOL_HIST, D_POOL), F32),
    ] + [jax.ShapeDtypeStruct(r.shape, BF16) for r in riders]
    scratch = [pltpu.VMEM((tm, D_MODEL), BF16)] + slots
    return pl.pallas_call(
        body,
        grid=(steps,),
        in_specs=in_specs + rider_specs,
        out_specs=([x1_spec, state_specs[0] if carry else pl.BlockSpec(memory_space=pl.ANY)] + state_specs[1:]
                   + rider_specs),
        out_shape=out_shape,
        scratch_shapes=scratch,
        compiler_params=pltpu.CompilerParams(
            dimension_semantics=("arbitrary",), vmem_limit_bytes=V7X_VMEM_LIMIT_BYTES),
        name="mixer_carry" if carry else "mixer_streams",
    )(*x_args, weights["g_mix"], weights["w_qv"], weights["w_rest"], weights["w_kg"], bias, weights["g_head"],
      weights["w_grp"], weights["pool_scale"], weights["w_ba"], weights["w_bb"], weights["w_out"], *state, *riders)


def _ffn(x_sample, x_prompt, w_up, w_down, weights, *, tm_s, tm_p):
    ts, tp = x_sample.shape[0], x_prompt.shape[0]
    n_s, n_p = ts // tm_s, tp // tm_p
    clamp = lambda v, hi: jnp.clip(v, 0, hi)
    return pl.pallas_call(
        functools.partial(_ffn_kernel, sub=256, n_s=n_s),
        grid=(n_s + n_p,),
        in_specs=[
            pl.BlockSpec((tm_s, D_MODEL), lambda i: (clamp(i, n_s - 1), 0)),
            pl.BlockSpec((tm_p, D_MODEL), lambda i: (clamp(i - n_s, n_p - 1), 0)),
            _resident((1, D_MODEL)),
            _resident((D_MODEL, D_FF)),
            _resident((D_FF, D_MODEL)),
            _resident((1, D_MODEL)),
        ],
        out_specs=[
            pl.BlockSpec((tm_s, D_MODEL), lambda i: (clamp(i, n_s - 1), 0)),
            pl.BlockSpec((tm_p, D_MODEL), lambda i: (clamp(i - n_s, n_p - 1), 0)),
        ],
        out_shape=[jax.ShapeDtypeStruct((ts, D_MODEL), F32), jax.ShapeDtypeStruct((tp, D_MODEL), F32)],
        compiler_params=pltpu.CompilerParams(
            dimension_semantics=("arbitrary",), vmem_limit_bytes=V7X_VMEM_LIMIT_BYTES),
        name="ffn",
    )(x_sample, x_prompt, weights["g_ffn"], w_up, w_down, weights["g_final"])


def _split_w_in_kernel(w_ref, *refs, n_riders):
    i = pl.program_id(0)
    rider_in, (wqv_ref, wkg_ref, wrest_ref) = refs[:n_riders], refs[n_riders:n_riders + 3]
    rider_out, prev_s = refs[n_riders + 3:2 * n_riders + 3], refs[2 * n_riders + 3]
    n_qkv = _QKV_ROWS // _SPLIT_ROWS
    cur = w_ref[...]
    for src, dst in zip(rider_in, rider_out):
        dst[...] = src[...].astype(BF16)

    @pl.when((i == 0) | (i == 2))
    def _():
        wqv_ref[...] = cur.T.astype(BF16)

    @pl.when(i == 1)
    def _():
        wkg_ref[0:D_MODEL, :] = cur.astype(BF16)

    @pl.when(i == n_qkv)
    def _():
        gate = cur[0:2 * N_HEADS, :]
        wkg_ref[D_MODEL:D_MODEL + GATE_ROWS, :] = jnp.concatenate(
            [gate, pltpu.roll(gate, N_HEADS, axis=0)], axis=0).astype(BF16)

    @pl.when(i > n_qkv)
    def _():
        rows = jnp.concatenate([prev_s[2 * N_HEADS:, :], cur[0:2 * N_HEADS, :]], axis=0)
        wrest_ref[...] = rows.T.astype(BF16)

    prev_s[...] = cur


def _split_w_in(w_in_t, riders=()):
    n_in = w_in_t.shape[0]
    assert n_in == _REST0 + _REST_ROWS and _SPLIT_ROWS == D_MODEL and _QKV_ROWS == 3 * _SPLIT_ROWS
    n_qkv, n_rest = _QKV_ROWS // _SPLIT_ROWS, pl.cdiv(_REST_ROWS, _SPLIT_ROWS)
    steps = n_qkv + 1 + n_rest
    last_chunk = pl.cdiv(n_in, _SPLIT_ROWS) - 1
    assert all(r.shape[0] % (16 * steps) == 0 for r in riders)
    rider_specs = [pl.BlockSpec((r.shape[0] // steps, r.shape[1]), lambda i: (i, 0)) for r in riders]
    return pl.pallas_call(
        functools.partial(_split_w_in_kernel, n_riders=len(riders)),
        grid=(steps,),
        in_specs=[pl.BlockSpec((_SPLIT_ROWS, D_MODEL), lambda i: (jnp.minimum(i, last_chunk), 0))] + rider_specs,
        out_specs=[
            pl.BlockSpec((D_MODEL, _SPLIT_ROWS), lambda i: (0, jnp.where(i < 2, 0, 1))),
            pl.BlockSpec((D_MODEL + GATE_ROWS, D_MODEL), lambda i: (0, 0)),
            pl.BlockSpec((D_MODEL, _SPLIT_ROWS), lambda i: (0, jnp.clip(i - n_qkv - 1, 0, n_rest - 1))),
        ] + rider_specs,
        out_shape=[
            jax.ShapeDtypeStruct((D_MODEL, 2 * D_MODEL), BF16),
            jax.ShapeDtypeStruct((D_MODEL + GATE_ROWS, D_MODEL), BF16),
            jax.ShapeDtypeStruct((D_MODEL, _REST_ROWS), BF16),
        ] + [jax.ShapeDtypeStruct(r.shape, BF16) for r in riders],
        scratch_shapes=[pltpu.VMEM((_SPLIT_ROWS, D_MODEL), F32)],
        compiler_params=pltpu.CompilerParams(
            dimension_semantics=("arbitrary",), vmem_limit_bytes=V7X_VMEM_LIMIT_BYTES),
        name="split_w_in",
    )(w_in_t, *riders)


def _pack_weights(w_in, b_igate, b_fgate, g_norm_mix, g_head, w_pool_grp, pool_scale, w_branch_mlstm,
                  w_branch_pool, w_out, g_norm_ffn, w_up, w_down, g_final):
    d = D_MODEL
    w_qv, w_kg, w_rest, w_ba, w_bb, w_o, w_grp = _split_w_in(
        w_in.T, riders=(w_branch_mlstm, w_branch_pool, w_out, w_pool_grp.reshape(-1, POOL_GROUP)))
    return {
        "w_qv": w_qv,
        "w_kg": w_kg,
        "w_rest": w_rest,
        "gate_bias": jnp.concatenate([b_igate, b_fgate, b_fgate, b_igate]).astype(F32),
        "g_mix": g_norm_mix.reshape(1, d),
        "g_head": g_head.reshape(1, d),
        "w_grp": w_grp.reshape(w_pool_grp.shape),
        "pool_scale": pool_scale.reshape(1, D_POOL),
        "w_ba": w_ba,
        "w_bb": w_bb,
        "w_out": w_o,
        "g_ffn": g_norm_ffn.reshape(1, d),
        "g_final": g_final.reshape(1, d),
    }


def kernel(x_prompt, x_sample, state_C, state_n, state_m, state_pool, w_in, b_igate, b_fgate, g_norm_mix, g_head,
           w_pool_grp, pool_scale, w_branch_mlstm, w_branch_pool, w_out, g_norm_ffn, w_up, w_down, g_final):
    depth = w_in.shape[0]
    bp, sp, d = x_prompt.shape
    bs, ss, _ = x_sample.shape
    assert depth == 1 and bp == 1 and d == D_MODEL
    hp = x_prompt.reshape(bp * sp, d)
    hs = x_sample.reshape(bs * ss, d)
    weights = _pack_weights(w_in[0], b_igate[0], b_fgate[0], g_norm_mix[0], g_head[0], w_pool_grp[0], pool_scale[0],
                            w_branch_mlstm[0], w_branch_pool[0], w_out[0], g_norm_ffn[0], w_up[0], w_down[0],
                            g_final)
    hp, cp, np_, mp, pp, w_up_b, w_down_b = _mixer(hp, weights, (), tm=256, seg=256, carry=True, pos0=0,
                                                   riders=(w_up[0], w_down[0]))
    sample_state = (state_C[0], state_n[0], state_m[0], state_pool[0])
    hs, cs, ns, ms, ps = _mixer(hs, weights, sample_state, tm=8 * ss, seg=ss, carry=False, pos0=PAST_LEN)
    y_sample, y_prompt = _ffn(hs, hp, w_up_b, w_down_b, weights, tm_s=512, tm_p=1024)
    y_prompt = y_prompt.reshape(bp, sp, d)
    y_sample = y_sample.reshape(bs, ss, d)
    return (y_prompt, y_sample, cp[None], np_[None], mp[None], pp[None], cs[None], ns[None], ms[None], ps[None])
```

```python
import collections
import functools

import jax
import jax.numpy as jnp
from jax import lax
from jax.experimental import pallas as pl
from jax.experimental.pallas import tpu as pltpu

F32 = jnp.float32
BF16 = jnp.bfloat16

D_MODEL = 1024
N_HEADS = 4
HEAD_DIM = 256
D_POOL = 512
POOL_GROUP = 128
POOL_WINDOWS = (2, 4, 8, 16)
POOL_HIST = max(POOL_WINDOWS) - 1
HIST_ROWS = 16
POOL_T0 = 32
D_FF = 4096
EPS = 1e-6
PAST_LEN = 2048
LANES = 128
GATE_ROWS = 16
RIDER_HOLD_STEPS = 2
STATE_RING_SLOTS = 16
V7X_VMEM_LIMIT_BYTES = 58 * 1024 * 1024

_K0, _V0, _QKV_ROWS = 1024, 2048, 3072
_REST0 = _QKV_ROWS + 2 * N_HEADS
_P0, _GA0, _GB0, _REST_ROWS = 1024, 1536, 2560, 3584
_SPLIT_ROWS = 1024

_Slot = collections.namedtuple("_Slot", "q v kt ktb og ga gb gate abuf pd")
_Weights = collections.namedtuple(
    "_Weights", "gmix wqv wrest wkg bias ghead wgrp pscale wba wbb wout")
_CARRY_ORDER = "FPFFFFPPFPPFPFP"
_STREAMS_ORDER = "PPPPFPFPFPFFFFF"


def _dot(a, b):
    return jnp.dot(a, b, preferred_element_type=F32)


def _dot_nt(a, b):
    return lax.dot_general(a, b, (((1,), (1,)), ((), ())), preferred_element_type=F32)


def _rmsnorm(x, g):
    return x * lax.rsqrt(jnp.mean(x * x, axis=-1, keepdims=True) + EPS) * g


def _sigmoid(x):
    return 1.0 / (1.0 + jnp.exp(-x))


def _log_sigmoid(x):
    return jnp.minimum(x, 0.0) - jnp.log1p(jnp.exp(-jnp.abs(x)))


def _masks(tm, seg):
    row = lax.broadcasted_iota(jnp.int32, (tm, tm), 0)
    col = lax.broadcasted_iota(jnp.int32, (tm, tm), 1)
    if tm == seg:
        return col <= row, row <= col
    shift = seg.bit_length() - 1
    same = (row >> shift) == (col >> shift)
    return (col <= row) & same, (row <= col) & same


def _init_scratch(*slots):
    for slot in slots:
        slot.abuf[:, 0:POOL_T0, :] = jnp.zeros((slot.abuf.shape[0], POOL_T0, D_POOL), F32)


def _project(x, w, slot, hist, pos, *, tm, seg):
    nseg = tm // seg
    u = _rmsnorm(x, w.gmix[...]).astype(BF16)
    kg = _dot_nt(w.wkg[...], u)
    kt = kg[0:D_MODEL] * (HEAD_DIM ** -0.5)
    slot.kt[...] = kt
    slot.ktb[...] = kt.astype(BF16)
    gates = kg[D_MODEL:D_MODEL + GATE_ROWS] + w.bias[...]
    yield
    p = _dot(u, w.wrest[:, _P0:_GA0])
    for j in range(nseg):
        slot.abuf[j, POOL_T0 - hist[j].shape[0]:POOL_T0, :] = hist[j]
        slot.abuf[j, POOL_T0:POOL_T0 + seg, :] = p[j * seg:(j + 1) * seg]
    for gi, win in enumerate(POOL_WINDOWS):
        ls = slice(gi * POOL_GROUP, (gi + 1) * POOL_GROUP)
        cnt = jnp.minimum(pos + 1, win).astype(F32)
        for j in range(nseg):
            start = POOL_T0 - 8 * gi
            tot = slot.abuf[j, start:POOL_T0 + seg, ls] + slot.abuf[j, start - 1:POOL_T0 + seg - 1, ls]
            shift = 2
            while shift < win:
                n = tot.shape[0]
                tot = tot[8:n] + tot[8 - shift:n - shift]
                shift *= 2
            tok = slot.abuf[j, POOL_T0:POOL_T0 + seg, ls]
            slot.pd[j * seg:(j + 1) * seg, ls] = (tot / cnt - tok).astype(BF16)
    yield
    slot.q[...] = _dot(u, w.wqv[:, 0:D_MODEL]).astype(BF16)
    _, upper = _masks(tm, seg)
    lf = _log_sigmoid(gates[8:16])
    hi = lf.astype(BF16).astype(F32)
    r1 = lf - hi
    mid = r1.astype(BF16).astype(F32)
    lo = r1 - mid
    pieces = jnp.concatenate([hi, mid, lo, jnp.zeros_like(lo)], axis=0).astype(BF16)
    cs = _dot(pieces, jnp.where(upper, 1.0, 0.0).astype(BF16))
    b = cs[0:8] + cs[8:16] + cs[16:24]
    slot.gate[0:8, :] = gates[0:8] - b
    slot.gate[8:16, :] = lf
    yield
    slot.v[...] = _dot(u, w.wqv[:, D_MODEL:2 * D_MODEL]).astype(BF16)
    yield
    slot.og[...] = _dot(u, w.wrest[:, 0:_P0])
    yield
    slot.ga[...] = _dot(u, w.wrest[:, _GA0:_GB0])
    yield
    slot.gb[...] = _dot(u, w.wrest[:, _GB0:_REST_ROWS])
    yield


def _finish(x, out_ref, rows, w, slot, hcat_s, c_rd, n_rd, m_rd, c_store, n_store, m_store, *, tm, seg):
    nseg = tm // seg
    causal, _ = _masks(tm, seg)

    heads = []
    for h in range(N_HEADS):
        sl = slice(h * HEAD_DIM, (h + 1) * HEAD_DIM)
        a_m = jnp.where(causal, jnp.broadcast_to(slot.gate[h:h + 1, :], (tm, tm)), -jnp.inf)
        mprev = jnp.concatenate(
            [jnp.broadcast_to(m_rd(j, h), (seg, 1)) for j in range(nseg)], axis=0)
        g = jnp.maximum(jnp.max(a_m, axis=-1, keepdims=True), mprev)
        dm = jnp.exp(a_m - g)
        lf_b = jnp.broadcast_to(slot.gate[8 + h:9 + h, :], (tm, tm))
        bcol = jnp.sum(jnp.where(causal, lf_b, 0.0), axis=-1, keepdims=True)
        mcol = bcol + g
        wi = jnp.exp(mprev - g)

        qh = slot.q[:, sl]
        vh = slot.v[:, sl]
        s = _dot(qh, slot.ktb[sl, :]) * dm
        qf = qh.astype(F32)
        qc_parts, qn_parts = [], []
        for j in range(nseg):
            rs = slice(j * seg, (j + 1) * seg)
            qc_parts.append(_dot(qh[rs], c_rd[j, h].astype(BF16)))
            qn_parts.append(jnp.sum(qf[rs] * n_rd(j, h), axis=-1, keepdims=True))
        qc = jnp.concatenate(qc_parts, axis=0)
        qn = jnp.concatenate(qn_parts, axis=0)
        heads.append((sl, dm, mcol, wi, vh, s, qc, qn))
    yield

    for h, (sl, dm, mcol, wi, vh, s, qc, qn) in enumerate(heads):
        kth = slot.kt[sl, :]
        rowsum = jnp.sum(s, axis=-1, keepdims=True)
        sv = _dot(s.astype(BF16), vh)
        num = wi * qc + sv
        den = wi * qn + rowsum
        hh = num / jnp.maximum(jnp.abs(den), jnp.exp(-mcol))
        hh = hh * lax.rsqrt(jnp.mean(hh * hh, axis=-1, keepdims=True) + EPS) * w.ghead[:, sl]
        hh = hh * _sigmoid(slot.og[:, sl])
        hcat_s[:, sl] = hh.astype(BF16)

        wrows, n_decayed = [], []
        for j in range(nseg):
            r = (j + 1) * seg - 1
            w_row = dm[r:r + 1, :]
            wrows.append(w_row)
            decay = wi[r:r + 1, :]
            kw = (kth * w_row).astype(BF16)
            c_store(j, h, decay * c_rd[j, h] + _dot(kw, vh))
            n_decayed.append(decay * n_rd(j, h))
            m_store(j, h, mcol[r:r + 1, :])
        ridx = lax.broadcasted_iota(jnp.int32, (GATE_ROWS, tm), 0)
        wmat = jnp.zeros((GATE_ROWS, tm), F32)
        for j in range(nseg):
            wmat = jnp.where(ridx == j, jnp.broadcast_to(wrows[j], (GATE_ROWS, tm)), wmat)
        nupd = _dot_nt(wmat.astype(BF16), slot.ktb[sl, :])
        for j in range(nseg):
            n_store(j, h, n_decayed[j] + nupd[j:j + 1, :])
        yield

    grp_out = [_dot(slot.pd[:, gi * POOL_GROUP:(gi + 1) * POOL_GROUP], w.wgrp[gi])
               for gi in range(len(POOL_WINDOWS))]
    pooled = jnp.concatenate(grp_out, axis=-1) * w.pscale[...]
    yield

    branch_a = _dot(hcat_s[...], w.wba[...])
    branch_b = _dot(pooled.astype(BF16), w.wbb[...])
    yield
    mixed = _sigmoid(slot.ga[...]) * branch_a + _sigmoid(slot.gb[...]) * branch_b
    out_ref[rows, :] = x + _dot(mixed.astype(BF16), w.wout[...])
    yield


def _run(order, **gens):
    for name in order:
        next(gens[name], None)
    for gen in gens.values():
        for _ in gen:
            pass


class _StateRing:
    def __init__(self, dst_hbm, ring, sems, first_stream, step, last_step, stores_per_step):
        self.dst, self.ring, self.sems, self.first = dst_hbm, ring, sems, first_stream
        self.step, self.last_step = step, last_step
        self.bank = ring.shape[0] // 2
        assert stores_per_step % ring.shape[0] == 0
        self.count = 0

    def _copy(self, slot, stream, h):
        return pltpu.make_async_copy(self.ring.at[slot], self.dst.at[stream, h], self.sems.at[slot])

    def _wait_bank(self, bank):
        for slot in range(bank * self.bank, (bank + 1) * self.bank):
            self._copy(slot, 0, 0).wait()

    def store(self, j, h, value):
        slot = self.count % self.ring.shape[0]
        if slot % self.bank == 0:
            bank = slot // self.bank
            if self.count < self.ring.shape[0]:
                pl.when(self.step > 0)(functools.partial(self._wait_bank, bank))
            else:
                self._wait_bank(bank)
        self.count += 1
        self.ring[slot] = value
        self._copy(slot, self.first + j, h).start()

    def finish(self):
        @pl.when(self.step == self.last_step)
        def _():
            self._wait_bank(0)
            self._wait_bank(1)


def _carry_kernel(x_ref, xnext_ref, gmix_ref, wqv_ref, wrest_ref, wkg_ref, bias_ref, ghead_ref, wgrp_ref,
                  pscale_ref, wba_ref, wbb_ref, wout_ref, *rest, tm, n_riders):
    i = pl.program_id(0)
    rider_in, rest = rest[:n_riders], rest[n_riders:]
    x1_ref, c_out, n_out, m_out, hist_out = rest[:5]
    rider_out, (hcat_s, n_s, m_s, *slot_refs) = rest[5:5 + n_riders], rest[5 + n_riders:]
    w = _Weights(gmix_ref, wqv_ref, wrest_ref, wkg_ref, bias_ref, ghead_ref, wgrp_ref, pscale_ref,
                 wba_ref, wbb_ref, wout_ref)
    nslot = len(_Slot._fields)
    slot_a, slot_b = _Slot(*slot_refs[:nslot]), _Slot(*slot_refs[nslot:])
    kw = dict(tm=tm, seg=tm)
    for src, dst in zip(rider_in, rider_out):
        dst[...] = src[...].astype(BF16)

    row = lax.broadcasted_iota(jnp.int32, (tm, POOL_GROUP), 0)

    @pl.when(i == 0)
    def _():
        _init_scratch(slot_a, slot_b)
        c_out[...] = jnp.zeros(c_out.shape, F32)
        n_s[...] = jnp.zeros(n_s.shape, F32)
        m_s[...] = jnp.zeros(m_s.shape, F32)
        _run("", P=_project(x_ref[0:tm, :], w, slot_a, [jnp.zeros((HIST_ROWS, D_POOL), F32)], row, **kw))

    head_cols = [slice(h * HEAD_DIM, (h + 1) * HEAD_DIM) for h in range(N_HEADS)]
    head_lanes = [slice(h * LANES, (h + 1) * LANES) for h in range(N_HEADS)]

    def c_store(j, h, value):
        c_out[j, h] = value

    def n_store(j, h, value):
        n_s[:, head_cols[h]] = value

    def m_store(j, h, value):
        m_s[:, head_lanes[h]] = jnp.broadcast_to(value, (1, LANES))

    state_io = (c_out, lambda j, h: n_s[:, head_cols[h]], lambda j, h: m_s[:, h * LANES:h * LANES + 1],
                c_store, n_store, m_store)

    def finish_beside(rows, slot, tile, x_proj, slot_proj):
        new_hist = slot.abuf[0, POOL_T0 + tm - HIST_ROWS:POOL_T0 + tm, :]
        _run(_CARRY_ORDER,
             F=_finish(x_ref[rows, :], x1_ref, rows, w, slot, hcat_s, *state_io, **kw),
             P=_project(x_proj, w, slot_proj, [new_hist], row + (tile + 1) * tm, **kw))

    finish_beside(slice(0, tm), slot_a, 2 * i, x_ref[tm:2 * tm, :], slot_b)
    finish_beside(slice(tm, 2 * tm), slot_b, 2 * i + 1, xnext_ref[...], slot_a)
    for h in range(N_HEADS):
        n_out[0, h:h + 1, :] = n_s[:, head_cols[h]]
        m_out[:, h:h + 1] = m_s[:, h * LANES:h * LANES + 1]
    hist_out[0] = slot_b.abuf[0, POOL_T0 + tm - POOL_HIST:POOL_T0 + tm, :]


def _streams_kernel(x_ref, gmix_ref, wqv_ref, wrest_ref, wkg_ref, bias_ref, ghead_ref, wgrp_ref,
                    pscale_ref, wba_ref, wbb_ref, wout_ref, c_in, n_in, m_in, hist_in,
                    x1_ref, c_hbm, n_out, m_out, hist_out,
                    hcat_s, cring_s, cring_sem, *slot_refs, tm, seg, pos0):
    i = pl.program_id(0)
    nseg = tm // seg
    w = _Weights(gmix_ref, wqv_ref, wrest_ref, wkg_ref, bias_ref, ghead_ref, wgrp_ref, pscale_ref,
                 wba_ref, wbb_ref, wout_ref)
    slot = _Slot(*slot_refs)

    @pl.when(i == 0)
    def _():
        _init_scratch(slot)

    x = x_ref[...]
    pos = lax.broadcasted_iota(jnp.int32, (seg, POOL_GROUP), 0) + pos0
    ring = _StateRing(c_hbm, cring_s, cring_sem, first_stream=i * nseg, step=i, last_step=pl.num_programs(0) - 1,
                      stores_per_step=nseg * N_HEADS)

    def n_store(j, h, value):
        n_out[j, h:h + 1, :] = value

    def m_store(j, h, value):
        m_out[j:j + 1, h:h + 1] = value

    _run(_STREAMS_ORDER,
         P=_project(x, w, slot, [hist_in[:, j, :] for j in range(nseg)], pos, tm=tm, seg=seg),
         F=_finish(x, x1_ref, slice(0, tm), w, slot, hcat_s, c_in, lambda j, h: n_in[j, h:h + 1, :],
                   lambda j, h: m_in[j:j + 1, h:h + 1], ring.store, n_store, m_store, tm=tm, seg=seg))
    ring.finish()
    for j in range(nseg):
        hist_out[:, j, :] = slot.abuf[j, POOL_T0 + seg - POOL_HIST:POOL_T0 + seg, :]


def _ffn_tile(x_ref, y_ref, gffn_ref, wup_ref, wdown_ref, gfin_ref, sub):
    n_sub = x_ref.shape[0] // sub
    rows = [slice(r * sub, (r + 1) * sub) for r in range(n_sub)]
    act = {}

    def up(r):
        u = _rmsnorm(x_ref[rows[r], :], gffn_ref[...]).astype(BF16)
        act[r] = jnp.square(jnp.maximum(_dot(u, wup_ref[...]), 0.0)).astype(BF16)

    def down(r):
        x2 = x_ref[rows[r], :] + _dot(act.pop(r), wdown_ref[...])
        y_ref[rows[r], :] = _rmsnorm(x2, gfin_ref[...])

    up(0)
    for r in range(1, n_sub):
        up(r)
        down(r - 1)
    down(n_sub - 1)


def _ffn_kernel(xs_ref, xp_ref, gffn_ref, wup_ref, wdown_ref, gfin_ref, ys_ref, yp_ref, *, sub, n_s):
    i = pl.program_id(0)

    @pl.when(i < n_s)
    def _():
        _ffn_tile(xs_ref, ys_ref, gffn_ref, wup_ref, wdown_ref, gfin_ref, sub)

    @pl.when(i >= n_s)
    def _():
        _ffn_tile(xp_ref, yp_ref, gffn_ref, wup_ref, wdown_ref, gfin_ref, sub)


def _resident(shape):
    zeros = (0,) * len(shape)
    return pl.BlockSpec(shape, lambda i: zeros, pipeline_mode=pl.Buffered(1))


def _slot_scratch(tm, seg):
    return [
        pltpu.VMEM((tm, D_MODEL), BF16),
        pltpu.VMEM((tm, D_MODEL), BF16),
        pltpu.VMEM((D_MODEL, tm), F32),
        pltpu.VMEM((D_MODEL, tm), BF16),
        pltpu.VMEM((tm, D_MODEL), F32),
        pltpu.VMEM((tm, D_MODEL), F32),
        pltpu.VMEM((tm, D_MODEL), F32),
        pltpu.VMEM((GATE_ROWS, tm), F32),
        pltpu.VMEM((tm // seg, POOL_T0 + seg, D_POOL), F32),
        pltpu.VMEM((tm, D_POOL), BF16),
    ]


def _mixer(x, weights, state, *, tm, seg, carry, pos0, riders=()):
    t = x.shape[0]
    nseg = tm // seg
    n_streams = 1 if carry else t // seg
    sblk = 1 if carry else nseg
    smap = (lambda i: 0) if carry else (lambda i: i)
    bias = jnp.broadcast_to(weights["gate_bias"][:, None], (GATE_ROWS, tm))

    if carry:
        assert pos0 == 0 and nseg == 1 and t % (2 * tm) == 0
        steps = t // (2 * tm)
        last_tile = t // tm - 1
        x_specs = [pl.BlockSpec((2 * tm, D_MODEL), lambda i: (i, 0)),
                   pl.BlockSpec((tm, D_MODEL), lambda i: (jnp.minimum(2 * i + 2, last_tile), 0))]
        x_args = [x, x]
        x1_spec = pl.BlockSpec((2 * tm, D_MODEL), lambda i: (i, 0))
        body = functools.partial(_carry_kernel, tm=tm, n_riders=len(riders))
        slots = [pltpu.VMEM((1, D_MODEL), F32), pltpu.VMEM((1, N_HEADS * LANES), F32)] + 2 * _slot_scratch(tm, seg)
        assert steps % RIDER_HOLD_STEPS == 0 and all(r.shape[0] % (16 * steps) == 0 for r in riders)
    else:
        steps = t // tm
        x_specs = [pl.BlockSpec((tm, D_MODEL), lambda i: (i, 0))]
        x_args = [x]
        x1_spec = pl.BlockSpec((tm, D_MODEL), lambda i: (i, 0))
        body = functools.partial(_streams_kernel, tm=tm, seg=seg, pos0=pos0)
        slots = [pltpu.VMEM((STATE_RING_SLOTS, HEAD_DIM, HEAD_DIM), F32),
                 pltpu.SemaphoreType.DMA((STATE_RING_SLOTS,))] + _slot_scratch(tm, seg)

    state_specs = [
        pl.BlockSpec((sblk, N_HEADS, HEAD_DIM, HEAD_DIM), lambda i: (smap(i), 0, 0, 0)),
        pl.BlockSpec((sblk, N_HEADS, HEAD_DIM), lambda i: (smap(i), 0, 0)),
        pl.BlockSpec((sblk, N_HEADS), lambda i: (smap(i), 0)),
        pl.BlockSpec((1, POOL_HIST, D_POOL), lambda i: (0, 0, 0)) if carry else
        pl.BlockSpec((POOL_HIST, sblk, D_POOL), lambda i: (0, i, 0)),
    ]
    in_specs = x_specs + [
        _resident((1, D_MODEL)),
        _resident((D_MODEL, 2 * D_MODEL)),
        _resident((D_MODEL, _REST_ROWS)),
        _resident((D_MODEL + GATE_ROWS, D_MODEL)),
        _resident((GATE_ROWS, tm)),
        _resident((1, D_MODEL)),
        _resident((len(POOL_WINDOWS), POOL_GROUP, POOL_GROUP)),
        _resident((1, D_POOL)),
        _resident((D_MODEL, D_MODEL)),
        _resident((D_POOL, D_MODEL)),
        _resident((D_MODEL, D_MODEL)),
    ] + ([] if carry else state_specs)
    rider_specs = [pl.BlockSpec((r.shape[0] * RIDER_HOLD_STEPS // steps, r.shape[1]),
                                lambda i: (i // RIDER_HOLD_STEPS, 0)) for r in riders]
    out_shape = [
        jax.ShapeDtypeStruct((t, D_MODEL), F32),
        jax.ShapeDtypeStruct((n_streams, N_HEADS, HEAD_DIM, HEAD_DIM), F32),
        jax.ShapeDtypeStruct((n_streams, N_HEADS, HEAD_DIM), F32),
        jax.ShapeDtypeStruct((n_streams, N_HEADS), F32),
        jax.ShapeDtypeStruct((1, POOL_HIST, D_POOL) if carry else (POOL_HIST, n_streams, D_POOL), F32),
    ] + [jax.ShapeDtypeStruct(r.shape, BF16) for r in riders]
    scratch = [pltpu.VMEM((tm, D_MODEL), BF16)] + slots
    return pl.pallas_call(
        body,
        grid=(steps,),
        in_specs=in_specs + rider_specs,
        out_specs=([x1_spec, state_specs[0] if carry else pl.BlockSpec(memory_space=pl.ANY)] + state_specs[1:]
                   + rider_specs),
        out_shape=out_shape,
        scratch_shapes=scratch,
        compiler_params=pltpu.CompilerParams(
            dimension_semantics=("arbitrary",), vmem_limit_bytes=V7X_VMEM_LIMIT_BYTES),
        name="mixer_carry" if carry else "mixer_streams",
    )(*x_args, weights["g_mix"], weights["w_qv"], weights["w_rest"], weights["w_kg"], bias, weights["g_head"],
      weights["w_grp"], weights["pool_scale"], weights["w_ba"], weights["w_bb"], weights["w_out"], *state, *riders)


def _ffn(x_sample, x_prompt, w_up, w_down, weights, *, tm_s, tm_p):
    ts, tp = x_sample.shape[0], x_prompt.shape[0]
    n_s, n_p = ts // tm_s, tp // tm_p
    clamp = lambda v, hi: jnp.clip(v, 0, hi)
    return pl.pallas_call(
        functools.partial(_ffn_kernel, sub=256, n_s=n_s),
        grid=(n_s + n_p,),
        in_specs=[
            pl.BlockSpec((tm_s, D_MODEL), lambda i: (clamp(i, n_s - 1), 0)),
            pl.BlockSpec((tm_p, D_MODEL), lambda i: (clamp(i - n_s, n_p - 1), 0)),
            _resident((1, D_MODEL)),
            _resident((D_MODEL, D_FF)),
            _resident((D_FF, D_MODEL)),
            _resident((1, D_MODEL)),
        ],
        out_specs=[
            pl.BlockSpec((tm_s, D_MODEL), lambda i: (clamp(i, n_s - 1), 0)),
            pl.BlockSpec((tm_p, D_MODEL), lambda i: (clamp(i - n_s, n_p - 1), 0)),
        ],
        out_shape=[jax.ShapeDtypeStruct((ts, D_MODEL), F32), jax.ShapeDtypeStruct((tp, D_MODEL), F32)],
        compiler_params=pltpu.CompilerParams(
            dimension_semantics=("arbitrary",), vmem_limit_bytes=V7X_VMEM_LIMIT_BYTES),
        name="ffn",
    )(x_sample, x_prompt, weights["g_ffn"], w_up, w_down, weights["g_final"])


def _split_w_in_kernel(w_ref, *refs, n_riders):
    i = pl.program_id(0)
    rider_in, (wqv_ref, wkg_ref, wrest_ref) = refs[:n_riders], refs[n_riders:n_riders + 3]
    rider_out, prev_s = refs[n_riders + 3:2 * n_riders + 3], refs[2 * n_riders + 3]
    n_qkv = _QKV_ROWS // _SPLIT_ROWS
    cur = w_ref[...]
    for src, dst in zip(rider_in, rider_out):
        dst[...] = src[...].astype(BF16)

    @pl.when((i == 0) | (i == 2))
    def _():
        wqv_ref[...] = cur.T.astype(BF16)

    @pl.when(i == 1)
    def _():
        wkg_ref[0:D_MODEL, :] = cur.astype(BF16)

    @pl.when(i == n_qkv)
    def _():
        gate = cur[0:2 * N_HEADS, :]
        wkg_ref[D_MODEL:D_MODEL + GATE_ROWS, :] = jnp.concatenate(
            [gate, pltpu.roll(gate, N_HEADS, axis=0)], axis=0).astype(BF16)

    @pl.when(i > n_qkv)
    def _():
        rows = jnp.concatenate([prev_s[2 * N_HEADS:, :], cur[0:2 * N_HEADS, :]], axis=0)
        wrest_ref[...] = rows.T.astype(BF16)

    prev_s[...] = cur


def _split_w_in(w_in_t, riders=()):
    n_in = w_in_t.shape[0]
    assert n_in == _REST0 + _REST_ROWS and _SPLIT_ROWS == D_MODEL and _QKV_ROWS == 3 * _SPLIT_ROWS
    n_qkv, n_rest = _QKV_ROWS // _SPLIT_ROWS, pl.cdiv(_REST_ROWS, _SPLIT_ROWS)
    steps = n_qkv + 1 + n_rest
    last_chunk = pl.cdiv(n_in, _SPLIT_ROWS) - 1
    assert all(r.shape[0] % (16 * steps) == 0 for r in riders)
    rider_specs = [pl.BlockSpec((r.shape[0] // steps, r.shape[1]), lambda i: (i, 0)) for r in riders]
    return pl.pallas_call(
        functools.partial(_split_w_in_kernel, n_riders=len(riders)),
        grid=(steps,),
        in_specs=[pl.BlockSpec((_SPLIT_ROWS, D_MODEL), lambda i: (jnp.minimum(i, last_chunk), 0))] + rider_specs,
        out_specs=[
            pl.BlockSpec((D_MODEL, _SPLIT_ROWS), lambda i: (0, jnp.where(i < 2, 0, 1))),
            pl.BlockSpec((D_MODEL + GATE_ROWS, D_MODEL), lambda i: (0, 0)),
            pl.BlockSpec((D_MODEL, _SPLIT_ROWS), lambda i: (0, jnp.clip(i - n_qkv - 1, 0, n_rest - 1))),
        ] + rider_specs,
        out_shape=[
            jax.ShapeDtypeStruct((D_MODEL, 2 * D_MODEL), BF16),
            jax.ShapeDtypeStruct((D_MODEL + GATE_ROWS, D_MODEL), BF16),
            jax.ShapeDtypeStruct((D_MODEL, _REST_ROWS), BF16),
        ] + [jax.ShapeDtypeStruct(r.shape, BF16) for r in riders],
        scratch_shapes=[pltpu.VMEM((_SPLIT_ROWS, D_MODEL), F32)],
        compiler_params=pltpu.CompilerParams(
            dimension_semantics=("arbitrary",), vmem_limit_bytes=V7X_VMEM_LIMIT_BYTES),
        name="split_w_in",
    )(w_in_t, *riders)


def _pack_weights(w_in, b_igate, b_fgate, g_norm_mix, g_head, w_pool_grp, pool_scale, w_branch_mlstm,
                  w_branch_pool, w_out, g_norm_ffn, w_up, w_down, g_final):
    d = D_MODEL
    w_qv, w_kg, w_rest, w_ba, w_bb, w_o, w_grp = _split_w_in(
        w_in.T, riders=(w_branch_mlstm, w_branch_pool, w_out, w_pool_grp.reshape(-1, POOL_GROUP)))
    return {
        "w_qv": w_qv,
        "w_kg": w_kg,
        "w_rest": w_rest,
        "gate_bias": jnp.concatenate([b_igate, b_fgate, b_fgate, b_igate]).astype(F32),
        "g_mix": g_norm_mix.reshape(1, d),
        "g_head": g_head.reshape(1, d),
        "w_grp": w_grp.reshape(w_pool_grp.shape),
        "pool_scale": pool_scale.reshape(1, D_POOL),
        "w_ba": w_ba,
        "w_bb": w_bb,
        "w_out": w_o,
        "g_ffn": g_norm_ffn.reshape(1, d),
        "g_final": g_final.reshape(1, d),
    }


def kernel(x_prompt, x_sample, state_C, state_n, state_m, state_pool, w_in, b_igate, b_fgate, g_norm_mix, g_head,
           w_pool_grp, pool_scale, w_branch_mlstm, w_branch_pool, w_out, g_norm_ffn, w_up, w_down, g_final):
    depth = w_in.shape[0]
    bp, sp, d = x_prompt.shape
    bs, ss, _ = x_sample.shape
    assert depth == 1 and bp == 1 and d == D_MODEL
    hp = x_prompt.reshape(bp * sp, d)
    hs = x_sample.reshape(bs * ss, d)
    weights = _pack_weights(w_in[0], b_igate[0], b_fgate[0], g_norm_mix[0], g_head[0], w_pool_grp[0], pool_scale[0],
                            w_branch_mlstm[0], w_branch_pool[0], w_out[0], g_norm_ffn[0], w_up[0], w_down[0],
                            g_final)
    hp, cp, np_, mp, pp, w_up_b, w_down_b = _mixer(hp, weights, (), tm=256, seg=256, carry=True, pos0=0,
                                                   riders=(w_up[0], w_down[0]))
    sample_state = (state_C[0], state_n[0], state_m[0], jnp.swapaxes(state_pool[0], 0, 1))
    hs, cs, ns, ms, ps = _mixer(hs, weights, sample_state, tm=8 * ss, seg=ss, carry=False, pos0=PAST_LEN)
    y_sample, y_prompt = _ffn(hs, hp, w_up_b, w_down_b, weights, tm_s=512, tm_p=1024)
    y_prompt = y_prompt.reshape(bp, sp, d)
    y_sample = y_sample.reshape(bs, ss, d)
    return (y_prompt, y_sample, cp[None], np_[None], mp[None], pp[None], cs[None], ns[None], ms[None],
            jnp.swapaxes(ps, 0, 1)[None])
```

```python
import collections
import functools

import jax
import jax.numpy as jnp
from jax import lax
from jax.experimental import pallas as pl
from jax.experimental.pallas import tpu as pltpu

F32 = jnp.float32
BF16 = jnp.bfloat16

D_MODEL = 1024
N_HEADS = 4
HEAD_DIM = 256
D_POOL = 512
POOL_GROUP = 128
POOL_WINDOWS = (2, 4, 8, 16)
POOL_HIST = max(POOL_WINDOWS) - 1
HIST_ROWS = 16
POOL_T0 = 32
D_FF = 4096
EPS = 1e-6
PAST_LEN = 2048
LANES = 128
GATE_ROWS = 16
RIDER_HOLD_STEPS = 2
STATE_RING_SLOTS = 16
V7X_VMEM_LIMIT_BYTES = 58 * 1024 * 1024

_K0, _V0, _QKV_ROWS = 1024, 2048, 3072
_REST0 = _QKV_ROWS + 2 * N_HEADS
_P0, _GA0, _GB0, _REST_ROWS = 1024, 1536, 2560, 3584
_SPLIT_ROWS = 1024

_Slot = collections.namedtuple("_Slot", "q v kt ktb og ga gb gate abuf pd")
_Weights = collections.namedtuple(
    "_Weights", "gmix wqv wrest wkg bias ghead wgrp pscale wba wbb wout")
_CARRY_ORDER = "FPFFFFPPFPPFPFP"
_STREAMS_ORDER = "PPPPFPFPFPFFFFF"


def _dot(a, b):
    return jnp.dot(a, b, preferred_element_type=F32)


def _dot_nt(a, b):
    return lax.dot_general(a, b, (((1,), (1,)), ((), ())), preferred_element_type=F32)


def _rmsnorm(x, g):
    return x * lax.rsqrt(jnp.mean(x * x, axis=-1, keepdims=True) + EPS) * g


def _sigmoid(x):
    return 1.0 / (1.0 + jnp.exp(-x))


def _log_sigmoid(x):
    return jnp.minimum(x, 0.0) - jnp.log1p(jnp.exp(-jnp.abs(x)))


def _masks(tm, seg):
    row = lax.broadcasted_iota(jnp.int32, (tm, tm), 0)
    col = lax.broadcasted_iota(jnp.int32, (tm, tm), 1)
    if tm == seg:
        return col <= row, row <= col
    shift = seg.bit_length() - 1
    same = (row >> shift) == (col >> shift)
    return (col <= row) & same, (row <= col) & same


def _init_scratch(*slots):
    for slot in slots:
        slot.abuf[:, 0:POOL_T0, :] = jnp.zeros((slot.abuf.shape[0], POOL_T0, D_POOL), F32)


def _project(x, w, slot, hist, pos, *, tm, seg):
    nseg = tm // seg
    u = _rmsnorm(x, w.gmix[...]).astype(BF16)
    kg = _dot_nt(w.wkg[...], u)
    kt = kg[0:D_MODEL] * (HEAD_DIM ** -0.5)
    slot.kt[...] = kt
    slot.ktb[...] = kt.astype(BF16)
    gates = kg[D_MODEL:D_MODEL + GATE_ROWS] + w.bias[...]
    yield
    p = _dot(u, w.wrest[:, _P0:_GA0])
    for j in range(nseg):
        slot.abuf[j, POOL_T0 - hist[j].shape[0]:POOL_T0, :] = hist[j]
        slot.abuf[j, POOL_T0:POOL_T0 + seg, :] = p[j * seg:(j + 1) * seg]
    for gi, win in enumerate(POOL_WINDOWS):
        ls = slice(gi * POOL_GROUP, (gi + 1) * POOL_GROUP)
        cnt = jnp.minimum(pos + 1, win).astype(F32)
        for j in range(nseg):
            start = POOL_T0 - 8 * gi
            tot = slot.abuf[j, start:POOL_T0 + seg, ls] + slot.abuf[j, start - 1:POOL_T0 + seg - 1, ls]
            shift = 2
            while shift < win:
                n = tot.shape[0]
                tot = tot[8:n] + tot[8 - shift:n - shift]
                shift *= 2
            tok = slot.abuf[j, POOL_T0:POOL_T0 + seg, ls]
            slot.pd[j * seg:(j + 1) * seg, ls] = (tot / cnt - tok).astype(BF16)
    yield
    slot.q[...] = _dot(u, w.wqv[:, 0:D_MODEL]).astype(BF16)
    _, upper = _masks(tm, seg)
    lf = _log_sigmoid(gates[8:16])
    hi = lf.astype(BF16).astype(F32)
    r1 = lf - hi
    mid = r1.astype(BF16).astype(F32)
    lo = r1 - mid
    pieces = jnp.concatenate([hi, mid, lo, jnp.zeros_like(lo)], axis=0).astype(BF16)
    cs = _dot(pieces, jnp.where(upper, 1.0, 0.0).astype(BF16))
    b = cs[0:8] + cs[8:16] + cs[16:24]
    slot.gate[0:8, :] = gates[0:8] - b
    slot.gate[8:16, :] = lf
    yield
    slot.v[...] = _dot(u, w.wqv[:, D_MODEL:2 * D_MODEL]).astype(BF16)
    yield
    slot.og[...] = _dot(u, w.wrest[:, 0:_P0])
    yield
    slot.ga[...] = _dot(u, w.wrest[:, _GA0:_GB0])
    yield
    slot.gb[...] = _dot(u, w.wrest[:, _GB0:_REST_ROWS])
    yield


def _finish(x, out_ref, rows, w, slot, hcat_s, c_rd, n_rd, m_rd, c_store, n_store, m_store, *, tm, seg):
    nseg = tm // seg
    causal, _ = _masks(tm, seg)

    heads = []
    for h in range(N_HEADS):
        sl = slice(h * HEAD_DIM, (h + 1) * HEAD_DIM)
        a_m = jnp.where(causal, jnp.broadcast_to(slot.gate[h:h + 1, :], (tm, tm)), -jnp.inf)
        mprev = jnp.concatenate(
            [jnp.broadcast_to(m_rd(j, h), (seg, 1)) for j in range(nseg)], axis=0)
        g = jnp.maximum(jnp.max(a_m, axis=-1, keepdims=True), mprev)
        dm = jnp.exp(a_m - g)
        lf_b = jnp.broadcast_to(slot.gate[8 + h:9 + h, :], (tm, tm))
        bcol = jnp.sum(jnp.where(causal, lf_b, 0.0), axis=-1, keepdims=True)
        mcol = bcol + g
        wi = jnp.exp(mprev - g)

        qh = slot.q[:, sl]
        vh = slot.v[:, sl]
        s = _dot(qh, slot.ktb[sl, :]) * dm
        qf = qh.astype(F32)
        qc_parts, qn_parts = [], []
        for j in range(nseg):
            rs = slice(j * seg, (j + 1) * seg)
            qc_parts.append(_dot(qh[rs], c_rd[j, h].astype(BF16)))
            qn_parts.append(jnp.sum(qf[rs] * n_rd(j, h), axis=-1, keepdims=True))
        qc = jnp.concatenate(qc_parts, axis=0)
        qn = jnp.concatenate(qn_parts, axis=0)
        heads.append((sl, dm, mcol, wi, vh, s, qc, qn))
    yield

    for h, (sl, dm, mcol, wi, vh, s, qc, qn) in enumerate(heads):
        kth = slot.kt[sl, :]
        rowsum = jnp.sum(s, axis=-1, keepdims=True)
        sv = _dot(s.astype(BF16), vh)
        num = wi * qc + sv
        den = wi * qn + rowsum
        hh = num / jnp.maximum(jnp.abs(den), jnp.exp(-mcol))
        hh = hh * lax.rsqrt(jnp.mean(hh * hh, axis=-1, keepdims=True) + EPS) * w.ghead[:, sl]
        hh = hh * _sigmoid(slot.og[:, sl])
        hcat_s[:, sl] = hh.astype(BF16)

        wrows, n_decayed = [], []
        for j in range(nseg):
            r = (j + 1) * seg - 1
            w_row = dm[r:r + 1, :]
            wrows.append(w_row)
            decay = wi[r:r + 1, :]
            kw = (kth * w_row).astype(BF16)
            c_store(j, h, decay * c_rd[j, h] + _dot(kw, vh))
            n_decayed.append(decay * n_rd(j, h))
            m_store(j, h, mcol[r:r + 1, :])
        ridx = lax.broadcasted_iota(jnp.int32, (GATE_ROWS, tm), 0)
        wmat = jnp.zeros((GATE_ROWS, tm), F32)
        for j in range(nseg):
            wmat = jnp.where(ridx == j, jnp.broadcast_to(wrows[j], (GATE_ROWS, tm)), wmat)
        nupd = _dot_nt(wmat.astype(BF16), slot.ktb[sl, :])
        for j in range(nseg):
            n_store(j, h, n_decayed[j] + nupd[j:j + 1, :])
        yield

    grp_out = [_dot(slot.pd[:, gi * POOL_GROUP:(gi + 1) * POOL_GROUP], w.wgrp[gi])
               for gi in range(len(POOL_WINDOWS))]
    pooled = jnp.concatenate(grp_out, axis=-1) * w.pscale[...]
    yield

    branch_a = _dot(hcat_s[...], w.wba[...])
    branch_b = _dot(pooled.astype(BF16), w.wbb[...])
    yield
    mixed = _sigmoid(slot.ga[...]) * branch_a + _sigmoid(slot.gb[...]) * branch_b
    out_ref[rows, :] = x + _dot(mixed.astype(BF16), w.wout[...])
    yield


def _run(order, **gens):
    for name in order:
        next(gens[name], None)
    for gen in gens.values():
        for _ in gen:
            pass


class _StateRing:
    def __init__(self, dst_hbm, ring, sems, first_stream, step, last_step, stores_per_step):
        self.dst, self.ring, self.sems, self.first = dst_hbm, ring, sems, first_stream
        self.step, self.last_step = step, last_step
        self.bank = ring.shape[0] // 2
        assert stores_per_step % ring.shape[0] == 0
        self.count = 0

    def _copy(self, slot, stream, h):
        return pltpu.make_async_copy(self.ring.at[slot], self.dst.at[stream, h], self.sems.at[slot])

    def _wait_bank(self, bank):
        for slot in range(bank * self.bank, (bank + 1) * self.bank):
            self._copy(slot, 0, 0).wait()

    def store(self, j, h, value):
        slot = self.count % self.ring.shape[0]
        if slot % self.bank == 0:
            bank = slot // self.bank
            if self.count < self.ring.shape[0]:
                pl.when(self.step > 0)(functools.partial(self._wait_bank, bank))
            else:
                self._wait_bank(bank)
        self.count += 1
        self.ring[slot] = value
        self._copy(slot, self.first + j, h).start()

    def finish(self):
        @pl.when(self.step == self.last_step)
        def _():
            self._wait_bank(0)
            self._wait_bank(1)


def _carry_kernel(x_ref, xnext_ref, gmix_ref, wqv_ref, wrest_ref, wkg_ref, bias_ref, ghead_ref, wgrp_ref,
                  pscale_ref, wba_ref, wbb_ref, wout_ref, *rest, tm, n_riders):
    i = pl.program_id(0)
    rider_in, rest = rest[:n_riders], rest[n_riders:]
    x1_ref, c_out, n_out, m_out, hist_out = rest[:5]
    rider_out, (hcat_s, n_s, m_s, *slot_refs) = rest[5:5 + n_riders], rest[5 + n_riders:]
    w = _Weights(gmix_ref, wqv_ref, wrest_ref, wkg_ref, bias_ref, ghead_ref, wgrp_ref, pscale_ref,
                 wba_ref, wbb_ref, wout_ref)
    nslot = len(_Slot._fields)
    slot_a, slot_b = _Slot(*slot_refs[:nslot]), _Slot(*slot_refs[nslot:])
    kw = dict(tm=tm, seg=tm)
    for src, dst in zip(rider_in, rider_out):
        dst[...] = src[...].astype(BF16)

    row = lax.broadcasted_iota(jnp.int32, (tm, POOL_GROUP), 0)

    @pl.when(i == 0)
    def _():
        _init_scratch(slot_a, slot_b)
        c_out[...] = jnp.zeros(c_out.shape, F32)
        n_s[...] = jnp.zeros(n_s.shape, F32)
        m_s[...] = jnp.zeros(m_s.shape, F32)
        _run("", P=_project(x_ref[0:tm, :], w, slot_a, [jnp.zeros((HIST_ROWS, D_POOL), F32)], row, **kw))

    head_cols = [slice(h * HEAD_DIM, (h + 1) * HEAD_DIM) for h in range(N_HEADS)]
    head_lanes = [slice(h * LANES, (h + 1) * LANES) for h in range(N_HEADS)]

    def c_store(j, h, value):
        c_out[j, h] = value

    def n_store(j, h, value):
        n_s[:, head_cols[h]] = value

    def m_store(j, h, value):
        m_s[:, head_lanes[h]] = jnp.broadcast_to(value, (1, LANES))

    state_io = (c_out, lambda j, h: n_s[:, head_cols[h]], lambda j, h: m_s[:, h * LANES:h * LANES + 1],
                c_store, n_store, m_store)

    def finish_beside(rows, slot, tile, x_proj, slot_proj):
        new_hist = slot.abuf[0, POOL_T0 + tm - HIST_ROWS:POOL_T0 + tm, :]
        _run(_CARRY_ORDER,
             F=_finish(x_ref[rows, :], x1_ref, rows, w, slot, hcat_s, *state_io, **kw),
             P=_project(x_proj, w, slot_proj, [new_hist], row + (tile + 1) * tm, **kw))

    finish_beside(slice(0, tm), slot_a, 2 * i, x_ref[tm:2 * tm, :], slot_b)
    finish_beside(slice(tm, 2 * tm), slot_b, 2 * i + 1, xnext_ref[...], slot_a)
    for h in range(N_HEADS):
        n_out[0, h:h + 1, :] = n_s[:, head_cols[h]]
        m_out[:, h:h + 1] = m_s[:, h * LANES:h * LANES + 1]
    hist_out[0] = slot_b.abuf[0, POOL_T0 + tm - POOL_HIST:POOL_T0 + tm, :]


def _streams_kernel(x_ref, gmix_ref, wqv_ref, wrest_ref, wkg_ref, bias_ref, ghead_ref, wgrp_ref,
                    pscale_ref, wba_ref, wbb_ref, wout_ref, c_in, n_in, m_in, hist_in,
                    x1_ref, c_hbm, n_out, m_out, hist_out,
                    hcat_s, cring_s, cring_sem, *slot_refs, tm, seg, pos0):
    i = pl.program_id(0)
    nseg = tm // seg
    w = _Weights(gmix_ref, wqv_ref, wrest_ref, wkg_ref, bias_ref, ghead_ref, wgrp_ref, pscale_ref,
                 wba_ref, wbb_ref, wout_ref)
    slot = _Slot(*slot_refs)

    @pl.when(i == 0)
    def _():
        _init_scratch(slot)

    x = x_ref[...]
    pos = lax.broadcasted_iota(jnp.int32, (seg, POOL_GROUP), 0) + pos0
    ring = _StateRing(c_hbm, cring_s, cring_sem, first_stream=i * nseg, step=i, last_step=pl.num_programs(0) - 1,
                      stores_per_step=nseg * N_HEADS)

    def n_store(j, h, value):
        n_out[j, h:h + 1, :] = value

    def m_store(j, h, value):
        m_out[j:j + 1, h:h + 1] = value

    _run(_STREAMS_ORDER,
         P=_project(x, w, slot, [hist_in[:, j, :] for j in range(nseg)], pos, tm=tm, seg=seg),
         F=_finish(x, x1_ref, slice(0, tm), w, slot, hcat_s, c_in, lambda j, h: n_in[j, h:h + 1, :],
                   lambda j, h: m_in[j:j + 1, h:h + 1], ring.store, n_store, m_store, tm=tm, seg=seg))
    ring.finish()
    for j in range(nseg):
        hist_out[:, j, :] = slot.abuf[j, POOL_T0 + seg - POOL_HIST:POOL_T0 + seg, :]


def _ffn_tile(x_ref, y_ref, gffn_ref, wup_ref, wdown_ref, gfin_ref, sub):
    n_sub = x_ref.shape[0] // sub
    rows = [slice(r * sub, (r + 1) * sub) for r in range(n_sub)]
    act = {}

    def up(r):
        u = _rmsnorm(x_ref[rows[r], :], gffn_ref[...]).astype(BF16)
        act[r] = jnp.square(jnp.maximum(_dot(u, wup_ref[...]), 0.0)).astype(BF16)

    def down(r):
        x2 = x_ref[rows[r], :] + _dot(act.pop(r), wdown_ref[...])
        y_ref[rows[r], :] = _rmsnorm(x2, gfin_ref[...])

    up(0)
    for r in range(1, n_sub):
        up(r)
        down(r - 1)
    down(n_sub - 1)


def _ffn_kernel(xs_ref, xp_ref, gffn_ref, wup_ref, wdown_ref, gfin_ref, ys_ref, yp_ref, *, sub, n_s):
    i = pl.program_id(0)

    @pl.when(i < n_s)
    def _():
        _ffn_tile(xs_ref, ys_ref, gffn_ref, wup_ref, wdown_ref, gfin_ref, sub)

    @pl.when(i >= n_s)
    def _():
        _ffn_tile(xp_ref, yp_ref, gffn_ref, wup_ref, wdown_ref, gfin_ref, sub)


def _resident(shape):
    zeros = (0,) * len(shape)
    return pl.BlockSpec(shape, lambda i: zeros, pipeline_mode=pl.Buffered(1))


def _slot_scratch(tm, seg):
    return [
        pltpu.VMEM((tm, D_MODEL), BF16),
        pltpu.VMEM((tm, D_MODEL), BF16),
        pltpu.VMEM((D_MODEL, tm), F32),
        pltpu.VMEM((D_MODEL, tm), BF16),
        pltpu.VMEM((tm, D_MODEL), F32),
        pltpu.VMEM((tm, D_MODEL), F32),
        pltpu.VMEM((tm, D_MODEL), F32),
        pltpu.VMEM((GATE_ROWS, tm), F32),
        pltpu.VMEM((tm // seg, POOL_T0 + seg, D_POOL), F32),
        pltpu.VMEM((tm, D_POOL), BF16),
    ]


def _mixer(x, weights, state, *, tm, seg, carry, pos0, riders=()):
    t = x.shape[0]
    nseg = tm // seg
    n_streams = 1 if carry else t // seg
    sblk = 1 if carry else nseg
    smap = (lambda i: 0) if carry else (lambda i: i)
    bias = jnp.broadcast_to(weights["gate_bias"][:, None], (GATE_ROWS, tm))

    if carry:
        assert pos0 == 0 and nseg == 1 and t % (2 * tm) == 0
        steps = t // (2 * tm)
        last_tile = t // tm - 1
        x_specs = [pl.BlockSpec((2 * tm, D_MODEL), lambda i: (i, 0)),
                   pl.BlockSpec((tm, D_MODEL), lambda i: (jnp.minimum(2 * i + 2, last_tile), 0))]
        x_args = [x, x]
        x1_spec = pl.BlockSpec((2 * tm, D_MODEL), lambda i: (i, 0))
        body = functools.partial(_carry_kernel, tm=tm, n_riders=len(riders))
        slots = [pltpu.VMEM((1, D_MODEL), F32), pltpu.VMEM((1, N_HEADS * LANES), F32)] + 2 * _slot_scratch(tm, seg)
        assert steps % RIDER_HOLD_STEPS == 0 and all(r.shape[0] % (16 * steps) == 0 for r in riders)
    else:
        steps = t // tm
        x_specs = [pl.BlockSpec((tm, D_MODEL), lambda i: (i, 0))]
        x_args = [x]
        x1_spec = pl.BlockSpec((tm, D_MODEL), lambda i: (i, 0))
        body = functools.partial(_streams_kernel, tm=tm, seg=seg, pos0=pos0)
        slots = [pltpu.VMEM((STATE_RING_SLOTS, HEAD_DIM, HEAD_DIM), F32),
                 pltpu.SemaphoreType.DMA((STATE_RING_SLOTS,))] + _slot_scratch(tm, seg)

    state_specs = [
        pl.BlockSpec((sblk, N_HEADS, HEAD_DIM, HEAD_DIM), lambda i: (smap(i), 0, 0, 0)),
        pl.BlockSpec((sblk, N_HEADS, HEAD_DIM), lambda i: (smap(i), 0, 0)),
        pl.BlockSpec((sblk, N_HEADS), lambda i: (smap(i), 0)),
        pl.BlockSpec((1, POOL_HIST, D_POOL), lambda i: (0, 0, 0)) if carry else
        pl.BlockSpec((POOL_HIST, sblk, D_POOL), lambda i: (0, i, 0)),
    ]
    in_specs = x_specs + [
        _resident((1, D_MODEL)),
        _resident((D_MODEL, 2 * D_MODEL)),
        _resident((D_MODEL, _REST_ROWS)),
        _resident((D_MODEL + GATE_ROWS, D_MODEL)),
        _resident((GATE_ROWS, tm)),
        _resident((1, D_MODEL)),
        _resident((len(POOL_WINDOWS), POOL_GROUP, POOL_GROUP)),
        _resident((1, D_POOL)),
        _resident((D_MODEL, D_MODEL)),
        _resident((D_POOL, D_MODEL)),
        _resident((D_MODEL, D_MODEL)),
    ] + ([] if carry else state_specs)
    rider_specs = [pl.BlockSpec((r.shape[0] * RIDER_HOLD_STEPS // steps, r.shape[1]),
                                lambda i: (i // RIDER_HOLD_STEPS, 0)) for r in riders]
    out_shape = [
        jax.ShapeDtypeStruct((t, D_MODEL), F32),
        jax.ShapeDtypeStruct((n_streams, N_HEADS, HEAD_DIM, HEAD_DIM), F32),
        jax.ShapeDtypeStruct((n_streams, N_HEADS, HEAD_DIM), F32),
        jax.ShapeDtypeStruct((n_streams, N_HEADS), F32),
        jax.ShapeDtypeStruct((1, POOL_HIST, D_POOL) if carry else (POOL_HIST, n_streams, D_POOL), F32),
    ] + [jax.ShapeDtypeStruct(r.shape, BF16) for r in riders]
    scratch = [pltpu.VMEM((tm, D_MODEL), BF16)] + slots
    return pl.pallas_call(
        body,
        grid=(steps,),
        in_specs=in_specs + rider_specs,
        out_specs=([x1_spec, state_specs[0] if carry else pl.BlockSpec(memory_space=pl.ANY)] + state_specs[1:]
                   + rider_specs),
        out_shape=out_shape,
        scratch_shapes=scratch,
        compiler_params=pltpu.CompilerParams(
            dimension_semantics=("arbitrary",), vmem_limit_bytes=V7X_VMEM_LIMIT_BYTES),
        name="mixer_carry" if carry else "mixer_streams",
    )(*x_args, weights["g_mix"], weights["w_qv"], weights["w_rest"], weights["w_kg"], bias, weights["g_head"],
      weights["w_grp"], weights["pool_scale"], weights["w_ba"], weights["w_bb"], weights["w_out"], *state, *riders)


def _ffn(x_sample, x_prompt, w_up, w_down, weights, *, tm_s, tm_p):
    ts, tp = x_sample.shape[0], x_prompt.shape[0]
    n_s, n_p = ts // tm_s, tp // tm_p
    clamp = lambda v, hi: jnp.clip(v, 0, hi)
    return pl.pallas_call(
        functools.partial(_ffn_kernel, sub=512, n_s=n_s),
        grid=(n_s + n_p,),
        in_specs=[
            pl.BlockSpec((tm_s, D_MODEL), lambda i: (clamp(i, n_s - 1), 0)),
            pl.BlockSpec((tm_p, D_MODEL), lambda i: (clamp(i - n_s, n_p - 1), 0)),
            _resident((1, D_MODEL)),
            _resident((D_MODEL, D_FF)),
            _resident((D_FF, D_MODEL)),
            _resident((1, D_MODEL)),
        ],
        out_specs=[
            pl.BlockSpec((tm_s, D_MODEL), lambda i: (clamp(i, n_s - 1), 0)),
            pl.BlockSpec((tm_p, D_MODEL), lambda i: (clamp(i - n_s, n_p - 1), 0)),
        ],
        out_shape=[jax.ShapeDtypeStruct((ts, D_MODEL), F32), jax.ShapeDtypeStruct((tp, D_MODEL), F32)],
        compiler_params=pltpu.CompilerParams(
            dimension_semantics=("arbitrary",), vmem_limit_bytes=V7X_VMEM_LIMIT_BYTES),
        name="ffn",
    )(x_sample, x_prompt, weights["g_ffn"], w_up, w_down, weights["g_final"])


def _split_w_in_kernel(w_ref, *refs, n_riders):
    i = pl.program_id(0)
    rider_in, (wqv_ref, wkg_ref, wrest_ref) = refs[:n_riders], refs[n_riders:n_riders + 3]
    rider_out, prev_s = refs[n_riders + 3:2 * n_riders + 3], refs[2 * n_riders + 3]
    n_qkv = _QKV_ROWS // _SPLIT_ROWS
    cur = w_ref[...]
    for src, dst in zip(rider_in, rider_out):
        dst[...] = src[...].astype(BF16)

    @pl.when((i == 0) | (i == 2))
    def _():
        wqv_ref[...] = cur.T.astype(BF16)

    @pl.when(i == 1)
    def _():
        wkg_ref[0:D_MODEL, :] = cur.astype(BF16)

    @pl.when(i == n_qkv)
    def _():
        gate = cur[0:2 * N_HEADS, :]
        wkg_ref[D_MODEL:D_MODEL + GATE_ROWS, :] = jnp.concatenate(
            [gate, pltpu.roll(gate, N_HEADS, axis=0)], axis=0).astype(BF16)

    @pl.when(i > n_qkv)
    def _():
        rows = jnp.concatenate([prev_s[2 * N_HEADS:, :], cur[0:2 * N_HEADS, :]], axis=0)
        wrest_ref[...] = rows.T.astype(BF16)

    prev_s[...] = cur


def _split_w_in(w_in_t, riders=()):
    n_in = w_in_t.shape[0]
    assert n_in == _REST0 + _REST_ROWS and _SPLIT_ROWS == D_MODEL and _QKV_ROWS == 3 * _SPLIT_ROWS
    n_qkv, n_rest = _QKV_ROWS // _SPLIT_ROWS, pl.cdiv(_REST_ROWS, _SPLIT_ROWS)
    steps = n_qkv + 1 + n_rest
    last_chunk = pl.cdiv(n_in, _SPLIT_ROWS) - 1
    assert all(r.shape[0] % (16 * steps) == 0 for r in riders)
    rider_specs = [pl.BlockSpec((r.shape[0] // steps, r.shape[1]), lambda i: (i, 0)) for r in riders]
    return pl.pallas_call(
        functools.partial(_split_w_in_kernel, n_riders=len(riders)),
        grid=(steps,),
        in_specs=[pl.BlockSpec((_SPLIT_ROWS, D_MODEL), lambda i: (jnp.minimum(i, last_chunk), 0))] + rider_specs,
        out_specs=[
            pl.BlockSpec((D_MODEL, _SPLIT_ROWS), lambda i: (0, jnp.where(i < 2, 0, 1))),
            pl.BlockSpec((D_MODEL + GATE_ROWS, D_MODEL), lambda i: (0, 0)),
            pl.BlockSpec((D_MODEL, _SPLIT_ROWS), lambda i: (0, jnp.clip(i - n_qkv - 1, 0, n_rest - 1))),
        ] + rider_specs,
        out_shape=[
            jax.ShapeDtypeStruct((D_MODEL, 2 * D_MODEL), BF16),
            jax.ShapeDtypeStruct((D_MODEL + GATE_ROWS, D_MODEL), BF16),
            jax.ShapeDtypeStruct((D_MODEL, _REST_ROWS), BF16),
        ] + [jax.ShapeDtypeStruct(r.shape, BF16) for r in riders],
        scratch_shapes=[pltpu.VMEM((_SPLIT_ROWS, D_MODEL), F32)],
        compiler_params=pltpu.CompilerParams(
            dimension_semantics=("arbitrary",), vmem_limit_bytes=V7X_VMEM_LIMIT_BYTES),
        name="split_w_in",
    )(w_in_t, *riders)


def _pack_weights(w_in, b_igate, b_fgate, g_norm_mix, g_head, w_pool_grp, pool_scale, w_branch_mlstm,
                  w_branch_pool, w_out, g_norm_ffn, w_up, w_down, g_final):
    d = D_MODEL
    w_qv, w_kg, w_rest, w_ba, w_bb, w_o, w_grp = _split_w_in(
        w_in.T, riders=(w_branch_mlstm, w_branch_pool, w_out, w_pool_grp.reshape(-1, POOL_GROUP)))
    return {
        "w_qv": w_qv,
        "w_kg": w_kg,
        "w_rest": w_rest,
        "gate_bias": jnp.concatenate([b_igate, b_fgate, b_fgate, b_igate]).astype(F32),
        "g_mix": g_norm_mix.reshape(1, d),
        "g_head": g_head.reshape(1, d),
        "w_grp": w_grp.reshape(w_pool_grp.shape),
        "pool_scale": pool_scale.reshape(1, D_POOL),
        "w_ba": w_ba,
        "w_bb": w_bb,
        "w_out": w_o,
        "g_ffn": g_norm_ffn.reshape(1, d),
        "g_final": g_final.reshape(1, d),
    }


def kernel(x_prompt, x_sample, state_C, state_n, state_m, state_pool, w_in, b_igate, b_fgate, g_norm_mix, g_head,
           w_pool_grp, pool_scale, w_branch_mlstm, w_branch_pool, w_out, g_norm_ffn, w_up, w_down, g_final):
    depth = w_in.shape[0]
    bp, sp, d = x_prompt.shape
    bs, ss, _ = x_sample.shape
    assert depth == 1 and bp == 1 and d == D_MODEL
    hp = x_prompt.reshape(bp * sp, d)
    hs = x_sample.reshape(bs * ss, d)
    weights = _pack_weights(w_in[0], b_igate[0], b_fgate[0], g_norm_mix[0], g_head[0], w_pool_grp[0], pool_scale[0],
                            w_branch_mlstm[0], w_branch_pool[0], w_out[0], g_norm_ffn[0], w_up[0], w_down[0],
                            g_final)
    hp, cp, np_, mp, pp, w_up_b, w_down_b = _mixer(hp, weights, (), tm=256, seg=256, carry=True, pos0=0,
                                                   riders=(w_up[0], w_down[0]))
    sample_state = (state_C[0], state_n[0], state_m[0], jnp.swapaxes(state_pool[0], 0, 1))
    hs, cs, ns, ms, ps = _mixer(hs, weights, sample_state, tm=8 * ss, seg=ss, carry=False, pos0=PAST_LEN)
    y_sample, y_prompt = _ffn(hs, hp, w_up_b, w_down_b, weights, tm_s=512, tm_p=1024)
    y_prompt = y_prompt.reshape(bp, sp, d)
    y_sample = y_sample.reshape(bs, ss, d)
    return (y_prompt, y_sample, cp[None], np_[None], mp[None], pp[None], cs[None], ns[None], ms[None],
            jnp.swapaxes(ps, 0, 1)[None])
```

```python
import collections
import functools

import jax
import jax.numpy as jnp
from jax import lax
from jax.experimental import pallas as pl
from jax.experimental.pallas import tpu as pltpu

F32 = jnp.float32
BF16 = jnp.bfloat16

D_MODEL = 1024
N_HEADS = 4
HEAD_DIM = 256
D_POOL = 512
POOL_GROUP = 128
POOL_WINDOWS = (2, 4, 8, 16)
POOL_HIST = max(POOL_WINDOWS) - 1
HIST_ROWS = 16
POOL_T0 = 32
D_FF = 4096
EPS = 1e-6
PAST_LEN = 2048
LANES = 128
GATE_ROWS = 16
RIDER_HOLD_STEPS = 2
STATE_RING_SLOTS = 16
V7X_VMEM_LIMIT_BYTES = 58 * 1024 * 1024

_K0, _V0, _QKV_ROWS = 1024, 2048, 3072
_REST0 = _QKV_ROWS + 2 * N_HEADS
_P0, _GA0, _GB0, _REST_ROWS = 1024, 1536, 2560, 3584
_SPLIT_ROWS = 1024

_Slot = collections.namedtuple("_Slot", "q v kt ktb og ga gb gate abuf pd")
_Weights = collections.namedtuple(
    "_Weights", "gmix wqv wrest wkg bias ghead wgrp pscale wba wbb wout")
_CARRY_ORDER = "FPFFFFPPFPPFPFP"
_STREAMS_ORDER = "PPPPFPFPFPFFFFF"


def _dot(a, b):
    return jnp.dot(a, b, preferred_element_type=F32)


def _dot_nt(a, b):
    return lax.dot_general(a, b, (((1,), (1,)), ((), ())), preferred_element_type=F32)


def _rmsnorm(x, g):
    return x * lax.rsqrt(jnp.mean(x * x, axis=-1, keepdims=True) + EPS) * g


def _sigmoid(x):
    return 1.0 / (1.0 + jnp.exp(-x))


def _log_sigmoid(x):
    return jnp.minimum(x, 0.0) - jnp.log1p(jnp.exp(-jnp.abs(x)))


def _masks(tm, seg):
    row = lax.broadcasted_iota(jnp.int32, (tm, tm), 0)
    col = lax.broadcasted_iota(jnp.int32, (tm, tm), 1)
    if tm == seg:
        return col <= row, row <= col
    shift = seg.bit_length() - 1
    same = (row >> shift) == (col >> shift)
    return (col <= row) & same, (row <= col) & same


def _init_scratch(*slots):
    for slot in slots:
        slot.abuf[:, 0:POOL_T0, :] = jnp.zeros((slot.abuf.shape[0], POOL_T0, D_POOL), F32)


def _project(x, w, slot, hist, pos, *, tm, seg):
    nseg = tm // seg
    u = _rmsnorm(x, w.gmix[...]).astype(BF16)
    kg = _dot_nt(w.wkg[...], u)
    kt = kg[0:D_MODEL] * (HEAD_DIM ** -0.5)
    slot.kt[...] = kt
    slot.ktb[...] = kt.astype(BF16)
    gates = kg[D_MODEL:D_MODEL + GATE_ROWS] + w.bias[...]
    yield
    p = _dot(u, w.wrest[:, _P0:_GA0])
    for j in range(nseg):
        slot.abuf[j, POOL_T0 - hist[j].shape[0]:POOL_T0, :] = hist[j]
        slot.abuf[j, POOL_T0:POOL_T0 + seg, :] = p[j * seg:(j + 1) * seg]
    for gi, win in enumerate(POOL_WINDOWS):
        ls = slice(gi * POOL_GROUP, (gi + 1) * POOL_GROUP)
        cnt = jnp.minimum(pos + 1, win).astype(F32)
        for j in range(nseg):
            start = POOL_T0 - 8 * gi
            tot = slot.abuf[j, start:POOL_T0 + seg, ls] + slot.abuf[j, start - 1:POOL_T0 + seg - 1, ls]
            shift = 2
            while shift < win:
                n = tot.shape[0]
                tot = tot[8:n] + tot[8 - shift:n - shift]
                shift *= 2
            tok = slot.abuf[j, POOL_T0:POOL_T0 + seg, ls]
            slot.pd[j * seg:(j + 1) * seg, ls] = (tot / cnt - tok).astype(BF16)
    yield
    slot.q[...] = _dot(u, w.wqv[:, 0:D_MODEL]).astype(BF16)
    _, upper = _masks(tm, seg)
    lf = _log_sigmoid(gates[8:16])
    hi = lf.astype(BF16).astype(F32)
    r1 = lf - hi
    mid = r1.astype(BF16).astype(F32)
    lo = r1 - mid
    pieces = jnp.concatenate([hi, mid, lo, jnp.zeros_like(lo)], axis=0).astype(BF16)
    cs = _dot(pieces, jnp.where(upper, 1.0, 0.0).astype(BF16))
    b = cs[0:8] + cs[8:16] + cs[16:24]
    slot.gate[0:8, :] = gates[0:8] - b
    slot.gate[8:16, :] = lf
    yield
    slot.v[...] = _dot(u, w.wqv[:, D_MODEL:2 * D_MODEL]).astype(BF16)
    yield
    slot.og[...] = _dot(u, w.wrest[:, 0:_P0])
    yield
    slot.ga[...] = _dot(u, w.wrest[:, _GA0:_GB0])
    yield
    slot.gb[...] = _dot(u, w.wrest[:, _GB0:_REST_ROWS])
    yield


def _finish(x, out_ref, rows, w, slot, hcat_s, c_rd, n_rd, m_rd, c_store, n_store, m_store, *, tm, seg):
    nseg = tm // seg
    causal, _ = _masks(tm, seg)

    heads = []
    for h in range(N_HEADS):
        sl = slice(h * HEAD_DIM, (h + 1) * HEAD_DIM)
        a_m = jnp.where(causal, jnp.broadcast_to(slot.gate[h:h + 1, :], (tm, tm)), -jnp.inf)
        mprev = jnp.concatenate(
            [jnp.broadcast_to(m_rd(j, h), (seg, 1)) for j in range(nseg)], axis=0)
        g = jnp.maximum(jnp.max(a_m, axis=-1, keepdims=True), mprev)
        dm = jnp.exp(a_m - g)
        lf_b = jnp.broadcast_to(slot.gate[8 + h:9 + h, :], (tm, tm))
        bcol = jnp.sum(jnp.where(causal, lf_b, 0.0), axis=-1, keepdims=True)
        mcol = bcol + g
        wi = jnp.exp(mprev - g)

        qh = slot.q[:, sl]
        vh = slot.v[:, sl]
        s = _dot(qh, slot.ktb[sl, :]) * dm
        qf = qh.astype(F32)
        qc_parts, qn_parts = [], []
        for j in range(nseg):
            rs = slice(j * seg, (j + 1) * seg)
            qc_parts.append(_dot(qh[rs], c_rd[j, h].astype(BF16)))
            qn_parts.append(jnp.sum(qf[rs] * n_rd(j, h), axis=-1, keepdims=True))
        qc = jnp.concatenate(qc_parts, axis=0)
        qn = jnp.concatenate(qn_parts, axis=0)
        heads.append((sl, dm, mcol, wi, vh, s, qc, qn))
    yield

    for h, (sl, dm, mcol, wi, vh, s, qc, qn) in enumerate(heads):
        kth = slot.kt[sl, :]
        rowsum = jnp.sum(s, axis=-1, keepdims=True)
        sv = _dot(s.astype(BF16), vh)
        num = wi * qc + sv
        den = wi * qn + rowsum
        hh = num / jnp.maximum(jnp.abs(den), jnp.exp(-mcol))
        hh = hh * lax.rsqrt(jnp.mean(hh * hh, axis=-1, keepdims=True) + EPS) * w.ghead[:, sl]
        hh = hh * _sigmoid(slot.og[:, sl])
        hcat_s[:, sl] = hh.astype(BF16)

        wrows, n_decayed = [], []
        for j in range(nseg):
            r = (j + 1) * seg - 1
            w_row = dm[r:r + 1, :]
            wrows.append(w_row)
            decay = wi[r:r + 1, :]
            kw = (kth * w_row).astype(BF16)
            c_store(j, h, decay * c_rd[j, h] + _dot(kw, vh))
            n_decayed.append(decay * n_rd(j, h))
            m_store(j, h, mcol[r:r + 1, :])
        ridx = lax.broadcasted_iota(jnp.int32, (GATE_ROWS, tm), 0)
        wmat = jnp.zeros((GATE_ROWS, tm), F32)
        for j in range(nseg):
            wmat = jnp.where(ridx == j, jnp.broadcast_to(wrows[j], (GATE_ROWS, tm)), wmat)
        nupd = _dot_nt(wmat.astype(BF16), slot.ktb[sl, :])
        for j in range(nseg):
            n_store(j, h, n_decayed[j] + nupd[j:j + 1, :])
        yield

    grp_out = [_dot(slot.pd[:, gi * POOL_GROUP:(gi + 1) * POOL_GROUP], w.wgrp[gi])
               for gi in range(len(POOL_WINDOWS))]
    pooled = jnp.concatenate(grp_out, axis=-1) * w.pscale[...]
    yield

    branch_a = _dot(hcat_s[...], w.wba[...])
    branch_b = _dot(pooled.astype(BF16), w.wbb[...])
    yield
    mixed = _sigmoid(slot.ga[...]) * branch_a + _sigmoid(slot.gb[...]) * branch_b
    out_ref[rows, :] = x + _dot(mixed.astype(BF16), w.wout[...])
    yield


def _run(order, **gens):
    for name in order:
        next(gens[name], None)
    for gen in gens.values():
        for _ in gen:
            pass


class _StateRing:
    def __init__(self, dst_hbm, ring, sems, first_stream, step, last_step, stores_per_step):
        self.dst, self.ring, self.sems, self.first = dst_hbm, ring, sems, first_stream
        self.step, self.last_step = step, last_step
        self.bank = ring.shape[0] // 2
        assert stores_per_step % ring.shape[0] == 0
        self.count = 0

    def _copy(self, slot, stream, h):
        return pltpu.make_async_copy(self.ring.at[slot], self.dst.at[stream, h], self.sems.at[slot])

    def _wait_bank(self, bank):
        for slot in range(bank * self.bank, (bank + 1) * self.bank):
            self._copy(slot, 0, 0).wait()

    def store(self, j, h, value):
        slot = self.count % self.ring.shape[0]
        if slot % self.bank == 0:
            bank = slot // self.bank
            if self.count < self.ring.shape[0]:
                pl.when(self.step > 0)(functools.partial(self._wait_bank, bank))
            else:
                self._wait_bank(bank)
        self.count += 1
        self.ring[slot] = value
        self._copy(slot, self.first + j, h).start()

    def finish(self):
        @pl.when(self.step == self.last_step)
        def _():
            self._wait_bank(0)
            self._wait_bank(1)


def _carry_kernel(x_ref, xnext_ref, gmix_ref, wqv_ref, wrest_ref, wkg_ref, bias_ref, ghead_ref, wgrp_ref,
                  pscale_ref, wba_ref, wbb_ref, wout_ref, *rest, tm, n_riders):
    i = pl.program_id(0)
    rider_in, rest = rest[:n_riders], rest[n_riders:]
    x1_ref, c_out, n_out, m_out, hist_out = rest[:5]
    rider_out, (hcat_s, n_s, m_s, *slot_refs) = rest[5:5 + n_riders], rest[5 + n_riders:]
    w = _Weights(gmix_ref, wqv_ref, wrest_ref, wkg_ref, bias_ref, ghead_ref, wgrp_ref, pscale_ref,
                 wba_ref, wbb_ref, wout_ref)
    nslot = len(_Slot._fields)
    slot_a, slot_b = _Slot(*slot_refs[:nslot]), _Slot(*slot_refs[nslot:])
    kw = dict(tm=tm, seg=tm)
    for src, dst in zip(rider_in, rider_out):
        dst[...] = src[...].astype(BF16)

    row = lax.broadcasted_iota(jnp.int32, (tm, POOL_GROUP), 0)

    @pl.when(i == 0)
    def _():
        _init_scratch(slot_a, slot_b)
        c_out[...] = jnp.zeros(c_out.shape, F32)
        n_s[...] = jnp.zeros(n_s.shape, F32)
        m_s[...] = jnp.zeros(m_s.shape, F32)
        _run("", P=_project(x_ref[0:tm, :], w, slot_a, [jnp.zeros((HIST_ROWS, D_POOL), F32)], row, **kw))

    head_cols = [slice(h * HEAD_DIM, (h + 1) * HEAD_DIM) for h in range(N_HEADS)]
    head_lanes = [slice(h * LANES, (h + 1) * LANES) for h in range(N_HEADS)]

    def c_store(j, h, value):
        c_out[j, h] = value

    def n_store(j, h, value):
        n_s[:, head_cols[h]] = value

    def m_store(j, h, value):
        m_s[:, head_lanes[h]] = jnp.broadcast_to(value, (1, LANES))

    state_io = (c_out, lambda j, h: n_s[:, head_cols[h]], lambda j, h: m_s[:, h * LANES:h * LANES + 1],
                c_store, n_store, m_store)

    def finish_beside(rows, slot, tile, x_proj, slot_proj):
        new_hist = slot.abuf[0, POOL_T0 + tm - HIST_ROWS:POOL_T0 + tm, :]
        _run(_CARRY_ORDER,
             F=_finish(x_ref[rows, :], x1_ref, rows, w, slot, hcat_s, *state_io, **kw),
             P=_project(x_proj, w, slot_proj, [new_hist], row + (tile + 1) * tm, **kw))

    finish_beside(slice(0, tm), slot_a, 2 * i, x_ref[tm:2 * tm, :], slot_b)
    finish_beside(slice(tm, 2 * tm), slot_b, 2 * i + 1, xnext_ref[...], slot_a)
    for h in range(N_HEADS):
        n_out[0, h:h + 1, :] = n_s[:, head_cols[h]]
        m_out[:, h:h + 1] = m_s[:, h * LANES:h * LANES + 1]
    hist_out[0] = slot_b.abuf[0, POOL_T0 + tm - POOL_HIST:POOL_T0 + tm, :]


def _streams_kernel(x_ref, gmix_ref, wqv_ref, wrest_ref, wkg_ref, bias_ref, ghead_ref, wgrp_ref,
                    pscale_ref, wba_ref, wbb_ref, wout_ref, c_in, n_in, m_in, hist_in,
                    x1_ref, c_hbm, n_out, m_out, hist_out,
                    hcat_s, cring_s, cring_sem, *slot_refs, tm, seg, pos0):
    i = pl.program_id(0)
    nseg = tm // seg
    w = _Weights(gmix_ref, wqv_ref, wrest_ref, wkg_ref, bias_ref, ghead_ref, wgrp_ref, pscale_ref,
                 wba_ref, wbb_ref, wout_ref)
    slot = _Slot(*slot_refs)

    @pl.when(i == 0)
    def _():
        _init_scratch(slot)

    x = x_ref[...]
    pos = lax.broadcasted_iota(jnp.int32, (seg, POOL_GROUP), 0) + pos0
    ring = _StateRing(c_hbm, cring_s, cring_sem, first_stream=i * nseg, step=i, last_step=pl.num_programs(0) - 1,
                      stores_per_step=nseg * N_HEADS)

    def n_store(j, h, value):
        n_out[j, h:h + 1, :] = value

    def m_store(j, h, value):
        m_out[j:j + 1, h:h + 1] = value

    _run(_STREAMS_ORDER,
         P=_project(x, w, slot, [hist_in[:, j, :] for j in range(nseg)], pos, tm=tm, seg=seg),
         F=_finish(x, x1_ref, slice(0, tm), w, slot, hcat_s, c_in, lambda j, h: n_in[j, h:h + 1, :],
                   lambda j, h: m_in[j:j + 1, h:h + 1], ring.store, n_store, m_store, tm=tm, seg=seg))
    ring.finish()
    for j in range(nseg):
        hist_out[:, j, :] = slot.abuf[j, POOL_T0 + seg - POOL_HIST:POOL_T0 + seg, :]


def _ffn_tile(x_ref, y_ref, gffn_ref, wup_ref, wdown_ref, gfin_ref, sub, chunk):
    sub = min(sub, x_ref.shape[0])
    rows = [slice(r * sub, (r + 1) * sub) for r in range(x_ref.shape[0] // sub)]
    cols = [slice(c * chunk, (c + 1) * chunk) for c in range(D_FF // chunk)]
    units = [(r, c) for r in range(len(rows)) for c in range(len(cols))]
    normed, act, acc = {}, {}, {}

    def up(r, c):
        if c == 0:
            normed[r] = _rmsnorm(x_ref[rows[r], :], gffn_ref[...]).astype(BF16)
        act[r, c] = jnp.square(jnp.maximum(_dot(normed[r], wup_ref[:, cols[c]]), 0.0)).astype(BF16)

    def down(r, c):
        part = _dot(act.pop((r, c)), wdown_ref[cols[c], :])
        acc[r] = part if c == 0 else acc[r] + part
        if c == len(cols) - 1:
            y_ref[rows[r], :] = _rmsnorm(x_ref[rows[r], :] + acc.pop(r), gfin_ref[...])

    up(*units[0])
    for prev, cur in zip(units, units[1:]):
        up(*cur)
        down(*prev)
    down(*units[-1])


def _ffn_kernel(xs_ref, xp_ref, gffn_ref, wup_ref, wdown_ref, gfin_ref, ys_ref, yp_ref, *, sub, chunk, n_s):
    i = pl.program_id(0)

    @pl.when(i < n_s)
    def _():
        _ffn_tile(xs_ref, ys_ref, gffn_ref, wup_ref, wdown_ref, gfin_ref, sub, chunk)

    @pl.when(i >= n_s)
    def _():
        _ffn_tile(xp_ref, yp_ref, gffn_ref, wup_ref, wdown_ref, gfin_ref, sub, chunk)


def _resident(shape):
    zeros = (0,) * len(shape)
    return pl.BlockSpec(shape, lambda i: zeros, pipeline_mode=pl.Buffered(1))


def _slot_scratch(tm, seg):
    return [
        pltpu.VMEM((tm, D_MODEL), BF16),
        pltpu.VMEM((tm, D_MODEL), BF16),
        pltpu.VMEM((D_MODEL, tm), F32),
        pltpu.VMEM((D_MODEL, tm), BF16),
        pltpu.VMEM((tm, D_MODEL), F32),
        pltpu.VMEM((tm, D_MODEL), F32),
        pltpu.VMEM((tm, D_MODEL), F32),
        pltpu.VMEM((GATE_ROWS, tm), F32),
        pltpu.VMEM((tm // seg, POOL_T0 + seg, D_POOL), F32),
        pltpu.VMEM((tm, D_POOL), BF16),
    ]


def _mixer(x, weights, state, *, tm, seg, carry, pos0, riders=()):
    t = x.shape[0]
    nseg = tm // seg
    n_streams = 1 if carry else t // seg
    sblk = 1 if carry else nseg
    smap = (lambda i: 0) if carry else (lambda i: i)
    bias = jnp.broadcast_to(weights["gate_bias"][:, None], (GATE_ROWS, tm))

    if carry:
        assert pos0 == 0 and nseg == 1 and t % (2 * tm) == 0
        steps = t // (2 * tm)
        last_tile = t // tm - 1
        x_specs = [pl.BlockSpec((2 * tm, D_MODEL), lambda i: (i, 0)),
                   pl.BlockSpec((tm, D_MODEL), lambda i: (jnp.minimum(2 * i + 2, last_tile), 0))]
        x_args = [x, x]
        x1_spec = pl.BlockSpec((2 * tm, D_MODEL), lambda i: (i, 0))
        body = functools.partial(_carry_kernel, tm=tm, n_riders=len(riders))
        slots = [pltpu.VMEM((1, D_MODEL), F32), pltpu.VMEM((1, N_HEADS * LANES), F32)] + 2 * _slot_scratch(tm, seg)
        assert steps % RIDER_HOLD_STEPS == 0 and all(r.shape[0] % (16 * steps) == 0 for r in riders)
    else:
        steps = t // tm
        x_specs = [pl.BlockSpec((tm, D_MODEL), lambda i: (i, 0))]
        x_args = [x]
        x1_spec = pl.BlockSpec((tm, D_MODEL), lambda i: (i, 0))
        body = functools.partial(_streams_kernel, tm=tm, seg=seg, pos0=pos0)
        slots = [pltpu.VMEM((STATE_RING_SLOTS, HEAD_DIM, HEAD_DIM), F32),
                 pltpu.SemaphoreType.DMA((STATE_RING_SLOTS,))] + _slot_scratch(tm, seg)

    state_specs = [
        pl.BlockSpec((sblk, N_HEADS, HEAD_DIM, HEAD_DIM), lambda i: (smap(i), 0, 0, 0)),
        pl.BlockSpec((sblk, N_HEADS, HEAD_DIM), lambda i: (smap(i), 0, 0)),
        pl.BlockSpec((sblk, N_HEADS), lambda i: (smap(i), 0)),
        pl.BlockSpec((1, POOL_HIST, D_POOL), lambda i: (0, 0, 0)) if carry else
        pl.BlockSpec((POOL_HIST, sblk, D_POOL), lambda i: (0, i, 0)),
    ]
    in_specs = x_specs + [
        _resident((1, D_MODEL)),
        _resident((D_MODEL, 2 * D_MODEL)),
        _resident((D_MODEL, _REST_ROWS)),
        _resident((D_MODEL + GATE_ROWS, D_MODEL)),
        _resident((GATE_ROWS, tm)),
        _resident((1, D_MODEL)),
        _resident((len(POOL_WINDOWS), POOL_GROUP, POOL_GROUP)),
        _resident((1, D_POOL)),
        _resident((D_MODEL, D_MODEL)),
        _resident((D_POOL, D_MODEL)),
        _resident((D_MODEL, D_MODEL)),
    ] + ([] if carry else state_specs)
    rider_specs = [pl.BlockSpec((r.shape[0] * RIDER_HOLD_STEPS // steps, r.shape[1]),
                                lambda i: (i // RIDER_HOLD_STEPS, 0)) for r in riders]
    out_shape = [
        jax.ShapeDtypeStruct((t, D_MODEL), F32),
        jax.ShapeDtypeStruct((n_streams, N_HEADS, HEAD_DIM, HEAD_DIM), F32),
        jax.ShapeDtypeStruct((n_streams, N_HEADS, HEAD_DIM), F32),
        jax.ShapeDtypeStruct((n_streams, N_HEADS), F32),
        jax.ShapeDtypeStruct((1, POOL_HIST, D_POOL) if carry else (POOL_HIST, n_streams, D_POOL), F32),
    ] + [jax.ShapeDtypeStruct(r.shape, BF16) for r in riders]
    scratch = [pltpu.VMEM((tm, D_MODEL), BF16)] + slots
    return pl.pallas_call(
        body,
        grid=(steps,),
        in_specs=in_specs + rider_specs,
        out_specs=([x1_spec, state_specs[0] if carry else pl.BlockSpec(memory_space=pl.ANY)] + state_specs[1:]
                   + rider_specs),
        out_shape=out_shape,
        scratch_shapes=scratch,
        compiler_params=pltpu.CompilerParams(
            dimension_semantics=("arbitrary",), vmem_limit_bytes=V7X_VMEM_LIMIT_BYTES),
        name="mixer_carry" if carry else "mixer_streams",
    )(*x_args, weights["g_mix"], weights["w_qv"], weights["w_rest"], weights["w_kg"], bias, weights["g_head"],
      weights["w_grp"], weights["pool_scale"], weights["w_ba"], weights["w_bb"], weights["w_out"], *state, *riders)


def _ffn(x_sample, x_prompt, w_up, w_down, weights, *, tm_s, tm_p):
    ts, tp = x_sample.shape[0], x_prompt.shape[0]
    n_s, n_p = ts // tm_s, tp // tm_p
    clamp = lambda v, hi: jnp.clip(v, 0, hi)
    return pl.pallas_call(
        functools.partial(_ffn_kernel, sub=1024, chunk=1024, n_s=n_s),
        grid=(n_s + n_p,),
        in_specs=[
            pl.BlockSpec((tm_s, D_MODEL), lambda i: (clamp(i, n_s - 1), 0)),
            pl.BlockSpec((tm_p, D_MODEL), lambda i: (clamp(i - n_s, n_p - 1), 0)),
            _resident((1, D_MODEL)),
            _resident((D_MODEL, D_FF)),
            _resident((D_FF, D_MODEL)),
            _resident((1, D_MODEL)),
        ],
        out_specs=[
            pl.BlockSpec((tm_s, D_MODEL), lambda i: (clamp(i, n_s - 1), 0)),
            pl.BlockSpec((tm_p, D_MODEL), lambda i: (clamp(i - n_s, n_p - 1), 0)),
        ],
        out_shape=[jax.ShapeDtypeStruct((ts, D_MODEL), F32), jax.ShapeDtypeStruct((tp, D_MODEL), F32)],
        compiler_params=pltpu.CompilerParams(
            dimension_semantics=("arbitrary",), vmem_limit_bytes=V7X_VMEM_LIMIT_BYTES),
        name="ffn",
    )(x_sample, x_prompt, weights["g_ffn"], w_up, w_down, weights["g_final"])


def _split_w_in_kernel(w_ref, *refs, n_riders):
    i = pl.program_id(0)
    rider_in, (wqv_ref, wkg_ref, wrest_ref) = refs[:n_riders], refs[n_riders:n_riders + 3]
    rider_out, prev_s = refs[n_riders + 3:2 * n_riders + 3], refs[2 * n_riders + 3]
    n_qkv = _QKV_ROWS // _SPLIT_ROWS
    cur = w_ref[...]
    for src, dst in zip(rider_in, rider_out):
        dst[...] = src[...].astype(BF16)

    @pl.when((i == 0) | (i == 2))
    def _():
        wqv_ref[...] = cur.T.astype(BF16)

    @pl.when(i == 1)
    def _():
        wkg_ref[0:D_MODEL, :] = cur.astype(BF16)

    @pl.when(i == n_qkv)
    def _():
        gate = cur[0:2 * N_HEADS, :]
        wkg_ref[D_MODEL:D_MODEL + GATE_ROWS, :] = jnp.concatenate(
            [gate, pltpu.roll(gate, N_HEADS, axis=0)], axis=0).astype(BF16)

    @pl.when(i > n_qkv)
    def _():
        rows = jnp.concatenate([prev_s[2 * N_HEADS:, :], cur[0:2 * N_HEADS, :]], axis=0)
        wrest_ref[...] = rows.T.astype(BF16)

    prev_s[...] = cur


def _split_w_in(w_in_t, riders=()):
    n_in = w_in_t.shape[0]
    assert n_in == _REST0 + _REST_ROWS and _SPLIT_ROWS == D_MODEL and _QKV_ROWS == 3 * _SPLIT_ROWS
    n_qkv, n_rest = _QKV_ROWS // _SPLIT_ROWS, pl.cdiv(_REST_ROWS, _SPLIT_ROWS)
    steps = n_qkv + 1 + n_rest
    last_chunk = pl.cdiv(n_in, _SPLIT_ROWS) - 1
    assert all(r.shape[0] % (16 * steps) == 0 for r in riders)
    rider_specs = [pl.BlockSpec((r.shape[0] // steps, r.shape[1]), lambda i: (i, 0)) for r in riders]
    return pl.pallas_call(
        functools.partial(_split_w_in_kernel, n_riders=len(riders)),
        grid=(steps,),
        in_specs=[pl.BlockSpec((_SPLIT_ROWS, D_MODEL), lambda i: (jnp.minimum(i, last_chunk), 0))] + rider_specs,
        out_specs=[
            pl.BlockSpec((D_MODEL, _SPLIT_ROWS), lambda i: (0, jnp.where(i < 2, 0, 1))),
            pl.BlockSpec((D_MODEL + GATE_ROWS, D_MODEL), lambda i: (0, 0)),
            pl.BlockSpec((D_MODEL, _SPLIT_ROWS), lambda i: (0, jnp.clip(i - n_qkv - 1, 0, n_rest - 1))),
        ] + rider_specs,
        out_shape=[
            jax.ShapeDtypeStruct((D_MODEL, 2 * D_MODEL), BF16),
            jax.ShapeDtypeStruct((D_MODEL + GATE_ROWS, D_MODEL), BF16),
            jax.ShapeDtypeStruct((D_MODEL, _REST_ROWS), BF16),
        ] + [jax.ShapeDtypeStruct(r.shape, BF16) for r in riders],
        scratch_shapes=[pltpu.VMEM((_SPLIT_ROWS, D_MODEL), F32)],
        compiler_params=pltpu.CompilerParams(
            dimension_semantics=("arbitrary",), vmem_limit_bytes=V7X_VMEM_LIMIT_BYTES),
        name="split_w_in",
    )(w_in_t, *riders)


def _pack_weights(w_in, b_igate, b_fgate, g_norm_mix, g_head, w_pool_grp, pool_scale, w_branch_mlstm,
                  w_branch_pool, w_out, g_norm_ffn, w_up, w_down, g_final):
    d = D_MODEL
    w_qv, w_kg, w_rest, w_ba, w_bb, w_o, w_grp = _split_w_in(
        w_in.T, riders=(w_branch_mlstm, w_branch_pool, w_out, w_pool_grp.reshape(-1, POOL_GROUP)))
    return {
        "w_qv": w_qv,
        "w_kg": w_kg,
        "w_rest": w_rest,
        "gate_bias": jnp.concatenate([b_igate, b_fgate, b_fgate, b_igate]).astype(F32),
        "g_mix": g_norm_mix.reshape(1, d),
        "g_head": g_head.reshape(1, d),
        "w_grp": w_grp.reshape(w_pool_grp.shape),
        "pool_scale": pool_scale.reshape(1, D_POOL),
        "w_ba": w_ba,
        "w_bb": w_bb,
        "w_out": w_o,
        "g_ffn": g_norm_ffn.reshape(1, d),
        "g_final": g_final.reshape(1, d),
    }


def kernel(x_prompt, x_sample, state_C, state_n, state_m, state_pool, w_in, b_igate, b_fgate, g_norm_mix, g_head,
           w_pool_grp, pool_scale, w_branch_mlstm, w_branch_pool, w_out, g_norm_ffn, w_up, w_down, g_final):
    depth = w_in.shape[0]
    bp, sp, d = x_prompt.shape
    bs, ss, _ = x_sample.shape
    assert depth == 1 and bp == 1 and d == D_MODEL
    hp = x_prompt.reshape(bp * sp, d)
    hs = x_sample.reshape(bs * ss, d)
    weights = _pack_weights(w_in[0], b_igate[0], b_fgate[0], g_norm_mix[0], g_head[0], w_pool_grp[0], pool_scale[0],
                            w_branch_mlstm[0], w_branch_pool[0], w_out[0], g_norm_ffn[0], w_up[0], w_down[0],
                            g_final)
    hp, cp, np_, mp, pp, w_up_b, w_down_b = _mixer(hp, weights, (), tm=256, seg=256, carry=True, pos0=0,
                                                   riders=(w_up[0], w_down[0]))
    sample_state = (state_C[0], state_n[0], state_m[0], jnp.swapaxes(state_pool[0], 0, 1))
    hs, cs, ns, ms, ps = _mixer(hs, weights, sample_state, tm=8 * ss, seg=ss, carry=False, pos0=PAST_LEN)
    y_sample, y_prompt = _ffn(hs, hp, w_up_b, w_down_b, weights, tm_s=512, tm_p=1024)
    y_prompt = y_prompt.reshape(bp, sp, d)
    y_sample = y_sample.reshape(bs, ss, d)
    return (y_prompt, y_sample, cp[None], np_[None], mp[None], pp[None], cs[None], ns[None], ms[None],
            jnp.swapaxes(ps, 0, 1)[None])
```

```python
import collections
import functools

import jax
import jax.numpy as jnp
from jax import lax
from jax.experimental import pallas as pl
from jax.experimental.pallas import tpu as pltpu

F32 = jnp.float32
BF16 = jnp.bfloat16

D_MODEL = 1024
N_HEADS = 4
HEAD_DIM = 256
D_POOL = 512
POOL_GROUP = 128
POOL_WINDOWS = (2, 4, 8, 16)
POOL_HIST = max(POOL_WINDOWS) - 1
HIST_ROWS = 16
POOL_T0 = 32
D_FF = 4096
EPS = 1e-6
PAST_LEN = 2048
LANES = 128
GATE_ROWS = 16
RIDER_HOLD_STEPS = 2
STATE_RING_SLOTS = 16
V7X_VMEM_LIMIT_BYTES = 58 * 1024 * 1024

_K0, _V0, _QKV_ROWS = 1024, 2048, 3072
_REST0 = _QKV_ROWS + 2 * N_HEADS
_P0, _GA0, _GB0, _REST_ROWS = 1024, 1536, 2560, 3584
_SPLIT_ROWS = 1024

_Slot = collections.namedtuple("_Slot", "q v kt ktb og ga gb gate abuf pd")
_Weights = collections.namedtuple(
    "_Weights", "gmix wqv wrest wkg bias ghead wgrp pscale wba wbb wout")
_CARRY_ORDER = "FPFFFFPPFPPFPFP"
_STREAMS_ORDER = "PPPPFPFPFPFFFFF"


def _dot(a, b):
    return jnp.dot(a, b, preferred_element_type=F32)


def _dot_nt(a, b):
    return lax.dot_general(a, b, (((1,), (1,)), ((), ())), preferred_element_type=F32)


def _rmsnorm(x, g):
    return x * lax.rsqrt(jnp.mean(x * x, axis=-1, keepdims=True) + EPS) * g


def _sigmoid(x):
    return 1.0 / (1.0 + jnp.exp(-x))


def _log_sigmoid(x):
    return jnp.minimum(x, 0.0) - jnp.log1p(jnp.exp(-jnp.abs(x)))


def _masks(tm, seg):
    row = lax.broadcasted_iota(jnp.int32, (tm, tm), 0)
    col = lax.broadcasted_iota(jnp.int32, (tm, tm), 1)
    if tm == seg:
        return col <= row, row <= col
    shift = seg.bit_length() - 1
    same = (row >> shift) == (col >> shift)
    return (col <= row) & same, (row <= col) & same


def _init_scratch(*slots):
    for slot in slots:
        slot.abuf[:, 0:POOL_T0, :] = jnp.zeros((slot.abuf.shape[0], POOL_T0, D_POOL), F32)


def _project(x, w, slot, hist, pos, *, tm, seg):
    nseg = tm // seg
    u = _rmsnorm(x, w.gmix[...]).astype(BF16)
    kg = _dot_nt(w.wkg[...], u)
    kt = kg[0:D_MODEL] * (HEAD_DIM ** -0.5)
    slot.kt[...] = kt
    slot.ktb[...] = kt.astype(BF16)
    gates = kg[D_MODEL:D_MODEL + GATE_ROWS] + w.bias[...]
    yield
    p = _dot(u, w.wrest[:, _P0:_GA0])
    for j in range(nseg):
        slot.abuf[j, POOL_T0 - hist[j].shape[0]:POOL_T0, :] = hist[j]
        slot.abuf[j, POOL_T0:POOL_T0 + seg, :] = p[j * seg:(j + 1) * seg]
    for gi, win in enumerate(POOL_WINDOWS):
        ls = slice(gi * POOL_GROUP, (gi + 1) * POOL_GROUP)
        cnt = jnp.minimum(pos + 1, win).astype(F32)
        for j in range(nseg):
            start = POOL_T0 - 8 * gi
            tot = slot.abuf[j, start:POOL_T0 + seg, ls] + slot.abuf[j, start - 1:POOL_T0 + seg - 1, ls]
            shift = 2
            while shift < win:
                n = tot.shape[0]
                tot = tot[8:n] + tot[8 - shift:n - shift]
                shift *= 2
            tok = slot.abuf[j, POOL_T0:POOL_T0 + seg, ls]
            slot.pd[j * seg:(j + 1) * seg, ls] = (tot / cnt - tok).astype(BF16)
    yield
    slot.q[...] = _dot(u, w.wqv[:, 0:D_MODEL]).astype(BF16)
    _, upper = _masks(tm, seg)
    lf = _log_sigmoid(gates[8:16])
    hi = lf.astype(BF16).astype(F32)
    r1 = lf - hi
    mid = r1.astype(BF16).astype(F32)
    lo = r1 - mid
    pieces = jnp.concatenate([hi, mid, lo, jnp.zeros_like(lo)], axis=0).astype(BF16)
    cs = _dot(pieces, jnp.where(upper, 1.0, 0.0).astype(BF16))
    b = cs[0:8] + cs[8:16] + cs[16:24]
    slot.gate[0:8, :] = gates[0:8] - b
    slot.gate[8:16, :] = lf
    yield
    slot.v[...] = _dot(u, w.wqv[:, D_MODEL:2 * D_MODEL]).astype(BF16)
    yield
    slot.og[...] = _dot(u, w.wrest[:, 0:_P0])
    yield
    slot.ga[...] = _dot(u, w.wrest[:, _GA0:_GB0])
    yield
    slot.gb[...] = _dot(u, w.wrest[:, _GB0:_REST_ROWS])
    yield


def _finish(x, out_ref, rows, w, slot, hcat_s, c_rd, n_rd, m_rd, c_store, n_store, m_store, *, tm, seg):
    nseg = tm // seg
    causal, _ = _masks(tm, seg)

    heads = []
    for h in range(N_HEADS):
        sl = slice(h * HEAD_DIM, (h + 1) * HEAD_DIM)
        a_m = jnp.where(causal, jnp.broadcast_to(slot.gate[h:h + 1, :], (tm, tm)), -jnp.inf)
        mprev = jnp.concatenate(
            [jnp.broadcast_to(m_rd(j, h), (seg, 1)) for j in range(nseg)], axis=0)
        g = jnp.maximum(jnp.max(a_m, axis=-1, keepdims=True), mprev)
        dm = jnp.exp(a_m - g)
        lf_b = jnp.broadcast_to(slot.gate[8 + h:9 + h, :], (tm, tm))
        bcol = jnp.sum(jnp.where(causal, lf_b, 0.0), axis=-1, keepdims=True)
        mcol = bcol + g
        wi = jnp.exp(mprev - g)

        qh = slot.q[:, sl]
        vh = slot.v[:, sl]
        s = _dot(qh, slot.ktb[sl, :]) * dm
        qf = qh.astype(F32)
        qc_parts, qn_parts = [], []
        for j in range(nseg):
            rs = slice(j * seg, (j + 1) * seg)
            qc_parts.append(_dot(qh[rs], c_rd[j, h].astype(BF16)))
            qn_parts.append(jnp.sum(qf[rs] * n_rd(j, h), axis=-1, keepdims=True))
        qc = jnp.concatenate(qc_parts, axis=0)
        qn = jnp.concatenate(qn_parts, axis=0)
        heads.append((sl, dm, mcol, wi, vh, s, qc, qn))
    yield

    for h, (sl, dm, mcol, wi, vh, s, qc, qn) in enumerate(heads):
        kth = slot.kt[sl, :]
        rowsum = jnp.sum(s, axis=-1, keepdims=True)
        sv = _dot(s.astype(BF16), vh)
        num = wi * qc + sv
        den = wi * qn + rowsum
        hh = num / jnp.maximum(jnp.abs(den), jnp.exp(-mcol))
        hh = hh * lax.rsqrt(jnp.mean(hh * hh, axis=-1, keepdims=True) + EPS) * w.ghead[:, sl]
        hh = hh * _sigmoid(slot.og[:, sl])
        hcat_s[:, sl] = hh.astype(BF16)

        wrows, n_decayed = [], []
        for j in range(nseg):
            r = (j + 1) * seg - 1
            w_row = dm[r:r + 1, :]
            wrows.append(w_row)
            decay = wi[r:r + 1, :]
            kw = (kth * w_row).astype(BF16)
            c_store(j, h, decay * c_rd[j, h] + _dot(kw, vh))
            n_decayed.append(decay * n_rd(j, h))
            m_store(j, h, mcol[r:r + 1, :])
        ridx = lax.broadcasted_iota(jnp.int32, (GATE_ROWS, tm), 0)
        wmat = jnp.zeros((GATE_ROWS, tm), F32)
        for j in range(nseg):
            wmat = jnp.where(ridx == j, jnp.broadcast_to(wrows[j], (GATE_ROWS, tm)), wmat)
        nupd = _dot_nt(wmat.astype(BF16), slot.ktb[sl, :])
        for j in range(nseg):
            n_store(j, h, n_decayed[j] + nupd[j:j + 1, :])
        yield

    grp_out = [_dot(slot.pd[:, gi * POOL_GROUP:(gi + 1) * POOL_GROUP], w.wgrp[gi])
               for gi in range(len(POOL_WINDOWS))]
    pooled = jnp.concatenate(grp_out, axis=-1) * w.pscale[...]
    yield

    branch_a = _dot(hcat_s[...], w.wba[...])
    branch_b = _dot(pooled.astype(BF16), w.wbb[...])
    yield
    mixed = _sigmoid(slot.ga[...]) * branch_a + _sigmoid(slot.gb[...]) * branch_b
    out_ref[rows, :] = x + _dot(mixed.astype(BF16), w.wout[...])
    yield


def _run(order, **gens):
    for name in order:
        next(gens[name], None)
    for gen in gens.values():
        for _ in gen:
            pass


class _StateRing:
    def __init__(self, dst_hbm, ring, sems, first_stream, step, last_step, stores_per_step):
        self.dst, self.ring, self.sems, self.first = dst_hbm, ring, sems, first_stream
        self.step, self.last_step = step, last_step
        self.bank = ring.shape[0] // 2
        assert stores_per_step % ring.shape[0] == 0
        self.count = 0

    def _copy(self, slot, stream, h):
        return pltpu.make_async_copy(self.ring.at[slot], self.dst.at[stream, h], self.sems.at[slot])

    def _wait_bank(self, bank):
        for slot in range(bank * self.bank, (bank + 1) * self.bank):
            self._copy(slot, 0, 0).wait()

    def store(self, j, h, value):
        slot = self.count % self.ring.shape[0]
        if slot % self.bank == 0:
            bank = slot // self.bank
            if self.count < self.ring.shape[0]:
                pl.when(self.step > 0)(functools.partial(self._wait_bank, bank))
            else:
                self._wait_bank(bank)
        self.count += 1
        self.ring[slot] = value
        self._copy(slot, self.first + j, h).start()

    def finish(self):
        @pl.when(self.step == self.last_step)
        def _():
            self._wait_bank(0)
            self._wait_bank(1)


def _carry_kernel(x_ref, xnext_ref, gmix_ref, wqv_ref, wrest_ref, wkg_ref, bias_ref, ghead_ref, wgrp_ref,
                  pscale_ref, wba_ref, wbb_ref, wout_ref, *rest, tm, n_riders):
    i = pl.program_id(0)
    rider_in, rest = rest[:n_riders], rest[n_riders:]
    x1_ref, c_out, n_out, m_out, hist_out = rest[:5]
    rider_out, (hcat_s, n_s, m_s, *slot_refs) = rest[5:5 + n_riders], rest[5 + n_riders:]
    w = _Weights(gmix_ref, wqv_ref, wrest_ref, wkg_ref, bias_ref, ghead_ref, wgrp_ref, pscale_ref,
                 wba_ref, wbb_ref, wout_ref)
    nslot = len(_Slot._fields)
    slot_a, slot_b = _Slot(*slot_refs[:nslot]), _Slot(*slot_refs[nslot:])
    kw = dict(tm=tm, seg=tm)
    for src, dst in zip(rider_in, rider_out):
        dst[...] = src[...].astype(BF16)

    row = lax.broadcasted_iota(jnp.int32, (tm, POOL_GROUP), 0)

    @pl.when(i == 0)
    def _():
        _init_scratch(slot_a, slot_b)
        c_out[...] = jnp.zeros(c_out.shape, F32)
        n_s[...] = jnp.zeros(n_s.shape, F32)
        m_s[...] = jnp.zeros(m_s.shape, F32)
        _run("", P=_project(x_ref[0:tm, :], w, slot_a, [jnp.zeros((HIST_ROWS, D_POOL), F32)], row, **kw))

    head_cols = [slice(h * HEAD_DIM, (h + 1) * HEAD_DIM) for h in range(N_HEADS)]
    head_lanes = [slice(h * LANES, (h + 1) * LANES) for h in range(N_HEADS)]

    def c_store(j, h, value):
        c_out[j, h] = value

    def n_store(j, h, value):
        n_s[:, head_cols[h]] = value

    def m_store(j, h, value):
        m_s[:, head_lanes[h]] = jnp.broadcast_to(value, (1, LANES))

    state_io = (c_out, lambda j, h: n_s[:, head_cols[h]], lambda j, h: m_s[:, h * LANES:h * LANES + 1],
                c_store, n_store, m_store)

    def finish_beside(rows, slot, tile, x_proj, slot_proj):
        new_hist = slot.abuf[0, POOL_T0 + tm - HIST_ROWS:POOL_T0 + tm, :]
        _run(_CARRY_ORDER,
             F=_finish(x_ref[rows, :], x1_ref, rows, w, slot, hcat_s, *state_io, **kw),
             P=_project(x_proj, w, slot_proj, [new_hist], row + (tile + 1) * tm, **kw))

    finish_beside(slice(0, tm), slot_a, 2 * i, x_ref[tm:2 * tm, :], slot_b)
    finish_beside(slice(tm, 2 * tm), slot_b, 2 * i + 1, xnext_ref[...], slot_a)
    for h in range(N_HEADS):
        n_out[0, h:h + 1, :] = n_s[:, head_cols[h]]
        m_out[:, h:h + 1] = m_s[:, h * LANES:h * LANES + 1]
    hist_out[0] = slot_b.abuf[0, POOL_T0 + tm - POOL_HIST:POOL_T0 + tm, :]


def _streams_kernel(x_ref, gmix_ref, wqv_ref, wrest_ref, wkg_ref, bias_ref, ghead_ref, wgrp_ref,
                    pscale_ref, wba_ref, wbb_ref, wout_ref, c_in, n_in, m_in, hist_in,
                    x1_ref, c_hbm, n_out, m_out, hist_out,
                    hcat_s, cring_s, cring_sem, *slot_refs, tm, seg, pos0):
    i = pl.program_id(0)
    nseg = tm // seg
    w = _Weights(gmix_ref, wqv_ref, wrest_ref, wkg_ref, bias_ref, ghead_ref, wgrp_ref, pscale_ref,
                 wba_ref, wbb_ref, wout_ref)
    slot = _Slot(*slot_refs)

    @pl.when(i == 0)
    def _():
        _init_scratch(slot)

    x = x_ref[...]
    pos = lax.broadcasted_iota(jnp.int32, (seg, POOL_GROUP), 0) + pos0
    ring = _StateRing(c_hbm, cring_s, cring_sem, first_stream=i * nseg, step=i, last_step=pl.num_programs(0) - 1,
                      stores_per_step=nseg * N_HEADS)

    def n_store(j, h, value):
        n_out[j, h:h + 1, :] = value

    def m_store(j, h, value):
        m_out[j:j + 1, h:h + 1] = value

    _run(_STREAMS_ORDER,
         P=_project(x, w, slot, [hist_in[:, j, :] for j in range(nseg)], pos, tm=tm, seg=seg),
         F=_finish(x, x1_ref, slice(0, tm), w, slot, hcat_s, c_in, lambda j, h: n_in[j, h:h + 1, :],
                   lambda j, h: m_in[j:j + 1, h:h + 1], ring.store, n_store, m_store, tm=tm, seg=seg))
    ring.finish()
    for j in range(nseg):
        hist_out[:, j, :] = slot.abuf[j, POOL_T0 + seg - POOL_HIST:POOL_T0 + seg, :]


def _ffn_tile(x_ref, y_ref, gffn_ref, wup_ref, wdown_ref, gfin_ref, sub):
    n_sub = x_ref.shape[0] // sub
    rows = [slice(r * sub, (r + 1) * sub) for r in range(n_sub)]
    act = {}

    def up(r):
        u = _rmsnorm(x_ref[rows[r], :], gffn_ref[...]).astype(BF16)
        act[r] = jnp.square(jnp.maximum(_dot(u, wup_ref[...]), 0.0)).astype(BF16)

    def down(r):
        x2 = x_ref[rows[r], :] + _dot(act.pop(r), wdown_ref[...])
        y_ref[rows[r], :] = _rmsnorm(x2, gfin_ref[...])

    up(0)
    for r in range(1, n_sub):
        up(r)
        down(r - 1)
    down(n_sub - 1)


def _ffn_kernel(xs_ref, xp_ref, gffn_ref, wup_ref, wdown_ref, gfin_ref, ys_ref, yp_ref, *, sub, n_s):
    i = pl.program_id(0)

    @pl.when(i < n_s)
    def _():
        _ffn_tile(xs_ref, ys_ref, gffn_ref, wup_ref, wdown_ref, gfin_ref, sub)

    @pl.when(i >= n_s)
    def _():
        _ffn_tile(xp_ref, yp_ref, gffn_ref, wup_ref, wdown_ref, gfin_ref, sub)


def _resident(shape):
    zeros = (0,) * len(shape)
    return pl.BlockSpec(shape, lambda i: zeros, pipeline_mode=pl.Buffered(1))


def _slot_scratch(tm, seg):
    return [
        pltpu.VMEM((tm, D_MODEL), BF16),
        pltpu.VMEM((tm, D_MODEL), BF16),
        pltpu.VMEM((D_MODEL, tm), F32),
        pltpu.VMEM((D_MODEL, tm), BF16),
        pltpu.VMEM((tm, D_MODEL), F32),
        pltpu.VMEM((tm, D_MODEL), F32),
        pltpu.VMEM((tm, D_MODEL), F32),
        pltpu.VMEM((GATE_ROWS, tm), F32),
        pltpu.VMEM((tm // seg, POOL_T0 + seg, D_POOL), F32),
        pltpu.VMEM((tm, D_POOL), BF16),
    ]


def _mixer(x, weights, state, *, tm, seg, carry, pos0, riders=()):
    t = x.shape[0]
    nseg = tm // seg
    n_streams = 1 if carry else t // seg
    sblk = 1 if carry else nseg
    smap = (lambda i: 0) if carry else (lambda i: i)
    bias = jnp.broadcast_to(weights["gate_bias"][:, None], (GATE_ROWS, tm))

    if carry:
        assert pos0 == 0 and nseg == 1 and t % (2 * tm) == 0
        steps = t // (2 * tm)
        last_tile = t // tm - 1
        x_specs = [pl.BlockSpec((2 * tm, D_MODEL), lambda i: (i, 0)),
                   pl.BlockSpec((tm, D_MODEL), lambda i: (jnp.minimum(2 * i + 2, last_tile), 0))]
        x_args = [x, x]
        x1_spec = pl.BlockSpec((2 * tm, D_MODEL), lambda i: (i, 0))
        body = functools.partial(_carry_kernel, tm=tm, n_riders=len(riders))
        slots = [pltpu.VMEM((1, D_MODEL), F32), pltpu.VMEM((1, N_HEADS * LANES), F32)] + 2 * _slot_scratch(tm, seg)
        assert steps % RIDER_HOLD_STEPS == 0 and all(r.shape[0] % (16 * steps) == 0 for r in riders)
    else:
        steps = t // tm
        x_specs = [pl.BlockSpec((tm, D_MODEL), lambda i: (i, 0))]
        x_args = [x]
        x1_spec = pl.BlockSpec((tm, D_MODEL), lambda i: (i, 0))
        body = functools.partial(_streams_kernel, tm=tm, seg=seg, pos0=pos0)
        slots = [pltpu.VMEM((STATE_RING_SLOTS, HEAD_DIM, HEAD_DIM), F32),
                 pltpu.SemaphoreType.DMA((STATE_RING_SLOTS,))] + _slot_scratch(tm, seg)

    state_specs = [
        pl.BlockSpec((sblk, N_HEADS, HEAD_DIM, HEAD_DIM), lambda i: (smap(i), 0, 0, 0)),
        pl.BlockSpec((sblk, N_HEADS, HEAD_DIM), lambda i: (smap(i), 0, 0)),
        pl.BlockSpec((sblk, N_HEADS), lambda i: (smap(i), 0)),
        pl.BlockSpec((1, POOL_HIST, D_POOL), lambda i: (0, 0, 0)) if carry else
        pl.BlockSpec((POOL_HIST, sblk, D_POOL), lambda i: (0, i, 0)),
    ]
    in_specs = x_specs + [
        _resident((1, D_MODEL)),
        _resident((D_MODEL, 2 * D_MODEL)),
        _resident((D_MODEL, _REST_ROWS)),
        _resident((D_MODEL + GATE_ROWS, D_MODEL)),
        _resident((GATE_ROWS, tm)),
        _resident((1, D_MODEL)),
        _resident((len(POOL_WINDOWS), POOL_GROUP, POOL_GROUP)),
        _resident((1, D_POOL)),
        _resident((D_MODEL, D_MODEL)),
        _resident((D_POOL, D_MODEL)),
        _resident((D_MODEL, D_MODEL)),
    ] + ([] if carry else state_specs)
    rider_specs = [pl.BlockSpec((r.shape[0] * RIDER_HOLD_STEPS // steps, r.shape[1]),
                                lambda i: (i // RIDER_HOLD_STEPS, 0)) for r in riders]
    out_shape = [
        jax.ShapeDtypeStruct((t, D_MODEL), F32),
        jax.ShapeDtypeStruct((n_streams, N_HEADS, HEAD_DIM, HEAD_DIM), F32),
        jax.ShapeDtypeStruct((n_streams, N_HEADS, HEAD_DIM), F32),
        jax.ShapeDtypeStruct((n_streams, N_HEADS), F32),
        jax.ShapeDtypeStruct((1, POOL_HIST, D_POOL) if carry else (POOL_HIST, n_streams, D_POOL), F32),
    ] + [jax.ShapeDtypeStruct(r.shape, BF16) for r in riders]
    scratch = [pltpu.VMEM((tm, D_MODEL), BF16)] + slots
    return pl.pallas_call(
        body,
        grid=(steps,),
        in_specs=in_specs + rider_specs,
        out_specs=([x1_spec, state_specs[0] if carry else pl.BlockSpec(memory_space=pl.ANY)] + state_specs[1:]
                   + rider_specs),
        out_shape=out_shape,
        scratch_shapes=scratch,
        compiler_params=pltpu.CompilerParams(
            dimension_semantics=("arbitrary",), vmem_limit_bytes=V7X_VMEM_LIMIT_BYTES),
        name="mixer_carry" if carry else "mixer_streams",
    )(*x_args, weights["g_mix"], weights["w_qv"], weights["w_rest"], weights["w_kg"], bias, weights["g_head"],
      weights["w_grp"], weights["pool_scale"], weights["w_ba"], weights["w_bb"], weights["w_out"], *state, *riders)


def _ffn(x_sample, x_prompt, w_up, w_down, weights, *, tm_s, tm_p):
    ts, tp = x_sample.shape[0], x_prompt.shape[0]
    n_s, n_p = ts // tm_s, tp // tm_p
    clamp = lambda v, hi: jnp.clip(v, 0, hi)
    return pl.pallas_call(
        functools.partial(_ffn_kernel, sub=512, n_s=n_s),
        grid=(n_s + n_p,),
        in_specs=[
            pl.BlockSpec((tm_s, D_MODEL), lambda i: (clamp(i, n_s - 1), 0)),
            pl.BlockSpec((tm_p, D_MODEL), lambda i: (clamp(i - n_s, n_p - 1), 0)),
            _resident((1, D_MODEL)),
            _resident((D_MODEL, D_FF)),
            _resident((D_FF, D_MODEL)),
            _resident((1, D_MODEL)),
        ],
        out_specs=[
            pl.BlockSpec((tm_s, D_MODEL), lambda i: (clamp(i, n_s - 1), 0)),
            pl.BlockSpec((tm_p, D_MODEL), lambda i: (clamp(i - n_s, n_p - 1), 0)),
        ],
        out_shape=[jax.ShapeDtypeStruct((ts, D_MODEL), F32), jax.ShapeDtypeStruct((tp, D_MODEL), F32)],
        compiler_params=pltpu.CompilerParams(
            dimension_semantics=("arbitrary",), vmem_limit_bytes=V7X_VMEM_LIMIT_BYTES),
        name="ffn",
    )(x_sample, x_prompt, weights["g_ffn"], w_up, w_down, weights["g_final"])


def _split_w_in_kernel(w_ref, *refs, n_riders):
    i = pl.program_id(0)
    rider_in, (wqv_ref, wkg_ref, wrest_ref) = refs[:n_riders], refs[n_riders:n_riders + 3]
    rider_out, prev_s = refs[n_riders + 3:2 * n_riders + 3], refs[2 * n_riders + 3]
    n_qkv = _QKV_ROWS // _SPLIT_ROWS
    cur = w_ref[...]
    for src, dst in zip(rider_in, rider_out):
        dst[...] = src[...].astype(BF16)

    @pl.when((i == 0) | (i == 2))
    def _():
        wqv_ref[...] = cur.T.astype(BF16)

    @pl.when(i == 1)
    def _():
        wkg_ref[0:D_MODEL, :] = cur.astype(BF16)

    @pl.when(i == n_qkv)
    def _():
        gate = cur[0:2 * N_HEADS, :]
        wkg_ref[D_MODEL:D_MODEL + GATE_ROWS, :] = jnp.concatenate(
            [gate, pltpu.roll(gate, N_HEADS, axis=0)], axis=0).astype(BF16)

    @pl.when(i > n_qkv)
    def _():
        rows = jnp.concatenate([prev_s[2 * N_HEADS:, :], cur[0:2 * N_HEADS, :]], axis=0)
        wrest_ref[...] = rows.T.astype(BF16)

    prev_s[...] = cur


def _split_w_in(w_in_t, riders=()):
    n_in = w_in_t.shape[0]
    assert n_in == _REST0 + _REST_ROWS and _SPLIT_ROWS == D_MODEL and _QKV_ROWS == 3 * _SPLIT_ROWS
    n_qkv, n_rest = _QKV_ROWS // _SPLIT_ROWS, pl.cdiv(_REST_ROWS, _SPLIT_ROWS)
    steps = n_qkv + 1 + n_rest
    last_chunk = pl.cdiv(n_in, _SPLIT_ROWS) - 1
    assert all(r.shape[0] % (16 * steps) == 0 for r in riders)
    rider_specs = [pl.BlockSpec((r.shape[0] // steps, r.shape[1]), lambda i: (i, 0)) for r in riders]
    return pl.pallas_call(
        functools.partial(_split_w_in_kernel, n_riders=len(riders)),
        grid=(steps,),
        in_specs=[pl.BlockSpec((_SPLIT_ROWS, D_MODEL), lambda i: (jnp.minimum(i, last_chunk), 0))] + rider_specs,
        out_specs=[
            pl.BlockSpec((D_MODEL, _SPLIT_ROWS), lambda i: (0, jnp.where(i < 2, 0, 1))),
            pl.BlockSpec((D_MODEL + GATE_ROWS, D_MODEL), lambda i: (0, 0)),
            pl.BlockSpec((D_MODEL, _SPLIT_ROWS), lambda i: (0, jnp.clip(i - n_qkv - 1, 0, n_rest - 1))),
        ] + rider_specs,
        out_shape=[
            jax.ShapeDtypeStruct((D_MODEL, 2 * D_MODEL), BF16),
            jax.ShapeDtypeStruct((D_MODEL + GATE_ROWS, D_MODEL), BF16),
            jax.ShapeDtypeStruct((D_MODEL, _REST_ROWS), BF16),
        ] + [jax.ShapeDtypeStruct(r.shape, BF16) for r in riders],
        scratch_shapes=[pltpu.VMEM((_SPLIT_ROWS, D_MODEL), F32)],
        compiler_params=pltpu.CompilerParams(
            dimension_semantics=("arbitrary",), vmem_limit_bytes=V7X_VMEM_LIMIT_BYTES),
        name="split_w_in",
    )(w_in_t, *riders)


def _pack_weights(w_in, b_igate, b_fgate, g_norm_mix, g_head, w_pool_grp, pool_scale, w_branch_mlstm,
                  w_branch_pool, w_out, g_norm_ffn, w_up, w_down, g_final):
    d = D_MODEL
    w_qv, w_kg, w_rest, w_ba, w_bb, w_o, w_grp = _split_w_in(
        w_in.T, riders=(w_branch_mlstm, w_branch_pool, w_out, w_pool_grp.reshape(-1, POOL_GROUP)))
    return {
        "w_qv": w_qv,
        "w_kg": w_kg,
        "w_rest": w_rest,
        "gate_bias": jnp.concatenate([b_igate, b_fgate, b_fgate, b_igate]).astype(F32),
        "g_mix": g_norm_mix.reshape(1, d),
        "g_head": g_head.reshape(1, d),
        "w_grp": w_grp.reshape(w_pool_grp.shape),
        "pool_scale": pool_scale.reshape(1, D_POOL),
        "w_ba": w_ba,
        "w_bb": w_bb,
        "w_out": w_o,
        "g_ffn": g_norm_ffn.reshape(1, d),
        "g_final": g_final.reshape(1, d),
    }


def kernel(x_prompt, x_sample, state_C, state_n, state_m, state_pool, w_in, b_igate, b_fgate, g_norm_mix, g_head,
           w_pool_grp, pool_scale, w_branch_mlstm, w_branch_pool, w_out, g_norm_ffn, w_up, w_down, g_final):
    depth = w_in.shape[0]
    bp, sp, d = x_prompt.shape
    bs, ss, _ = x_sample.shape
    assert depth == 1 and bp == 1 and d == D_MODEL
    hp = x_prompt.reshape(bp * sp, d)
    hs = x_sample.reshape(bs * ss, d)
    weights = _pack_weights(w_in[0], b_igate[0], b_fgate[0], g_norm_mix[0], g_head[0], w_pool_grp[0], pool_scale[0],
                            w_branch_mlstm[0], w_branch_pool[0], w_out[0], g_norm_ffn[0], w_up[0], w_down[0],
                            g_final)
    hp, cp, np_, mp, pp, w_up_b, w_down_b = _mixer(hp, weights, (), tm=256, seg=256, carry=True, pos0=0,
                                                   riders=(w_up[0], w_down[0]))
    sample_state = (state_C[0], state_n[0], state_m[0], jnp.swapaxes(state_pool[0], 0, 1))
    hs, cs, ns, ms, ps = _mixer(hs, weights, sample_state, tm=8 * ss, seg=ss, carry=False, pos0=PAST_LEN)
    y_sample, y_prompt = _ffn(hs, hp, w_up_b, w_down_b, weights, tm_s=512, tm_p=1024)
    y_prompt = y_prompt.reshape(bp, sp, d)
    y_sample = y_sample.reshape(bs, ss, d)
    return (y_prompt, y_sample, cp[None], np_[None], mp[None], pp[None], cs[None], ns[None], ms[None],
            jnp.swapaxes(ps, 0, 1)[None])
```

```python
import collections
import functools

import jax
import jax.numpy as jnp
from jax import lax
from jax.experimental import pallas as pl
from jax.experimental.pallas import tpu as pltpu

F32 = jnp.float32
BF16 = jnp.bfloat16

D_MODEL = 1024
N_HEADS = 4
HEAD_DIM = 256
D_POOL = 512
POOL_GROUP = 128
POOL_WINDOWS = (2, 4, 8, 16)
POOL_HIST = max(POOL_WINDOWS) - 1
HIST_ROWS = 16
POOL_T0 = 32
D_FF = 4096
EPS = 1e-6
PAST_LEN = 2048
LANES = 128
GATE_ROWS = 16
RIDER_HOLD_STEPS = 2
STATE_RING_SLOTS = 16
V7X_VMEM_LIMIT_BYTES = 58 * 1024 * 1024

_K0, _V0, _QKV_ROWS = 1024, 2048, 3072
_REST0 = _QKV_ROWS + 2 * N_HEADS
_P0, _GA0, _GB0, _REST_ROWS = 1024, 1536, 2560, 3584
_SPLIT_ROWS = 1024

_Slot = collections.namedtuple("_Slot", "q v kt ktb og ga gb gate abuf pd")
_Weights = collections.namedtuple(
    "_Weights", "gmix wqv wrest wkg bias ghead wgrp pscale wba wbb wout")
_CARRY_ORDER = "FPFFFFPPFPPFPFP"
_STREAMS_ORDER = "PPPPFPFPFPFFFFF"


def _dot(a, b):
    return jnp.dot(a, b, preferred_element_type=F32)


def _dot_nt(a, b):
    return lax.dot_general(a, b, (((1,), (1,)), ((), ())), preferred_element_type=F32)


def _rmsnorm(x, g):
    return x * lax.rsqrt(jnp.mean(x * x, axis=-1, keepdims=True) + EPS) * g


def _sigmoid(x):
    return 1.0 / (1.0 + jnp.exp(-x))


def _log_sigmoid(x):
    return jnp.minimum(x, 0.0) - jnp.log1p(jnp.exp(-jnp.abs(x)))


def _masks(tm, seg):
    row = lax.broadcasted_iota(jnp.int32, (tm, tm), 0)
    col = lax.broadcasted_iota(jnp.int32, (tm, tm), 1)
    if tm == seg:
        return col <= row, row <= col
    shift = seg.bit_length() - 1
    same = (row >> shift) == (col >> shift)
    return (col <= row) & same, (row <= col) & same


def _init_scratch(*slots):
    for slot in slots:
        slot.abuf[:, 0:POOL_T0, :] = jnp.zeros((slot.abuf.shape[0], POOL_T0, D_POOL), F32)


def _project(x, w, slot, hist, pos, *, tm, seg):
    nseg = tm // seg
    u = _rmsnorm(x, w.gmix[...]).astype(BF16)
    kg = _dot_nt(w.wkg[...], u)
    kt = kg[0:D_MODEL] * (HEAD_DIM ** -0.5)
    slot.kt[...] = kt
    slot.ktb[...] = kt.astype(BF16)
    gates = kg[D_MODEL:D_MODEL + GATE_ROWS] + w.bias[...]
    yield
    p = _dot(u, w.wrest[:, _P0:_GA0])
    for j in range(nseg):
        slot.abuf[j, POOL_T0 - hist[j].shape[0]:POOL_T0, :] = hist[j]
        slot.abuf[j, POOL_T0:POOL_T0 + seg, :] = p[j * seg:(j + 1) * seg]
    for gi, win in enumerate(POOL_WINDOWS):
        ls = slice(gi * POOL_GROUP, (gi + 1) * POOL_GROUP)
        cnt = jnp.minimum(pos + 1, win).astype(F32)
        for j in range(nseg):
            start = POOL_T0 - 8 * gi
            tot = slot.abuf[j, start:POOL_T0 + seg, ls] + slot.abuf[j, start - 1:POOL_T0 + seg - 1, ls]
            shift = 2
            while shift < win:
                n = tot.shape[0]
                tot = tot[8:n] + tot[8 - shift:n - shift]
                shift *= 2
            tok = slot.abuf[j, POOL_T0:POOL_T0 + seg, ls]
            slot.pd[j * seg:(j + 1) * seg, ls] = (tot / cnt - tok).astype(BF16)
    yield
    slot.q[...] = _dot(u, w.wqv[:, 0:D_MODEL]).astype(BF16)
    _, upper = _masks(tm, seg)
    lf = _log_sigmoid(gates[8:16])
    hi = lf.astype(BF16).astype(F32)
    r1 = lf - hi
    mid = r1.astype(BF16).astype(F32)
    lo = r1 - mid
    pieces = jnp.concatenate([hi, mid, lo, jnp.zeros_like(lo)], axis=0).astype(BF16)
    cs = _dot(pieces, jnp.where(upper, 1.0, 0.0).astype(BF16))
    b = cs[0:8] + cs[8:16] + cs[16:24]
    slot.gate[0:8, :] = gates[0:8] - b
    slot.gate[8:16, :] = lf
    yield
    slot.v[...] = _dot(u, w.wqv[:, D_MODEL:2 * D_MODEL]).astype(BF16)
    yield
    slot.og[...] = _dot(u, w.wrest[:, 0:_P0])
    yield
    slot.ga[...] = _dot(u, w.wrest[:, _GA0:_GB0])
    yield
    slot.gb[...] = _dot(u, w.wrest[:, _GB0:_REST_ROWS])
    yield


def _finish(x, out_ref, rows, w, slot, hcat_s, c_rd, n_rd, m_rd, c_store, n_store, m_store, *, tm, seg):
    nseg = tm // seg
    causal, _ = _masks(tm, seg)

    heads = []
    for h in range(N_HEADS):
        sl = slice(h * HEAD_DIM, (h + 1) * HEAD_DIM)
        a_m = jnp.where(causal, jnp.broadcast_to(slot.gate[h:h + 1, :], (tm, tm)), -jnp.inf)
        mprev = jnp.concatenate(
            [jnp.broadcast_to(m_rd(j, h), (seg, 1)) for j in range(nseg)], axis=0)
        g = jnp.maximum(jnp.max(a_m, axis=-1, keepdims=True), mprev)
        dm = jnp.exp(a_m - g)
        lf_b = jnp.broadcast_to(slot.gate[8 + h:9 + h, :], (tm, tm))
        bcol = jnp.sum(jnp.where(causal, lf_b, 0.0), axis=-1, keepdims=True)
        mcol = bcol + g
        wi = jnp.exp(mprev - g)

        qh = slot.q[:, sl]
        vh = slot.v[:, sl]
        s = _dot(qh, slot.ktb[sl, :]) * dm
        qf = qh.astype(F32)
        qc_parts, qn_parts = [], []
        for j in range(nseg):
            rs = slice(j * seg, (j + 1) * seg)
            qc_parts.append(_dot(qh[rs], c_rd[j, h].astype(BF16)))
            qn_parts.append(jnp.sum(qf[rs] * n_rd(j, h), axis=-1, keepdims=True))
        qc = jnp.concatenate(qc_parts, axis=0)
        qn = jnp.concatenate(qn_parts, axis=0)
        heads.append((sl, dm, mcol, wi, vh, s, qc, qn))
    yield

    for h, (sl, dm, mcol, wi, vh, s, qc, qn) in enumerate(heads):
        kth = slot.kt[sl, :]
        rowsum = jnp.sum(s, axis=-1, keepdims=True)
        sv = _dot(s.astype(BF16), vh)
        num = wi * qc + sv
        den = wi * qn + rowsum
        hh = num / jnp.maximum(jnp.abs(den), jnp.exp(-mcol))
        hh = hh * lax.rsqrt(jnp.mean(hh * hh, axis=-1, keepdims=True) + EPS) * w.ghead[:, sl]
        hh = hh * _sigmoid(slot.og[:, sl])
        hcat_s[:, sl] = hh.astype(BF16)

        wrows, n_decayed = [], []
        for j in range(nseg):
            r = (j + 1) * seg - 1
            w_row = dm[r:r + 1, :]
            wrows.append(w_row)
            decay = wi[r:r + 1, :]
            kw = (kth * w_row).astype(BF16)
            c_store(j, h, decay * c_rd[j, h] + _dot(kw, vh))
            n_decayed.append(decay * n_rd(j, h))
            m_store(j, h, mcol[r:r + 1, :])
        ridx = lax.broadcasted_iota(jnp.int32, (GATE_ROWS, tm), 0)
        wmat = jnp.zeros((GATE_ROWS, tm), F32)
        for j in range(nseg):
            wmat = jnp.where(ridx == j, jnp.broadcast_to(wrows[j], (GATE_ROWS, tm)), wmat)
        nupd = _dot_nt(wmat.astype(BF16), slot.ktb[sl, :])
        for j in range(nseg):
            n_store(j, h, n_decayed[j] + nupd[j:j + 1, :])
        yield

    grp_out = [_dot(slot.pd[:, gi * POOL_GROUP:(gi + 1) * POOL_GROUP], w.wgrp[gi])
               for gi in range(len(POOL_WINDOWS))]
    pooled = jnp.concatenate(grp_out, axis=-1) * w.pscale[...]
    yield

    branch_a = _dot(hcat_s[...], w.wba[...])
    branch_b = _dot(pooled.astype(BF16), w.wbb[...])
    yield
    mixed = _sigmoid(slot.ga[...]) * branch_a + _sigmoid(slot.gb[...]) * branch_b
    out_ref[rows, :] = x + _dot(mixed.astype(BF16), w.wout[...])
    yield


def _run(order, **gens):
    for name in order:
        next(gens[name], None)
    for gen in gens.values():
        for _ in gen:
            pass


class _StateRing:
    def __init__(self, dst_hbm, ring, sems, first_stream, step, last_step, stores_per_step):
        self.dst, self.ring, self.sems, self.first = dst_hbm, ring, sems, first_stream
        self.step, self.last_step = step, last_step
        self.bank = ring.shape[0] // 2
        assert stores_per_step % ring.shape[0] == 0
        self.count = 0

    def _copy(self, slot, stream, h):
        return pltpu.make_async_copy(self.ring.at[slot], self.dst.at[stream, h], self.sems.at[slot])

    def _wait_bank(self, bank):
        for slot in range(bank * self.bank, (bank + 1) * self.bank):
            self._copy(slot, 0, 0).wait()

    def store(self, j, h, value):
        slot = self.count % self.ring.shape[0]
        if slot % self.bank == 0:
            bank = slot // self.bank
            if self.count < self.ring.shape[0]:
                pl.when(self.step > 0)(functools.partial(self._wait_bank, bank))
            else:
                self._wait_bank(bank)
        self.count += 1
        self.ring[slot] = value
        self._copy(slot, self.first + j, h).start()

    def finish(self):
        @pl.when(self.step == self.last_step)
        def _():
            self._wait_bank(0)
            self._wait_bank(1)


def _carry_kernel(x_ref, xnext_ref, gmix_ref, wqv_ref, wrest_ref, wkg_ref, bias_ref, ghead_ref, wgrp_ref,
                  pscale_ref, wba_ref, wbb_ref, wout_ref, *rest, tm, n_riders):
    i = pl.program_id(0)
    rider_in, rest = rest[:n_riders], rest[n_riders:]
    x1_ref, c_out, n_out, m_out, hist_out = rest[:5]
    rider_out, (hcat_s, n_s, m_s, *slot_refs) = rest[5:5 + n_riders], rest[5 + n_riders:]
    w = _Weights(gmix_ref, wqv_ref, wrest_ref, wkg_ref, bias_ref, ghead_ref, wgrp_ref, pscale_ref,
                 wba_ref, wbb_ref, wout_ref)
    nslot = len(_Slot._fields)
    slot_a, slot_b = _Slot(*slot_refs[:nslot]), _Slot(*slot_refs[nslot:])
    kw = dict(tm=tm, seg=tm)
    for src, dst in zip(rider_in, rider_out):
        dst[...] = src[...].astype(BF16)

    row = lax.broadcasted_iota(jnp.int32, (tm, POOL_GROUP), 0)

    @pl.when(i == 0)
    def _():
        _init_scratch(slot_a, slot_b)
        c_out[...] = jnp.zeros(c_out.shape, F32)
        n_s[...] = jnp.zeros(n_s.shape, F32)
        m_s[...] = jnp.zeros(m_s.shape, F32)
        _run("", P=_project(x_ref[0:tm, :], w, slot_a, [jnp.zeros((HIST_ROWS, D_POOL), F32)], row, **kw))

    head_cols = [slice(h * HEAD_DIM, (h + 1) * HEAD_DIM) for h in range(N_HEADS)]
    head_lanes = [slice(h * LANES, (h + 1) * LANES) for h in range(N_HEADS)]

    def c_store(j, h, value):
        c_out[j, h] = value

    def n_store(j, h, value):
        n_s[:, head_cols[h]] = value

    def m_store(j, h, value):
        m_s[:, head_lanes[h]] = jnp.broadcast_to(value, (1, LANES))

    state_io = (c_out, lambda j, h: n_s[:, head_cols[h]], lambda j, h: m_s[:, h * LANES:h * LANES + 1],
                c_store, n_store, m_store)

    def finish_beside(rows, slot, tile, x_proj, slot_proj):
        new_hist = slot.abuf[0, POOL_T0 + tm - HIST_ROWS:POOL_T0 + tm, :]
        _run(_CARRY_ORDER,
             F=_finish(x_ref[rows, :], x1_ref, rows, w, slot, hcat_s, *state_io, **kw),
             P=_project(x_proj, w, slot_proj, [new_hist], row + (tile + 1) * tm, **kw))

    finish_beside(slice(0, tm), slot_a, 2 * i, x_ref[tm:2 * tm, :], slot_b)
    finish_beside(slice(tm, 2 * tm), slot_b, 2 * i + 1, xnext_ref[...], slot_a)
    for h in range(N_HEADS):
        n_out[0, h:h + 1, :] = n_s[:, head_cols[h]]
        m_out[:, h:h + 1] = m_s[:, h * LANES:h * LANES + 1]
    hist_out[0] = slot_b.abuf[0, POOL_T0 + tm - POOL_HIST:POOL_T0 + tm, :]


def _streams_kernel(x_ref, gmix_ref, wqv_ref, wrest_ref, wkg_ref, bias_ref, ghead_ref, wgrp_ref,
                    pscale_ref, wba_ref, wbb_ref, wout_ref, c_in, n_in, m_in, hist_in,
                    x1_ref, c_hbm, n_out, m_out, hist_out,
                    hcat_s, cring_s, cring_sem, cin_s, cin_sem, *slot_refs, tm, seg, pos0):
    i = pl.program_id(0)
    nseg = tm // seg
    w = _Weights(gmix_ref, wqv_ref, wrest_ref, wkg_ref, bias_ref, ghead_ref, wgrp_ref, pscale_ref,
                 wba_ref, wbb_ref, wout_ref)
    slot = _Slot(*slot_refs)
    buf = lax.rem(i, 2)

    def c_fetch(step, b):
        return pltpu.make_async_copy(c_in.at[pl.ds(step * nseg, nseg)], cin_s.at[b], cin_sem.at[b])

    @pl.when(i == 0)
    def _():
        _init_scratch(slot)
        c_fetch(0, 0).start()

    @pl.when(i + 1 < pl.num_programs(0))
    def _():
        c_fetch(i + 1, 1 - buf).start()

    x = x_ref[...]
    pos = lax.broadcasted_iota(jnp.int32, (seg, POOL_GROUP), 0) + pos0
    ring = _StateRing(c_hbm, cring_s, cring_sem, first_stream=i * nseg, step=i, last_step=pl.num_programs(0) - 1,
                      stores_per_step=nseg * N_HEADS)

    def n_store(j, h, value):
        n_out[j, h:h + 1, :] = value

    def m_store(j, h, value):
        m_out[j:j + 1, h:h + 1] = value

    def finish_when_fetched():
        c_fetch(i, buf).wait()
        yield from _finish(x, x1_ref, slice(0, tm), w, slot, hcat_s, cin_s.at[buf], lambda j, h: n_in[j, h:h + 1, :],
                           lambda j, h: m_in[j:j + 1, h:h + 1], ring.store, n_store, m_store, tm=tm, seg=seg)

    _run(_STREAMS_ORDER,
         P=_project(x, w, slot, [hist_in[:, j, :] for j in range(nseg)], pos, tm=tm, seg=seg),
         F=finish_when_fetched())
    ring.finish()
    for j in range(nseg):
        hist_out[:, j, :] = slot.abuf[j, POOL_T0 + seg - POOL_HIST:POOL_T0 + seg, :]


def _ffn_tile(x_ref, y_ref, gffn_ref, wup_ref, wdown_ref, gfin_ref, sub):
    n_sub = x_ref.shape[0] // sub
    rows = [slice(r * sub, (r + 1) * sub) for r in range(n_sub)]
    act = {}

    def up(r):
        u = _rmsnorm(x_ref[rows[r], :], gffn_ref[...]).astype(BF16)
        act[r] = jnp.square(jnp.maximum(_dot(u, wup_ref[...]), 0.0)).astype(BF16)

    def down(r):
        x2 = x_ref[rows[r], :] + _dot(act.pop(r), wdown_ref[...])
        y_ref[rows[r], :] = _rmsnorm(x2, gfin_ref[...])

    up(0)
    for r in range(1, n_sub):
        up(r)
        down(r - 1)
    down(n_sub - 1)


def _ffn_kernel(xs_ref, xp_ref, gffn_ref, wup_ref, wdown_ref, gfin_ref, ys_ref, yp_ref, *, sub, n_s):
    i = pl.program_id(0)

    @pl.when(i < n_s)
    def _():
        _ffn_tile(xs_ref, ys_ref, gffn_ref, wup_ref, wdown_ref, gfin_ref, sub)

    @pl.when(i >= n_s)
    def _():
        _ffn_tile(xp_ref, yp_ref, gffn_ref, wup_ref, wdown_ref, gfin_ref, sub)


def _resident(shape):
    zeros = (0,) * len(shape)
    return pl.BlockSpec(shape, lambda i: zeros, pipeline_mode=pl.Buffered(1))


def _slot_scratch(tm, seg):
    return [
        pltpu.VMEM((tm, D_MODEL), BF16),
        pltpu.VMEM((tm, D_MODEL), BF16),
        pltpu.VMEM((D_MODEL, tm), F32),
        pltpu.VMEM((D_MODEL, tm), BF16),
        pltpu.VMEM((tm, D_MODEL), F32),
        pltpu.VMEM((tm, D_MODEL), F32),
        pltpu.VMEM((tm, D_MODEL), F32),
        pltpu.VMEM((GATE_ROWS, tm), F32),
        pltpu.VMEM((tm // seg, POOL_T0 + seg, D_POOL), F32),
        pltpu.VMEM((tm, D_POOL), BF16),
    ]


def _mixer(x, weights, state, *, tm, seg, carry, pos0, riders=()):
    t = x.shape[0]
    nseg = tm // seg
    n_streams = 1 if carry else t // seg
    sblk = 1 if carry else nseg
    smap = (lambda i: 0) if carry else (lambda i: i)
    bias = jnp.broadcast_to(weights["gate_bias"][:, None], (GATE_ROWS, tm))

    if carry:
        assert pos0 == 0 and nseg == 1 and t % (2 * tm) == 0
        steps = t // (2 * tm)
        last_tile = t // tm - 1
        x_specs = [pl.BlockSpec((2 * tm, D_MODEL), lambda i: (i, 0)),
                   pl.BlockSpec((tm, D_MODEL), lambda i: (jnp.minimum(2 * i + 2, last_tile), 0))]
        x_args = [x, x]
        x1_spec = pl.BlockSpec((2 * tm, D_MODEL), lambda i: (i, 0))
        body = functools.partial(_carry_kernel, tm=tm, n_riders=len(riders))
        slots = [pltpu.VMEM((1, D_MODEL), F32), pltpu.VMEM((1, N_HEADS * LANES), F32)] + 2 * _slot_scratch(tm, seg)
        assert steps % RIDER_HOLD_STEPS == 0 and all(r.shape[0] % (16 * steps) == 0 for r in riders)
    else:
        steps = t // tm
        x_specs = [pl.BlockSpec((tm, D_MODEL), lambda i: (i, 0))]
        x_args = [x]
        x1_spec = pl.BlockSpec((tm, D_MODEL), lambda i: (i, 0))
        body = functools.partial(_streams_kernel, tm=tm, seg=seg, pos0=pos0)
        slots = [pltpu.VMEM((STATE_RING_SLOTS, HEAD_DIM, HEAD_DIM), F32),
                 pltpu.SemaphoreType.DMA((STATE_RING_SLOTS,)),
                 pltpu.VMEM((2, nseg, N_HEADS, HEAD_DIM, HEAD_DIM), F32),
                 pltpu.SemaphoreType.DMA((2,))] + _slot_scratch(tm, seg)

    state_specs = [
        pl.BlockSpec((sblk, N_HEADS, HEAD_DIM, HEAD_DIM), lambda i: (smap(i), 0, 0, 0)),
        pl.BlockSpec((sblk, N_HEADS, HEAD_DIM), lambda i: (smap(i), 0, 0)),
        pl.BlockSpec((sblk, N_HEADS), lambda i: (smap(i), 0)),
        pl.BlockSpec((1, POOL_HIST, D_POOL), lambda i: (0, 0, 0)) if carry else
        pl.BlockSpec((POOL_HIST, sblk, D_POOL), lambda i: (0, i, 0)),
    ]
    in_specs = x_specs + [
        _resident((1, D_MODEL)),
        _resident((D_MODEL, 2 * D_MODEL)),
        _resident((D_MODEL, _REST_ROWS)),
        _resident((D_MODEL + GATE_ROWS, D_MODEL)),
        _resident((GATE_ROWS, tm)),
        _resident((1, D_MODEL)),
        _resident((len(POOL_WINDOWS), POOL_GROUP, POOL_GROUP)),
        _resident((1, D_POOL)),
        _resident((D_MODEL, D_MODEL)),
        _resident((D_POOL, D_MODEL)),
        _resident((D_MODEL, D_MODEL)),
    ] + ([] if carry else [pl.BlockSpec(memory_space=pl.ANY)] + state_specs[1:])
    rider_specs = [pl.BlockSpec((r.shape[0] * RIDER_HOLD_STEPS // steps, r.shape[1]),
                                lambda i: (i // RIDER_HOLD_STEPS, 0)) for r in riders]
    out_shape = [
        jax.ShapeDtypeStruct((t, D_MODEL), F32),
        jax.ShapeDtypeStruct((n_streams, N_HEADS, HEAD_DIM, HEAD_DIM), F32),
        jax.ShapeDtypeStruct((n_streams, N_HEADS, HEAD_DIM), F32),
        jax.ShapeDtypeStruct((n_streams, N_HEADS), F32),
        jax.ShapeDtypeStruct((1, POOL_HIST, D_POOL) if carry else (POOL_HIST, n_streams, D_POOL), F32),
    ] + [jax.ShapeDtypeStruct(r.shape, BF16) for r in riders]
    scratch = [pltpu.VMEM((tm, D_MODEL), BF16)] + slots
    return pl.pallas_call(
        body,
        grid=(steps,),
        in_specs=in_specs + rider_specs,
        out_specs=([x1_spec, state_specs[0] if carry else pl.BlockSpec(memory_space=pl.ANY)] + state_specs[1:]
                   + rider_specs),
        out_shape=out_shape,
        scratch_shapes=scratch,
        compiler_params=pltpu.CompilerParams(
            dimension_semantics=("arbitrary",), vmem_limit_bytes=V7X_VMEM_LIMIT_BYTES),
        name="mixer_carry" if carry else "mixer_streams",
    )(*x_args, weights["g_mix"], weights["w_qv"], weights["w_rest"], weights["w_kg"], bias, weights["g_head"],
      weights["w_grp"], weights["pool_scale"], weights["w_ba"], weights["w_bb"], weights["w_out"], *state, *riders)


def _ffn(x_sample, x_prompt, w_up, w_down, weights, *, tm_s, tm_p):
    ts, tp = x_sample.shape[0], x_prompt.shape[0]
    n_s, n_p = ts // tm_s, tp // tm_p
    clamp = lambda v, hi: jnp.clip(v, 0, hi)
    return pl.pallas_call(
        functools.partial(_ffn_kernel, sub=512, n_s=n_s),
        grid=(n_s + n_p,),
        in_specs=[
            pl.BlockSpec((tm_s, D_MODEL), lambda i: (clamp(i, n_s - 1), 0)),
            pl.BlockSpec((tm_p, D_MODEL), lambda i: (clamp(i - n_s, n_p - 1), 0)),
            _resident((1, D_MODEL)),
            _resident((D_MODEL, D_FF)),
            _resident((D_FF, D_MODEL)),
            _resident((1, D_MODEL)),
        ],
        out_specs=[
            pl.BlockSpec((tm_s, D_MODEL), lambda i: (clamp(i, n_s - 1), 0)),
            pl.BlockSpec((tm_p, D_MODEL), lambda i: (clamp(i - n_s, n_p - 1), 0)),
        ],
        out_shape=[jax.ShapeDtypeStruct((ts, D_MODEL), F32), jax.ShapeDtypeStruct((tp, D_MODEL), F32)],
        compiler_params=pltpu.CompilerParams(
            dimension_semantics=("arbitrary",), vmem_limit_bytes=V7X_VMEM_LIMIT_BYTES),
        name="ffn",
    )(x_sample, x_prompt, weights["g_ffn"], w_up, w_down, weights["g_final"])


def _split_w_in_kernel(w_ref, *refs, n_riders):
    i = pl.program_id(0)
    rider_in, (wqv_ref, wkg_ref, wrest_ref) = refs[:n_riders], refs[n_riders:n_riders + 3]
    rider_out, prev_s = refs[n_riders + 3:2 * n_riders + 3], refs[2 * n_riders + 3]
    n_qkv = _QKV_ROWS // _SPLIT_ROWS
    cur = w_ref[...]
    for src, dst in zip(rider_in, rider_out):
        dst[...] = src[...].astype(BF16)

    @pl.when((i == 0) | (i == 2))
    def _():
        wqv_ref[...] = cur.T.astype(BF16)

    @pl.when(i == 1)
    def _():
        wkg_ref[0:D_MODEL, :] = cur.astype(BF16)

    @pl.when(i == n_qkv)
    def _():
        gate = cur[0:2 * N_HEADS, :]
        wkg_ref[D_MODEL:D_MODEL + GATE_ROWS, :] = jnp.concatenate(
            [gate, pltpu.roll(gate, N_HEADS, axis=0)], axis=0).astype(BF16)

    @pl.when(i > n_qkv)
    def _():
        rows = jnp.concatenate([prev_s[2 * N_HEADS:, :], cur[0:2 * N_HEADS, :]], axis=0)
        wrest_ref[...] = rows.T.astype(BF16)

    prev_s[...] = cur


def _split_w_in(w_in_t, riders=()):
    n_in = w_in_t.shape[0]
    assert n_in == _REST0 + _REST_ROWS and _SPLIT_ROWS == D_MODEL and _QKV_ROWS == 3 * _SPLIT_ROWS
    n_qkv, n_rest = _QKV_ROWS // _SPLIT_ROWS, pl.cdiv(_REST_ROWS, _SPLIT_ROWS)
    steps = n_qkv + 1 + n_rest
    last_chunk = pl.cdiv(n_in, _SPLIT_ROWS) - 1
    assert all(r.shape[0] % (16 * steps) == 0 for r in riders)
    rider_specs = [pl.BlockSpec((r.shape[0] // steps, r.shape[1]), lambda i: (i, 0)) for r in riders]
    return pl.pallas_call(
        functools.partial(_split_w_in_kernel, n_riders=len(riders)),
        grid=(steps,),
        in_specs=[pl.BlockSpec((_SPLIT_ROWS, D_MODEL), lambda i: (jnp.minimum(i, last_chunk), 0))] + rider_specs,
        out_specs=[
            pl.BlockSpec((D_MODEL, _SPLIT_ROWS), lambda i: (0, jnp.where(i < 2, 0, 1))),
            pl.BlockSpec((D_MODEL + GATE_ROWS, D_MODEL), lambda i: (0, 0)),
            pl.BlockSpec((D_MODEL, _SPLIT_ROWS), lambda i: (0, jnp.clip(i - n_qkv - 1, 0, n_rest - 1))),
        ] + rider_specs,
        out_shape=[
            jax.ShapeDtypeStruct((D_MODEL, 2 * D_MODEL), BF16),
            jax.ShapeDtypeStruct((D_MODEL + GATE_ROWS, D_MODEL), BF16),
            jax.ShapeDtypeStruct((D_MODEL, _REST_ROWS), BF16),
        ] + [jax.ShapeDtypeStruct(r.shape, BF16) for r in riders],
        scratch_shapes=[pltpu.VMEM((_SPLIT_ROWS, D_MODEL), F32)],
        compiler_params=pltpu.CompilerParams(
            dimension_semantics=("arbitrary",), vmem_limit_bytes=V7X_VMEM_LIMIT_BYTES),
        name="split_w_in",
    )(w_in_t, *riders)


def _pack_weights(w_in, b_igate, b_fgate, g_norm_mix, g_head, w_pool_grp, pool_scale, w_branch_mlstm,
                  w_branch_pool, w_out, g_norm_ffn, w_up, w_down, g_final):
    d = D_MODEL
    w_qv, w_kg, w_rest, w_ba, w_bb, w_o, w_grp = _split_w_in(
        w_in.T, riders=(w_branch_mlstm, w_branch_pool, w_out, w_pool_grp.reshape(-1, POOL_GROUP)))
    return {
        "w_qv": w_qv,
        "w_kg": w_kg,
        "w_rest": w_rest,
        "gate_bias": jnp.concatenate([b_igate, b_fgate, b_fgate, b_igate]).astype(F32),
        "g_mix": g_norm_mix.reshape(1, d),
        "g_head": g_head.reshape(1, d),
        "w_grp": w_grp.reshape(w_pool_grp.shape),
        "pool_scale": pool_scale.reshape(1, D_POOL),
        "w_ba": w_ba,
        "w_bb": w_bb,
        "w_out": w_o,
        "g_ffn": g_norm_ffn.reshape(1, d),
        "g_final": g_final.reshape(1, d),
    }


def kernel(x_prompt, x_sample, state_C, state_n, state_m, state_pool, w_in, b_igate, b_fgate, g_norm_mix, g_head,
           w_pool_grp, pool_scale, w_branch_mlstm, w_branch_pool, w_out, g_norm_ffn, w_up, w_down, g_final):
    depth = w_in.shape[0]
    bp, sp, d = x_prompt.shape
    bs, ss, _ = x_sample.shape
    assert depth == 1 and bp == 1 and d == D_MODEL
    hp = x_prompt.reshape(bp * sp, d)
    hs = x_sample.reshape(bs * ss, d)
    weights = _pack_weights(w_in[0], b_igate[0], b_fgate[0], g_norm_mix[0], g_head[0], w_pool_grp[0], pool_scale[0],
                            w_branch_mlstm[0], w_branch_pool[0], w_out[0], g_norm_ffn[0], w_up[0], w_down[0],
                            g_final)
    hp, cp, np_, mp, pp, w_up_b, w_down_b = _mixer(hp, weights, (), tm=256, seg=256, carry=True, pos0=0,
                                                   riders=(w_up[0], w_down[0]))
    sample_state = (state_C[0], state_n[0], state_m[0], jnp.swapaxes(state_pool[0], 0, 1))
    hs, cs, ns, ms, ps = _mixer(hs, weights, sample_state, tm=8 * ss, seg=ss, carry=False, pos0=PAST_LEN)
    y_sample, y_prompt = _ffn(hs, hp, w_up_b, w_down_b, weights, tm_s=512, tm_p=1024)
    y_prompt = y_prompt.reshape(bp, sp, d)
    y_sample = y_sample.reshape(bs, ss, d)
    return (y_prompt, y_sample, cp[None], np_[None], mp[None], pp[None], cs[None], ns[None], ms[None],
            jnp.swapaxes(ps, 0, 1)[None])
```
